```python
import math
import jax
import jax.numpy as jnp
from jax import lax
import numpy as np

D_MODEL = 2048
BATCH = 4
SEQ = 4096
DEPTH = 2

N_EVEN = (DEPTH + 1) // 2
N_ODD = DEPTH // 2
MEM_LEN = 256
EPS = 1e-6
Q_BLOCK = 128

GDN_HEADS = 8
GDN_DK = 128
GDN_DV = 128
GDN_CONV = 5
GDN_CHUNK = 64
GDN_CONV_CH = GDN_HEADS * (2 * GDN_DK + GDN_DV)

DIFF_HEADS = 8
DIFF_DQK = 64
DIFF_DV = 128

SWA_HEADS = 8
SWA_KV_HEADS = 2
SWA_DH = 128
WINDOW = 128

MLA_HEADS = 8
MLA_NOPE = 128
MLA_ROPE = 64
MLA_DV = 128
MLA_KV_RANK = 512
ROPE_BASE = 10000.0

MEM_HEADS = 4
MEM_DH = 128

REL_BUCKETS = 32
REL_MAX_DIST = 128
REL_HEADS = 8

EV_SPLITS = [GDN_HEADS * GDN_DK, GDN_HEADS * GDN_DK, GDN_HEADS * GDN_DV, 2 * GDN_HEADS, 2 * GDN_HEADS, GDN_HEADS * GDN_DV,
             DIFF_HEADS * 2 * DIFF_DQK, DIFF_HEADS * 2 * DIFF_DQK, DIFF_HEADS * DIFF_DV, DIFF_HEADS * DIFF_DV,
             MEM_HEADS * MEM_DH, MEM_HEADS * MEM_DH]
EV_IN = sum(EV_SPLITS)
EV_OUT = GDN_HEADS * GDN_DV + DIFF_HEADS * DIFF_DV + MEM_HEADS * MEM_DH
OD_SPLITS = [SWA_HEADS * SWA_DH, SWA_KV_HEADS * SWA_DH, SWA_KV_HEADS * SWA_DH, SWA_HEADS * SWA_DH,
             MLA_HEADS * (MLA_NOPE + MLA_ROPE), MLA_KV_RANK, MLA_ROPE, MLA_HEADS * MLA_DV,
             MEM_HEADS * MEM_DH, MEM_HEADS * MEM_DH]
OD_IN = sum(OD_SPLITS)
OD_OUT = SWA_HEADS * SWA_DH + MLA_HEADS * MLA_DV + MEM_HEADS * MEM_DH

kernel_name = 'hybrid_bidir_gdn_diff_swa_mla_mem'

F32 = jnp.float32


def rms_norm(x, gain):
    xf = x.astype(F32)
    y = xf * lax.rsqrt(jnp.mean(xf * xf, axis=-1, keepdims=True) + EPS)
    return (y * gain.astype(F32)).astype(x.dtype)


def l2_norm(x):
    xf = x.astype(F32)
    return (xf * lax.rsqrt(jnp.sum(xf * xf, axis=-1, keepdims=True) + EPS)).astype(x.dtype)


def split_cols(h, sizes):
    return jnp.split(h, np.cumsum(sizes)[:-1].tolist(), axis=-1)


def t5_bucket(rel):
    half = REL_BUCKETS // 2
    max_exact = half // 2
    ret = (rel > 0).astype(jnp.int32) * half
    n = jnp.abs(rel)
    nf = jnp.maximum(n, 1).astype(F32)
    large = max_exact + (jnp.log(nf / max_exact) / math.log(REL_MAX_DIST / max_exact) * (half - max_exact)).astype(jnp.int32)
    large = jnp.minimum(large, half - 1)
    return ret + jnp.where(n < max_exact, n, large)


def rope(t, pos):
    half = t.shape[-1] // 2
    inv_freq = ROPE_BASE ** (-jnp.arange(half, dtype=F32) / half)
    ang = pos.astype(F32)[:, :, None, None] * inv_freq
    cos, sin = jnp.cos(ang), jnp.sin(ang)
    t1, t2 = t[..., :half].astype(F32), t[..., half:].astype(F32)
    return jnp.concatenate([t1 * cos - t2 * sin, t1 * sin + t2 * cos], axis=-1).astype(t.dtype)


def short_conv(x, w):
    pad = (GDN_CONV - 1) // 2
    return lax.conv_general_dilated(x, w[:, None, :].astype(x.dtype), window_strides=(1,), padding=[(pad, pad)],
                                    dimension_numbers=('NWC', 'WIO', 'NWC'), feature_group_count=x.shape[-1])


def gated_delta_chunked(q, k, v, g, beta):
    bsz, nh, seq, dk = q.shape
    dv = v.shape[-1]
    c = GDN_CHUNK
    nc = seq // c
    q = q * (dk ** -0.5)
    rs = lambda t: t.reshape(bsz, nh, nc, c, *t.shape[3:])
    q, k, v, g, beta = rs(q), rs(k), rs(v), rs(g), rs(beta)
    g = jnp.cumsum(g, axis=-1)
    lower_incl = jnp.tril(jnp.ones((c, c), bool))
    strict = jnp.tril(jnp.ones((c, c), bool), -1)
    diff = g[..., :, None] - g[..., None, :]
    decay = jnp.where(lower_incl, jnp.exp(jnp.where(lower_incl, diff, 0.0)), 0.0)
    k_beta = k * beta[..., None]
    m = jnp.where(strict, jnp.einsum('bhncd,bhnsd->bhncs', k_beta, k) * decay, 0.0)
    eye = jnp.eye(c, dtype=q.dtype)
    t_inv = lax.linalg.triangular_solve(eye + m, jnp.broadcast_to(eye, m.shape), left_side=True, lower=True,
                                        unit_diagonal=True)
    u = t_inv @ (v * beta[..., None])
    w = t_inv @ (k_beta * jnp.exp(g)[..., None])
    a_intra = jnp.where(lower_incl, jnp.einsum('bhncd,bhnsd->bhncs', q, k) * decay, 0.0)
    g_last = g[..., -1]
    q_dec = q * jnp.exp(g)[..., None]
    k_dec = k * jnp.exp(g_last[..., None] - g)[..., None]

    def step(state, inp):
        q_d, k_d, u_c, w_c, a_c, gl = inp
        v_new = u_c - jnp.einsum('bhcd,bhde->bhce', w_c, state)
        o = jnp.einsum('bhcd,bhde->bhce', q_d, state) + jnp.einsum('bhcs,bhse->bhce', a_c, v_new)
        state = state * jnp.exp(gl)[..., None, None] + jnp.einsum('bhcd,bhce->bhde', k_d, v_new)
        return state, o

    xs = tuple(jnp.moveaxis(t, 2, 0) for t in (q_dec, k_dec, u, w, a_intra, g_last))
    s0 = jnp.zeros((bsz, nh, dk, dv), q.dtype)
    _, o = lax.scan(step, s0, xs)
    return jnp.moveaxis(o, 0, 2).reshape(bsz, nh, seq, dv)


def gdn_branch(q, k, v, b, a, gate, conv_w, a_log, dt_bias, out_gain):
    bsz, seq, _ = q.shape
    qkv = jax.nn.silu(short_conv(jnp.concatenate([q, k, v], axis=-1), conv_w))
    q, k, v = split_cols(qkv, [GDN_HEADS * GDN_DK, GDN_HEADS * GDN_DK, GDN_HEADS * GDN_DV])

    def heads(t, d):
        return t.reshape(bsz, seq, GDN_HEADS, d).transpose(0, 2, 1, 3).astype(F32)

    q = l2_norm(heads(q, GDN_DK))
    k = l2_norm(heads(k, GDN_DK))
    v = heads(v, GDN_DV)

    def per_dir(t):
        return t.astype(F32).reshape(bsz, seq, 2, GDN_HEADS).transpose(2, 0, 3, 1)

    beta = jax.nn.sigmoid(per_dir(b))
    g = -jnp.exp(a_log.astype(F32))[:, None, :, None] * jax.nn.softplus(per_dir(a) + dt_bias.astype(F32)[:, None, :, None])
    o_fwd = gated_delta_chunked(q, k, v, g[0], beta[0])
    rev = lambda t: jnp.flip(t, axis=2)
    o_bwd = rev(gated_delta_chunked(rev(q), rev(k), rev(v), rev(g[1]), rev(beta[1])))
    o = (o_fwd + o_bwd).transpose(0, 2, 1, 3)
    o = rms_norm(o, out_gain) * jax.nn.silu(gate.reshape(bsz, seq, GDN_HEADS, GDN_DV).astype(F32))
    return o.reshape(bsz, seq, GDN_HEADS * GDN_DV).astype(gate.dtype)


def diff_attention(q, k, v, gate, rel_table, q_gain, k_gain, lam, subln_gain, lambda_init):
    bsz, seq, _ = q.shape
    nb = seq // Q_BLOCK
    q = rms_norm(q.reshape(bsz, seq, DIFF_HEADS, 2, DIFF_DQK), q_gain)
    k = rms_norm(k.reshape(bsz, seq, DIFF_HEADS, 2, DIFF_DQK), k_gain)
    v = v.reshape(bsz, seq, DIFF_HEADS, DIFF_DV)
    lam = lam.astype(F32)
    lam_full = jnp.exp(jnp.sum(lam[0] * lam[1])) - jnp.exp(jnp.sum(lam[2] * lam[3])) + lambda_init
    scale = DIFF_DQK ** -0.5
    key_idx = jnp.arange(seq)
    qb = q.reshape(bsz, nb, Q_BLOCK, DIFF_HEADS, 2, DIFF_DQK).swapaxes(0, 1)

    def block(args):
        qblk, n = args
        q_idx = n * Q_BLOCK + jnp.arange(Q_BLOCK)
        bias = rel_table[t5_bucket(key_idx[None, :] - q_idx[:, None])].transpose(2, 0, 1).astype(F32)
        s = jnp.einsum('bqhcd,bkhcd->bchqk', qblk, k).astype(F32) * scale + bias[None, None]
        p = jax.nn.softmax(s, axis=-1)
        p = p[:, 0] - lam_full * p[:, 1]
        return jnp.einsum('bhqk,bkhd->bqhd', p.astype(v.dtype), v)

    o = lax.map(block, (qb, jnp.arange(nb))).swapaxes(0, 1).reshape(bsz, seq, DIFF_HEADS, DIFF_DV)
    o = rms_norm(o, subln_gain) * (1.0 - lambda_init)
    o = o * jax.nn.silu(gate.reshape(bsz, seq, DIFF_HEADS, DIFF_DV))
    return o.reshape(bsz, seq, DIFF_HEADS * DIFF_DV)


def window_attention(q, k, v, gate, rel_table, q_gain, k_gain, sink):
    bsz, seq, _ = q.shape
    grp = SWA_HEADS // SWA_KV_HEADS
    nb = seq // Q_BLOCK
    nbr = WINDOW // Q_BLOCK
    span = 2 * nbr + 1
    q = rms_norm(q.reshape(bsz, nb, Q_BLOCK, SWA_KV_HEADS, grp, SWA_DH), q_gain)
    k = rms_norm(k.reshape(bsz, seq, SWA_KV_HEADS, SWA_DH), k_gain)
    v = v.reshape(bsz, seq, SWA_KV_HEADS, SWA_DH)

    def windows(t):
        tp = jnp.pad(t, ((0, 0), (WINDOW, WINDOW), (0, 0), (0, 0))).reshape(bsz, nb + 2 * nbr, Q_BLOCK, SWA_KV_HEADS, SWA_DH)
        return jnp.concatenate([tp[:, j:j + nb] for j in range(span)], axis=2)

    kw, vw = windows(k), windows(v)
    i = jnp.arange(Q_BLOCK)[:, None]
    j = jnp.arange(span * Q_BLOCK)[None, :]
    rel = j - WINDOW - i
    bias = rel_table[t5_bucket(rel)].transpose(2, 0, 1).reshape(SWA_KV_HEADS, grp, Q_BLOCK, span * Q_BLOCK).astype(F32)
    key_pos = jnp.arange(nb)[:, None, None] * Q_BLOCK + (j - WINDOW)[None]
    valid = (jnp.abs(rel) <= WINDOW)[None] & (key_pos >= 0) & (key_pos < seq)
    s = jnp.einsum('bnqhgd,bnkhd->bnhgqk', q, kw).astype(F32) * (SWA_DH ** -0.5) + bias[None, None]
    s = jnp.where(valid[None, :, None, None], s, -jnp.inf)
    sink_l = sink.astype(F32).reshape(SWA_KV_HEADS, grp)[None, None, :, :, None, None]
    mx = jnp.maximum(jnp.max(s, axis=-1, keepdims=True), sink_l)
    p = jnp.exp(s - mx)
    p = p / (jnp.sum(p, axis=-1, keepdims=True) + jnp.exp(sink_l - mx))
    o = jnp.einsum('bnhgqk,bnkhd->bnqhgd', p.astype(vw.dtype), vw).reshape(bsz, seq, SWA_HEADS, SWA_DH)
    o = o * jax.nn.silu(gate.reshape(bsz, seq, SWA_HEADS, SWA_DH))
    return o.reshape(bsz, seq, SWA_HEADS * SWA_DH)


def mla_attention(q, c_kv, k_rope, gate, positions, kv_gain, w_kv_up, q_gain, k_gain):
    bsz, seq, _ = q.shape
    nb = seq // Q_BLOCK
    dqk = MLA_NOPE + MLA_ROPE
    kv = (rms_norm(c_kv, kv_gain) @ w_kv_up).reshape(bsz, seq, MLA_HEADS, MLA_NOPE + MLA_DV)
    k_nope, v = kv[..., :MLA_NOPE], kv[..., MLA_NOPE:]
    k = jnp.concatenate([k_nope, jnp.broadcast_to(k_rope[:, :, None, :], (bsz, seq, MLA_HEADS, MLA_ROPE))], axis=-1)
    q = rms_norm(q.reshape(bsz, seq, MLA_HEADS, dqk), q_gain)
    k = rms_norm(k, k_gain)
    q = jnp.concatenate([q[..., :MLA_NOPE], rope(q[..., MLA_NOPE:], positions)], axis=-1)
    k = jnp.concatenate([k[..., :MLA_NOPE], rope(k[..., MLA_NOPE:], positions)], axis=-1)
    qb = q.reshape(bsz, nb, Q_BLOCK, MLA_HEADS, dqk).swapaxes(0, 1)

    def block(qblk):
        s = jnp.einsum('bqhd,bkhd->bhqk', qblk, k).astype(F32) * (dqk ** -0.5)
        p = jax.nn.softmax(s, axis=-1)
        return jnp.einsum('bhqk,bkhd->bqhd', p.astype(v.dtype), v)

    o = lax.map(block, qb).swapaxes(0, 1).reshape(bsz, seq, MLA_HEADS, MLA_DV)
    o = o * jax.nn.silu(gate.reshape(bsz, seq, MLA_HEADS, MLA_DV))
    return o.reshape(bsz, seq, MLA_HEADS * MLA_DV)


def memory_attention(q, gate, mem, mem_gain, w_kv, q_gain, k_gain):
    bsz, seq, _ = q.shape
    mlen = mem.shape[1]
    mk, mv = split_cols(rms_norm(mem, mem_gain) @ w_kv, [MEM_HEADS * MEM_DH, MEM_HEADS * MEM_DH])
    mk = rms_norm(mk.reshape(bsz, mlen, MEM_HEADS, MEM_DH), k_gain)
    mv = mv.reshape(bsz, mlen, MEM_HEADS, MEM_DH)
    q = rms_norm(q.reshape(bsz, seq, MEM_HEADS, MEM_DH), q_gain)
    s = jnp.einsum('bqhd,bmhd->bhqm', q, mk).astype(F32) * (MEM_DH ** -0.5)
    p = jax.nn.softmax(s, axis=-1)
    o = jnp.einsum('bhqm,bmhd->bqhd', p.astype(mv.dtype), mv)
    o = o * jax.nn.silu(gate.reshape(bsz, seq, MEM_HEADS, MEM_DH))
    return o.reshape(bsz, seq, MEM_HEADS * MEM_DH)


def even_layer(x, mem, rel_table, norm_g, w_in, conv_w, a_log, dt_bias, gdn_gain, dq_gain, dk_gain, lam, subln,
               mem_norm, mem_w_kv, mem_qn, mem_kn, w_out, lambda_init):
    h = rms_norm(x, norm_g) @ w_in
    gq, gk, gv, gb, ga, gg, dq, dk, dv, dg, mq, mg = split_cols(h, EV_SPLITS)
    y = jnp.concatenate([
        gdn_branch(gq, gk, gv, gb, ga, gg, conv_w, a_log, dt_bias, gdn_gain),
        diff_attention(dq, dk, dv, dg, rel_table, dq_gain, dk_gain, lam, subln, lambda_init),
        memory_attention(mq, mg, mem, mem_norm, mem_w_kv, mem_qn, mem_kn),
    ], axis=-1)
    return x + y @ w_out


def odd_layer(x, mem, positions, rel_table, norm_g, w_in, swa_qn, swa_kn, sink, kv_norm, w_kv_up, mla_qn, mla_kn,
              mem_norm, mem_w_kv, mem_qn, mem_kn, w_out):
    h = rms_norm(x, norm_g) @ w_in
    sq, sk, sv, sg, mlq, ckv, kr, mlg, mq, mg = split_cols(h, OD_SPLITS)
    y = jnp.concatenate([
        window_attention(sq, sk, sv, sg, rel_table, swa_qn, swa_kn, sink),
        mla_attention(mlq, ckv, kr, mlg, positions, kv_norm, w_kv_up, mla_qn, mla_kn),
        memory_attention(mq, mg, mem, mem_norm, mem_w_kv, mem_qn, mem_kn),
    ], axis=-1)
    return x + y @ w_out


def setup_inputs(seed: int = 0) -> dict:
    key = jax.random.key(seed)
    keys = jax.random.split(key, 33)

    def nrm(i, shape, scale):
        return scale * jax.random.normal(keys[i], shape, F32)

    def gain(i, shape):
        return 1.0 + 0.02 * jax.random.normal(keys[i], shape, F32)

    res = (2.0 * DEPTH) ** -0.5
    offsets = jax.random.randint(keys[2], (BATCH, 1), 0, 4096, dtype=jnp.int32)
    positions = offsets + jnp.arange(SEQ, dtype=jnp.int32)[None, :]
    a_log = jnp.log(jax.random.uniform(keys[7], (N_EVEN, 2, GDN_HEADS), F32, 1.0, 16.0))
    dt = jnp.exp(jax.random.uniform(keys[8], (N_EVEN, 2, GDN_HEADS), F32, math.log(1e-3), math.log(1e-1)))
    dt_bias = dt + jnp.log(-jnp.expm1(-dt))
    return {
        'x': nrm(0, (BATCH, SEQ, D_MODEL), 1.0),
        'mem': nrm(1, (BATCH, MEM_LEN, D_MODEL), 1.0),
        'positions': positions,
        'rel_bias': nrm(3, (REL_BUCKETS, REL_HEADS), 0.5),
        'ev_norm': gain(4, (N_EVEN, D_MODEL)),
        'ev_w_in': nrm(5, (N_EVEN, D_MODEL, EV_IN), D_MODEL ** -0.5),
        'ev_conv': nrm(6, (N_EVEN, GDN_CONV, GDN_CONV_CH), GDN_CONV ** -0.5),
        'ev_a_log': a_log,
        'ev_dt_bias': dt_bias,
        'ev_gdn_norm': gain(9, (N_EVEN, GDN_DV)),
        'ev_diff_qnorm': gain(10, (N_EVEN, DIFF_DQK)),
        'ev_diff_knorm': gain(11, (N_EVEN, DIFF_DQK)),
        'ev_diff_lambda': nrm(12, (N_EVEN, 4, DIFF_DQK), 0.1),
        'ev_diff_subln': gain(13, (N_EVEN, DIFF_DV)),
        'ev_mem_norm': gain(14, (N_EVEN, D_MODEL)),
        'ev_mem_w_kv': nrm(15, (N_EVEN, D_MODEL, 2 * MEM_HEADS * MEM_DH), D_MODEL ** -0.5),
        'ev_mem_qnorm': gain(16, (N_EVEN, MEM_DH)),
        'ev_mem_knorm': gain(17, (N_EVEN, MEM_DH)),
        'ev_w_out': nrm(18, (N_EVEN, EV_OUT, D_MODEL), res * EV_OUT ** -0.5),
        'od_norm': gain(19, (N_ODD, D_MODEL)),
        'od_w_in': nrm(20, (N_ODD, D_MODEL, OD_IN), D_MODEL ** -0.5),
        'od_swa_qnorm': gain(21, (N_ODD, SWA_DH)),
        'od_swa_knorm': gain(22, (N_ODD, SWA_DH)),
        'od_swa_sink': nrm(23, (N_ODD, SWA_HEADS), 0.5),
        'od_mla_kv_norm': gain(24, (N_ODD, MLA_KV_RANK)),
        'od_mla_w_kv_up': nrm(25, (N_ODD, MLA_KV_RANK, MLA_HEADS * (MLA_NOPE + MLA_DV)), MLA_KV_RANK ** -0.5),
        'od_mla_qnorm': gain(26, (N_ODD, MLA_NOPE + MLA_ROPE)),
        'od_mla_knorm': gain(27, (N_ODD, MLA_NOPE + MLA_ROPE)),
        'od_mem_norm': gain(28, (N_ODD, D_MODEL)),
        'od_mem_w_kv': nrm(29, (N_ODD, D_MODEL, 2 * MEM_HEADS * MEM_DH), D_MODEL ** -0.5),
        'od_mem_qnorm': gain(30, (N_ODD, MEM_DH)),
        'od_mem_knorm': gain(31, (N_ODD, MEM_DH)),
        'od_w_out': nrm(32, (N_ODD, OD_OUT, D_MODEL), res * OD_OUT ** -0.5),
    }


def reference(x, mem, positions, rel_bias,
              ev_norm, ev_w_in, ev_conv, ev_a_log, ev_dt_bias, ev_gdn_norm, ev_diff_qnorm, ev_diff_knorm,
              ev_diff_lambda, ev_diff_subln, ev_mem_norm, ev_mem_w_kv, ev_mem_qnorm, ev_mem_knorm, ev_w_out,
              od_norm, od_w_in, od_swa_qnorm, od_swa_knorm, od_swa_sink, od_mla_kv_norm, od_mla_w_kv_up,
              od_mla_qnorm, od_mla_knorm, od_mem_norm, od_mem_w_kv, od_mem_qnorm, od_mem_knorm, od_w_out):
    for layer in range(DEPTH):
        i = layer // 2
        if layer % 2 == 0:
            lambda_init = 0.8 - 0.6 * math.exp(-0.3 * layer)
            x = even_layer(x, mem, rel_bias, ev_norm[i], ev_w_in[i], ev_conv[i], ev_a_log[i], ev_dt_bias[i],
                           ev_gdn_norm[i], ev_diff_qnorm[i], ev_diff_knorm[i], ev_diff_lambda[i], ev_diff_subln[i],
                           ev_mem_norm[i], ev_mem_w_kv[i], ev_mem_qnorm[i], ev_mem_knorm[i], ev_w_out[i], lambda_init)
        else:
            x = odd_layer(x, mem, positions, rel_bias, od_norm[i], od_w_in[i], od_swa_qnorm[i], od_swa_knorm[i],
                          od_swa_sink[i], od_mla_kv_norm[i], od_mla_w_kv_up[i], od_mla_qnorm[i], od_mla_knorm[i],
                          od_mem_norm[i], od_mem_w_kv[i], od_mem_qnorm[i], od_mem_knorm[i], od_w_out[i])
    return x
```

```python
import functools
import math

import numpy as np
import jax
import jax.numpy as jnp
from jax import lax
from jax.experimental import pallas as pl
from jax.experimental.pallas import tpu as pltpu

F32 = jnp.float32
BF16 = jnp.bfloat16
EPS = 1e-6
NEG = -1e30

V7X_VMEM_BYTES = 64 * 1024 * 1024
VMEM_LIMIT = V7X_VMEM_BYTES - 8 * 1024 * 1024
LANES = 128

D_MODEL = 2048
MEM_LEN = 256
GDN_HEADS, GDN_DK, GDN_DV, GDN_CONV = 8, 128, 128, 5
DIFF_HEADS, DIFF_DQK, DIFF_DV = 8, 64, 128
SWA_HEADS, SWA_KV_HEADS, SWA_DH, WINDOW = 8, 2, 128, 128
MLA_HEADS, MLA_NOPE, MLA_ROPE, MLA_DV, MLA_KV_RANK = 8, 128, 64, 128, 512
ROPE_BASE = 10000.0
MEM_HEADS, MEM_DH = 4, 128
REL_BUCKETS, REL_MAX_DIST = 32, 128

GDN_CHUNK = 256
GDN_LEVELS = GDN_CHUNK.bit_length() - 1
ATT_TQ = 512
ATT_TK = 512
SWA_TQ = 256
SWA_TW = SWA_TQ + 2 * WINDOW

EV_GQ, EV_GK, EV_GV, EV_GG = 0, 1024, 2048, 3072
EV_DQ, EV_DK, EV_DV, EV_DG = 4096, 5120, 6144, 7168
EV_MQ, EV_MG = 8192, 8704
EV_MAIN = 9216
EV_SIDE_LO, EV_SIDE_HI = 3072, 3104

OD_MLQ_NOPE, OD_MLQ_ROPE = 0, 1024
OD_SQ, OD_SK, OD_SV, OD_SG = 1536, 2560, 2816, 3072
OD_CKV, OD_MLG, OD_MQ, OD_MG = 4096, 4608, 5632, 6144
OD_MAIN = 6656
OD_KR_LO, OD_KR_HI = 4608, 4672


def _cparams(sem):
    return pltpu.CompilerParams(dimension_semantics=sem, vmem_limit_bytes=VMEM_LIMIT)


def _dot(a, b):
    return jnp.dot(a, b, preferred_element_type=F32)


def _dot_nt(a, b):
    return lax.dot_general(a, b, (((1,), (1,)), ((), ())), preferred_element_type=F32)


def _dot_exact(a, b):
    return jnp.dot(a, b, preferred_element_type=F32, precision=lax.Precision.HIGHEST)


def _silu(x):
    return x * jax.nn.sigmoid(x)


def _softplus(x):
    return jnp.maximum(x, 0.0) + jnp.log(1.0 + jnp.exp(-jnp.abs(x)))


def _rms(x, gain):
    return x * lax.rsqrt(jnp.mean(x * x, axis=-1, keepdims=True) + EPS) * gain


def _norm_proj_kernel(x_ref, g_ref, w_ref, ws_ref, o_ref, os_ref, xn_ref):
    @pl.when(pl.program_id(1) == 0)
    def _():
        xn = _rms(x_ref[...], g_ref[...]).astype(BF16)
        xn_ref[...] = xn
        os_ref[...] = _dot(xn, ws_ref[...])

    o_ref[...] = _dot(xn_ref[...], w_ref[...])


def _norm_proj(x, gain, w_main, w_side, tm, tn):
    m, k = x.shape
    n = w_main.shape[1]
    ns = w_side.shape[1]
    assert m % tm == 0 and n % tn == 0
    return pl.pallas_call(
        _norm_proj_kernel,
        grid=(m // tm, n // tn),
        in_specs=[
            pl.BlockSpec((tm, k), lambda i, j: (i, 0)),
            pl.BlockSpec((1, k), lambda i, j: (0, 0)),
            pl.BlockSpec((k, tn), lambda i, j: (0, j)),
            pl.BlockSpec((k, ns), lambda i, j: (0, 0)),
        ],
        out_specs=[
            pl.BlockSpec((tm, tn), lambda i, j: (i, j)),
            pl.BlockSpec((tm, ns), lambda i, j: (i, 0)),
        ],
        out_shape=[jax.ShapeDtypeStruct((m, n), F32), jax.ShapeDtypeStruct((m, ns), F32)],
        scratch_shapes=[pltpu.VMEM((tm, k), BF16)],
        compiler_params=_cparams(("parallel", "arbitrary")),
        name="norm_proj",
    )(x, gain.reshape(1, k), w_main, w_side)


def _out_proj_kernel(x_ref, ya_ref, yb_ref, ym_ref, wa_ref, wb_ref, wm_ref, o_ref):
    acc = _dot(ya_ref[...], wa_ref[...])
    acc = acc + _dot(yb_ref[...], wb_ref[...])
    acc = acc + _dot(ym_ref[...], wm_ref[...])
    o_ref[...] = x_ref[...] + acc


def _out_proj(x, ya, yb, ym, w_out, tm):
    m, d = x.shape
    na, nb, nm = ya.shape[1], yb.shape[1], ym.shape[1]
    wa = w_out[:na].astype(BF16)
    wb = w_out[na:na + nb].astype(BF16)
    wm = w_out[na + nb:].astype(BF16)
    row = lambda i: (i, 0)
    fixed = lambda i: (0, 0)
    return pl.pallas_call(
        _out_proj_kernel,
        grid=(m // tm,),
        in_specs=[
            pl.BlockSpec((tm, d), row), pl.BlockSpec((tm, na), row), pl.BlockSpec((tm, nb), row),
            pl.BlockSpec((tm, nm), row),
            pl.BlockSpec((na, d), fixed), pl.BlockSpec((nb, d), fixed), pl.BlockSpec((nm, d), fixed),
        ],
        out_specs=pl.BlockSpec((tm, d), row),
        out_shape=jax.ShapeDtypeStruct((m, d), F32),
        compiler_params=_cparams(("parallel",)),
        name="out_proj",
    )(x, ya, yb, ym, wa, wb, wm)


def _gdn_kernel(q_ref, k_ref, v_ref, gate_ref, bac_ref, bar_ref, cwq_ref, cwk_ref, cwv_ref,
                prc_ref, prr_ref, gain_ref, y_ref,
                xp_ref, u_ref, w_ref, qd_ref, kdt_ref, a_ref, egl_ref, o_ref):
    seq = q_ref.shape[0]
    c = GDN_CHUNK
    nc = seq // c
    pad = 8
    scale = GDN_DK ** -0.5

    for i, src in enumerate((q_ref, k_ref, v_ref)):
        xp_ref[i, 0:pad, :] = jnp.zeros((pad, LANES), F32)
        xp_ref[i, pad + seq:2 * pad + seq, :] = jnp.zeros((pad, LANES), F32)
        xp_ref[i, pad:pad + seq, :] = src[...]

    row = lax.broadcasted_iota(jnp.int32, (c, c), 0)
    col = lax.broadcasted_iota(jnp.int32, (c, c), 1)
    lower = row >= col
    upper = row <= col
    tri_l = lower.astype(F32)
    tri_u = upper.astype(F32)
    eye = (row == col).astype(F32)
    rxc = row ^ col

    def conv(i, cw_ref, t0):
        win = xp_ref[i, pl.ds(t0, c + 2 * pad), :]
        half = (GDN_CONV - 1) // 2
        acc = win[pad - half:pad - half + c] * cw_ref[0:1, :]
        for j in range(1, GDN_CONV):
            acc = acc + win[pad - half + j:pad - half + j + c] * cw_ref[j:j + 1, :]
        return _silu(acc)

    def l2n(x):
        return x * lax.rsqrt(jnp.sum(x * x, axis=-1, keepdims=True) + EPS)

    head = pl.program_id(1)
    lane = lax.broadcasted_iota(jnp.int32, (1, LANES), 1)

    def pick(x, idx):
        return jnp.sum(jnp.where(lane == idx, x, 0.0), axis=-1, keepdims=True)

    def prep(n, carry):
        t0 = pl.multiple_of(n * c, c)
        rows = pl.ds(t0, c)
        q = l2n(conv(0, cwq_ref, t0))
        k = l2n(conv(1, cwk_ref, t0))
        v = conv(2, cwv_ref, t0)
        qs = q * scale
        kb = k.astype(BF16)
        kk = _dot_nt(kb, kb)
        qk = _dot_nt(qs.astype(BF16), kb)

        bac = bac_ref[rows, :]
        beta_c = jax.nn.sigmoid(bac)
        g_c = -jnp.exp(prc_ref[0:1, :]) * _softplus(bac + prc_ref[1:2, :])
        pre_c = _dot_exact(tri_l, g_c)
        suf_c = _dot_exact(tri_u, g_c)
        tot_c = pre_c + suf_c - g_c
        bar = bar_ref[:, rows]
        g_r = -jnp.exp(prr_ref[:, 0:1]) * _softplus(bar + prr_ref[:, 1:2])
        pre_r = _dot_exact(g_r, tri_u)
        suf_r = _dot_exact(g_r, tri_l)

        o_ref[rows, :] = jnp.zeros((c, LANES), F32)
        for d in range(2):
            if d == 0:
                gc_all, gr, incl, strict = pre_c, pre_r[2:3, :], lower, row > col
            else:
                gc_all, gr, incl, strict = suf_c, suf_r[3:4, :], upper, row < col
            gc = pick(gc_all, 2 * GDN_HEADS + d * GDN_HEADS + head)
            beta = pick(beta_c, d * GDN_HEADS + head)
            tot = pick(tot_c, 2 * GDN_HEADS + d * GDN_HEADS + head)
            dec = jnp.where(incl, jnp.exp(jnp.minimum(gc - gr, 0.0)), 0.0)
            mm = jnp.where(strict, beta * kk * dec, 0.0)
            a_ref[d, rows, :] = (qk * dec).astype(BF16)
            mb = mm.astype(BF16)
            zb = jnp.zeros_like(mb)
            p = eye - jnp.where(rxc == 1, mm, 0.0)
            for bit in range(1, GDN_LEVELS):
                off = jnp.where((rxc >> bit) == 1, mb, zb)
                pb = p.astype(BF16)
                p = p - _dot(_dot(pb, off).astype(BF16), pb)
            eg = jnp.exp(gc)
            rhs = jnp.concatenate([v * beta, k * (beta * eg)], axis=1).astype(BF16)
            uw = _dot(p.astype(BF16), rhs)
            u_ref[d, rows, :] = uw[:, :GDN_DV].astype(BF16)
            w_ref[d, rows, :] = uw[:, GDN_DV:].astype(BF16)
            qd_ref[d, rows, :] = (qs * eg).astype(BF16)
            kd = k * jnp.exp(tot - gc)
            kdt_ref[d, :, rows] = kd.T.astype(BF16)
            egl_ref[d, pl.ds(pl.multiple_of(n * 8, 8), 8), :] = jnp.broadcast_to(jnp.exp(tot[0:8, :]), (8, LANES))
        return carry

    lax.fori_loop(0, nc, prep, 0)

    def scan(n, carry):
        states = list(carry)
        for d in range(2):
            idx = n if d == 0 else nc - 1 - n
            t0 = pl.multiple_of(idx * c, c)
            rows = pl.ds(t0, c)
            s = states[d]
            sb = s.astype(BF16)
            vnew = u_ref[d, rows, :].astype(F32) - _dot(w_ref[d, rows, :], sb)
            vb = vnew.astype(BF16)
            o = _dot(qd_ref[d, rows, :], sb) + _dot(a_ref[d, rows, :], vb)
            o_ref[rows, :] = o_ref[rows, :] + o
            egl = egl_ref[d, pl.ds(pl.multiple_of(idx * 8, 8), 8), :][0:1, :]
            states[d] = s * egl + _dot(kdt_ref[d, :, rows], vb)
        return tuple(states)

    zero = jnp.zeros((GDN_DK, GDN_DV), F32)
    lax.fori_loop(0, nc, scan, (zero, zero))

    def fin(n, carry):
        rows = pl.ds(pl.multiple_of(n * c, c), c)
        y = _rms(o_ref[rows, :], gain_ref[...]) * _silu(gate_ref[rows, :])
        y_ref[rows, :] = y.astype(y_ref.dtype)
        return carry

    lax.fori_loop(0, nc, fin, 0)


def _gdn(h, bac, bar, conv_w, prc, prr, gain, bsz, seq):
    nh = GDN_HEADS
    blk = lambda off: pl.BlockSpec((seq, LANES), lambda b, hh, off=off: (b, off // LANES + hh))
    cw = lambda off: pl.BlockSpec((GDN_CONV, LANES), lambda b, hh, off=off: (0, off // LANES + hh))
    c = GDN_CHUNK
    assert seq % c == 0
    return pl.pallas_call(
        _gdn_kernel,
        grid=(bsz, nh),
        in_specs=[
            blk(EV_GQ), blk(EV_GK), blk(EV_GV), blk(EV_GG),
            pl.BlockSpec((seq, LANES), lambda b, hh: (b, 0)),
            pl.BlockSpec((None, None, 8, seq), lambda b, hh: (b, hh, 0, 0)),
            cw(0), cw(GDN_HEADS * GDN_DK), cw(2 * GDN_HEADS * GDN_DK),
            pl.BlockSpec((2, LANES), lambda b, hh: (0, 0)),
            pl.BlockSpec((None, 8, 2), lambda b, hh: (hh, 0, 0)),
            pl.BlockSpec((1, LANES), lambda b, hh: (0, 0)),
        ],
        out_specs=pl.BlockSpec((seq, LANES), lambda b, hh: (b, hh)),
        out_shape=jax.ShapeDtypeStruct((bsz * seq, nh * GDN_DV), BF16),
        scratch_shapes=[
            pltpu.VMEM((3, seq + 16, LANES), F32),
            pltpu.VMEM((2, seq, LANES), BF16),
            pltpu.VMEM((2, seq, LANES), BF16),
            pltpu.VMEM((2, seq, LANES), BF16),
            pltpu.VMEM((2, LANES, seq), BF16),
            pltpu.VMEM((2, seq, c), BF16),
            pltpu.VMEM((2, (seq // c) * 8, LANES), F32),
            pltpu.VMEM((seq, LANES), F32),
        ],
        compiler_params=_cparams(("parallel", "parallel")),
        name="gdn",
    )(h, h, h, h, bac, bar, conv_w, conv_w, conv_w, prc, prr, gain.reshape(1, LANES))


def _online_step(s, vc, m, l, acc):
    m_new = jnp.maximum(m, jnp.max(s, axis=-1, keepdims=True))
    alpha = jnp.exp(m - m_new)
    p = jnp.exp(s - m_new)
    l = alpha * l + jnp.sum(p, axis=-1, keepdims=True)
    acc = alpha * acc + _dot(p.astype(BF16), vc)
    return m_new, l, acc


def _diff_prep_kernel(q_ref, k_ref, v_ref, qg_ref, kg_ref, qo_ref, ko_ref, vo_ref):
    lane = lax.broadcasted_iota(jnp.int32, (1, LANES), 1)
    lo = lane < DIFF_DQK

    def halfnorm(x, gain):
        x2 = x * x
        s_lo = jnp.sum(jnp.where(lo, x2, 0.0), axis=-1, keepdims=True)
        s_hi = jnp.sum(jnp.where(lo, 0.0, x2), axis=-1, keepdims=True)
        inv = jnp.where(lo, lax.rsqrt(s_lo / DIFF_DQK + EPS), lax.rsqrt(s_hi / DIFF_DQK + EPS))
        return x * inv * gain

    for hh in range(DIFF_HEADS):
        sl = slice(hh * LANES, (hh + 1) * LANES)
        qo_ref[:, sl] = (halfnorm(q_ref[:, sl], qg_ref[...]) * (DIFF_DQK ** -0.5)).astype(BF16)
        ko_ref[:, sl] = halfnorm(k_ref[:, sl], kg_ref[...]).astype(BF16)
    vo_ref[...] = v_ref[...].astype(BF16)


def _diff_prep(h, qg, kg, tm):
    m = h.shape[0]
    w = DIFF_HEADS * LANES
    spec = lambda off: pl.BlockSpec((tm, w), lambda i, off=off: (i, off // w))
    vec = pl.BlockSpec((1, LANES), lambda i: (0, 0))
    out = jax.ShapeDtypeStruct((m, w), BF16)
    return pl.pallas_call(
        _diff_prep_kernel,
        grid=(m // tm,),
        in_specs=[spec(EV_DQ), spec(EV_DK), spec(EV_DV), vec, vec],
        out_specs=[pl.BlockSpec((tm, w), lambda i: (i, 0))] * 3,
        out_shape=[out, out, out],
        compiler_params=_cparams(("parallel",)),
        name="diff_prep",
    )(h, h, h, jnp.tile(qg, 2).reshape(1, LANES), jnp.tile(kg, 2).reshape(1, LANES))


def _diff_flash_kernel(q_ref, k_ref, v_ref, gate_ref, bias_ref, lam_ref, sub_ref, y_ref, *, lambda_init):
    tq = q_ref.shape[0]
    seq = k_ref.shape[0]
    tk = bias_ref.shape[-1]
    qi = pl.program_id(2)
    q = q_ref[...]
    lane = lax.broadcasted_iota(jnp.int32, (1, LANES), 1)
    zero = jnp.zeros_like(q)
    q0 = jnp.where(lane < DIFF_DQK, q, zero)
    q1 = jnp.where(lane < DIFF_DQK, zero, q)

    def body(j, carry):
        m0, l0, a0, m1, l1, a1 = carry
        rows = pl.ds(pl.multiple_of(j * tk, tk), tk)
        kc = k_ref[rows, :]
        vc = v_ref[rows, :]
        bias = bias_ref[jnp.clip(j - qi, -2, 2) + 2]
        m0, l0, a0 = _online_step(_dot_nt(q0, kc) + bias, vc, m0, l0, a0)
        m1, l1, a1 = _online_step(_dot_nt(q1, kc) + bias, vc, m1, l1, a1)
        return m0, l0, a0, m1, l1, a1

    mi = jnp.full((tq, 1), NEG, F32)
    li = jnp.zeros((tq, 1), F32)
    ai = jnp.zeros((tq, DIFF_DV), F32)
    m0, l0, a0, m1, l1, a1 = lax.fori_loop(0, seq // tk, body, (mi, li, ai, mi, li, ai))

    lam = lam_ref[...]
    lam_full = (jnp.exp(jnp.sum(lam[0:1] * lam[1:2], axis=-1, keepdims=True))
                - jnp.exp(jnp.sum(lam[2:3] * lam[3:4], axis=-1, keepdims=True)) + lambda_init)
    o = a0 / l0 - lam_full * (a1 / l1)
    o = _rms(o, sub_ref[...]) * (1.0 - lambda_init)
    y_ref[...] = (o * _silu(gate_ref[...])).astype(y_ref.dtype)


def _diff_flash(dq, dk, dv, h, bias, lam, subln, lambda_init, bsz, seq):
    tq, tk = ATT_TQ, ATT_TK
    nq = seq // tq
    nh = DIFF_HEADS
    return pl.pallas_call(
        functools.partial(_diff_flash_kernel, lambda_init=lambda_init),
        grid=(bsz, nh, nq),
        in_specs=[
            pl.BlockSpec((tq, LANES), lambda b, hh, i: (b * nq + i, hh)),
            pl.BlockSpec((seq, LANES), lambda b, hh, i: (b, hh)),
            pl.BlockSpec((seq, LANES), lambda b, hh, i: (b, hh)),
            pl.BlockSpec((tq, LANES), lambda b, hh, i: (b * nq + i, EV_DG // LANES + hh)),
            pl.BlockSpec((None, 5, tq, tk), lambda b, hh, i: (hh, 0, 0, 0)),
            pl.BlockSpec((4, DIFF_DQK), lambda b, hh, i: (0, 0)),
            pl.BlockSpec((1, LANES), lambda b, hh, i: (0, 0)),
        ],
        out_specs=pl.BlockSpec((tq, LANES), lambda b, hh, i: (b * nq + i, hh)),
        out_shape=jax.ShapeDtypeStruct((bsz * seq, nh * DIFF_DV), BF16),
        compiler_params=_cparams(("parallel", "parallel", "arbitrary")),
        name="diff_flash",
    )(dq, dk, dv, h, bias, lam, subln.reshape(1, LANES))


def _mem_attn_kernel(q_ref, gate_ref, mk_ref, mv_ref, qg_ref, kg_ref, y_ref):
    q = (_rms(q_ref[...], qg_ref[...]) * (MEM_DH ** -0.5)).astype(BF16)
    mk = _rms(mk_ref[...], kg_ref[...]).astype(BF16)
    s = _dot_nt(q, mk)
    p = jnp.exp(s - jnp.max(s, axis=-1, keepdims=True))
    o = _dot(p.astype(BF16), mv_ref[...].astype(BF16)) / jnp.sum(p, axis=-1, keepdims=True)
    y_ref[...] = (o * _silu(gate_ref[...])).astype(y_ref.dtype)


def _mem_attn(h, memkv, qg, kg, q_off, g_off, bsz, seq, tq):
    nq = seq // tq
    nh = MEM_HEADS
    vec = pl.BlockSpec((1, LANES), lambda b, hh, i: (0, 0))
    return pl.pallas_call(
        _mem_attn_kernel,
        grid=(bsz, nh, nq),
        in_specs=[
            pl.BlockSpec((tq, LANES), lambda b, hh, i: (b * nq + i, q_off // LANES + hh)),
            pl.BlockSpec((tq, LANES), lambda b, hh, i: (b * nq + i, g_off // LANES + hh)),
            pl.BlockSpec((MEM_LEN, LANES), lambda b, hh, i: (b, hh)),
            pl.BlockSpec((MEM_LEN, LANES), lambda b, hh, i: (b, nh + hh)),
            vec, vec,
        ],
        out_specs=pl.BlockSpec((tq, LANES), lambda b, hh, i: (b * nq + i, hh)),
        out_shape=jax.ShapeDtypeStruct((bsz * seq, nh * MEM_DH), BF16),
        compiler_params=_cparams(("parallel", "parallel", "parallel")),
        name="mem_attn",
    )(h, h, memkv, memkv, qg.reshape(1, LANES), kg.reshape(1, LANES))


def _swa_kernel(sink_ref, q_ref, k_ref, v_ref, gate_ref, bias_ref, qg_ref, kg_ref, y_ref):
    tq = q_ref.shape[0]
    seq = k_ref.shape[0]
    grp = SWA_HEADS // SWA_KV_HEADS
    kvh = pl.program_id(1)
    qi = pl.program_id(2)
    ws = pl.multiple_of(jnp.clip(qi * tq - WINDOW, 0, seq - SWA_TW), WINDOW)
    kw = _rms(k_ref[pl.ds(ws, SWA_TW), :], kg_ref[...]).astype(BF16)
    vw = v_ref[pl.ds(ws, SWA_TW), :].astype(BF16)
    for g in range(grp):
        sl = slice(g * SWA_DH, (g + 1) * SWA_DH)
        q = (_rms(q_ref[:, sl], qg_ref[...]) * (SWA_DH ** -0.5)).astype(BF16)
        s = _dot_nt(q, kw) + bias_ref[g]
        sink = sink_ref[kvh * grp + g]
        mx = jnp.maximum(jnp.max(s, axis=-1, keepdims=True), sink)
        p = jnp.exp(s - mx)
        den = jnp.sum(p, axis=-1, keepdims=True) + jnp.exp(sink - mx)
        o = _dot(p.astype(BF16), vw) / den
        y_ref[:, sl] = (o * _silu(gate_ref[:, sl])).astype(y_ref.dtype)


def _swa(h, bias, sink, qg, kg, bsz, seq):
    tq = SWA_TQ
    nq = seq // tq
    grp = SWA_HEADS // SWA_KV_HEADS
    gw = grp * SWA_DH
    assert seq >= SWA_TW and nq >= 2
    vec = pl.BlockSpec((1, LANES), lambda b, kv, i: (0, 0))

    def bias_idx(b, kv, i):
        return (jnp.where(i == 0, 0, jnp.where(i == nq - 1, 2, 1)), kv, 0, 0)

    return pl.pallas_call(
        _swa_kernel,
        grid=(bsz, SWA_KV_HEADS, nq),
        in_specs=[
            pl.BlockSpec(memory_space=pltpu.SMEM),
            pl.BlockSpec((tq, gw), lambda b, kv, i: (b * nq + i, OD_SQ // gw + kv)),
            pl.BlockSpec((seq, LANES), lambda b, kv, i: (b, OD_SK // LANES + kv)),
            pl.BlockSpec((seq, LANES), lambda b, kv, i: (b, OD_SV // LANES + kv)),
            pl.BlockSpec((tq, gw), lambda b, kv, i: (b * nq + i, OD_SG // gw + kv)),
            pl.BlockSpec((None, grp, tq, SWA_TW), bias_idx),
            vec, vec,
        ],
        out_specs=pl.BlockSpec((tq, gw), lambda b, kv, i: (b * nq + i, kv)),
        out_shape=jax.ShapeDtypeStruct((bsz * seq, SWA_HEADS * SWA_DH), BF16),
        compiler_params=_cparams(("parallel", "parallel", "arbitrary")),
        name="swa",
    )(sink, h, h, h, h, bias, qg.reshape(1, LANES), kg.reshape(1, LANES))


def _mla_prep_kernel(q_ref, ckv_ref, kr_ref, cos_ref, sin_ref, kvg_ref, wup_ref,
                     qgn_ref, qgr_ref, kgn_ref, kgr_ref, qo_ref, ko_ref, vo_ref):
    dqk = MLA_NOPE + MLA_ROPE
    lane = lax.broadcasted_iota(jnp.int32, (1, LANES), 1)
    lo = lane < MLA_ROPE
    first = lane < MLA_ROPE // 2
    cos = cos_ref[...]
    sin = sin_ref[...]

    def rope(t):
        rot = jnp.where(first, pltpu.roll(t, LANES - MLA_ROPE // 2, 1), pltpu.roll(t, MLA_ROPE // 2, 1))
        return t * cos + rot * sin

    ckv = _rms(ckv_ref[...], kvg_ref[...]).astype(BF16)
    kv = _dot(ckv, wup_ref[...])
    kr = kr_ref[...]
    ss_kr = jnp.sum(kr * kr, axis=-1, keepdims=True)
    c = dqk ** -0.5
    hw = MLA_NOPE + MLA_DV
    for hh in range(MLA_HEADS):
        kn = kv[:, hh * hw:hh * hw + MLA_NOPE]
        inv = lax.rsqrt((jnp.sum(kn * kn, axis=-1, keepdims=True) + ss_kr) / dqk + EPS)
        ko_ref[:, hh * 2 * LANES:hh * 2 * LANES + LANES] = (kn * inv * kgn_ref[...]).astype(BF16)
        ko_ref[:, hh * 2 * LANES + LANES:(hh + 1) * 2 * LANES] = rope(kr * inv * kgr_ref[...]).astype(BF16)
        vo_ref[:, hh * LANES:(hh + 1) * LANES] = kv[:, hh * hw + MLA_NOPE:(hh + 1) * hw].astype(BF16)

        qn = q_ref[:, hh * LANES:(hh + 1) * LANES]
        pair = q_ref[:, OD_MLQ_ROPE + (hh // 2) * LANES:OD_MLQ_ROPE + (hh // 2 + 1) * LANES]
        if hh % 2 == 1:
            pair = pltpu.roll(pair, MLA_ROPE, 1)
        qr = jnp.where(lo, pair, 0.0)
        inv = lax.rsqrt((jnp.sum(qn * qn, axis=-1, keepdims=True)
                         + jnp.sum(qr * qr, axis=-1, keepdims=True)) / dqk + EPS)
        qo_ref[:, hh * 2 * LANES:hh * 2 * LANES + LANES] = (qn * inv * qgn_ref[...] * c).astype(BF16)
        qo_ref[:, hh * 2 * LANES + LANES:(hh + 1) * 2 * LANES] = (rope(qr * inv * qgr_ref[...]) * c).astype(BF16)


def _mla_prep(h, side, cos, sin, kv_gain, w_up, qg, kg, tm):
    m = h.shape[0]
    nh = MLA_HEADS
    qw = nh * (MLA_NOPE + MLA_ROPE)
    pad = lambda g: jnp.pad(g[MLA_NOPE:], (0, LANES - MLA_ROPE)).reshape(1, LANES)
    vec = pl.BlockSpec((1, LANES), lambda i: (0, 0))
    row = lambda i: (i, 0)
    return pl.pallas_call(
        _mla_prep_kernel,
        grid=(m // tm,),
        in_specs=[
            pl.BlockSpec((tm, qw), row),
            pl.BlockSpec((tm, MLA_KV_RANK), lambda i: (i, OD_CKV // MLA_KV_RANK)),
            pl.BlockSpec((tm, LANES), row),
            pl.BlockSpec((tm, LANES), row),
            pl.BlockSpec((tm, LANES), row),
            pl.BlockSpec((1, MLA_KV_RANK), lambda i: (0, 0)),
            pl.BlockSpec(w_up.shape, lambda i: (0, 0)),
            vec, vec, vec, vec,
        ],
        out_specs=[
            pl.BlockSpec((tm, nh * 2 * LANES), row),
            pl.BlockSpec((tm, nh * 2 * LANES), row),
            pl.BlockSpec((tm, nh * MLA_DV), row),
        ],
        out_shape=[
            jax.ShapeDtypeStruct((m, nh * 2 * LANES), BF16),
            jax.ShapeDtypeStruct((m, nh * 2 * LANES), BF16),
            jax.ShapeDtypeStruct((m, nh * MLA_DV), BF16),
        ],
        compiler_params=_cparams(("parallel",)),
        name="mla_prep",
    )(h, h, side, cos, sin, kv_gain.reshape(1, MLA_KV_RANK), w_up.astype(BF16),
      qg[:MLA_NOPE].reshape(1, LANES), pad(qg), kg[:MLA_NOPE].reshape(1, LANES), pad(kg))


def _mla_flash_kernel(q_ref, k_ref, v_ref, gate_ref, y_ref):
    tq = q_ref.shape[0]
    seq = k_ref.shape[0]
    tk = ATT_TK
    q = q_ref[...]

    def body(j, carry):
        rows = pl.ds(pl.multiple_of(j * tk, tk), tk)
        return _online_step(_dot_nt(q, k_ref[rows, :]), v_ref[rows, :], *carry)

    m, l, acc = lax.fori_loop(0, seq // tk, body, (jnp.full((tq, 1), NEG, F32), jnp.zeros((tq, 1), F32),
                                                   jnp.zeros((tq, MLA_DV), F32)))
    y_ref[...] = (acc / l * _silu(gate_ref[...])).astype(y_ref.dtype)


def _mla_flash(qm, km, vm, h, bsz, seq):
    tq = ATT_TQ
    nq = seq // tq
    nh = MLA_HEADS
    return pl.pallas_call(
        _mla_flash_kernel,
        grid=(bsz, nh, nq),
        in_specs=[
            pl.BlockSpec((tq, 2 * LANES), lambda b, hh, i: (b * nq + i, hh)),
            pl.BlockSpec((seq, 2 * LANES), lambda b, hh, i: (b, hh)),
            pl.BlockSpec((seq, LANES), lambda b, hh, i: (b, hh)),
            pl.BlockSpec((tq, LANES), lambda b, hh, i: (b * nq + i, OD_MLG // LANES + hh)),
        ],
        out_specs=pl.BlockSpec((tq, LANES), lambda b, hh, i: (b * nq + i, hh)),
        out_shape=jax.ShapeDtypeStruct((bsz * seq, nh * MLA_DV), BF16),
        compiler_params=_cparams(("parallel", "parallel", "arbitrary")),
        name="mla_flash",
    )(qm, km, vm, h)


def _t5_bucket(rel):
    half = REL_BUCKETS // 2
    max_exact = half // 2
    ret = (rel > 0).astype(jnp.int32) * half
    n = jnp.abs(rel)
    nf = jnp.maximum(n, 1).astype(F32)
    large = max_exact + (jnp.log(nf / max_exact) / math.log(REL_MAX_DIST / max_exact)
                         * (half - max_exact)).astype(jnp.int32)
    large = jnp.minimum(large, half - 1)
    return ret + jnp.where(n < max_exact, n, large)


def _diff_bias_tiles(rel_table, tq, tk):
    assert tq == tk and tk + 1 >= REL_MAX_DIST
    qi = jnp.arange(tq)[:, None]
    kj = jnp.arange(tk)[None, :]
    tiles = [rel_table[_t5_bucket(d * tk + kj - qi)] for d in (-2, -1, 0, 1, 2)]
    return jnp.stack(tiles, axis=0).transpose(3, 0, 1, 2).astype(F32)


def _swa_bias_tiles(rel_table, seq):
    qi = jnp.arange(SWA_TQ)[:, None]
    kj = jnp.arange(SWA_TW)[None, :]
    tiles = []
    for shift in (0, WINDOW, SWA_TW - SWA_TQ):
        rel = kj - shift - qi
        b = rel_table[_t5_bucket(rel)].transpose(2, 0, 1).astype(F32)
        tiles.append(jnp.where((jnp.abs(rel) <= WINDOW)[None], b, NEG))
    return jnp.stack(tiles, axis=0)


def _od_perm():
    heads = np.arange(MLA_HEADS)[:, None] * (MLA_NOPE + MLA_ROPE)
    mlq = 2560
    nope = (mlq + heads + np.arange(MLA_NOPE)[None, :]).reshape(-1)
    rope = (mlq + heads + MLA_NOPE + np.arange(MLA_ROPE)[None, :]).reshape(-1)
    rng = lambda a, b: np.arange(a, b)
    return np.concatenate([nope, rope, rng(0, 2560), rng(4096, OD_KR_LO), rng(OD_KR_HI, 6720)])


def _rope_tables(positions):
    half = MLA_ROPE // 2
    inv_freq = ROPE_BASE ** (-jnp.arange(half, dtype=F32) / half)
    ang = positions.astype(F32).reshape(-1, 1) * inv_freq
    cos, sin = jnp.cos(ang), jnp.sin(ang)
    z = jnp.zeros((ang.shape[0], LANES - MLA_ROPE), F32)
    return jnp.concatenate([cos, cos, z], axis=1), jnp.concatenate([-sin, sin, z], axis=1)


def _side_weight(w, lo, hi):
    return jnp.pad(w[:, lo:hi], ((0, 0), (0, LANES - (hi - lo)))).astype(BF16)


def _mem_kv(mem2, gain, w_kv):
    zero_side = jnp.zeros((D_MODEL, LANES), BF16)
    memkv, _ = _norm_proj(mem2, gain, w_kv.astype(BF16), zero_side, tm=mem2.shape[0] // 2, tn=512)
    return memkv


def _even_layer(x2, mem2, rel_bias, norm_g, w_in, conv_w, a_log, dt_bias, gdn_gain, dq_gain, dk_gain, lam, subln,
                mem_norm, mem_w_kv, mem_qn, mem_kn, w_out, lambda_init, bsz, seq):
    w_main = jnp.concatenate([w_in[:, :EV_SIDE_LO], w_in[:, EV_SIDE_HI:]], axis=1).astype(BF16)
    h, side = _norm_proj(x2, norm_g, w_main, _side_weight(w_in, EV_SIDE_LO, EV_SIDE_HI), tm=1024, tn=512)

    nh = GDN_HEADS
    bar = side[:, :4 * nh].reshape(bsz, seq, 4, nh).transpose(0, 3, 2, 1)
    bar = jnp.pad(bar, ((0, 0), (0, 0), (0, 4), (0, 0)))
    lane_pad = lambda t: jnp.pad(t.reshape(-1), (2 * nh, LANES - 4 * nh))
    prc = jnp.stack([lane_pad(a_log), lane_pad(dt_bias)], axis=0)
    prr = jnp.pad(jnp.stack([a_log.T, dt_bias.T], axis=-1), ((0, 0), (2, 4), (0, 0)))
    ya = _gdn(h, side, bar, conv_w, prc, prr, gdn_gain, bsz, seq)

    dq, dk, dv = _diff_prep(h, dq_gain, dk_gain, tm=512)
    bias = _diff_bias_tiles(rel_bias, ATT_TQ, ATT_TK)
    yb = _diff_flash(dq, dk, dv, h, bias, lam, subln, lambda_init, bsz, seq)

    memkv = _mem_kv(mem2, mem_norm, mem_w_kv)
    ym = _mem_attn(h, memkv, mem_qn, mem_kn, EV_MQ, EV_MG, bsz, seq, tq=1024)
    return _out_proj(x2, ya, yb, ym, w_out, tm=512)


def _odd_layer(x2, mem2, positions, rel_bias, norm_g, w_in, swa_qn, swa_kn, sink, kv_norm, w_kv_up, mla_qn, mla_kn,
               mem_norm, mem_w_kv, mem_qn, mem_kn, w_out, bsz, seq):
    w_main = w_in[:, _od_perm()].astype(BF16)
    h, side = _norm_proj(x2, norm_g, w_main, _side_weight(w_in, OD_KR_LO, OD_KR_HI), tm=1024, tn=512)

    ya = _swa(h, _swa_bias_tiles(rel_bias, seq), sink, swa_qn, swa_kn, bsz, seq)

    cos, sin = _rope_tables(positions)
    qm, km, vm = _mla_prep(h, side, cos, sin, kv_norm, w_kv_up, mla_qn, mla_kn, tm=256)
    yb = _mla_flash(qm, km, vm, h, bsz, seq)

    memkv = _mem_kv(mem2, mem_norm, mem_w_kv)
    ym = _mem_attn(h, memkv, mem_qn, mem_kn, OD_MQ, OD_MG, bsz, seq, tq=1024)
    return _out_proj(x2, ya, yb, ym, w_out, tm=512)


def kernel(x, mem, positions, rel_bias, ev_norm, ev_w_in, ev_conv, ev_a_log, ev_dt_bias, ev_gdn_norm, ev_diff_qnorm, ev_diff_knorm, ev_diff_lambda, ev_diff_subln, ev_mem_norm, ev_mem_w_kv, ev_mem_qnorm, ev_mem_knorm, ev_w_out, od_norm, od_w_in, od_swa_qnorm, od_swa_knorm, od_swa_sink, od_mla_kv_norm, od_mla_w_kv_up, od_mla_qnorm, od_mla_knorm, od_mem_norm, od_mem_w_kv, od_mem_qnorm, od_mem_knorm, od_w_out):
    bsz, seq, d = x.shape
    depth = ev_norm.shape[0] + od_norm.shape[0]
    x2 = x.reshape(bsz * seq, d)
    mem2 = mem.reshape(bsz * mem.shape[1], d)
    for layer in range(depth):
        i = layer // 2
        if layer % 2 == 0:
            lambda_init = 0.8 - 0.6 * math.exp(-0.3 * layer)
            x2 = _even_layer(x2, mem2, rel_bias, ev_norm[i], ev_w_in[i], ev_conv[i], ev_a_log[i], ev_dt_bias[i],
                             ev_gdn_norm[i], ev_diff_qnorm[i], ev_diff_knorm[i], ev_diff_lambda[i], ev_diff_subln[i],
                             ev_mem_norm[i], ev_mem_w_kv[i], ev_mem_qnorm[i], ev_mem_knorm[i], ev_w_out[i],
                             lambda_init, bsz, seq)
        else:
            x2 = _odd_layer(x2, mem2, positions, rel_bias, od_norm[i], od_w_in[i], od_swa_qnorm[i], od_swa_knorm[i],
                            od_swa_sink[i], od_mla_kv_norm[i], od_mla_w_kv_up[i], od_mla_qnorm[i], od_mla_knorm[i],
                            od_mem_norm[i], od_mem_w_kv[i], od_mem_qnorm[i], od_mem_knorm[i], od_w_out[i], bsz, seq)
    return x2.reshape(bsz, seq, d)
```

```python
import functools
import math

import numpy as np
import jax
import jax.numpy as jnp
from jax import lax
from jax.experimental import pallas as pl
from jax.experimental.pallas import tpu as pltpu

F32 = jnp.float32
BF16 = jnp.bfloat16
EPS = 1e-6
NEG = -1e30

V7X_VMEM_BYTES = 64 * 1024 * 1024
VMEM_LIMIT = V7X_VMEM_BYTES - 8 * 1024 * 1024
LANES = 128

D_MODEL = 2048
MEM_LEN = 256
GDN_HEADS, GDN_DK, GDN_DV, GDN_CONV = 8, 128, 128, 5
DIFF_HEADS, DIFF_DQK, DIFF_DV = 8, 64, 128
SWA_HEADS, SWA_KV_HEADS, SWA_DH, WINDOW = 8, 2, 128, 128
MLA_HEADS, MLA_NOPE, MLA_ROPE, MLA_DV, MLA_KV_RANK = 8, 128, 64, 128, 512
ROPE_BASE = 10000.0
MEM_HEADS, MEM_DH = 4, 128
REL_BUCKETS, REL_MAX_DIST = 32, 128

GDN_CHUNK = 256
GDN_LEVELS = GDN_CHUNK.bit_length() - 1
ATT_TQ = 512
ATT_TK = 512
SWA_TQ = 256
SWA_TW = SWA_TQ + 2 * WINDOW

EV_GQ, EV_GK, EV_GV, EV_GG = 0, 1024, 2048, 3072
EV_DQ, EV_DK, EV_DV, EV_DG = 4096, 5120, 6144, 7168
EV_MQ, EV_MG = 8192, 8704
EV_MAIN = 9216
EV_SIDE_LO, EV_SIDE_HI = 3072, 3104

OD_MLQ_NOPE, OD_MLQ_ROPE = 0, 1024
OD_SQ, OD_SK, OD_SV, OD_SG = 1536, 2560, 2816, 3072
OD_CKV, OD_MLG, OD_MQ, OD_MG = 4096, 4608, 5632, 6144
OD_MAIN = 6656
OD_KR_LO, OD_KR_HI = 4608, 4672


def _cparams(sem):
    return pltpu.CompilerParams(dimension_semantics=sem, vmem_limit_bytes=VMEM_LIMIT)


def _dot(a, b):
    return jnp.dot(a, b, preferred_element_type=F32)


def _dot_nt(a, b):
    return lax.dot_general(a, b, (((1,), (1,)), ((), ())), preferred_element_type=F32)


def _dot_exact(a, b):
    return jnp.dot(a, b, preferred_element_type=F32, precision=lax.Precision.HIGHEST)


def _silu(x):
    return x * jax.nn.sigmoid(x)


def _softplus(x):
    return jnp.maximum(x, 0.0) + jnp.log(1.0 + jnp.exp(-jnp.abs(x)))


def _rms(x, gain):
    return x * lax.rsqrt(jnp.mean(x * x, axis=-1, keepdims=True) + EPS) * gain


def _norm_proj_kernel(x_ref, g_ref, w_ref, ws_ref, o_ref, os_ref, xn_ref):
    @pl.when(pl.program_id(1) == 0)
    def _():
        xn = _rms(x_ref[...], g_ref[...]).astype(BF16)
        xn_ref[...] = xn
        os_ref[...] = _dot(xn, ws_ref[...])

    o_ref[...] = _dot(xn_ref[...], w_ref[...])


def _norm_proj(x, gain, w_main, w_side, tm, tn):
    m, k = x.shape
    n = w_main.shape[1]
    ns = w_side.shape[1]
    assert m % tm == 0 and n % tn == 0
    return pl.pallas_call(
        _norm_proj_kernel,
        grid=(m // tm, n // tn),
        in_specs=[
            pl.BlockSpec((tm, k), lambda i, j: (i, 0)),
            pl.BlockSpec((1, k), lambda i, j: (0, 0)),
            pl.BlockSpec((k, tn), lambda i, j: (0, j)),
            pl.BlockSpec((k, ns), lambda i, j: (0, 0)),
        ],
        out_specs=[
            pl.BlockSpec((tm, tn), lambda i, j: (i, j)),
            pl.BlockSpec((tm, ns), lambda i, j: (i, 0)),
        ],
        out_shape=[jax.ShapeDtypeStruct((m, n), F32), jax.ShapeDtypeStruct((m, ns), F32)],
        scratch_shapes=[pltpu.VMEM((tm, k), BF16)],
        compiler_params=_cparams(("parallel", "arbitrary")),
        name="norm_proj",
    )(x, gain.reshape(1, k), w_main, w_side)


def _out_proj_kernel(x_ref, ya_ref, yb_ref, ym_ref, wa_ref, wb_ref, wm_ref, o_ref):
    acc = _dot(ya_ref[...], wa_ref[...])
    acc = acc + _dot(yb_ref[...], wb_ref[...])
    acc = acc + _dot(ym_ref[...], wm_ref[...])
    o_ref[...] = x_ref[...] + acc


def _out_proj(x, ya, yb, ym, w_out, tm):
    m, d = x.shape
    na, nb, nm = ya.shape[1], yb.shape[1], ym.shape[1]
    wa = w_out[:na].astype(BF16)
    wb = w_out[na:na + nb].astype(BF16)
    wm = w_out[na + nb:].astype(BF16)
    row = lambda i: (i, 0)
    fixed = lambda i: (0, 0)
    return pl.pallas_call(
        _out_proj_kernel,
        grid=(m // tm,),
        in_specs=[
            pl.BlockSpec((tm, d), row), pl.BlockSpec((tm, na), row), pl.BlockSpec((tm, nb), row),
            pl.BlockSpec((tm, nm), row),
            pl.BlockSpec((na, d), fixed), pl.BlockSpec((nb, d), fixed), pl.BlockSpec((nm, d), fixed),
        ],
        out_specs=pl.BlockSpec((tm, d), row),
        out_shape=jax.ShapeDtypeStruct((m, d), F32),
        compiler_params=_cparams(("parallel",)),
        name="out_proj",
    )(x, ya, yb, ym, wa, wb, wm)


def _gdn_kernel(q_ref, k_ref, v_ref, gate_ref, bac_ref, bar_ref, cwq_ref, cwk_ref, cwv_ref,
                prc_ref, prr_ref, gain_ref, y_ref,
                xp_ref, u_ref, w_ref, qd_ref, kdt_ref, a_ref, egl_ref, o_ref):
    seq = q_ref.shape[0]
    c = GDN_CHUNK
    nc = seq // c
    pad = 8
    scale = GDN_DK ** -0.5

    for i, src in enumerate((q_ref, k_ref, v_ref)):
        xp_ref[i, 0:pad, :] = jnp.zeros((pad, LANES), F32)
        xp_ref[i, pad + seq:2 * pad + seq, :] = jnp.zeros((pad, LANES), F32)
        xp_ref[i, pad:pad + seq, :] = src[...]

    row = lax.broadcasted_iota(jnp.int32, (c, c), 0)
    col = lax.broadcasted_iota(jnp.int32, (c, c), 1)
    lower = row >= col
    upper = row <= col
    tri_l = lower.astype(F32)
    tri_u = upper.astype(F32)
    eye = (row == col).astype(F32)
    rxc = row ^ col

    def conv(i, cw_ref, t0):
        win = xp_ref[i, pl.ds(t0, c + 2 * pad), :]
        half = (GDN_CONV - 1) // 2
        acc = win[pad - half:pad - half + c] * cw_ref[0:1, :]
        for j in range(1, GDN_CONV):
            acc = acc + win[pad - half + j:pad - half + j + c] * cw_ref[j:j + 1, :]
        return _silu(acc)

    def l2n(x):
        return x * lax.rsqrt(jnp.sum(x * x, axis=-1, keepdims=True) + EPS)

    head = pl.program_id(1)
    lane = lax.broadcasted_iota(jnp.int32, (1, LANES), 1)

    def pick(x, idx):
        return jnp.sum(jnp.where(lane == idx, x, 0.0), axis=-1, keepdims=True)

    def prep(n, carry):
        t0 = pl.multiple_of(n * c, c)
        rows = pl.ds(t0, c)
        q = l2n(conv(0, cwq_ref, t0))
        k = l2n(conv(1, cwk_ref, t0))
        v = conv(2, cwv_ref, t0)
        qs = q * scale
        kb = k.astype(BF16)
        kk = _dot_nt(kb, kb)
        qk = _dot_nt(qs.astype(BF16), kb)

        bac = bac_ref[rows, :]
        beta_c = jax.nn.sigmoid(bac)
        g_c = -jnp.exp(prc_ref[0:1, :]) * _softplus(bac + prc_ref[1:2, :])
        pre_c = _dot_exact(tri_l, g_c)
        suf_c = _dot_exact(tri_u, g_c)
        tot_c = pre_c + suf_c - g_c
        bar = bar_ref[:, rows]
        g_r = -jnp.exp(prr_ref[:, 0:1]) * _softplus(bar + prr_ref[:, 1:2])
        pre_r = _dot_exact(g_r, tri_u)
        suf_r = _dot_exact(g_r, tri_l)

        o_ref[rows, :] = jnp.zeros((c, LANES), F32)
        for d in range(2):
            if d == 0:
                gc_all, gr, incl, strict = pre_c, pre_r[2:3, :], lower, row > col
            else:
                gc_all, gr, incl, strict = suf_c, suf_r[3:4, :], upper, row < col
            gc = pick(gc_all, 2 * GDN_HEADS + d * GDN_HEADS + head)
            beta = pick(beta_c, d * GDN_HEADS + head)
            tot = pick(tot_c, 2 * GDN_HEADS + d * GDN_HEADS + head)
            dec = jnp.where(incl, jnp.exp(jnp.minimum(gc - gr, 0.0)), 0.0)
            mm = jnp.where(strict, beta * kk * dec, 0.0)
            a_ref[d, rows, :] = (qk * dec).astype(BF16)
            mb = mm.astype(BF16)
            zb = jnp.zeros_like(mb)
            p = eye - jnp.where(rxc == 1, mm, 0.0)
            for bit in range(1, GDN_LEVELS):
                off = jnp.where((rxc >> bit) == 1, mb, zb)
                pb = p.astype(BF16)
                p = p - _dot(_dot(pb, off).astype(BF16), pb)
            eg = jnp.exp(gc)
            rhs = jnp.concatenate([v * beta, k * (beta * eg)], axis=1).astype(BF16)
            uw = _dot(p.astype(BF16), rhs)
            u_ref[d, rows, :] = uw[:, :GDN_DV].astype(BF16)
            w_ref[d, rows, :] = uw[:, GDN_DV:].astype(BF16)
            qd_ref[d, rows, :] = (qs * eg).astype(BF16)
            kd = k * jnp.exp(tot - gc)
            kdt_ref[d, :, rows] = kd.T.astype(BF16)
            egl_ref[d, pl.ds(pl.multiple_of(n * 8, 8), 8), :] = jnp.broadcast_to(jnp.exp(tot[0:8, :]), (8, LANES))
        return carry

    lax.fori_loop(0, nc, prep, 0)

    def scan(n, carry):
        states = list(carry)
        for d in range(2):
            idx = n if d == 0 else nc - 1 - n
            t0 = pl.multiple_of(idx * c, c)
            rows = pl.ds(t0, c)
            s = states[d]
            sb = s.astype(BF16)
            vnew = u_ref[d, rows, :].astype(F32) - _dot(w_ref[d, rows, :], sb)
            vb = vnew.astype(BF16)
            o = _dot(qd_ref[d, rows, :], sb) + _dot(a_ref[d, rows, :], vb)
            o_ref[rows, :] = o_ref[rows, :] + o
            egl = egl_ref[d, pl.ds(pl.multiple_of(idx * 8, 8), 8), :][0:1, :]
            states[d] = s * egl + _dot(kdt_ref[d, :, rows], vb)
        return tuple(states)

    zero = jnp.zeros((GDN_DK, GDN_DV), F32)
    lax.fori_loop(0, nc, scan, (zero, zero))

    def fin(n, carry):
        rows = pl.ds(pl.multiple_of(n * c, c), c)
        y = _rms(o_ref[rows, :], gain_ref[...]) * _silu(gate_ref[rows, :])
        y_ref[rows, :] = y.astype(y_ref.dtype)
        return carry

    lax.fori_loop(0, nc, fin, 0)


def _gdn(h, bac, bar, conv_w, prc, prr, gain, bsz, seq):
    nh = GDN_HEADS
    blk = lambda off: pl.BlockSpec((seq, LANES), lambda b, hh, off=off: (b, off // LANES + hh))
    cw = lambda off: pl.BlockSpec((GDN_CONV, LANES), lambda b, hh, off=off: (0, off // LANES + hh))
    c = GDN_CHUNK
    assert seq % c == 0
    return pl.pallas_call(
        _gdn_kernel,
        grid=(bsz, nh),
        in_specs=[
            blk(EV_GQ), blk(EV_GK), blk(EV_GV), blk(EV_GG),
            pl.BlockSpec((seq, LANES), lambda b, hh: (b, 0)),
            pl.BlockSpec((None, None, 8, seq), lambda b, hh: (b, hh, 0, 0)),
            cw(0), cw(GDN_HEADS * GDN_DK), cw(2 * GDN_HEADS * GDN_DK),
            pl.BlockSpec((2, LANES), lambda b, hh: (0, 0)),
            pl.BlockSpec((None, 8, 2), lambda b, hh: (hh, 0, 0)),
            pl.BlockSpec((1, LANES), lambda b, hh: (0, 0)),
        ],
        out_specs=pl.BlockSpec((seq, LANES), lambda b, hh: (b, hh)),
        out_shape=jax.ShapeDtypeStruct((bsz * seq, nh * GDN_DV), BF16),
        scratch_shapes=[
            pltpu.VMEM((3, seq + 16, LANES), F32),
            pltpu.VMEM((2, seq, LANES), BF16),
            pltpu.VMEM((2, seq, LANES), BF16),
            pltpu.VMEM((2, seq, LANES), BF16),
            pltpu.VMEM((2, LANES, seq), BF16),
            pltpu.VMEM((2, seq, c), BF16),
            pltpu.VMEM((2, (seq // c) * 8, LANES), F32),
            pltpu.VMEM((seq, LANES), F32),
        ],
        compiler_params=_cparams(("parallel", "parallel")),
        name="gdn",
    )(h, h, h, h, bac, bar, conv_w, conv_w, conv_w, prc, prr, gain.reshape(1, LANES))


def _online_step(s, vc, m, l, acc):
    m_new = jnp.maximum(m, jnp.max(s, axis=-1, keepdims=True))
    alpha = jnp.exp(m - m_new)
    p = jnp.exp(s - m_new)
    l = alpha * l + jnp.sum(p, axis=-1, keepdims=True)
    acc = alpha * acc + _dot(p.astype(BF16), vc)
    return m_new, l, acc


def _diff_prep_kernel(q_ref, k_ref, v_ref, qg_ref, kg_ref, qo_ref, ko_ref, vo_ref):
    lane = lax.broadcasted_iota(jnp.int32, (1, LANES), 1)
    lo = lane < DIFF_DQK

    def halfnorm(x, gain):
        x2 = x * x
        s_lo = jnp.sum(jnp.where(lo, x2, 0.0), axis=-1, keepdims=True)
        s_hi = jnp.sum(jnp.where(lo, 0.0, x2), axis=-1, keepdims=True)
        inv = jnp.where(lo, lax.rsqrt(s_lo / DIFF_DQK + EPS), lax.rsqrt(s_hi / DIFF_DQK + EPS))
        return x * inv * gain

    for hh in range(DIFF_HEADS):
        sl = slice(hh * LANES, (hh + 1) * LANES)
        qo_ref[:, sl] = (halfnorm(q_ref[:, sl], qg_ref[...]) * (DIFF_DQK ** -0.5)).astype(BF16)
        ko_ref[:, sl] = halfnorm(k_ref[:, sl], kg_ref[...]).astype(BF16)
    vo_ref[...] = v_ref[...].astype(BF16)


def _diff_prep(h, qg, kg, tm):
    m = h.shape[0]
    w = DIFF_HEADS * LANES
    spec = lambda off: pl.BlockSpec((tm, w), lambda i, off=off: (i, off // w))
    vec = pl.BlockSpec((1, LANES), lambda i: (0, 0))
    out = jax.ShapeDtypeStruct((m, w), BF16)
    return pl.pallas_call(
        _diff_prep_kernel,
        grid=(m // tm,),
        in_specs=[spec(EV_DQ), spec(EV_DK), spec(EV_DV), vec, vec],
        out_specs=[pl.BlockSpec((tm, w), lambda i: (i, 0))] * 3,
        out_shape=[out, out, out],
        compiler_params=_cparams(("parallel",)),
        name="diff_prep",
    )(h, h, h, jnp.tile(qg, 2).reshape(1, LANES), jnp.tile(kg, 2).reshape(1, LANES))


def _toeplitz(vec, tq, tk):
    w = vec.shape[-1]
    full = pltpu.roll(jnp.broadcast_to(vec, (tq, w)), w - (tq - 1), 1, stride=1, stride_axis=0)
    return full[:, :tk]


def _diff_flash_kernel(q_ref, k_ref, v_ref, gate_ref, bvec_ref, lam_ref, sub_ref, y_ref, bias_ref, *, lambda_init):
    tq = q_ref.shape[0]
    seq = k_ref.shape[0]
    tk = bias_ref.shape[-1]
    qi = pl.program_id(2)

    @pl.when(qi == 0)
    def _():
        for d in range(bias_ref.shape[0]):
            bias_ref[d] = _toeplitz(bvec_ref[d:d + 1, :], tq, tk)

    q = q_ref[...]
    lane = lax.broadcasted_iota(jnp.int32, (1, LANES), 1)
    zero = jnp.zeros_like(q)
    q0 = jnp.where(lane < DIFF_DQK, q, zero)
    q1 = jnp.where(lane < DIFF_DQK, zero, q)

    def body(j, carry):
        m0, l0, a0, m1, l1, a1 = carry
        rows = pl.ds(pl.multiple_of(j * tk, tk), tk)
        kc = k_ref[rows, :]
        vc = v_ref[rows, :]
        bias = bias_ref[jnp.clip(j - qi, -2, 2) + 2]
        m0, l0, a0 = _online_step(_dot_nt(q0, kc) + bias, vc, m0, l0, a0)
        m1, l1, a1 = _online_step(_dot_nt(q1, kc) + bias, vc, m1, l1, a1)
        return m0, l0, a0, m1, l1, a1

    mi = jnp.full((tq, 1), NEG, F32)
    li = jnp.zeros((tq, 1), F32)
    ai = jnp.zeros((tq, DIFF_DV), F32)
    m0, l0, a0, m1, l1, a1 = lax.fori_loop(0, seq // tk, body, (mi, li, ai, mi, li, ai))

    lam = lam_ref[...]
    lam_full = (jnp.exp(jnp.sum(lam[0:1] * lam[1:2], axis=-1, keepdims=True))
                - jnp.exp(jnp.sum(lam[2:3] * lam[3:4], axis=-1, keepdims=True)) + lambda_init)
    o = a0 / l0 - lam_full * (a1 / l1)
    o = _rms(o, sub_ref[...]) * (1.0 - lambda_init)
    y_ref[...] = (o * _silu(gate_ref[...])).astype(y_ref.dtype)


def _diff_flash(dq, dk, dv, h, bvec, lam, subln, lambda_init, bsz, seq):
    tq, tk = ATT_TQ, ATT_TK
    nq = seq // tq
    nh = DIFF_HEADS
    return pl.pallas_call(
        functools.partial(_diff_flash_kernel, lambda_init=lambda_init),
        grid=(bsz, nh, nq),
        in_specs=[
            pl.BlockSpec((tq, LANES), lambda b, hh, i: (b * nq + i, hh)),
            pl.BlockSpec((seq, LANES), lambda b, hh, i: (b, hh)),
            pl.BlockSpec((seq, LANES), lambda b, hh, i: (b, hh)),
            pl.BlockSpec((tq, LANES), lambda b, hh, i: (b * nq + i, EV_DG // LANES + hh)),
            pl.BlockSpec((None,) + bvec.shape[1:], lambda b, hh, i: (hh, 0, 0)),
            pl.BlockSpec((4, DIFF_DQK), lambda b, hh, i: (0, 0)),
            pl.BlockSpec((1, LANES), lambda b, hh, i: (0, 0)),
        ],
        out_specs=pl.BlockSpec((tq, LANES), lambda b, hh, i: (b * nq + i, hh)),
        out_shape=jax.ShapeDtypeStruct((bsz * seq, nh * DIFF_DV), BF16),
        scratch_shapes=[pltpu.VMEM((5, tq, tk), F32)],
        compiler_params=_cparams(("parallel", "parallel", "arbitrary")),
        name="diff_flash",
    )(dq, dk, dv, h, bvec, lam, subln.reshape(1, LANES))


def _mem_attn_kernel(q_ref, gate_ref, mk_ref, mv_ref, qg_ref, kg_ref, y_ref):
    q = (_rms(q_ref[...], qg_ref[...]) * (MEM_DH ** -0.5)).astype(BF16)
    mk = _rms(mk_ref[...], kg_ref[...]).astype(BF16)
    s = _dot_nt(q, mk)
    p = jnp.exp(s - jnp.max(s, axis=-1, keepdims=True))
    o = _dot(p.astype(BF16), mv_ref[...].astype(BF16)) / jnp.sum(p, axis=-1, keepdims=True)
    y_ref[...] = (o * _silu(gate_ref[...])).astype(y_ref.dtype)


def _mem_attn(h, memkv, qg, kg, q_off, g_off, bsz, seq, tq):
    nq = seq // tq
    nh = MEM_HEADS
    vec = pl.BlockSpec((1, LANES), lambda b, hh, i: (0, 0))
    return pl.pallas_call(
        _mem_attn_kernel,
        grid=(bsz, nh, nq),
        in_specs=[
            pl.BlockSpec((tq, LANES), lambda b, hh, i: (b * nq + i, q_off // LANES + hh)),
            pl.BlockSpec((tq, LANES), lambda b, hh, i: (b * nq + i, g_off // LANES + hh)),
            pl.BlockSpec((MEM_LEN, LANES), lambda b, hh, i: (b, hh)),
            pl.BlockSpec((MEM_LEN, LANES), lambda b, hh, i: (b, nh + hh)),
            vec, vec,
        ],
        out_specs=pl.BlockSpec((tq, LANES), lambda b, hh, i: (b * nq + i, hh)),
        out_shape=jax.ShapeDtypeStruct((bsz * seq, nh * MEM_DH), BF16),
        compiler_params=_cparams(("parallel", "parallel", "parallel")),
        name="mem_attn",
    )(h, h, memkv, memkv, qg.reshape(1, LANES), kg.reshape(1, LANES))


def _swa_kernel(sink_ref, q_ref, k_ref, v_ref, gate_ref, bias_ref, qg_ref, kg_ref, y_ref):
    tq = q_ref.shape[0]
    seq = k_ref.shape[0]
    grp = SWA_HEADS // SWA_KV_HEADS
    kvh = pl.program_id(1)
    qi = pl.program_id(2)
    ws = pl.multiple_of(jnp.clip(qi * tq - WINDOW, 0, seq - SWA_TW), WINDOW)
    kw = _rms(k_ref[pl.ds(ws, SWA_TW), :], kg_ref[...]).astype(BF16)
    vw = v_ref[pl.ds(ws, SWA_TW), :].astype(BF16)
    for g in range(grp):
        sl = slice(g * SWA_DH, (g + 1) * SWA_DH)
        q = (_rms(q_ref[:, sl], qg_ref[...]) * (SWA_DH ** -0.5)).astype(BF16)
        s = _dot_nt(q, kw) + _toeplitz(bias_ref[g:g + 1, :], tq, SWA_TW)
        sink = sink_ref[kvh * grp + g]
        mx = jnp.maximum(jnp.max(s, axis=-1, keepdims=True), sink)
        p = jnp.exp(s - mx)
        den = jnp.sum(p, axis=-1, keepdims=True) + jnp.exp(sink - mx)
        o = _dot(p.astype(BF16), vw) / den
        y_ref[:, sl] = (o * _silu(gate_ref[:, sl])).astype(y_ref.dtype)


def _swa(h, bias, sink, qg, kg, bsz, seq):
    tq = SWA_TQ
    nq = seq // tq
    grp = SWA_HEADS // SWA_KV_HEADS
    gw = grp * SWA_DH
    assert seq >= SWA_TW and nq >= 2
    vec = pl.BlockSpec((1, LANES), lambda b, kv, i: (0, 0))

    def bias_idx(b, kv, i):
        return (jnp.where(i == 0, 0, jnp.where(i == nq - 1, 2, 1)), kv, 0, 0)

    return pl.pallas_call(
        _swa_kernel,
        grid=(bsz, SWA_KV_HEADS, nq),
        in_specs=[
            pl.BlockSpec(memory_space=pltpu.SMEM),
            pl.BlockSpec((tq, gw), lambda b, kv, i: (b * nq + i, OD_SQ // gw + kv)),
            pl.BlockSpec((seq, LANES), lambda b, kv, i: (b, OD_SK // LANES + kv)),
            pl.BlockSpec((seq, LANES), lambda b, kv, i: (b, OD_SV // LANES + kv)),
            pl.BlockSpec((tq, gw), lambda b, kv, i: (b * nq + i, OD_SG // gw + kv)),
            pl.BlockSpec((None, None, grp, bias.shape[-1]), bias_idx),
            vec, vec,
        ],
        out_specs=pl.BlockSpec((tq, gw), lambda b, kv, i: (b * nq + i, kv)),
        out_shape=jax.ShapeDtypeStruct((bsz * seq, SWA_HEADS * SWA_DH), BF16),
        compiler_params=_cparams(("parallel", "parallel", "arbitrary")),
        name="swa",
    )(sink, h, h, h, h, bias, qg.reshape(1, LANES), kg.reshape(1, LANES))


def _mla_prep_kernel(q_ref, ckv_ref, kr_ref, cos_ref, sin_ref, kvg_ref, wup_ref,
                     qgn_ref, qgr_ref, kgn_ref, kgr_ref, qo_ref, ko_ref, vo_ref):
    dqk = MLA_NOPE + MLA_ROPE
    lane = lax.broadcasted_iota(jnp.int32, (1, LANES), 1)
    lo = lane < MLA_ROPE
    first = lane < MLA_ROPE // 2
    cos = cos_ref[...]
    sin = sin_ref[...]

    def rope(t):
        rot = jnp.where(first, pltpu.roll(t, LANES - MLA_ROPE // 2, 1), pltpu.roll(t, MLA_ROPE // 2, 1))
        return t * cos + rot * sin

    ckv = _rms(ckv_ref[...], kvg_ref[...]).astype(BF16)
    kv = _dot(ckv, wup_ref[...])
    kr = kr_ref[...]
    ss_kr = jnp.sum(kr * kr, axis=-1, keepdims=True)
    c = dqk ** -0.5
    hw = MLA_NOPE + MLA_DV
    for hh in range(MLA_HEADS):
        kn = kv[:, hh * hw:hh * hw + MLA_NOPE]
        inv = lax.rsqrt((jnp.sum(kn * kn, axis=-1, keepdims=True) + ss_kr) / dqk + EPS)
        ko_ref[:, hh * 2 * LANES:hh * 2 * LANES + LANES] = (kn * inv * kgn_ref[...]).astype(BF16)
        ko_ref[:, hh * 2 * LANES + LANES:(hh + 1) * 2 * LANES] = rope(kr * inv * kgr_ref[...]).astype(BF16)
        vo_ref[:, hh * LANES:(hh + 1) * LANES] = kv[:, hh * hw + MLA_NOPE:(hh + 1) * hw].astype(BF16)

        qn = q_ref[:, hh * LANES:(hh + 1) * LANES]
        pair = q_ref[:, OD_MLQ_ROPE + (hh // 2) * LANES:OD_MLQ_ROPE + (hh // 2 + 1) * LANES]
        if hh % 2 == 1:
            pair = pltpu.roll(pair, MLA_ROPE, 1)
        qr = jnp.where(lo, pair, 0.0)
        inv = lax.rsqrt((jnp.sum(qn * qn, axis=-1, keepdims=True)
                         + jnp.sum(qr * qr, axis=-1, keepdims=True)) / dqk + EPS)
        qo_ref[:, hh * 2 * LANES:hh * 2 * LANES + LANES] = (qn * inv * qgn_ref[...] * c).astype(BF16)
        qo_ref[:, hh * 2 * LANES + LANES:(hh + 1) * 2 * LANES] = (rope(qr * inv * qgr_ref[...]) * c).astype(BF16)


def _mla_prep(h, side, cos, sin, kv_gain, w_up, qg, kg, tm):
    m = h.shape[0]
    nh = MLA_HEADS
    qw = nh * (MLA_NOPE + MLA_ROPE)
    pad = lambda g: jnp.pad(g[MLA_NOPE:], (0, LANES - MLA_ROPE)).reshape(1, LANES)
    vec = pl.BlockSpec((1, LANES), lambda i: (0, 0))
    row = lambda i: (i, 0)
    return pl.pallas_call(
        _mla_prep_kernel,
        grid=(m // tm,),
        in_specs=[
            pl.BlockSpec((tm, qw), row),
            pl.BlockSpec((tm, MLA_KV_RANK), lambda i: (i, OD_CKV // MLA_KV_RANK)),
            pl.BlockSpec((tm, LANES), row),
            pl.BlockSpec((tm, LANES), row),
            pl.BlockSpec((tm, LANES), row),
            pl.BlockSpec((1, MLA_KV_RANK), lambda i: (0, 0)),
            pl.BlockSpec(w_up.shape, lambda i: (0, 0)),
            vec, vec, vec, vec,
        ],
        out_specs=[
            pl.BlockSpec((tm, nh * 2 * LANES), row),
            pl.BlockSpec((tm, nh * 2 * LANES), row),
            pl.BlockSpec((tm, nh * MLA_DV), row),
        ],
        out_shape=[
            jax.ShapeDtypeStruct((m, nh * 2 * LANES), BF16),
            jax.ShapeDtypeStruct((m, nh * 2 * LANES), BF16),
            jax.ShapeDtypeStruct((m, nh * MLA_DV), BF16),
        ],
        compiler_params=_cparams(("parallel",)),
        name="mla_prep",
    )(h, h, side, cos, sin, kv_gain.reshape(1, MLA_KV_RANK), w_up.astype(BF16),
      qg[:MLA_NOPE].reshape(1, LANES), pad(qg), kg[:MLA_NOPE].reshape(1, LANES), pad(kg))


def _mla_flash_kernel(q_ref, k_ref, v_ref, gate_ref, y_ref):
    tq = q_ref.shape[0]
    seq = k_ref.shape[0]
    tk = ATT_TK
    q = q_ref[...]

    def body(j, carry):
        rows = pl.ds(pl.multiple_of(j * tk, tk), tk)
        return _online_step(_dot_nt(q, k_ref[rows, :]), v_ref[rows, :], *carry)

    m, l, acc = lax.fori_loop(0, seq // tk, body, (jnp.full((tq, 1), NEG, F32), jnp.zeros((tq, 1), F32),
                                                   jnp.zeros((tq, MLA_DV), F32)))
    y_ref[...] = (acc / l * _silu(gate_ref[...])).astype(y_ref.dtype)


def _mla_flash(qm, km, vm, h, bsz, seq):
    tq = ATT_TQ
    nq = seq // tq
    nh = MLA_HEADS
    return pl.pallas_call(
        _mla_flash_kernel,
        grid=(bsz, nh, nq),
        in_specs=[
            pl.BlockSpec((tq, 2 * LANES), lambda b, hh, i: (b * nq + i, hh)),
            pl.BlockSpec((seq, 2 * LANES), lambda b, hh, i: (b, hh)),
            pl.BlockSpec((seq, LANES), lambda b, hh, i: (b, hh)),
            pl.BlockSpec((tq, LANES), lambda b, hh, i: (b * nq + i, OD_MLG // LANES + hh)),
        ],
        out_specs=pl.BlockSpec((tq, LANES), lambda b, hh, i: (b * nq + i, hh)),
        out_shape=jax.ShapeDtypeStruct((bsz * seq, nh * MLA_DV), BF16),
        compiler_params=_cparams(("parallel", "parallel", "arbitrary")),
        name="mla_flash",
    )(qm, km, vm, h)


def _t5_bucket(rel):
    half = REL_BUCKETS // 2
    max_exact = half // 2
    ret = (rel > 0).astype(jnp.int32) * half
    n = jnp.abs(rel)
    nf = jnp.maximum(n, 1).astype(F32)
    large = max_exact + (jnp.log(nf / max_exact) / math.log(REL_MAX_DIST / max_exact)
                         * (half - max_exact)).astype(jnp.int32)
    large = jnp.minimum(large, half - 1)
    return ret + jnp.where(n < max_exact, n, large)


def _round_up(n, m):
    return (n + m - 1) // m * m


def _diff_bias_vecs(rel_table, tq, tk):
    assert tq == tk and tk + 1 >= REL_MAX_DIST
    m = jnp.arange(_round_up(tq + tk - 1, LANES))
    rows = [rel_table[_t5_bucket(m - (tq - 1) + d * tk)] for d in (-2, -1, 0, 1, 2)]
    return jnp.pad(jnp.stack(rows, axis=0).transpose(2, 0, 1), ((0, 0), (0, 3), (0, 0))).astype(F32)


def _swa_bias_vecs(rel_table):
    m = jnp.arange(_round_up(SWA_TQ + SWA_TW - 1, LANES))
    rows = []
    for shift in (0, WINDOW, SWA_TW - SWA_TQ):
        rel = m - (SWA_TQ - 1) - shift
        rows.append(jnp.where((jnp.abs(rel) <= WINDOW)[None], rel_table[_t5_bucket(rel)].T, NEG))
    return jnp.stack(rows, axis=0).reshape(3, SWA_KV_HEADS, SWA_HEADS // SWA_KV_HEADS, -1).astype(F32)


def _od_perm():
    heads = np.arange(MLA_HEADS)[:, None] * (MLA_NOPE + MLA_ROPE)
    mlq = 2560
    nope = (mlq + heads + np.arange(MLA_NOPE)[None, :]).reshape(-1)
    rope = (mlq + heads + MLA_NOPE + np.arange(MLA_ROPE)[None, :]).reshape(-1)
    rng = lambda a, b: np.arange(a, b)
    return np.concatenate([nope, rope, rng(0, 2560), rng(4096, OD_KR_LO), rng(OD_KR_HI, 6720)])


def _rope_tables(positions):
    half = MLA_ROPE // 2
    inv_freq = ROPE_BASE ** (-jnp.arange(half, dtype=F32) / half)
    ang = positions.astype(F32).reshape(-1, 1) * inv_freq
    cos, sin = jnp.cos(ang), jnp.sin(ang)
    z = jnp.zeros((ang.shape[0], LANES - MLA_ROPE), F32)
    return jnp.concatenate([cos, cos, z], axis=1), jnp.concatenate([-sin, sin, z], axis=1)


def _side_weight(w, lo, hi):
    return jnp.pad(w[:, lo:hi], ((0, 0), (0, LANES - (hi - lo)))).astype(BF16)


def _mem_kv(mem2, gain, w_kv):
    zero_side = jnp.zeros((D_MODEL, LANES), BF16)
    memkv, _ = _norm_proj(mem2, gain, w_kv.astype(BF16), zero_side, tm=mem2.shape[0] // 2, tn=512)
    return memkv


def _even_layer(x2, mem2, rel_bias, norm_g, w_in, conv_w, a_log, dt_bias, gdn_gain, dq_gain, dk_gain, lam, subln,
                mem_norm, mem_w_kv, mem_qn, mem_kn, w_out, lambda_init, bsz, seq):
    w_main = jnp.concatenate([w_in[:, :EV_SIDE_LO], w_in[:, EV_SIDE_HI:]], axis=1).astype(BF16)
    h, side = _norm_proj(x2, norm_g, w_main, _side_weight(w_in, EV_SIDE_LO, EV_SIDE_HI), tm=1024, tn=512)

    nh = GDN_HEADS
    bar = side[:, :4 * nh].reshape(bsz, seq, 4, nh).transpose(0, 3, 2, 1)
    bar = jnp.pad(bar, ((0, 0), (0, 0), (0, 4), (0, 0)))
    lane_pad = lambda t: jnp.pad(t.reshape(-1), (2 * nh, LANES - 4 * nh))
    prc = jnp.stack([lane_pad(a_log), lane_pad(dt_bias)], axis=0)
    prr = jnp.pad(jnp.stack([a_log.T, dt_bias.T], axis=-1), ((0, 0), (2, 4), (0, 0)))
    ya = _gdn(h, side, bar, conv_w, prc, prr, gdn_gain, bsz, seq)

    dq, dk, dv = _diff_prep(h, dq_gain, dk_gain, tm=512)
    bias = _diff_bias_vecs(rel_bias, ATT_TQ, ATT_TK)
    yb = _diff_flash(dq, dk, dv, h, bias, lam, subln, lambda_init, bsz, seq)

    memkv = _mem_kv(mem2, mem_norm, mem_w_kv)
    ym = _mem_attn(h, memkv, mem_qn, mem_kn, EV_MQ, EV_MG, bsz, seq, tq=1024)
    return _out_proj(x2, ya, yb, ym, w_out, tm=512)


def _odd_layer(x2, mem2, positions, rel_bias, norm_g, w_in, swa_qn, swa_kn, sink, kv_norm, w_kv_up, mla_qn, mla_kn,
               mem_norm, mem_w_kv, mem_qn, mem_kn, w_out, bsz, seq):
    w_main = w_in[:, _od_perm()].astype(BF16)
    h, side = _norm_proj(x2, norm_g, w_main, _side_weight(w_in, OD_KR_LO, OD_KR_HI), tm=1024, tn=512)

    ya = _swa(h, _swa_bias_vecs(rel_bias), sink, swa_qn, swa_kn, bsz, seq)

    cos, sin = _rope_tables(positions)
    qm, km, vm = _mla_prep(h, side, cos, sin, kv_norm, w_kv_up, mla_qn, mla_kn, tm=256)
    yb = _mla_flash(qm, km, vm, h, bsz, seq)

    memkv = _mem_kv(mem2, mem_norm, mem_w_kv)
    ym = _mem_attn(h, memkv, mem_qn, mem_kn, OD_MQ, OD_MG, bsz, seq, tq=1024)
    return _out_proj(x2, ya, yb, ym, w_out, tm=512)


def kernel(x, mem, positions, rel_bias, ev_norm, ev_w_in, ev_conv, ev_a_log, ev_dt_bias, ev_gdn_norm, ev_diff_qnorm, ev_diff_knorm, ev_diff_lambda, ev_diff_subln, ev_mem_norm, ev_mem_w_kv, ev_mem_qnorm, ev_mem_knorm, ev_w_out, od_norm, od_w_in, od_swa_qnorm, od_swa_knorm, od_swa_sink, od_mla_kv_norm, od_mla_w_kv_up, od_mla_qnorm, od_mla_knorm, od_mem_norm, od_mem_w_kv, od_mem_qnorm, od_mem_knorm, od_w_out):
    bsz, seq, d = x.shape
    depth = ev_norm.shape[0] + od_norm.shape[0]
    x2 = x.reshape(bsz * seq, d)
    mem2 = mem.reshape(bsz * mem.shape[1], d)
    for layer in range(depth):
        i = layer // 2
        if layer % 2 == 0:
            lambda_init = 0.8 - 0.6 * math.exp(-0.3 * layer)
            x2 = _even_layer(x2, mem2, rel_bias, ev_norm[i], ev_w_in[i], ev_conv[i], ev_a_log[i], ev_dt_bias[i],
                             ev_gdn_norm[i], ev_diff_qnorm[i], ev_diff_knorm[i], ev_diff_lambda[i], ev_diff_subln[i],
                             ev_mem_norm[i], ev_mem_w_kv[i], ev_mem_qnorm[i], ev_mem_knorm[i], ev_w_out[i],
                             lambda_init, bsz, seq)
        else:
            x2 = _odd_layer(x2, mem2, positions, rel_bias, od_norm[i], od_w_in[i], od_swa_qnorm[i], od_swa_knorm[i],
                            od_swa_sink[i], od_mla_kv_norm[i], od_mla_w_kv_up[i], od_mla_qnorm[i], od_mla_knorm[i],
                            od_mem_norm[i], od_mem_w_kv[i], od_mem_qnorm[i], od_mem_knorm[i], od_w_out[i], bsz, seq)
    return x2.reshape(bsz, seq, d)
```

```python
import functools
import math

import numpy as np
import jax
import jax.numpy as jnp
from jax import lax
from jax.experimental import pallas as pl
from jax.experimental.pallas import tpu as pltpu

F32 = jnp.float32
BF16 = jnp.bfloat16
EPS = 1e-6
NEG = -1e30

V7X_VMEM_BYTES = 64 * 1024 * 1024
VMEM_LIMIT = V7X_VMEM_BYTES - 8 * 1024 * 1024
LANES = 128

D_MODEL = 2048
MEM_LEN = 256
GDN_HEADS, GDN_DK, GDN_DV, GDN_CONV = 8, 128, 128, 5
DIFF_HEADS, DIFF_DQK, DIFF_DV = 8, 64, 128
SWA_HEADS, SWA_KV_HEADS, SWA_DH, WINDOW = 8, 2, 128, 128
MLA_HEADS, MLA_NOPE, MLA_ROPE, MLA_DV, MLA_KV_RANK = 8, 128, 64, 128, 512
ROPE_BASE = 10000.0
MEM_HEADS, MEM_DH = 4, 128
REL_BUCKETS, REL_MAX_DIST = 32, 128

GDN_CHUNK = 256
GDN_LEVELS = GDN_CHUNK.bit_length() - 1
GDN_PREP_CHUNKS = 2
ATT_TQ = 512
ATT_TK = 512
ATT_RB = 32
LOG2E = math.log2(math.e)
SWA_TQ = 256
SWA_TW = SWA_TQ + 2 * WINDOW

EV_GQ, EV_GK, EV_GV, EV_GG = 0, 1024, 2048, 3072
EV_DQ, EV_DK, EV_DV, EV_DG = 4096, 5120, 6144, 7168
EV_MQ, EV_MG = 8192, 8704
EV_MAIN = 9216
EV_SIDE_LO, EV_SIDE_HI = 3072, 3104

OD_MLQ_NOPE, OD_MLQ_ROPE = 0, 1024
OD_SQ, OD_SK, OD_SV, OD_SG = 1536, 2560, 2816, 3072
OD_CKV, OD_MLG, OD_MQ, OD_MG = 4096, 4608, 5632, 6144
OD_MAIN = 6656
OD_KR_LO, OD_KR_HI = 4608, 4672


def _cparams(sem):
    return pltpu.CompilerParams(dimension_semantics=sem, vmem_limit_bytes=VMEM_LIMIT)


def _dot(a, b):
    return jnp.dot(a, b, preferred_element_type=F32)


def _dot_nt(a, b):
    return lax.dot_general(a, b, (((1,), (1,)), ((), ())), preferred_element_type=F32)


def _silu(x):
    return x * jax.nn.sigmoid(x)


def _softplus(x):
    return jnp.maximum(x, 0.0) + jnp.log(1.0 + jnp.exp(-jnp.abs(x)))


def _rms(x, gain):
    return x * lax.rsqrt(jnp.mean(x * x, axis=-1, keepdims=True) + EPS) * gain


def _norm_proj_kernel(x_ref, g_ref, w_ref, ws_ref, o_ref, os_ref, xn_ref):
    @pl.when(pl.program_id(1) == 0)
    def _():
        xn = _rms(x_ref[...], g_ref[...]).astype(BF16)
        xn_ref[...] = xn
        os_ref[...] = _dot(xn, ws_ref[...])

    o_ref[...] = _dot(xn_ref[...], w_ref[...])


def _norm_proj(x, gain, w_main, w_side, tm, tn):
    m, k = x.shape
    n = w_main.shape[1]
    ns = w_side.shape[1]
    assert m % tm == 0 and n % tn == 0
    return pl.pallas_call(
        _norm_proj_kernel,
        grid=(m // tm, n // tn),
        in_specs=[
            pl.BlockSpec((tm, k), lambda i, j: (i, 0)),
            pl.BlockSpec((1, k), lambda i, j: (0, 0)),
            pl.BlockSpec((k, tn), lambda i, j: (0, j)),
            pl.BlockSpec((k, ns), lambda i, j: (0, 0)),
        ],
        out_specs=[
            pl.BlockSpec((tm, tn), lambda i, j: (i, j)),
            pl.BlockSpec((tm, ns), lambda i, j: (i, 0)),
        ],
        out_shape=[jax.ShapeDtypeStruct((m, n), F32), jax.ShapeDtypeStruct((m, ns), F32)],
        scratch_shapes=[pltpu.VMEM((tm, k), BF16)],
        compiler_params=_cparams(("parallel", "arbitrary")),
        name="norm_proj",
    )(x, gain.reshape(1, k), w_main, w_side)


def _out_proj_kernel(x_ref, ya_ref, yb_ref, ym_ref, wa_ref, wb_ref, wm_ref, o_ref):
    acc = _dot(ya_ref[...], wa_ref[...])
    acc = acc + _dot(yb_ref[...], wb_ref[...])
    acc = acc + _dot(ym_ref[...], wm_ref[...])
    o_ref[...] = x_ref[...] + acc


def _out_proj(x, ya, yb, ym, w_out, tm):
    m, d = x.shape
    na, nb, nm = ya.shape[1], yb.shape[1], ym.shape[1]
    wa = w_out[:na].astype(BF16)
    wb = w_out[na:na + nb].astype(BF16)
    wm = w_out[na + nb:].astype(BF16)
    row = lambda i: (i, 0)
    fixed = lambda i: (0, 0)
    return pl.pallas_call(
        _out_proj_kernel,
        grid=(m // tm,),
        in_specs=[
            pl.BlockSpec((tm, d), row), pl.BlockSpec((tm, na), row), pl.BlockSpec((tm, nb), row),
            pl.BlockSpec((tm, nm), row),
            pl.BlockSpec((na, d), fixed), pl.BlockSpec((nb, d), fixed), pl.BlockSpec((nm, d), fixed),
        ],
        out_specs=pl.BlockSpec((tm, d), row),
        out_shape=jax.ShapeDtypeStruct((m, d), F32),
        compiler_params=_cparams(("parallel",)),
        name="out_proj",
    )(x, ya, yb, ym, wa, wb, wm)


def _gdn_kernel(q_ref, k_ref, v_ref, gate_ref, bar_ref, cwq_ref, cwk_ref, cwv_ref, prr_ref, gain_ref, y_ref,
                xp_ref, u_ref, w_ref, qd_ref, kdt_ref, a_ref, egl_ref, o_ref, lvl_ref):
    seq = q_ref.shape[0]
    c = GDN_CHUNK
    nc = seq // c
    pad = 8
    scale = GDN_DK ** -0.5

    for i, src in enumerate((q_ref, k_ref, v_ref)):
        xp_ref[i, 0:pad, :] = jnp.zeros((pad, LANES), F32)
        xp_ref[i, pad + seq:2 * pad + seq, :] = jnp.zeros((pad, LANES), F32)
        xp_ref[i, pad:pad + seq, :] = src[...]

    row = lax.broadcasted_iota(jnp.int32, (c, c), 0)
    col = lax.broadcasted_iota(jnp.int32, (c, c), 1)
    eye = (row == col).astype(BF16)
    rxc = row ^ col
    for bit in range(GDN_LEVELS):
        lvl_ref[bit] = ((rxc >> bit) == 1).astype(BF16)

    lane_c = lax.broadcasted_iota(jnp.int32, (8, c), 1)
    sub_c = lax.broadcasted_iota(jnp.int32, (8, c), 0)

    def prefix(x):
        s = 1
        while s < c:
            x = x + jnp.where(lane_c >= s, pltpu.roll(x, s, 1), 0.0)
            s *= 2
        return x

    def suffix(x):
        s = 1
        while s < c:
            x = x + jnp.where(lane_c < c - s, pltpu.roll(x, c - s, 1), 0.0)
            s *= 2
        return x

    def conv(i, cw_ref, t0):
        win = xp_ref[i, pl.ds(t0, c + 2 * pad), :]
        half = (GDN_CONV - 1) // 2
        acc = win[pad - half:pad - half + c] * cw_ref[0:1, :]
        for j in range(1, GDN_CONV):
            acc = acc + win[pad - half + j:pad - half + j + c] * cw_ref[j:j + 1, :]
        return _silu(acc)

    def l2n(x):
        return x * lax.rsqrt(jnp.sum(x * x, axis=-1, keepdims=True) + EPS)

    def chunk_chains(n):
        t0 = pl.multiple_of(n * c, c)
        rows = pl.ds(t0, c)
        q = l2n(conv(0, cwq_ref, t0))
        k = l2n(conv(1, cwk_ref, t0))
        v = conv(2, cwv_ref, t0)
        qs = q * scale
        kb = k.astype(BF16)
        kk = _dot_nt(kb, kb)
        qk = _dot_nt(qs.astype(BF16), kb)

        bar = bar_ref[:, rows]
        g_r = -jnp.exp(prr_ref[:, 0:1]) * _softplus(bar + prr_ref[:, 1:2])
        pre_r = prefix(g_r)
        suf_r = suffix(g_r)
        tot_r = jnp.sum(g_r, axis=1, keepdims=True)
        packed = jnp.where(sub_c < 2, jax.nn.sigmoid(bar), jnp.where(sub_c == 2, pre_r, suf_r))
        cols = jnp.concatenate([packed, jnp.zeros((LANES - 8, c), F32)], axis=0).T

        o_ref[rows, :] = jnp.zeros((c, LANES), F32)
        chains = []
        for d in range(2):
            if d == 0:
                gr, incl, strict = pre_r[2:3, :], row >= col, row > col
            else:
                gr, incl, strict = suf_r[3:4, :], row <= col, row < col
            beta = cols[:, d:d + 1]
            gc = cols[:, 2 + d:3 + d]
            dec = jnp.exp(jnp.where(incl, gc - gr, NEG))
            mb = jnp.where(strict, beta * kk * dec, 0.0).astype(BF16)
            a_ref[d, rows, :] = (qk * dec).astype(BF16)
            chains.append(dict(n=n, rows=rows, d=d, k=k, v=v, qs=qs, beta=beta, gc=gc, tot=tot_r[2 + d:3 + d, :],
                               mb=mb, p=eye - mb * lvl_ref[0]))
        return chains

    def finish_chain(ch):
        d, rows, k, beta, gc, tot = ch["d"], ch["rows"], ch["k"], ch["beta"], ch["gc"], ch["tot"]
        eg = jnp.exp(gc)
        rhs = jnp.concatenate([ch["v"] * beta, k * (beta * eg)], axis=1).astype(BF16)
        uw = _dot(ch["p"], rhs)
        u_ref[d, rows, :] = uw[:, :GDN_DV].astype(BF16)
        w_ref[d, rows, :] = uw[:, GDN_DV:].astype(BF16)
        qd_ref[d, rows, :] = (ch["qs"] * eg).astype(BF16)
        kd = k * jnp.exp(tot - gc)
        kdt_ref[d, :, rows] = kd.T.astype(BF16)
        egl_ref[d, pl.ds(pl.multiple_of(ch["n"] * 8, 8), 8), :] = jnp.broadcast_to(jnp.exp(tot), (8, LANES))

    def prep(n, carry):
        chains = []
        for i in range(GDN_PREP_CHUNKS):
            chains += chunk_chains(n * GDN_PREP_CHUNKS + i)
        for bit in range(1, GDN_LEVELS):
            for ch in chains:
                ch["x"] = _dot(ch["p"], ch["mb"] * lvl_ref[bit]).astype(BF16)
            for ch in chains:
                ch["p"] = ch["p"] - _dot(ch["x"], ch["p"]).astype(BF16)
        for ch in chains:
            finish_chain(ch)
        return carry

    lax.fori_loop(0, nc // GDN_PREP_CHUNKS, prep, 0)

    def scan(n, carry):
        states = list(carry)
        for d in range(2):
            idx = n if d == 0 else nc - 1 - n
            t0 = pl.multiple_of(idx * c, c)
            rows = pl.ds(t0, c)
            s = states[d]
            sb = s.astype(BF16)
            vnew = u_ref[d, rows, :].astype(F32) - _dot(w_ref[d, rows, :], sb)
            vb = vnew.astype(BF16)
            o = _dot(qd_ref[d, rows, :], sb) + _dot(a_ref[d, rows, :], vb)
            o_ref[rows, :] = o_ref[rows, :] + o
            egl = egl_ref[d, pl.ds(pl.multiple_of(idx * 8, 8), 8), :][0:1, :]
            states[d] = s * egl + _dot(kdt_ref[d, :, rows], vb)
        return tuple(states)

    zero = jnp.zeros((GDN_DK, GDN_DV), F32)
    lax.fori_loop(0, nc, scan, (zero, zero))

    def fin(n, carry):
        rows = pl.ds(pl.multiple_of(n * c, c), c)
        y = _rms(o_ref[rows, :], gain_ref[...]) * _silu(gate_ref[rows, :])
        y_ref[rows, :] = y.astype(y_ref.dtype)
        return carry

    lax.fori_loop(0, nc, fin, 0)


def _gdn(h, bar, conv_w, prr, gain, bsz, seq):
    nh = GDN_HEADS
    blk = lambda off: pl.BlockSpec((seq, LANES), lambda b, hh, off=off: (b, off // LANES + hh))
    cw = lambda off: pl.BlockSpec((GDN_CONV, LANES), lambda b, hh, off=off: (0, off // LANES + hh))
    c = GDN_CHUNK
    assert seq % (GDN_PREP_CHUNKS * c) == 0
    return pl.pallas_call(
        _gdn_kernel,
        grid=(bsz, nh),
        in_specs=[
            blk(EV_GQ), blk(EV_GK), blk(EV_GV), blk(EV_GG),
            pl.BlockSpec((None, None, 8, seq), lambda b, hh: (b, hh, 0, 0)),
            cw(0), cw(GDN_HEADS * GDN_DK), cw(2 * GDN_HEADS * GDN_DK),
            pl.BlockSpec((None, 8, 2), lambda b, hh: (hh, 0, 0)),
            pl.BlockSpec((1, LANES), lambda b, hh: (0, 0)),
        ],
        out_specs=pl.BlockSpec((seq, LANES), lambda b, hh: (b, hh)),
        out_shape=jax.ShapeDtypeStruct((bsz * seq, nh * GDN_DV), BF16),
        scratch_shapes=[
            pltpu.VMEM((3, seq + 16, LANES), F32),
            pltpu.VMEM((2, seq, LANES), BF16),
            pltpu.VMEM((2, seq, LANES), BF16),
            pltpu.VMEM((2, seq, LANES), BF16),
            pltpu.VMEM((2, LANES, seq), BF16),
            pltpu.VMEM((2, seq, c), BF16),
            pltpu.VMEM((2, (seq // c) * 8, LANES), F32),
            pltpu.VMEM((seq, LANES), F32),
            pltpu.VMEM((GDN_LEVELS, c, c), BF16),
        ],
        compiler_params=_cparams(("parallel", "parallel")),
        name="gdn",
    )(h, h, h, h, bar, conv_w, conv_w, conv_w, prr, gain.reshape(1, LANES))


def _flash_scratch(nmaps, tq, tk, dv):
    per_map = [pltpu.VMEM((tq, tk), F32), pltpu.VMEM((tq, tk), F32), pltpu.VMEM((tq, tk), BF16),
               pltpu.VMEM((tq, LANES), F32), pltpu.VMEM((tq, LANES), F32), pltpu.VMEM((tq, dv), F32)]
    return per_map * nmaps


def _flash_core(qs, k_ref, v_ref, bias_of, scratch):
    nmaps = len(qs)
    maps = [scratch[6 * i:6 * i + 6] for i in range(nmaps)]
    tq, tk = maps[0][0].shape
    nk = k_ref.shape[0] // tk
    assert nk % 2 == 0 and tk % LANES == 0 and tq % ATT_RB == 0
    nlb = tk // LANES

    for s0, s1, p, m, l, acc in maps:
        p[...] = jnp.zeros(p.shape, p.dtype)
        m[...] = jnp.full(m.shape, NEG, F32)
        l[...] = jnp.zeros(l.shape, F32)
        acc[...] = jnp.zeros(acc.shape, F32)

    def chunk(ref, j):
        return ref[pl.ds(pl.multiple_of(j * tk, tk), tk), :]

    def qk(i, j, slot):
        maps[i][slot][...] = _dot_nt(qs[i], chunk(k_ref, j))

    def pv(i, j):
        acc = maps[i][5]
        acc[...] = acc[...] + _dot(maps[i][2][...], chunk(v_ref, j))

    def softmax(i, j, slot):
        s_ref, p_ref, m_ref, l_ref, acc_ref = maps[i][slot], maps[i][2], maps[i][3], maps[i][4], maps[i][5]
        bias = bias_of(j)
        for rb in range(tq // ATT_RB):
            r = slice(rb * ATT_RB, (rb + 1) * ATT_RB)
            s = s_ref[r, :]
            if bias is not None:
                s = s + bias[r, :]
            blocks = [s[:, b * LANES:(b + 1) * LANES] for b in range(nlb)]
            mx = functools.reduce(jnp.maximum, blocks)
            m_old = m_ref[r, :]
            m_new = jnp.maximum(m_old, jnp.broadcast_to(jnp.max(mx, axis=-1, keepdims=True), m_old.shape))
            alpha = jnp.exp2(m_old - m_new)
            ps = [jnp.exp2(b - m_new) for b in blocks]
            row_sum = jnp.sum(functools.reduce(jnp.add, ps), axis=-1, keepdims=True)
            l_ref[r, :] = alpha * l_ref[r, :] + jnp.broadcast_to(row_sum, m_old.shape)
            m_ref[r, :] = m_new
            acc_ref[r, :] = acc_ref[r, :] * alpha
            p_ref[r, :] = jnp.concatenate(ps, axis=1).astype(BF16)

    for i in range(nmaps):
        qk(i, 0, 0)

    def step(j, slot):
        nxt = jnp.minimum(j + 1, nk - 1)
        for i in range(nmaps):
            if i == 0:
                pv(nmaps - 1, jnp.maximum(j - 1, 0))
            else:
                pv(i - 1, j)
            qk(i, nxt, 1 - slot)
            softmax(i, j, slot)

    def body(jj, carry):
        step(2 * jj, 0)
        step(2 * jj + 1, 1)
        return carry

    lax.fori_loop(0, nk // 2, body, 0)
    pv(nmaps - 1, nk - 1)
    return [(mp[5][...], mp[4][...]) for mp in maps]


def _diff_prep_kernel(q_ref, k_ref, v_ref, qg_ref, kg_ref, qo_ref, ko_ref, vo_ref):
    lane = lax.broadcasted_iota(jnp.int32, (1, LANES), 1)
    lo = lane < DIFF_DQK

    def halfnorm(x, gain):
        x2 = x * x
        s_lo = jnp.sum(jnp.where(lo, x2, 0.0), axis=-1, keepdims=True)
        s_hi = jnp.sum(jnp.where(lo, 0.0, x2), axis=-1, keepdims=True)
        inv = jnp.where(lo, lax.rsqrt(s_lo / DIFF_DQK + EPS), lax.rsqrt(s_hi / DIFF_DQK + EPS))
        return x * inv * gain

    for hh in range(DIFF_HEADS):
        sl = slice(hh * LANES, (hh + 1) * LANES)
        qo_ref[:, sl] = (halfnorm(q_ref[:, sl], qg_ref[...]) * (DIFF_DQK ** -0.5 * LOG2E)).astype(BF16)
        ko_ref[:, sl] = halfnorm(k_ref[:, sl], kg_ref[...]).astype(BF16)
    vo_ref[...] = v_ref[...].astype(BF16)


def _diff_prep(h, qg, kg, tm):
    m = h.shape[0]
    w = DIFF_HEADS * LANES
    spec = lambda off: pl.BlockSpec((tm, w), lambda i, off=off: (i, off // w))
    vec = pl.BlockSpec((1, LANES), lambda i: (0, 0))
    out = jax.ShapeDtypeStruct((m, w), BF16)
    return pl.pallas_call(
        _diff_prep_kernel,
        grid=(m // tm,),
        in_specs=[spec(EV_DQ), spec(EV_DK), spec(EV_DV), vec, vec],
        out_specs=[pl.BlockSpec((tm, w), lambda i: (i, 0))] * 3,
        out_shape=[out, out, out],
        compiler_params=_cparams(("parallel",)),
        name="diff_prep",
    )(h, h, h, jnp.tile(qg, 2).reshape(1, LANES), jnp.tile(kg, 2).reshape(1, LANES))


def _toeplitz(vec, tq, tk):
    w = vec.shape[-1]
    full = pltpu.roll(jnp.broadcast_to(vec, (tq, w)), w - (tq - 1), 1, stride=1, stride_axis=0)
    return full[:, :tk]


def _diff_flash_kernel(q_ref, k_ref, v_ref, gate_ref, bvec_ref, lam_ref, sub_ref, y_ref, bias_ref, *scratch,
                       lambda_init):
    tq = q_ref.shape[0]
    tk = bias_ref.shape[-1]
    qi = pl.program_id(2)

    @pl.when(qi == 0)
    def _():
        for d in range(bias_ref.shape[0]):
            bias_ref[d] = _toeplitz(bvec_ref[d:d + 1, :], tq, tk)

    q = q_ref[...]
    lane = lax.broadcasted_iota(jnp.int32, (1, LANES), 1)
    zero = jnp.zeros_like(q)
    q0 = jnp.where(lane < DIFF_DQK, q, zero)
    q1 = jnp.where(lane < DIFF_DQK, zero, q)

    def bias_of(j):
        return bias_ref.at[jnp.clip(j - qi, -2, 2) + 2]

    (a0, l0), (a1, l1) = _flash_core([q0, q1], k_ref, v_ref, bias_of, scratch)

    lam = lam_ref[...]
    lam_full = (jnp.exp(jnp.sum(lam[0:1] * lam[1:2], axis=-1, keepdims=True))
                - jnp.exp(jnp.sum(lam[2:3] * lam[3:4], axis=-1, keepdims=True)) + lambda_init)
    o = a0 / l0 - lam_full * (a1 / l1)
    o = _rms(o, sub_ref[...]) * (1.0 - lambda_init)
    y_ref[...] = (o * _silu(gate_ref[...])).astype(y_ref.dtype)


def _diff_flash(dq, dk, dv, h, bvec, lam, subln, lambda_init, bsz, seq):
    tq, tk = ATT_TQ, ATT_TK
    nq = seq // tq
    nh = DIFF_HEADS
    return pl.pallas_call(
        functools.partial(_diff_flash_kernel, lambda_init=lambda_init),
        grid=(bsz, nh, nq),
        in_specs=[
            pl.BlockSpec((tq, LANES), lambda b, hh, i: (b * nq + i, hh)),
            pl.BlockSpec((seq, LANES), lambda b, hh, i: (b, hh)),
            pl.BlockSpec((seq, LANES), lambda b, hh, i: (b, hh)),
            pl.BlockSpec((tq, LANES), lambda b, hh, i: (b * nq + i, EV_DG // LANES + hh)),
            pl.BlockSpec((None,) + bvec.shape[1:], lambda b, hh, i: (hh, 0, 0)),
            pl.BlockSpec((4, DIFF_DQK), lambda b, hh, i: (0, 0)),
            pl.BlockSpec((1, LANES), lambda b, hh, i: (0, 0)),
        ],
        out_specs=pl.BlockSpec((tq, LANES), lambda b, hh, i: (b * nq + i, hh)),
        out_shape=jax.ShapeDtypeStruct((bsz * seq, nh * DIFF_DV), BF16),
        scratch_shapes=[pltpu.VMEM((5, tq, tk), F32)] + _flash_scratch(2, tq, tk, DIFF_DV),
        compiler_params=_cparams(("parallel", "parallel", "arbitrary")),
        name="diff_flash",
    )(dq, dk, dv, h, bvec, lam, subln.reshape(1, LANES))


def _mem_attn_kernel(q_ref, gate_ref, mk_ref, mv_ref, qg_ref, kg_ref, y_ref):
    q = (_rms(q_ref[...], qg_ref[...]) * (MEM_DH ** -0.5)).astype(BF16)
    mk = _rms(mk_ref[...], kg_ref[...]).astype(BF16)
    s = _dot_nt(q, mk)
    p = jnp.exp(s - jnp.max(s, axis=-1, keepdims=True))
    o = _dot(p.astype(BF16), mv_ref[...].astype(BF16)) / jnp.sum(p, axis=-1, keepdims=True)
    y_ref[...] = (o * _silu(gate_ref[...])).astype(y_ref.dtype)


def _mem_attn(h, memkv, qg, kg, q_off, g_off, bsz, seq, tq):
    nq = seq // tq
    nh = MEM_HEADS
    vec = pl.BlockSpec((1, LANES), lambda b, hh, i: (0, 0))
    return pl.pallas_call(
        _mem_attn_kernel,
        grid=(bsz, nh, nq),
        in_specs=[
            pl.BlockSpec((tq, LANES), lambda b, hh, i: (b * nq + i, q_off // LANES + hh)),
            pl.BlockSpec((tq, LANES), lambda b, hh, i: (b * nq + i, g_off // LANES + hh)),
            pl.BlockSpec((MEM_LEN, LANES), lambda b, hh, i: (b, hh)),
            pl.BlockSpec((MEM_LEN, LANES), lambda b, hh, i: (b, nh + hh)),
            vec, vec,
        ],
        out_specs=pl.BlockSpec((tq, LANES), lambda b, hh, i: (b * nq + i, hh)),
        out_shape=jax.ShapeDtypeStruct((bsz * seq, nh * MEM_DH), BF16),
        compiler_params=_cparams(("parallel", "parallel", "parallel")),
        name="mem_attn",
    )(h, h, memkv, memkv, qg.reshape(1, LANES), kg.reshape(1, LANES))


def _swa_kernel(sink_ref, q_ref, k_ref, v_ref, gate_ref, bias_ref, qg_ref, kg_ref, y_ref):
    tq = q_ref.shape[0]
    seq = k_ref.shape[0]
    grp = SWA_HEADS // SWA_KV_HEADS
    kvh = pl.program_id(1)
    qi = pl.program_id(2)
    ws = pl.multiple_of(jnp.clip(qi * tq - WINDOW, 0, seq - SWA_TW), WINDOW)
    kw = _rms(k_ref[pl.ds(ws, SWA_TW), :], kg_ref[...]).astype(BF16)
    vw = v_ref[pl.ds(ws, SWA_TW), :].astype(BF16)
    for g in range(grp):
        sl = slice(g * SWA_DH, (g + 1) * SWA_DH)
        q = (_rms(q_ref[:, sl], qg_ref[...]) * (SWA_DH ** -0.5)).astype(BF16)
        s = _dot_nt(q, kw) + _toeplitz(bias_ref[g:g + 1, :], tq, SWA_TW)
        sink = sink_ref[kvh * grp + g]
        mx = jnp.maximum(jnp.max(s, axis=-1, keepdims=True), sink)
        p = jnp.exp(s - mx)
        den = jnp.sum(p, axis=-1, keepdims=True) + jnp.exp(sink - mx)
        o = _dot(p.astype(BF16), vw) / den
        y_ref[:, sl] = (o * _silu(gate_ref[:, sl])).astype(y_ref.dtype)


def _swa(h, bias, sink, qg, kg, bsz, seq):
    tq = SWA_TQ
    nq = seq // tq
    grp = SWA_HEADS // SWA_KV_HEADS
    gw = grp * SWA_DH
    assert seq >= SWA_TW and nq >= 2
    vec = pl.BlockSpec((1, LANES), lambda b, kv, i: (0, 0))

    def bias_idx(b, kv, i):
        return (jnp.where(i == 0, 0, jnp.where(i == nq - 1, 2, 1)), kv, 0, 0)

    return pl.pallas_call(
        _swa_kernel,
        grid=(bsz, SWA_KV_HEADS, nq),
        in_specs=[
            pl.BlockSpec(memory_space=pltpu.SMEM),
            pl.BlockSpec((tq, gw), lambda b, kv, i: (b * nq + i, OD_SQ // gw + kv)),
            pl.BlockSpec((seq, LANES), lambda b, kv, i: (b, OD_SK // LANES + kv)),
            pl.BlockSpec((seq, LANES), lambda b, kv, i: (b, OD_SV // LANES + kv)),
            pl.BlockSpec((tq, gw), lambda b, kv, i: (b * nq + i, OD_SG // gw + kv)),
            pl.BlockSpec((None, None, grp, bias.shape[-1]), bias_idx),
            vec, vec,
        ],
        out_specs=pl.BlockSpec((tq, gw), lambda b, kv, i: (b * nq + i, kv)),
        out_shape=jax.ShapeDtypeStruct((bsz * seq, SWA_HEADS * SWA_DH), BF16),
        compiler_params=_cparams(("parallel", "parallel", "arbitrary")),
        name="swa",
    )(sink, h, h, h, h, bias, qg.reshape(1, LANES), kg.reshape(1, LANES))


def _mla_prep_kernel(q_ref, ckv_ref, kr_ref, cos_ref, sin_ref, kvg_ref, wup_ref,
                     qgn_ref, qgr_ref, kgn_ref, kgr_ref, qo_ref, ko_ref, vo_ref):
    dqk = MLA_NOPE + MLA_ROPE
    lane = lax.broadcasted_iota(jnp.int32, (1, LANES), 1)
    lo = lane < MLA_ROPE
    first = lane < MLA_ROPE // 2
    cos = cos_ref[...]
    sin = sin_ref[...]

    def rope(t):
        rot = jnp.where(first, pltpu.roll(t, LANES - MLA_ROPE // 2, 1), pltpu.roll(t, MLA_ROPE // 2, 1))
        return t * cos + rot * sin

    ckv = _rms(ckv_ref[...], kvg_ref[...]).astype(BF16)
    kv = _dot(ckv, wup_ref[...])
    kr = kr_ref[...]
    ss_kr = jnp.sum(kr * kr, axis=-1, keepdims=True)
    c = dqk ** -0.5 * LOG2E
    hw = MLA_NOPE + MLA_DV
    for hh in range(MLA_HEADS):
        kn = kv[:, hh * hw:hh * hw + MLA_NOPE]
        inv = lax.rsqrt((jnp.sum(kn * kn, axis=-1, keepdims=True) + ss_kr) / dqk + EPS)
        ko_ref[:, hh * 2 * LANES:hh * 2 * LANES + LANES] = (kn * inv * kgn_ref[...]).astype(BF16)
        ko_ref[:, hh * 2 * LANES + LANES:(hh + 1) * 2 * LANES] = rope(kr * inv * kgr_ref[...]).astype(BF16)
        vo_ref[:, hh * LANES:(hh + 1) * LANES] = kv[:, hh * hw + MLA_NOPE:(hh + 1) * hw].astype(BF16)

        qn = q_ref[:, hh * LANES:(hh + 1) * LANES]
        pair = q_ref[:, OD_MLQ_ROPE + (hh // 2) * LANES:OD_MLQ_ROPE + (hh // 2 + 1) * LANES]
        if hh % 2 == 1:
            pair = pltpu.roll(pair, MLA_ROPE, 1)
        qr = jnp.where(lo, pair, 0.0)
        inv = lax.rsqrt((jnp.sum(qn * qn, axis=-1, keepdims=True)
                         + jnp.sum(qr * qr, axis=-1, keepdims=True)) / dqk + EPS)
        qo_ref[:, hh * 2 * LANES:hh * 2 * LANES + LANES] = (qn * inv * qgn_ref[...] * c).astype(BF16)
        qo_ref[:, hh * 2 * LANES + LANES:(hh + 1) * 2 * LANES] = (rope(qr * inv * qgr_ref[...]) * c).astype(BF16)


def _mla_prep(h, side, cos, sin, kv_gain, w_up, qg, kg, tm):
    m = h.shape[0]
    nh = MLA_HEADS
    qw = nh * (MLA_NOPE + MLA_ROPE)
    pad = lambda g: jnp.pad(g[MLA_NOPE:], (0, LANES - MLA_ROPE)).reshape(1, LANES)
    vec = pl.BlockSpec((1, LANES), lambda i: (0, 0))
    row = lambda i: (i, 0)
    return pl.pallas_call(
        _mla_prep_kernel,
        grid=(m // tm,),
        in_specs=[
            pl.BlockSpec((tm, qw), row),
            pl.BlockSpec((tm, MLA_KV_RANK), lambda i: (i, OD_CKV // MLA_KV_RANK)),
            pl.BlockSpec((tm, LANES), row),
            pl.BlockSpec((tm, LANES), row),
            pl.BlockSpec((tm, LANES), row),
            pl.BlockSpec((1, MLA_KV_RANK), lambda i: (0, 0)),
            pl.BlockSpec(w_up.shape, lambda i: (0, 0)),
            vec, vec, vec, vec,
        ],
        out_specs=[
            pl.BlockSpec((tm, nh * 2 * LANES), row),
            pl.BlockSpec((tm, nh * 2 * LANES), row),
            pl.BlockSpec((tm, nh * MLA_DV), row),
        ],
        out_shape=[
            jax.ShapeDtypeStruct((m, nh * 2 * LANES), BF16),
            jax.ShapeDtypeStruct((m, nh * 2 * LANES), BF16),
            jax.ShapeDtypeStruct((m, nh * MLA_DV), BF16),
        ],
        compiler_params=_cparams(("parallel",)),
        name="mla_prep",
    )(h, h, side, cos, sin, kv_gain.reshape(1, MLA_KV_RANK), w_up.astype(BF16),
      qg[:MLA_NOPE].reshape(1, LANES), pad(qg), kg[:MLA_NOPE].reshape(1, LANES), pad(kg))


def _mla_flash_kernel(q_ref, k_ref, v_ref, gate_ref, y_ref, *scratch):
    (acc, l), = _flash_core([q_ref[...]], k_ref, v_ref, lambda j: None, scratch)
    y_ref[...] = (acc / l * _silu(gate_ref[...])).astype(y_ref.dtype)


def _mla_flash(qm, km, vm, h, bsz, seq):
    tq = ATT_TQ
    nq = seq // tq
    nh = MLA_HEADS
    return pl.pallas_call(
        _mla_flash_kernel,
        grid=(bsz, nh, nq),
        in_specs=[
            pl.BlockSpec((tq, 2 * LANES), lambda b, hh, i: (b * nq + i, hh)),
            pl.BlockSpec((seq, 2 * LANES), lambda b, hh, i: (b, hh)),
            pl.BlockSpec((seq, LANES), lambda b, hh, i: (b, hh)),
            pl.BlockSpec((tq, LANES), lambda b, hh, i: (b * nq + i, OD_MLG // LANES + hh)),
        ],
        out_specs=pl.BlockSpec((tq, LANES), lambda b, hh, i: (b * nq + i, hh)),
        out_shape=jax.ShapeDtypeStruct((bsz * seq, nh * MLA_DV), BF16),
        scratch_shapes=_flash_scratch(1, tq, ATT_TK, MLA_DV),
        compiler_params=_cparams(("parallel", "parallel", "arbitrary")),
        name="mla_flash",
    )(qm, km, vm, h)


def _t5_bucket(rel):
    half = REL_BUCKETS // 2
    max_exact = half // 2
    ret = (rel > 0).astype(jnp.int32) * half
    n = jnp.abs(rel)
    nf = jnp.maximum(n, 1).astype(F32)
    large = max_exact + (jnp.log(nf / max_exact) / math.log(REL_MAX_DIST / max_exact)
                         * (half - max_exact)).astype(jnp.int32)
    large = jnp.minimum(large, half - 1)
    return ret + jnp.where(n < max_exact, n, large)


def _round_up(n, m):
    return (n + m - 1) // m * m


def _diff_bias_vecs(rel_table, tq, tk):
    assert tq == tk and tk + 1 >= REL_MAX_DIST
    m = jnp.arange(_round_up(tq + tk - 1, LANES))
    rows = [rel_table[_t5_bucket(m - (tq - 1) + d * tk)] for d in (-2, -1, 0, 1, 2)]
    return jnp.pad(jnp.stack(rows, axis=0).transpose(2, 0, 1) * LOG2E, ((0, 0), (0, 3), (0, 0))).astype(F32)


def _swa_bias_vecs(rel_table):
    m = jnp.arange(_round_up(SWA_TQ + SWA_TW - 1, LANES))
    rows = []
    for shift in (0, WINDOW, SWA_TW - SWA_TQ):
        rel = m - (SWA_TQ - 1) - shift
        rows.append(jnp.where((jnp.abs(rel) <= WINDOW)[None], rel_table[_t5_bucket(rel)].T, NEG))
    return jnp.stack(rows, axis=0).reshape(3, SWA_KV_HEADS, SWA_HEADS // SWA_KV_HEADS, -1).astype(F32)


def _od_perm():
    heads = np.arange(MLA_HEADS)[:, None] * (MLA_NOPE + MLA_ROPE)
    mlq = 2560
    nope = (mlq + heads + np.arange(MLA_NOPE)[None, :]).reshape(-1)
    rope = (mlq + heads + MLA_NOPE + np.arange(MLA_ROPE)[None, :]).reshape(-1)
    rng = lambda a, b: np.arange(a, b)
    return np.concatenate([nope, rope, rng(0, 2560), rng(4096, OD_KR_LO), rng(OD_KR_HI, 6720)])


def _rope_tables(positions):
    half = MLA_ROPE // 2
    inv_freq = ROPE_BASE ** (-jnp.arange(half, dtype=F32) / half)
    ang = positions.astype(F32).reshape(-1, 1) * inv_freq
    cos, sin = jnp.cos(ang), jnp.sin(ang)
    z = jnp.zeros((ang.shape[0], LANES - MLA_ROPE), F32)
    return jnp.concatenate([cos, cos, z], axis=1), jnp.concatenate([-sin, sin, z], axis=1)


def _side_weight(w, lo, hi):
    return jnp.pad(w[:, lo:hi], ((0, 0), (0, LANES - (hi - lo)))).astype(BF16)


def _mem_kv(mem2, gain, w_kv):
    zero_side = jnp.zeros((D_MODEL, LANES), BF16)
    memkv, _ = _norm_proj(mem2, gain, w_kv.astype(BF16), zero_side, tm=mem2.shape[0] // 2, tn=512)
    return memkv


def _even_layer(x2, mem2, rel_bias, norm_g, w_in, conv_w, a_log, dt_bias, gdn_gain, dq_gain, dk_gain, lam, subln,
                mem_norm, mem_w_kv, mem_qn, mem_kn, w_out, lambda_init, bsz, seq):
    w_main = jnp.concatenate([w_in[:, :EV_SIDE_LO], w_in[:, EV_SIDE_HI:]], axis=1).astype(BF16)
    h, side = _norm_proj(x2, norm_g, w_main, _side_weight(w_in, EV_SIDE_LO, EV_SIDE_HI), tm=1024, tn=512)

    nh = GDN_HEADS
    bar = side[:, :4 * nh].reshape(bsz, seq, 4, nh).transpose(0, 3, 2, 1)
    bar = jnp.pad(bar, ((0, 0), (0, 0), (0, 4), (0, 0)))
    prr = jnp.pad(jnp.stack([a_log.T, dt_bias.T], axis=-1), ((0, 0), (2, 4), (0, 0)))
    ya = _gdn(h, bar, conv_w, prr, gdn_gain, bsz, seq)

    dq, dk, dv = _diff_prep(h, dq_gain, dk_gain, tm=512)
    bias = _diff_bias_vecs(rel_bias, ATT_TQ, ATT_TK)
    yb = _diff_flash(dq, dk, dv, h, bias, lam, subln, lambda_init, bsz, seq)

    memkv = _mem_kv(mem2, mem_norm, mem_w_kv)
    ym = _mem_attn(h, memkv, mem_qn, mem_kn, EV_MQ, EV_MG, bsz, seq, tq=1024)
    return _out_proj(x2, ya, yb, ym, w_out, tm=512)


def _odd_layer(x2, mem2, positions, rel_bias, norm_g, w_in, swa_qn, swa_kn, sink, kv_norm, w_kv_up, mla_qn, mla_kn,
               mem_norm, mem_w_kv, mem_qn, mem_kn, w_out, bsz, seq):
    w_main = w_in[:, _od_perm()].astype(BF16)
    h, side = _norm_proj(x2, norm_g, w_main, _side_weight(w_in, OD_KR_LO, OD_KR_HI), tm=1024, tn=512)

    ya = _swa(h, _swa_bias_vecs(rel_bias), sink, swa_qn, swa_kn, bsz, seq)

    cos, sin = _rope_tables(positions)
    qm, km, vm = _mla_prep(h, side, cos, sin, kv_norm, w_kv_up, mla_qn, mla_kn, tm=256)
    yb = _mla_flash(qm, km, vm, h, bsz, seq)

    memkv = _mem_kv(mem2, mem_norm, mem_w_kv)
    ym = _mem_attn(h, memkv, mem_qn, mem_kn, OD_MQ, OD_MG, bsz, seq, tq=1024)
    return _out_proj(x2, ya, yb, ym, w_out, tm=512)


def kernel(x, mem, positions, rel_bias, ev_norm, ev_w_in, ev_conv, ev_a_log, ev_dt_bias, ev_gdn_norm, ev_diff_qnorm, ev_diff_knorm, ev_diff_lambda, ev_diff_subln, ev_mem_norm, ev_mem_w_kv, ev_mem_qnorm, ev_mem_knorm, ev_w_out, od_norm, od_w_in, od_swa_qnorm, od_swa_knorm, od_swa_sink, od_mla_kv_norm, od_mla_w_kv_up, od_mla_qnorm, od_mla_knorm, od_mem_norm, od_mem_w_kv, od_mem_qnorm, od_mem_knorm, od_w_out):
    bsz, seq, d = x.shape
    depth = ev_norm.shape[0] + od_norm.shape[0]
    x2 = x.reshape(bsz * seq, d)
    mem2 = mem.reshape(bsz * mem.shape[1], d)
    for layer in range(depth):
        i = layer // 2
        if layer % 2 == 0:
            lambda_init = 0.8 - 0.6 * math.exp(-0.3 * layer)
            x2 = _even_layer(x2, mem2, rel_bias, ev_norm[i], ev_w_in[i], ev_conv[i], ev_a_log[i], ev_dt_bias[i],
                             ev_gdn_norm[i], ev_diff_qnorm[i], ev_diff_knorm[i], ev_diff_lambda[i], ev_diff_subln[i],
                             ev_mem_norm[i], ev_mem_w_kv[i], ev_mem_qnorm[i], ev_mem_knorm[i], ev_w_out[i],
                             lambda_init, bsz, seq)
        else:
            x2 = _odd_layer(x2, mem2, positions, rel_bias, od_norm[i], od_w_in[i], od_swa_qnorm[i], od_swa_knorm[i],
                            od_swa_sink[i], od_mla_kv_norm[i], od_mla_w_kv_up[i], od_mla_qnorm[i], od_mla_knorm[i],
                            od_mem_norm[i], od_mem_w_kv[i], od_mem_qnorm[i], od_mem_knorm[i], od_w_out[i], bsz, seq)
    return x2.reshape(bsz, seq, d)
```

```python
import functools
import math

import numpy as np
import jax
import jax.numpy as jnp
from jax import lax
from jax.experimental import pallas as pl
from jax.experimental.pallas import tpu as pltpu

F32 = jnp.float32
BF16 = jnp.bfloat16
EPS = 1e-6
NEG = -1e30

V7X_VMEM_BYTES = 64 * 1024 * 1024
VMEM_LIMIT = V7X_VMEM_BYTES - 8 * 1024 * 1024
LANES = 128

D_MODEL = 2048
MEM_LEN = 256
GDN_HEADS, GDN_DK, GDN_DV, GDN_CONV = 8, 128, 128, 5
DIFF_HEADS, DIFF_DQK, DIFF_DV = 8, 64, 128
SWA_HEADS, SWA_KV_HEADS, SWA_DH, WINDOW = 8, 2, 128, 128
MLA_HEADS, MLA_NOPE, MLA_ROPE, MLA_DV, MLA_KV_RANK = 8, 128, 64, 128, 512
ROPE_BASE = 10000.0
MEM_HEADS, MEM_DH = 4, 128
REL_BUCKETS, REL_MAX_DIST = 32, 128

GDN_CHUNK = 256
GDN_LEVELS = (GDN_CHUNK // 2).bit_length() - 1
GDN_PREP_CHUNKS = 2
ATT_TQ = 512
ATT_TK = 512
ATT_RB = 32
LOG2E = math.log2(math.e)
SWA_TQ = 256
SWA_TW = SWA_TQ + 2 * WINDOW

EV_GQ, EV_GK, EV_GV, EV_GG = 0, 1024, 2048, 3072
EV_DQ, EV_DK, EV_DV, EV_DG = 4096, 5120, 6144, 7168
EV_MQ, EV_MG = 8192, 8704
EV_MAIN = 9216
EV_SIDE_LO, EV_SIDE_HI = 3072, 3104

OD_MLQ_NOPE, OD_MLQ_ROPE = 0, 1024
OD_SQ, OD_SK, OD_SV, OD_SG = 1536, 2560, 2816, 3072
OD_CKV, OD_MLG, OD_MQ, OD_MG = 4096, 4608, 5632, 6144
OD_MAIN = 6656
OD_KR_LO, OD_KR_HI = 4608, 4672


def _cparams(sem):
    return pltpu.CompilerParams(dimension_semantics=sem, vmem_limit_bytes=VMEM_LIMIT)


def _dot(a, b):
    return jnp.dot(a, b, preferred_element_type=F32)


def _dot_nt(a, b):
    return lax.dot_general(a, b, (((1,), (1,)), ((), ())), preferred_element_type=F32)


def _silu(x):
    return x * jax.nn.sigmoid(x)


def _softplus(x):
    return jnp.maximum(x, 0.0) + jnp.log(1.0 + jnp.exp(-jnp.abs(x)))


def _rms(x, gain):
    return x * lax.rsqrt(jnp.mean(x * x, axis=-1, keepdims=True) + EPS) * gain


def _norm_proj_kernel(x_ref, g_ref, w_ref, ws_ref, o_ref, os_ref, xn_ref):
    @pl.when(pl.program_id(1) == 0)
    def _():
        xn = _rms(x_ref[...], g_ref[...]).astype(BF16)
        xn_ref[...] = xn
        os_ref[...] = _dot(xn, ws_ref[...])

    o_ref[...] = _dot(xn_ref[...], w_ref[...])


def _norm_proj(x, gain, w_main, w_side, tm, tn):
    m, k = x.shape
    n = w_main.shape[1]
    ns = w_side.shape[1]
    assert m % tm == 0 and n % tn == 0
    return pl.pallas_call(
        _norm_proj_kernel,
        grid=(m // tm, n // tn),
        in_specs=[
            pl.BlockSpec((tm, k), lambda i, j: (i, 0)),
            pl.BlockSpec((1, k), lambda i, j: (0, 0)),
            pl.BlockSpec((k, tn), lambda i, j: (0, j)),
            pl.BlockSpec((k, ns), lambda i, j: (0, 0)),
        ],
        out_specs=[
            pl.BlockSpec((tm, tn), lambda i, j: (i, j)),
            pl.BlockSpec((tm, ns), lambda i, j: (i, 0)),
        ],
        out_shape=[jax.ShapeDtypeStruct((m, n), F32), jax.ShapeDtypeStruct((m, ns), F32)],
        scratch_shapes=[pltpu.VMEM((tm, k), BF16)],
        compiler_params=_cparams(("parallel", "arbitrary")),
        name="norm_proj",
    )(x, gain.reshape(1, k), w_main, w_side)


def _out_proj_kernel(x_ref, ya_ref, yb_ref, ym_ref, wa_ref, wb_ref, wm_ref, o_ref):
    acc = _dot(ya_ref[...], wa_ref[...])
    acc = acc + _dot(yb_ref[...], wb_ref[...])
    acc = acc + _dot(ym_ref[...], wm_ref[...])
    o_ref[...] = x_ref[...] + acc


def _out_proj(x, ya, yb, ym, w_out, tm):
    m, d = x.shape
    na, nb, nm = ya.shape[1], yb.shape[1], ym.shape[1]
    wa = w_out[:na].astype(BF16)
    wb = w_out[na:na + nb].astype(BF16)
    wm = w_out[na + nb:].astype(BF16)
    row = lambda i: (i, 0)
    fixed = lambda i: (0, 0)
    return pl.pallas_call(
        _out_proj_kernel,
        grid=(m // tm,),
        in_specs=[
            pl.BlockSpec((tm, d), row), pl.BlockSpec((tm, na), row), pl.BlockSpec((tm, nb), row),
            pl.BlockSpec((tm, nm), row),
            pl.BlockSpec((na, d), fixed), pl.BlockSpec((nb, d), fixed), pl.BlockSpec((nm, d), fixed),
        ],
        out_specs=pl.BlockSpec((tm, d), row),
        out_shape=jax.ShapeDtypeStruct((m, d), F32),
        compiler_params=_cparams(("parallel",)),
        name="out_proj",
    )(x, ya, yb, ym, wa, wb, wm)


def _gdn_kernel(q_ref, k_ref, v_ref, gate_ref, bar_ref, cwq_ref, cwk_ref, cwv_ref, prr_ref, gain_ref, y_ref,
                xp_ref, u_ref, w_ref, qd_ref, kdt_ref, a_ref, egl_ref, o_ref, lvl_ref, tri_ref):
    seq = q_ref.shape[0]
    c = GDN_CHUNK
    nc = seq // c
    pad = 8
    scale = GDN_DK ** -0.5

    for i, src in enumerate((q_ref, k_ref, v_ref)):
        xp_ref[i, 0:pad, :] = jnp.zeros((pad, LANES), F32)
        xp_ref[i, pad + seq:2 * pad + seq, :] = jnp.zeros((pad, LANES), F32)
        xp_ref[i, pad:pad + seq, :] = src[...]

    hc = c // 2
    row_h = lax.broadcasted_iota(jnp.int32, (hc, hc), 0)
    col_h = lax.broadcasted_iota(jnp.int32, (hc, hc), 1)
    for bit in range(GDN_LEVELS):
        lvl_ref[bit] = (((row_h ^ col_h) >> bit) == 1).astype(BF16)
    lvl_ref[GDN_LEVELS] = (row_h == col_h).astype(BF16)
    tri_ref[0] = jnp.where(row_h > col_h, 0.0, NEG)
    tri_ref[1] = jnp.where(row_h < col_h, 0.0, NEG)
    tri_ref[2] = (row_h == col_h).astype(F32)

    lane_c = lax.broadcasted_iota(jnp.int32, (8, c), 1)
    sub_c = lax.broadcasted_iota(jnp.int32, (8, c), 0)

    def prefix(x):
        s = 1
        while s < c:
            x = x + jnp.where(lane_c >= s, pltpu.roll(x, s, 1), 0.0)
            s *= 2
        return x

    def suffix(x):
        s = 1
        while s < c:
            x = x + jnp.where(lane_c < c - s, pltpu.roll(x, c - s, 1), 0.0)
            s *= 2
        return x

    def conv(i, cw_ref, t0):
        half = (GDN_CONV - 1) // 2
        acc = None
        for j in range(GDN_CONV):
            tap = xp_ref[i, pl.ds(t0 + (pad - half + j), c), :] * cw_ref[j:j + 1, :]
            acc = tap if acc is None else acc + tap
        return _silu(acc)

    ones_sq = jnp.ones((LANES, LANES), BF16)

    def l2n(x):
        x2 = x * x
        hi = x2.astype(BF16)
        lo = (x2 - hi.astype(F32)).astype(BF16)
        return x * lax.rsqrt(_dot(hi, ones_sq) + _dot(lo, ones_sq) + EPS)

    half = lambda i: slice(i * hc, (i + 1) * hc)


    def prologue(ns, chains):
        st = []
        for n in ns:
            t0 = pl.multiple_of(n * c, c)
            st.append(dict(n=n, t0=t0, rows=pl.ds(t0, c),
                           half_rows=[pl.ds(pl.multiple_of(t0 + i * hc, hc), hc) for i in range(2)]))
        for name, idx, cw_ref in (("q", 0, cwq_ref), ("k", 1, cwk_ref), ("v", 2, cwv_ref)):
            for s in st:
                s[name] = conv(idx, cw_ref, s["t0"])
            yield
        for s in st:
            s["q"], s["k"] = l2n(s["q"]), l2n(s["k"])
        yield
        for s in st:
            s["qs"] = s["q"] * scale
            s["kb"] = s["k"].astype(BF16)
            s["qk"] = _dot_nt(s["qs"].astype(BF16), s["kb"])
            bar = bar_ref[:, s["rows"]]
            g_r = (-LOG2E * jnp.exp(prr_ref[:, 0:1])) * _softplus(bar + prr_ref[:, 1:2])
            s["pre"], s["suf"] = prefix(g_r), suffix(g_r)
            s["tot"] = jnp.sum(g_r, axis=1, keepdims=True)
            packed = jnp.where(sub_c < 2, jax.nn.sigmoid(bar), jnp.where(sub_c == 2, s["pre"], s["suf"]))
            s["cols"] = jnp.concatenate([packed, jnp.zeros((LANES - 8, c), F32)], axis=0).T
            o_ref[s["rows"], :] = jnp.zeros((c, LANES), F32)
        yield
        for d in range(2):
            for s in st:
                s["kkb", d] = _dot_nt((s["k"] * s["cols"][:, d:d + 1]).astype(BF16), s["kb"])
            yield
        for d in range(2):
            first, second = (0, 1) if d == 0 else (1, 0)
            for s in st:
                gr = s["pre"][2:3, :] if d == 0 else s["suf"][3:4, :]
                gc = s["cols"][:, 2 + d:3 + d]
                blocks = {}
                for r, t in ((first, first), (second, second), (second, first)):
                    e = gc[half(r)] - gr[:, half(t)]
                    if r == t:
                        e = e + tri_ref[d]
                    dec = jnp.exp2(e)
                    blocks[r, t] = (s["kkb", d][half(r), half(t)] * dec).astype(BF16)
                    if r == t:
                        dec = dec + tri_ref[2]
                    a_ref[d, s["half_rows"][r], half(t)] = (s["qk"][half(r), half(t)] * dec).astype(BF16)
                a_ref[d, s["half_rows"][first], half(second)] = jnp.zeros((hc, hc), BF16)
                diag = [blocks[0, 0], blocks[1, 1]]
                chains.append(dict(n=s["n"], rows=s["rows"], half_rows=s["half_rows"], d=d, k=s["k"], v=s["v"],
                                   qs=s["qs"], beta=s["cols"][:, d:d + 1], gc=gc, tot=s["tot"][2 + d:3 + d, :],
                                   m=diag, off=blocks[second, first],
                                   p=[lvl_ref[GDN_LEVELS] - blk * lvl_ref[0] for blk in diag]))
            yield

    def levels(chains):
        for bit in range(1, GDN_LEVELS):
            lvl = lvl_ref[bit]
            for ch in chains:
                ch["x"] = [_dot(p, m * lvl).astype(BF16) for p, m in zip(ch["p"], ch["m"])]
            yield
            for ch in chains:
                ch["p"] = [p - _dot(x, p).astype(BF16) for p, x in zip(ch["p"], ch["x"])]
            yield

    def epilogue(chains):
        order = lambda ch: (0, 1) if ch["d"] == 0 else (1, 0)
        for ch in chains:
            ch["eg"] = jnp.exp2(ch["gc"])
            ch["rhs"] = jnp.concatenate([ch["v"] * ch["beta"], ch["k"] * (ch["beta"] * ch["eg"])], axis=1)
            ch["x1"] = _dot(ch["p"][order(ch)[0]], ch["rhs"][half(order(ch)[0])].astype(BF16))
        yield
        for ch in chains:
            ch["cross"] = _dot(ch["off"], ch["x1"].astype(BF16))
        yield
        for ch in chains:
            second = order(ch)[1]
            ch["x2"] = _dot(ch["p"][second], (ch["rhs"][half(second)] - ch["cross"]).astype(BF16))
        yield
        for ch in chains:
            d, rows = ch["d"], ch["rows"]
            for i, x in zip(order(ch), (ch["x1"], ch["x2"])):
                u_ref[d, ch["half_rows"][i], :] = x[:, :GDN_DV].astype(BF16)
                w_ref[d, ch["half_rows"][i], :] = x[:, GDN_DV:].astype(BF16)
            qd_ref[d, rows, :] = (ch["qs"] * ch["eg"]).astype(BF16)
            kd = ch["k"] * jnp.exp2(ch["tot"] - ch["gc"])
            kdt_ref[d, :, rows] = kd.T.astype(BF16)
            egl_ref[d, pl.ds(pl.multiple_of(ch["n"] * 8, 8), 8), :] = jnp.broadcast_to(jnp.exp2(ch["tot"]), (8, LANES))
        yield

    def interleave(*gens):
        gens = list(gens)
        while gens:
            for g in list(gens):
                if next(g, StopIteration) is StopIteration:
                    gens.remove(g)

    def prep(n, carry):
        base = n * 2 * GDN_PREP_CHUNKS
        group_a = [base + i for i in range(GDN_PREP_CHUNKS)]
        group_b = [base + GDN_PREP_CHUNKS + i for i in range(GDN_PREP_CHUNKS)]
        chains_a, chains_b = [], []
        interleave(prologue(group_a, chains_a))
        interleave(levels(chains_a), prologue(group_b, chains_b))
        interleave(levels(chains_b), epilogue(chains_a))
        interleave(epilogue(chains_b))
        return carry

    lax.fori_loop(0, nc // (2 * GDN_PREP_CHUNKS), prep, 0)

    def scan(n, carry):
        states = list(carry)
        for d in range(2):
            idx = n if d == 0 else nc - 1 - n
            t0 = pl.multiple_of(idx * c, c)
            rows = pl.ds(t0, c)
            s = states[d]
            sb = s.astype(BF16)
            vnew = u_ref[d, rows, :].astype(F32) - _dot(w_ref[d, rows, :], sb)
            vb = vnew.astype(BF16)
            o = _dot(qd_ref[d, rows, :], sb) + _dot(a_ref[d, rows, :], vb)
            o_ref[rows, :] = o_ref[rows, :] + o
            egl = egl_ref[d, pl.ds(pl.multiple_of(idx * 8, 8), 8), :][0:1, :]
            states[d] = s * egl + _dot(kdt_ref[d, :, rows], vb)
        return tuple(states)

    zero = jnp.zeros((GDN_DK, GDN_DV), F32)
    lax.fori_loop(0, nc, scan, (zero, zero))

    def fin(n, carry):
        rows = pl.ds(pl.multiple_of(n * c, c), c)
        y = _rms(o_ref[rows, :], gain_ref[...]) * _silu(gate_ref[rows, :])
        y_ref[rows, :] = y.astype(y_ref.dtype)
        return carry

    lax.fori_loop(0, nc, fin, 0)


def _gdn(h, bar, conv_w, prr, gain, bsz, seq):
    nh = GDN_HEADS
    blk = lambda off: pl.BlockSpec((seq, LANES), lambda b, hh, off=off: (b, off // LANES + hh))
    cw = lambda off: pl.BlockSpec((GDN_CONV, LANES), lambda b, hh, off=off: (0, off // LANES + hh))
    c = GDN_CHUNK
    assert seq % (2 * GDN_PREP_CHUNKS * c) == 0
    return pl.pallas_call(
        _gdn_kernel,
        grid=(bsz, nh),
        in_specs=[
            blk(EV_GQ), blk(EV_GK), blk(EV_GV), blk(EV_GG),
            pl.BlockSpec((None, None, 8, seq), lambda b, hh: (b, hh, 0, 0)),
            cw(0), cw(GDN_HEADS * GDN_DK), cw(2 * GDN_HEADS * GDN_DK),
            pl.BlockSpec((None, 8, 2), lambda b, hh: (hh, 0, 0)),
            pl.BlockSpec((1, LANES), lambda b, hh: (0, 0)),
        ],
        out_specs=pl.BlockSpec((seq, LANES), lambda b, hh: (b, hh)),
        out_shape=jax.ShapeDtypeStruct((bsz * seq, nh * GDN_DV), BF16),
        scratch_shapes=[
            pltpu.VMEM((3, seq + 16, LANES), F32),
            pltpu.VMEM((2, seq, LANES), BF16),
            pltpu.VMEM((2, seq, LANES), BF16),
            pltpu.VMEM((2, seq, LANES), BF16),
            pltpu.VMEM((2, LANES, seq), BF16),
            pltpu.VMEM((2, seq, c), BF16),
            pltpu.VMEM((2, (seq // c) * 8, LANES), F32),
            pltpu.VMEM((seq, LANES), F32),
            pltpu.VMEM((GDN_LEVELS + 1, c // 2, c // 2), BF16),
            pltpu.VMEM((3, c // 2, c // 2), F32),
        ],
        compiler_params=_cparams(("parallel", "parallel")),
        name="gdn",
    )(h, h, h, h, bar, conv_w, conv_w, conv_w, prr, gain.reshape(1, LANES))


def _flash_scratch(nmaps, tq, tk, dv):
    per_map = [pltpu.VMEM((tq, tk), F32), pltpu.VMEM((tq, tk), F32), pltpu.VMEM((tq, tk), BF16),
               pltpu.VMEM((tq, LANES), F32), pltpu.VMEM((tq, LANES), F32), pltpu.VMEM((tq, dv), F32)]
    return per_map * nmaps


def _flash_core(qs, k_ref, v_ref, bias_of, scratch):
    nmaps = len(qs)
    maps = [scratch[6 * i:6 * i + 6] for i in range(nmaps)]
    tq, tk = maps[0][0].shape
    nk = k_ref.shape[0] // tk
    assert nk % 2 == 0 and tk % LANES == 0 and tq % ATT_RB == 0
    nlb = tk // LANES

    for s0, s1, p, m, l, acc in maps:
        p[...] = jnp.zeros(p.shape, p.dtype)
        m[...] = jnp.full(m.shape, NEG, F32)
        l[...] = jnp.zeros(l.shape, F32)
        acc[...] = jnp.zeros(acc.shape, F32)

    def chunk(ref, j):
        return ref[pl.ds(pl.multiple_of(j * tk, tk), tk), :]

    def qk(i, j, slot):
        maps[i][slot][...] = _dot_nt(qs[i], chunk(k_ref, j))

    def pv(i, j):
        acc = maps[i][5]
        acc[...] = acc[...] + _dot(maps[i][2][...], chunk(v_ref, j))

    def softmax(i, j, slot):
        s_ref, p_ref, m_ref, l_ref, acc_ref = maps[i][slot], maps[i][2], maps[i][3], maps[i][4], maps[i][5]
        bias = bias_of(j)
        for rb in range(tq // ATT_RB):
            r = slice(rb * ATT_RB, (rb + 1) * ATT_RB)
            s = s_ref[r, :]
            if bias is not None:
                s = s + bias[r, :]
            blocks = [s[:, b * LANES:(b + 1) * LANES] for b in range(nlb)]
            mx = functools.reduce(jnp.maximum, blocks)
            m_old = m_ref[r, :]
            m_new = jnp.maximum(m_old, jnp.broadcast_to(jnp.max(mx, axis=-1, keepdims=True), m_old.shape))
            alpha = jnp.exp2(m_old - m_new)
            ps = [jnp.exp2(b - m_new) for b in blocks]
            row_sum = jnp.sum(functools.reduce(jnp.add, ps), axis=-1, keepdims=True)
            l_ref[r, :] = alpha * l_ref[r, :] + jnp.broadcast_to(row_sum, m_old.shape)
            m_ref[r, :] = m_new
            acc_ref[r, :] = acc_ref[r, :] * alpha
            p_ref[r, :] = jnp.concatenate(ps, axis=1).astype(BF16)

    for i in range(nmaps):
        qk(i, 0, 0)

    def step(j, slot):
        nxt = jnp.minimum(j + 1, nk - 1)
        for i in range(nmaps):
            if i == 0:
                pv(nmaps - 1, jnp.maximum(j - 1, 0))
            else:
                pv(i - 1, j)
            qk(i, nxt, 1 - slot)
            softmax(i, j, slot)

    def body(jj, carry):
        step(2 * jj, 0)
        step(2 * jj + 1, 1)
        return carry

    lax.fori_loop(0, nk // 2, body, 0)
    pv(nmaps - 1, nk - 1)
    return [(mp[5][...], mp[4][...]) for mp in maps]


def _diff_prep_kernel(q_ref, k_ref, v_ref, qg_ref, kg_ref, qo_ref, ko_ref, vo_ref):
    lane = lax.broadcasted_iota(jnp.int32, (1, LANES), 1)
    lo = lane < DIFF_DQK

    def halfnorm(x, gain):
        x2 = x * x
        s_lo = jnp.sum(jnp.where(lo, x2, 0.0), axis=-1, keepdims=True)
        s_hi = jnp.sum(jnp.where(lo, 0.0, x2), axis=-1, keepdims=True)
        inv = jnp.where(lo, lax.rsqrt(s_lo / DIFF_DQK + EPS), lax.rsqrt(s_hi / DIFF_DQK + EPS))
        return x * inv * gain

    for hh in range(DIFF_HEADS):
        sl = slice(hh * LANES, (hh + 1) * LANES)
        qo_ref[:, sl] = (halfnorm(q_ref[:, sl], qg_ref[...]) * (DIFF_DQK ** -0.5 * LOG2E)).astype(BF16)
        ko_ref[:, sl] = halfnorm(k_ref[:, sl], kg_ref[...]).astype(BF16)
    vo_ref[...] = v_ref[...].astype(BF16)


def _diff_prep(h, qg, kg, tm):
    m = h.shape[0]
    w = DIFF_HEADS * LANES
    spec = lambda off: pl.BlockSpec((tm, w), lambda i, off=off: (i, off // w))
    vec = pl.BlockSpec((1, LANES), lambda i: (0, 0))
    out = jax.ShapeDtypeStruct((m, w), BF16)
    return pl.pallas_call(
        _diff_prep_kernel,
        grid=(m // tm,),
        in_specs=[spec(EV_DQ), spec(EV_DK), spec(EV_DV), vec, vec],
        out_specs=[pl.BlockSpec((tm, w), lambda i: (i, 0))] * 3,
        out_shape=[out, out, out],
        compiler_params=_cparams(("parallel",)),
        name="diff_prep",
    )(h, h, h, jnp.tile(qg, 2).reshape(1, LANES), jnp.tile(kg, 2).reshape(1, LANES))


def _toeplitz(vec, tq, tk):
    w = vec.shape[-1]
    full = pltpu.roll(jnp.broadcast_to(vec, (tq, w)), w - (tq - 1), 1, stride=1, stride_axis=0)
    return full[:, :tk]


def _diff_flash_kernel(q_ref, k_ref, v_ref, gate_ref, bvec_ref, lam_ref, sub_ref, y_ref, bias_ref, *scratch,
                       lambda_init):
    tq = q_ref.shape[0]
    tk = bias_ref.shape[-1]
    qi = pl.program_id(2)

    @pl.when(qi == 0)
    def _():
        for d in range(bias_ref.shape[0]):
            bias_ref[d] = _toeplitz(bvec_ref[d:d + 1, :], tq, tk)

    q = q_ref[...]
    lane = lax.broadcasted_iota(jnp.int32, (1, LANES), 1)
    zero = jnp.zeros_like(q)
    q0 = jnp.where(lane < DIFF_DQK, q, zero)
    q1 = jnp.where(lane < DIFF_DQK, zero, q)

    def bias_of(j):
        return bias_ref.at[jnp.clip(j - qi, -2, 2) + 2]

    (a0, l0), (a1, l1) = _flash_core([q0, q1], k_ref, v_ref, bias_of, scratch)

    lam = lam_ref[...]
    lam_full = (jnp.exp(jnp.sum(lam[0:1] * lam[1:2], axis=-1, keepdims=True))
                - jnp.exp(jnp.sum(lam[2:3] * lam[3:4], axis=-1, keepdims=True)) + lambda_init)
    o = a0 / l0 - lam_full * (a1 / l1)
    o = _rms(o, sub_ref[...]) * (1.0 - lambda_init)
    y_ref[...] = (o * _silu(gate_ref[...])).astype(y_ref.dtype)


def _diff_flash(dq, dk, dv, h, bvec, lam, subln, lambda_init, bsz, seq):
    tq, tk = ATT_TQ, ATT_TK
    nq = seq // tq
    nh = DIFF_HEADS
    return pl.pallas_call(
        functools.partial(_diff_flash_kernel, lambda_init=lambda_init),
        grid=(bsz, nh, nq),
        in_specs=[
            pl.BlockSpec((tq, LANES), lambda b, hh, i: (b * nq + i, hh)),
            pl.BlockSpec((seq, LANES), lambda b, hh, i: (b, hh)),
            pl.BlockSpec((seq, LANES), lambda b, hh, i: (b, hh)),
            pl.BlockSpec((tq, LANES), lambda b, hh, i: (b * nq + i, EV_DG // LANES + hh)),
            pl.BlockSpec((None,) + bvec.shape[1:], lambda b, hh, i: (hh, 0, 0)),
            pl.BlockSpec((4, DIFF_DQK), lambda b, hh, i: (0, 0)),
            pl.BlockSpec((1, LANES), lambda b, hh, i: (0, 0)),
        ],
        out_specs=pl.BlockSpec((tq, LANES), lambda b, hh, i: (b * nq + i, hh)),
        out_shape=jax.ShapeDtypeStruct((bsz * seq, nh * DIFF_DV), BF16),
        scratch_shapes=[pltpu.VMEM((5, tq, tk), F32)] + _flash_scratch(2, tq, tk, DIFF_DV),
        compiler_params=_cparams(("parallel", "parallel", "arbitrary")),
        name="diff_flash",
    )(dq, dk, dv, h, bvec, lam, subln.reshape(1, LANES))


def _mem_attn_kernel(q_ref, gate_ref, mk_ref, mv_ref, qg_ref, kg_ref, y_ref):
    q = (_rms(q_ref[...], qg_ref[...]) * (MEM_DH ** -0.5)).astype(BF16)
    mk = _rms(mk_ref[...], kg_ref[...]).astype(BF16)
    s = _dot_nt(q, mk)
    p = jnp.exp(s - jnp.max(s, axis=-1, keepdims=True))
    o = _dot(p.astype(BF16), mv_ref[...].astype(BF16)) / jnp.sum(p, axis=-1, keepdims=True)
    y_ref[...] = (o * _silu(gate_ref[...])).astype(y_ref.dtype)


def _mem_attn(h, memkv, qg, kg, q_off, g_off, bsz, seq, tq):
    nq = seq // tq
    nh = MEM_HEADS
    vec = pl.BlockSpec((1, LANES), lambda b, hh, i: (0, 0))
    return pl.pallas_call(
        _mem_attn_kernel,
        grid=(bsz, nh, nq),
        in_specs=[
            pl.BlockSpec((tq, LANES), lambda b, hh, i: (b * nq + i, q_off // LANES + hh)),
            pl.BlockSpec((tq, LANES), lambda b, hh, i: (b * nq + i, g_off // LANES + hh)),
            pl.BlockSpec((MEM_LEN, LANES), lambda b, hh, i: (b, hh)),
            pl.BlockSpec((MEM_LEN, LANES), lambda b, hh, i: (b, nh + hh)),
            vec, vec,
        ],
        out_specs=pl.BlockSpec((tq, LANES), lambda b, hh, i: (b * nq + i, hh)),
        out_shape=jax.ShapeDtypeStruct((bsz * seq, nh * MEM_DH), BF16),
        compiler_params=_cparams(("parallel", "parallel", "parallel")),
        name="mem_attn",
    )(h, h, memkv, memkv, qg.reshape(1, LANES), kg.reshape(1, LANES))


def _swa_kernel(sink_ref, q_ref, k_ref, v_ref, gate_ref, bias_ref, qg_ref, kg_ref, y_ref, tile_ref):
    tq = q_ref.shape[0]
    seq = k_ref.shape[0]
    grp = SWA_HEADS // SWA_KV_HEADS
    kvh = pl.program_id(1)
    qi = pl.program_id(2)
    nq = pl.num_programs(2)

    @pl.when(jnp.logical_or(qi <= 1, qi == nq - 1))
    def _():
        for g in range(grp):
            tile_ref[g] = _toeplitz(bias_ref[g:g + 1, :], tq, SWA_TW)

    ws = pl.multiple_of(jnp.clip(qi * tq - WINDOW, 0, seq - SWA_TW), WINDOW)
    kw = _rms(k_ref[pl.ds(ws, SWA_TW), :], kg_ref[...]).astype(BF16)
    vw = v_ref[pl.ds(ws, SWA_TW), :].astype(BF16)
    heads = [slice(g * SWA_DH, (g + 1) * SWA_DH) for g in range(grp)]
    qs = [(_rms(q_ref[:, sl], qg_ref[...]) * (SWA_DH ** -0.5)).astype(BF16) for sl in heads]
    ss = [_dot_nt(q, kw) + tile_ref[g] for g, q in enumerate(qs)]
    ps, dens = [], []
    for g, s in enumerate(ss):
        sink = sink_ref[kvh * grp + g]
        mx = jnp.maximum(jnp.max(s, axis=-1, keepdims=True), sink)
        p = jnp.exp(s - mx)
        dens.append(jnp.sum(p, axis=-1, keepdims=True) + jnp.exp(sink - mx))
        ps.append(p.astype(BF16))
    for sl, p, den in zip(heads, ps, dens):
        y_ref[:, sl] = (_dot(p, vw) / den * _silu(gate_ref[:, sl])).astype(y_ref.dtype)


def _swa(h, bias, sink, qg, kg, bsz, seq):
    tq = SWA_TQ
    nq = seq // tq
    grp = SWA_HEADS // SWA_KV_HEADS
    gw = grp * SWA_DH
    assert seq >= SWA_TW and nq >= 2
    vec = pl.BlockSpec((1, LANES), lambda b, kv, i: (0, 0))

    def bias_idx(b, kv, i):
        return (jnp.where(i == 0, 0, jnp.where(i == nq - 1, 2, 1)), kv, 0, 0)

    return pl.pallas_call(
        _swa_kernel,
        grid=(bsz, SWA_KV_HEADS, nq),
        in_specs=[
            pl.BlockSpec(memory_space=pltpu.SMEM),
            pl.BlockSpec((tq, gw), lambda b, kv, i: (b * nq + i, OD_SQ // gw + kv)),
            pl.BlockSpec((seq, LANES), lambda b, kv, i: (b, OD_SK // LANES + kv)),
            pl.BlockSpec((seq, LANES), lambda b, kv, i: (b, OD_SV // LANES + kv)),
            pl.BlockSpec((tq, gw), lambda b, kv, i: (b * nq + i, OD_SG // gw + kv)),
            pl.BlockSpec((None, None, grp, bias.shape[-1]), bias_idx),
            vec, vec,
        ],
        out_specs=pl.BlockSpec((tq, gw), lambda b, kv, i: (b * nq + i, kv)),
        out_shape=jax.ShapeDtypeStruct((bsz * seq, SWA_HEADS * SWA_DH), BF16),
        scratch_shapes=[pltpu.VMEM((grp, tq, SWA_TW), F32)],
        compiler_params=_cparams(("parallel", "parallel", "arbitrary")),
        name="swa",
    )(sink, h, h, h, h, bias, qg.reshape(1, LANES), kg.reshape(1, LANES))


def _mla_prep_kernel(q_ref, ckv_ref, kr_ref, cos_ref, sin_ref, kvg_ref, wup_ref,
                     qgn_ref, qgr_ref, kgn_ref, kgr_ref, qo_ref, ko_ref, vo_ref):
    dqk = MLA_NOPE + MLA_ROPE
    lane = lax.broadcasted_iota(jnp.int32, (1, LANES), 1)
    lo = lane < MLA_ROPE
    first = lane < MLA_ROPE // 2
    cos = cos_ref[...]
    sin = sin_ref[...]

    def rope(t):
        rot = jnp.where(first, pltpu.roll(t, LANES - MLA_ROPE // 2, 1), pltpu.roll(t, MLA_ROPE // 2, 1))
        return t * cos + rot * sin

    ckv = _rms(ckv_ref[...], kvg_ref[...]).astype(BF16)
    kv = _dot(ckv, wup_ref[...])
    kr = kr_ref[...]
    ss_kr = jnp.sum(kr * kr, axis=-1, keepdims=True)
    c = dqk ** -0.5 * LOG2E
    hw = MLA_NOPE + MLA_DV
    for hh in range(MLA_HEADS):
        kn = kv[:, hh * hw:hh * hw + MLA_NOPE]
        inv = lax.rsqrt((jnp.sum(kn * kn, axis=-1, keepdims=True) + ss_kr) / dqk + EPS)
        ko_ref[:, hh * 2 * LANES:hh * 2 * LANES + LANES] = (kn * inv * kgn_ref[...]).astype(BF16)
        ko_ref[:, hh * 2 * LANES + LANES:(hh + 1) * 2 * LANES] = rope(kr * inv * kgr_ref[...]).astype(BF16)
        vo_ref[:, hh * LANES:(hh + 1) * LANES] = kv[:, hh * hw + MLA_NOPE:(hh + 1) * hw].astype(BF16)

        qn = q_ref[:, hh * LANES:(hh + 1) * LANES]
        pair = q_ref[:, OD_MLQ_ROPE + (hh // 2) * LANES:OD_MLQ_ROPE + (hh // 2 + 1) * LANES]
        if hh % 2 == 1:
            pair = pltpu.roll(pair, MLA_ROPE, 1)
        qr = jnp.where(lo, pair, 0.0)
        inv = lax.rsqrt((jnp.sum(qn * qn, axis=-1, keepdims=True)
                         + jnp.sum(qr * qr, axis=-1, keepdims=True)) / dqk + EPS)
        qo_ref[:, hh * 2 * LANES:hh * 2 * LANES + LANES] = (qn * inv * qgn_ref[...] * c).astype(BF16)
        qo_ref[:, hh * 2 * LANES + LANES:(hh + 1) * 2 * LANES] = (rope(qr * inv * qgr_ref[...]) * c).astype(BF16)


def _mla_prep(h, side, cos, sin, kv_gain, w_up, qg, kg, tm):
    m = h.shape[0]
    nh = MLA_HEADS
    qw = nh * (MLA_NOPE + MLA_ROPE)
    pad = lambda g: jnp.pad(g[MLA_NOPE:], (0, LANES - MLA_ROPE)).reshape(1, LANES)
    vec = pl.BlockSpec((1, LANES), lambda i: (0, 0))
    row = lambda i: (i, 0)
    return pl.pallas_call(
        _mla_prep_kernel,
        grid=(m // tm,),
        in_specs=[
            pl.BlockSpec((tm, qw), row),
            pl.BlockSpec((tm, MLA_KV_RANK), lambda i: (i, OD_CKV // MLA_KV_RANK)),
            pl.BlockSpec((tm, LANES), row),
            pl.BlockSpec((tm, LANES), row),
            pl.BlockSpec((tm, LANES), row),
            pl.BlockSpec((1, MLA_KV_RANK), lambda i: (0, 0)),
            pl.BlockSpec(w_up.shape, lambda i: (0, 0)),
            vec, vec, vec, vec,
        ],
        out_specs=[
            pl.BlockSpec((tm, nh * 2 * LANES), row),
            pl.BlockSpec((tm, nh * 2 * LANES), row),
            pl.BlockSpec((tm, nh * MLA_DV), row),
        ],
        out_shape=[
            jax.ShapeDtypeStruct((m, nh * 2 * LANES), BF16),
            jax.ShapeDtypeStruct((m, nh * 2 * LANES), BF16),
            jax.ShapeDtypeStruct((m, nh * MLA_DV), BF16),
        ],
        compiler_params=_cparams(("parallel",)),
        name="mla_prep",
    )(h, h, side, cos, sin, kv_gain.reshape(1, MLA_KV_RANK), w_up.astype(BF16),
      qg[:MLA_NOPE].reshape(1, LANES), pad(qg), kg[:MLA_NOPE].reshape(1, LANES), pad(kg))


def _mla_flash_kernel(q_ref, k_ref, v_ref, gate_ref, y_ref, *scratch):
    (acc, l), = _flash_core([q_ref[...]], k_ref, v_ref, lambda j: None, scratch)
    y_ref[...] = (acc / l * _silu(gate_ref[...])).astype(y_ref.dtype)


def _mla_flash(qm, km, vm, h, bsz, seq):
    tq = ATT_TQ
    nq = seq // tq
    nh = MLA_HEADS
    return pl.pallas_call(
        _mla_flash_kernel,
        grid=(bsz, nh, nq),
        in_specs=[
            pl.BlockSpec((tq, 2 * LANES), lambda b, hh, i: (b * nq + i, hh)),
            pl.BlockSpec((seq, 2 * LANES), lambda b, hh, i: (b, hh)),
            pl.BlockSpec((seq, LANES), lambda b, hh, i: (b, hh)),
            pl.BlockSpec((tq, LANES), lambda b, hh, i: (b * nq + i, OD_MLG // LANES + hh)),
        ],
        out_specs=pl.BlockSpec((tq, LANES), lambda b, hh, i: (b * nq + i, hh)),
        out_shape=jax.ShapeDtypeStruct((bsz * seq, nh * MLA_DV), BF16),
        scratch_shapes=_flash_scratch(1, tq, ATT_TK, MLA_DV),
        compiler_params=_cparams(("parallel", "parallel", "arbitrary")),
        name="mla_flash",
    )(qm, km, vm, h)


def _t5_bucket(rel):
    half = REL_BUCKETS // 2
    max_exact = half // 2
    ret = (rel > 0).astype(jnp.int32) * half
    n = jnp.abs(rel)
    nf = jnp.maximum(n, 1).astype(F32)
    large = max_exact + (jnp.log(nf / max_exact) / math.log(REL_MAX_DIST / max_exact)
                         * (half - max_exact)).astype(jnp.int32)
    large = jnp.minimum(large, half - 1)
    return ret + jnp.where(n < max_exact, n, large)


def _round_up(n, m):
    return (n + m - 1) // m * m


def _diff_bias_vecs(rel_table, tq, tk):
    assert tq == tk and tk + 1 >= REL_MAX_DIST
    m = jnp.arange(_round_up(tq + tk - 1, LANES))
    rows = [rel_table[_t5_bucket(m - (tq - 1) + d * tk)] for d in (-2, -1, 0, 1, 2)]
    return jnp.pad(jnp.stack(rows, axis=0).transpose(2, 0, 1) * LOG2E, ((0, 0), (0, 3), (0, 0))).astype(F32)


def _swa_bias_vecs(rel_table):
    m = jnp.arange(_round_up(SWA_TQ + SWA_TW - 1, LANES))
    rows = []
    for shift in (0, WINDOW, SWA_TW - SWA_TQ):
        rel = m - (SWA_TQ - 1) - shift
        rows.append(jnp.where((jnp.abs(rel) <= WINDOW)[None], rel_table[_t5_bucket(rel)].T, NEG))
    return jnp.stack(rows, axis=0).reshape(3, SWA_KV_HEADS, SWA_HEADS // SWA_KV_HEADS, -1).astype(F32)


def _od_perm():
    heads = np.arange(MLA_HEADS)[:, None] * (MLA_NOPE + MLA_ROPE)
    mlq = 2560
    nope = (mlq + heads + np.arange(MLA_NOPE)[None, :]).reshape(-1)
    rope = (mlq + heads + MLA_NOPE + np.arange(MLA_ROPE)[None, :]).reshape(-1)
    rng = lambda a, b: np.arange(a, b)
    return np.concatenate([nope, rope, rng(0, 2560), rng(4096, OD_KR_LO), rng(OD_KR_HI, 6720)])


def _rope_tables(positions):
    half = MLA_ROPE // 2
    inv_freq = ROPE_BASE ** (-jnp.arange(half, dtype=F32) / half)
    ang = positions.astype(F32).reshape(-1, 1) * inv_freq
    cos, sin = jnp.cos(ang), jnp.sin(ang)
    z = jnp.zeros((ang.shape[0], LANES - MLA_ROPE), F32)
    return jnp.concatenate([cos, cos, z], axis=1), jnp.concatenate([-sin, sin, z], axis=1)


def _side_weight(w, lo, hi):
    return jnp.pad(w[:, lo:hi], ((0, 0), (0, LANES - (hi - lo)))).astype(BF16)


def _mem_kv(mem2, gain, w_kv):
    zero_side = jnp.zeros((D_MODEL, LANES), BF16)
    memkv, _ = _norm_proj(mem2, gain, w_kv.astype(BF16), zero_side, tm=mem2.shape[0] // 2, tn=512)
    return memkv


def _even_layer(x2, mem2, rel_bias, norm_g, w_in, conv_w, a_log, dt_bias, gdn_gain, dq_gain, dk_gain, lam, subln,
                mem_norm, mem_w_kv, mem_qn, mem_kn, w_out, lambda_init, bsz, seq):
    w_main = jnp.concatenate([w_in[:, :EV_SIDE_LO], w_in[:, EV_SIDE_HI:]], axis=1).astype(BF16)
    h, side = _norm_proj(x2, norm_g, w_main, _side_weight(w_in, EV_SIDE_LO, EV_SIDE_HI), tm=1024, tn=512)

    nh = GDN_HEADS
    bar = side[:, :4 * nh].reshape(bsz, seq, 4, nh).transpose(0, 3, 2, 1)
    bar = jnp.pad(bar, ((0, 0), (0, 0), (0, 4), (0, 0)))
    prr = jnp.pad(jnp.stack([a_log.T, dt_bias.T], axis=-1), ((0, 0), (2, 4), (0, 0)))
    ya = _gdn(h, bar, conv_w, prr, gdn_gain, bsz, seq)

    dq, dk, dv = _diff_prep(h, dq_gain, dk_gain, tm=512)
    bias = _diff_bias_vecs(rel_bias, ATT_TQ, ATT_TK)
    yb = _diff_flash(dq, dk, dv, h, bias, lam, subln, lambda_init, bsz, seq)

    memkv = _mem_kv(mem2, mem_norm, mem_w_kv)
    ym = _mem_attn(h, memkv, mem_qn, mem_kn, EV_MQ, EV_MG, bsz, seq, tq=1024)
    return _out_proj(x2, ya, yb, ym, w_out, tm=512)


def _odd_layer(x2, mem2, positions, rel_bias, norm_g, w_in, swa_qn, swa_kn, sink, kv_norm, w_kv_up, mla_qn, mla_kn,
               mem_norm, mem_w_kv, mem_qn, mem_kn, w_out, bsz, seq):
    w_main = w_in[:, _od_perm()].astype(BF16)
    h, side = _norm_proj(x2, norm_g, w_main, _side_weight(w_in, OD_KR_LO, OD_KR_HI), tm=1024, tn=512)

    ya = _swa(h, _swa_bias_vecs(rel_bias), sink, swa_qn, swa_kn, bsz, seq)

    cos, sin = _rope_tables(positions)
    qm, km, vm = _mla_prep(h, side, cos, sin, kv_norm, w_kv_up, mla_qn, mla_kn, tm=256)
    yb = _mla_flash(qm, km, vm, h, bsz, seq)

    memkv = _mem_kv(mem2, mem_norm, mem_w_kv)
    ym = _mem_attn(h, memkv, mem_qn, mem_kn, OD_MQ, OD_MG, bsz, seq, tq=1024)
    return _out_proj(x2, ya, yb, ym, w_out, tm=512)


def kernel(x, mem, positions, rel_bias, ev_norm, ev_w_in, ev_conv, ev_a_log, ev_dt_bias, ev_gdn_norm, ev_diff_qnorm, ev_diff_knorm, ev_diff_lambda, ev_diff_subln, ev_mem_norm, ev_mem_w_kv, ev_mem_qnorm, ev_mem_knorm, ev_w_out, od_norm, od_w_in, od_swa_qnorm, od_swa_knorm, od_swa_sink, od_mla_kv_norm, od_mla_w_kv_up, od_mla_qnorm, od_mla_knorm, od_mem_norm, od_mem_w_kv, od_mem_qnorm, od_mem_knorm, od_w_out):
    bsz, seq, d = x.shape
    depth = ev_norm.shape[0] + od_norm.shape[0]
    x2 = x.reshape(bsz * seq, d)
    mem2 = mem.reshape(bsz * mem.shape[1], d)
    for layer in range(depth):
        i = layer // 2
        if layer % 2 == 0:
            lambda_init = 0.8 - 0.6 * math.exp(-0.3 * layer)
            x2 = _even_layer(x2, mem2, rel_bias, ev_norm[i], ev_w_in[i], ev_conv[i], ev_a_log[i], ev_dt_bias[i],
                             ev_gdn_norm[i], ev_diff_qnorm[i], ev_diff_knorm[i], ev_diff_lambda[i], ev_diff_subln[i],
                             ev_mem_norm[i], ev_mem_w_kv[i], ev_mem_qnorm[i], ev_mem_knorm[i], ev_w_out[i],
                             lambda_init, bsz, seq)
        else:
            x2 = _odd_layer(x2, mem2, positions, rel_bias, od_norm[i], od_w_in[i], od_swa_qnorm[i], od_swa_knorm[i],
                            od_swa_sink[i], od_mla_kv_norm[i], od_mla_w_kv_up[i], od_mla_qnorm[i], od_mla_knorm[i],
                            od_mem_norm[i], od_mem_w_kv[i], od_mem_qnorm[i], od_mem_knorm[i], od_w_out[i], bsz, seq)
    return x2.reshape(bsz, seq, d)
```

```python
import functools
import math

import numpy as np
import jax
import jax.numpy as jnp
from jax import lax
from jax.experimental import pallas as pl
from jax.experimental.pallas import tpu as pltpu

F32 = jnp.float32
BF16 = jnp.bfloat16
EPS = 1e-6
NEG = -1e30

V7X_VMEM_BYTES = 64 * 1024 * 1024
VMEM_LIMIT = V7X_VMEM_BYTES - 8 * 1024 * 1024
LANES = 128

D_MODEL = 2048
MEM_LEN = 256
GDN_HEADS, GDN_DK, GDN_DV, GDN_CONV = 8, 128, 128, 5
DIFF_HEADS, DIFF_DQK, DIFF_DV = 8, 64, 128
SWA_HEADS, SWA_KV_HEADS, SWA_DH, WINDOW = 8, 2, 128, 128
MLA_HEADS, MLA_NOPE, MLA_ROPE, MLA_DV, MLA_KV_RANK = 8, 128, 64, 128, 512
ROPE_BASE = 10000.0
MEM_HEADS, MEM_DH = 4, 128
REL_BUCKETS, REL_MAX_DIST = 32, 128

GDN_CHUNK = 256
GDN_LEVELS = (GDN_CHUNK // 2).bit_length() - 1
GDN_PREP_CHUNKS = 2
ATT_TQ = 512
ATT_TK = 512
ATT_RB = 32
LOG2E = math.log2(math.e)
SWA_TQ = 256
SWA_TW = SWA_TQ + 2 * WINDOW

EV_GQ, EV_GK, EV_GV, EV_GG = 0, 1024, 2048, 3072
EV_DQ, EV_DK, EV_DV, EV_DG = 4096, 5120, 6144, 7168
EV_MQ, EV_MG = 8192, 8704
EV_MAIN = 9216
EV_SIDE_LO, EV_SIDE_HI = 3072, 3104

OD_MLQ_NOPE, OD_MLQ_ROPE = 0, 1024
OD_SQ, OD_SK, OD_SV, OD_SG = 1536, 2560, 2816, 3072
OD_CKV, OD_MLG, OD_MQ, OD_MG = 4096, 4608, 5632, 6144
OD_MAIN = 6656
OD_KR_LO, OD_KR_HI = 4608, 4672


def _cparams(sem):
    return pltpu.CompilerParams(dimension_semantics=sem, vmem_limit_bytes=VMEM_LIMIT)


def _dot(a, b):
    return jnp.dot(a, b, preferred_element_type=F32)


def _dot_nt(a, b):
    return lax.dot_general(a, b, (((1,), (1,)), ((), ())), preferred_element_type=F32)


def _silu(x):
    x = x.astype(F32)
    return x * jax.nn.sigmoid(x)


def _softplus(x):
    return jnp.maximum(x, 0.0) + jnp.log(1.0 + jnp.exp(-jnp.abs(x)))


def _lane_sum(x, scale=1.0):
    return _dot(x.astype(BF16), jnp.ones((LANES, LANES), BF16)) * scale


def _rms(x, gain):
    x = x.astype(F32)
    if x.shape[-1] == LANES:
        ms = _lane_sum(x * x, 1.0 / LANES)
    else:
        ms = jnp.mean(x * x, axis=-1, keepdims=True)
    return x * lax.rsqrt(ms + EPS) * gain


def _norm_proj_kernel(x_ref, g_ref, w_ref, ws_ref, o_ref, os_ref, xn_ref):
    @pl.when(pl.program_id(1) == 0)
    def _():
        xn = _rms(x_ref[...], g_ref[...]).astype(BF16)
        xn_ref[...] = xn
        os_ref[...] = _dot(xn, ws_ref[...])

    o_ref[...] = _dot(xn_ref[...], w_ref[...]).astype(o_ref.dtype)


def _norm_proj(x, gain, w_main, w_side, tm, tn, out_dtype):
    m, k = x.shape
    n = w_main.shape[1]
    ns = w_side.shape[1]
    assert m % tm == 0 and n % tn == 0
    return pl.pallas_call(
        _norm_proj_kernel,
        grid=(m // tm, n // tn),
        in_specs=[
            pl.BlockSpec((tm, k), lambda i, j: (i, 0)),
            pl.BlockSpec((1, k), lambda i, j: (0, 0)),
            pl.BlockSpec((k, tn), lambda i, j: (0, j)),
            pl.BlockSpec((k, ns), lambda i, j: (0, 0)),
        ],
        out_specs=[
            pl.BlockSpec((tm, tn), lambda i, j: (i, j)),
            pl.BlockSpec((tm, ns), lambda i, j: (i, 0)),
        ],
        out_shape=[jax.ShapeDtypeStruct((m, n), out_dtype), jax.ShapeDtypeStruct((m, ns), F32)],
        scratch_shapes=[pltpu.VMEM((tm, k), BF16)],
        compiler_params=_cparams(("parallel", "arbitrary")),
        name="norm_proj",
    )(x, gain.reshape(1, k), w_main, w_side)


def _out_proj_kernel(x_ref, ya_ref, yb_ref, ym_ref, wa_ref, wb_ref, wm_ref, o_ref):
    acc = _dot(ya_ref[...], wa_ref[...])
    acc = acc + _dot(yb_ref[...], wb_ref[...])
    acc = acc + _dot(ym_ref[...], wm_ref[...])
    o_ref[...] = x_ref[...] + acc


def _out_proj(x, ya, yb, ym, w_out, tm):
    m, d = x.shape
    na, nb, nm = ya.shape[1], yb.shape[1], ym.shape[1]
    wa = w_out[:na].astype(BF16)
    wb = w_out[na:na + nb].astype(BF16)
    wm = w_out[na + nb:].astype(BF16)
    row = lambda i: (i, 0)
    fixed = lambda i: (0, 0)
    return pl.pallas_call(
        _out_proj_kernel,
        grid=(m // tm,),
        in_specs=[
            pl.BlockSpec((tm, d), row), pl.BlockSpec((tm, na), row), pl.BlockSpec((tm, nb), row),
            pl.BlockSpec((tm, nm), row),
            pl.BlockSpec((na, d), fixed), pl.BlockSpec((nb, d), fixed), pl.BlockSpec((nm, d), fixed),
        ],
        out_specs=pl.BlockSpec((tm, d), row),
        out_shape=jax.ShapeDtypeStruct((m, d), F32),
        compiler_params=_cparams(("parallel",)),
        name="out_proj",
    )(x, ya, yb, ym, wa, wb, wm)


def _gdn_kernel(q_ref, k_ref, v_ref, gate_ref, bar_ref, cwq_ref, cwk_ref, cwv_ref, prr_ref, gain_ref, y_ref,
                xp_ref, qd_ref, kw_ref, b_ref, egl_ref, o_ref, lvl_ref, tri_ref):
    seq = q_ref.shape[0]
    c = GDN_CHUNK
    nc = seq // c
    pad = 8
    scale = GDN_DK ** -0.5

    for i, src in enumerate((q_ref, k_ref, v_ref)):
        xp_ref[i, 0:pad, :] = jnp.zeros((pad, LANES), F32)
        xp_ref[i, pad + seq:2 * pad + seq, :] = jnp.zeros((pad, LANES), F32)
        xp_ref[i, pad:pad + seq, :] = src[...].astype(F32)

    hc = c // 2
    row_h = lax.broadcasted_iota(jnp.int32, (hc, hc), 0)
    col_h = lax.broadcasted_iota(jnp.int32, (hc, hc), 1)
    for bit in range(GDN_LEVELS):
        lvl_ref[bit] = (((row_h ^ col_h) >> bit) == 1).astype(BF16)
    lvl_ref[GDN_LEVELS] = (row_h == col_h).astype(BF16)
    tri_ref[0] = jnp.where(row_h > col_h, 0.0, NEG)
    tri_ref[1] = jnp.where(row_h < col_h, 0.0, NEG)
    tri_ref[2] = (row_h == col_h).astype(F32)

    lane_c = lax.broadcasted_iota(jnp.int32, (8, c), 1)
    sub_c = lax.broadcasted_iota(jnp.int32, (8, c), 0)

    def prefix(x):
        s = 1
        while s < c:
            x = x + jnp.where(lane_c >= s, pltpu.roll(x, s, 1), 0.0)
            s *= 2
        return x

    def suffix(x):
        s = 1
        while s < c:
            x = x + jnp.where(lane_c < c - s, pltpu.roll(x, c - s, 1), 0.0)
            s *= 2
        return x

    def conv(i, cw_ref, t0):
        half = (GDN_CONV - 1) // 2
        acc = None
        for j in range(GDN_CONV):
            tap = xp_ref[i, pl.ds(t0 + (pad - half + j), c), :] * cw_ref[j:j + 1, :]
            acc = tap if acc is None else acc + tap
        return _silu(acc)

    def l2n(x):
        return x * lax.rsqrt(_lane_sum(x * x) + EPS)

    half = lambda i: slice(i * hc, (i + 1) * hc)


    def prologue(ns, chains):
        st = []
        for n in ns:
            t0 = pl.multiple_of(n * c, c)
            st.append(dict(n=n, t0=t0, rows=pl.ds(t0, c)))
        for name, idx, cw_ref in (("q", 0, cwq_ref), ("k", 1, cwk_ref), ("v", 2, cwv_ref)):
            for s in st:
                s[name] = conv(idx, cw_ref, s["t0"])
            yield
        for s in st:
            s["q"], s["k"] = l2n(s["q"]), l2n(s["k"])
        yield
        for s in st:
            s["qs"] = s["q"] * scale
            s["kb"] = s["k"].astype(BF16)
            s["qk"] = _dot_nt(s["qs"].astype(BF16), s["kb"])
            bar = bar_ref[:, s["rows"]]
            g_r = (-LOG2E * jnp.exp(prr_ref[:, 0:1])) * _softplus(bar + prr_ref[:, 1:2])
            s["pre"], s["suf"] = prefix(g_r), suffix(g_r)
            s["tot"] = jnp.sum(g_r, axis=1, keepdims=True)
            packed = jnp.where(sub_c < 2, jax.nn.sigmoid(bar), jnp.where(sub_c == 2, s["pre"], s["suf"]))
            s["cols"] = jnp.concatenate([packed, jnp.zeros((LANES - 8, c), F32)], axis=0).T
            o_ref[s["rows"], :] = jnp.zeros((c, LANES), F32)
        yield
        for d in range(2):
            for s in st:
                s["kkb", d] = _dot_nt((s["k"] * s["cols"][:, d:d + 1]).astype(BF16), s["kb"])
            yield
        for d in range(2):
            first, second = (0, 1) if d == 0 else (1, 0)
            for s in st:
                gr = s["pre"][2:3, :] if d == 0 else s["suf"][3:4, :]
                gc = s["cols"][:, 2 + d:3 + d]
                blocks, attn = {}, {}
                for r, t in ((first, first), (second, second), (second, first)):
                    e = gc[half(r)] - gr[:, half(t)]
                    if r == t:
                        e = e + tri_ref[d]
                    dec = jnp.exp2(e)
                    blocks[r, t] = (s["kkb", d][half(r), half(t)] * dec).astype(BF16)
                    if r == t:
                        dec = dec + tri_ref[2]
                    attn[r, t] = (s["qk"][half(r), half(t)] * dec).astype(BF16)
                attn[first, second] = jnp.zeros((hc, hc), BF16)
                a_full = jnp.concatenate([jnp.concatenate([attn[r, 0], attn[r, 1]], axis=1) for r in range(2)], axis=0)
                diag = [blocks[0, 0], blocks[1, 1]]
                chains.append(dict(n=s["n"], rows=s["rows"], d=d, k=s["k"], v=s["v"],
                                   qs=s["qs"], beta=s["cols"][:, d:d + 1], gc=gc, tot=s["tot"][2 + d:3 + d, :],
                                   m=diag, off=blocks[second, first], a=a_full,
                                   p=[lvl_ref[GDN_LEVELS] - blk * lvl_ref[0] for blk in diag]))
            yield

    def levels(chains):
        for bit in range(1, GDN_LEVELS):
            lvl = lvl_ref[bit]
            for ch in chains:
                ch["x"] = [_dot(p, m * lvl).astype(BF16) for p, m in zip(ch["p"], ch["m"])]
            yield
            for ch in chains:
                ch["p"] = [p - _dot(x, p).astype(BF16) for p, x in zip(ch["p"], ch["x"])]
            yield

    def epilogue(chains):
        order = lambda ch: (0, 1) if ch["d"] == 0 else (1, 0)
        for ch in chains:
            ch["eg"] = jnp.exp2(ch["gc"])
            ch["rhs"] = jnp.concatenate([ch["v"] * ch["beta"], ch["k"] * (ch["beta"] * ch["eg"])], axis=1)
            ch["x1"] = _dot(ch["p"][order(ch)[0]], ch["rhs"][half(order(ch)[0])].astype(BF16))
        yield
        for ch in chains:
            ch["cross"] = _dot(ch["off"], ch["x1"].astype(BF16))
        yield
        for ch in chains:
            second = order(ch)[1]
            ch["x2"] = _dot(ch["p"][second], (ch["rhs"][half(second)] - ch["cross"]).astype(BF16))
        yield
        for ch in chains:
            xs = (ch["x1"], ch["x2"]) if ch["d"] == 0 else (ch["x2"], ch["x1"])
            ch["xb"] = jnp.concatenate(xs, axis=0).astype(BF16)
            kd = ch["k"] * jnp.exp2(ch["tot"] - ch["gc"])
            ch["kx"] = _dot(kd.T.astype(BF16), ch["xb"])
            ch["ax"] = _dot(ch["a"], ch["xb"])
        yield
        for ch in chains:
            d, rows = ch["d"], ch["rows"]
            blk = pl.ds(pl.multiple_of(ch["n"] * GDN_DK, GDN_DK), GDN_DK)
            b_ref[d, blk, :] = ch["kx"][:, :GDN_DV]
            kw_ref[d, blk, :] = ch["kx"][:, GDN_DV:].astype(BF16)
            o_ref[rows, :] = o_ref[rows, :] + ch["ax"][:, :GDN_DV]
            qd_ref[d, rows, :] = (ch["qs"] * ch["eg"] - ch["ax"][:, GDN_DV:]).astype(BF16)
            egl_ref[d, pl.ds(pl.multiple_of(ch["n"] * 8, 8), 8), :] = jnp.broadcast_to(jnp.exp2(ch["tot"]), (8, LANES))
        yield

    def interleave(*gens):
        gens = list(gens)
        while gens:
            for g in list(gens):
                if next(g, StopIteration) is StopIteration:
                    gens.remove(g)

    def prep(n, carry):
        base = n * 2 * GDN_PREP_CHUNKS
        group_a = [base + i for i in range(GDN_PREP_CHUNKS)]
        group_b = [base + GDN_PREP_CHUNKS + i for i in range(GDN_PREP_CHUNKS)]
        chains_a, chains_b = [], []
        interleave(prologue(group_a, chains_a))
        interleave(levels(chains_a), prologue(group_b, chains_b))
        interleave(levels(chains_b), epilogue(chains_a))
        interleave(epilogue(chains_b))
        return carry

    lax.fori_loop(0, nc // (2 * GDN_PREP_CHUNKS), prep, 0)

    def scan(n, carry):
        states = list(carry)
        for d in range(2):
            idx = n if d == 0 else nc - 1 - n
            t0 = pl.multiple_of(idx * c, c)
            rows = pl.ds(t0, c)
            blk = pl.ds(pl.multiple_of(idx * GDN_DK, GDN_DK), GDN_DK)
            s = states[d]
            sb = s.astype(BF16)
            o_ref[rows, :] = o_ref[rows, :] + _dot(qd_ref[d, rows, :], sb)
            egl = egl_ref[d, pl.ds(pl.multiple_of(idx * 8, 8), 8), :][0:1, :]
            states[d] = s * egl - _dot(kw_ref[d, blk, :], sb) + b_ref[d, blk, :]
        return tuple(states)

    zero = jnp.zeros((GDN_DK, GDN_DV), F32)
    lax.fori_loop(0, nc, scan, (zero, zero))

    def fin(n, carry):
        rows = pl.ds(pl.multiple_of(n * c, c), c)
        y = _rms(o_ref[rows, :], gain_ref[...]) * _silu(gate_ref[rows, :])
        y_ref[rows, :] = y.astype(y_ref.dtype)
        return carry

    lax.fori_loop(0, nc, fin, 0)


def _gdn(h, bar, conv_w, prr, gain, bsz, seq):
    nh = GDN_HEADS
    blk = lambda off: pl.BlockSpec((seq, LANES), lambda b, hh, off=off: (b, off // LANES + hh))
    cw = lambda off: pl.BlockSpec((GDN_CONV, LANES), lambda b, hh, off=off: (0, off // LANES + hh))
    c = GDN_CHUNK
    assert seq % (2 * GDN_PREP_CHUNKS * c) == 0
    return pl.pallas_call(
        _gdn_kernel,
        grid=(bsz, nh),
        in_specs=[
            blk(EV_GQ), blk(EV_GK), blk(EV_GV), blk(EV_GG),
            pl.BlockSpec((None, None, 8, seq), lambda b, hh: (b, hh, 0, 0)),
            cw(0), cw(GDN_HEADS * GDN_DK), cw(2 * GDN_HEADS * GDN_DK),
            pl.BlockSpec((None, 8, 2), lambda b, hh: (hh, 0, 0)),
            pl.BlockSpec((1, LANES), lambda b, hh: (0, 0)),
        ],
        out_specs=pl.BlockSpec((seq, LANES), lambda b, hh: (b, hh)),
        out_shape=jax.ShapeDtypeStruct((bsz * seq, nh * GDN_DV), BF16),
        scratch_shapes=[
            pltpu.VMEM((3, seq + 16, LANES), F32),
            pltpu.VMEM((2, seq, LANES), BF16),
            pltpu.VMEM((2, (seq // c) * GDN_DK, GDN_DV), BF16),
            pltpu.VMEM((2, (seq // c) * GDN_DK, GDN_DV), F32),
            pltpu.VMEM((2, (seq // c) * 8, LANES), F32),
            pltpu.VMEM((seq, LANES), F32),
            pltpu.VMEM((GDN_LEVELS + 1, c // 2, c // 2), BF16),
            pltpu.VMEM((3, c // 2, c // 2), F32),
        ],
        compiler_params=_cparams(("parallel", "parallel")),
        name="gdn",
    )(h, h, h, h, bar, conv_w, conv_w, conv_w, prr, gain.reshape(1, LANES))


def _flash_scratch(nmaps, tq, tk, dv):
    per_map = [pltpu.VMEM((tq, tk), F32), pltpu.VMEM((tq, tk), F32), pltpu.VMEM((tq, tk), BF16),
               pltpu.VMEM((tq, LANES), F32), pltpu.VMEM((tq, LANES), F32), pltpu.VMEM((tq, dv), F32)]
    return per_map * nmaps


def _flash_core(qs, k_ref, v_ref, bias_of, scratch):
    nmaps = len(qs)
    maps = [scratch[6 * i:6 * i + 6] for i in range(nmaps)]
    tq, tk = maps[0][0].shape
    nk = k_ref.shape[0] // tk
    assert nk % 2 == 0 and tk % LANES == 0 and tq % ATT_RB == 0
    nlb = tk // LANES

    for s0, s1, p, m, l, acc in maps:
        p[...] = jnp.zeros(p.shape, p.dtype)
        m[...] = jnp.full(m.shape, NEG, F32)
        l[...] = jnp.zeros(l.shape, F32)
        acc[...] = jnp.zeros(acc.shape, F32)

    def chunk(ref, j):
        return ref[pl.ds(pl.multiple_of(j * tk, tk), tk), :]

    def qk(i, j, slot):
        maps[i][slot][...] = _dot_nt(qs[i], chunk(k_ref, j))

    def pv(i, j):
        acc = maps[i][5]
        acc[...] = acc[...] + _dot(maps[i][2][...], chunk(v_ref, j))

    def softmax(i, j, slot):
        s_ref, p_ref, m_ref, l_ref, acc_ref = maps[i][slot], maps[i][2], maps[i][3], maps[i][4], maps[i][5]
        bias = bias_of(j)
        for rb in range(tq // ATT_RB):
            r = slice(rb * ATT_RB, (rb + 1) * ATT_RB)
            s = s_ref[r, :]
            if bias is not None:
                s = s + bias[r, :]
            blocks = [s[:, b * LANES:(b + 1) * LANES] for b in range(nlb)]
            mx = functools.reduce(jnp.maximum, blocks)
            m_old = m_ref[r, :]
            m_new = jnp.maximum(m_old, jnp.broadcast_to(jnp.max(mx, axis=-1, keepdims=True), m_old.shape))
            alpha = jnp.exp2(m_old - m_new)
            ps = [jnp.exp2(b - m_new) for b in blocks]
            row_sum = jnp.sum(functools.reduce(jnp.add, ps), axis=-1, keepdims=True)
            l_ref[r, :] = alpha * l_ref[r, :] + jnp.broadcast_to(row_sum, m_old.shape)
            m_ref[r, :] = m_new
            acc_ref[r, :] = acc_ref[r, :] * alpha
            p_ref[r, :] = jnp.concatenate(ps, axis=1).astype(BF16)

    for i in range(nmaps):
        qk(i, 0, 0)

    def step(j, slot):
        nxt = jnp.minimum(j + 1, nk - 1)
        for i in range(nmaps):
            if i == 0:
                pv(nmaps - 1, jnp.maximum(j - 1, 0))
            else:
                pv(i - 1, j)
            qk(i, nxt, 1 - slot)
            softmax(i, j, slot)

    def body(jj, carry):
        step(2 * jj, 0)
        step(2 * jj + 1, 1)
        return carry

    lax.fori_loop(0, nk // 2, body, 0)
    pv(nmaps - 1, nk - 1)
    return [(mp[5][...], mp[4][...]) for mp in maps]


def _diff_prep_kernel(q_ref, k_ref, v_ref, qg_ref, kg_ref, qo_ref, ko_ref, vo_ref):
    r = lax.broadcasted_iota(jnp.int32, (LANES, LANES), 0) < DIFF_DQK
    cc = lax.broadcasted_iota(jnp.int32, (LANES, LANES), 1) < DIFF_DQK
    half_mean = jnp.where(r == cc, 1.0 / DIFF_DQK, 0.0).astype(BF16)

    def halfnorm(x, gain):
        x = x.astype(F32)
        return x * lax.rsqrt(_dot((x * x).astype(BF16), half_mean) + EPS) * gain

    for hh in range(DIFF_HEADS):
        sl = slice(hh * LANES, (hh + 1) * LANES)
        qo_ref[:, sl] = (halfnorm(q_ref[:, sl], qg_ref[...]) * (DIFF_DQK ** -0.5 * LOG2E)).astype(BF16)
        ko_ref[:, sl] = halfnorm(k_ref[:, sl], kg_ref[...]).astype(BF16)
    vo_ref[...] = v_ref[...].astype(BF16)


def _diff_prep(h, qg, kg, tm):
    m = h.shape[0]
    w = DIFF_HEADS * LANES
    spec = lambda off: pl.BlockSpec((tm, w), lambda i, off=off: (i, off // w))
    vec = pl.BlockSpec((1, LANES), lambda i: (0, 0))
    out = jax.ShapeDtypeStruct((m, w), BF16)
    return pl.pallas_call(
        _diff_prep_kernel,
        grid=(m // tm,),
        in_specs=[spec(EV_DQ), spec(EV_DK), spec(EV_DV), vec, vec],
        out_specs=[pl.BlockSpec((tm, w), lambda i: (i, 0))] * 3,
        out_shape=[out, out, out],
        compiler_params=_cparams(("parallel",)),
        name="diff_prep",
    )(h, h, h, jnp.tile(qg, 2).reshape(1, LANES), jnp.tile(kg, 2).reshape(1, LANES))


def _toeplitz(vec, tq, tk):
    w = vec.shape[-1]
    full = pltpu.roll(jnp.broadcast_to(vec, (tq, w)), w - (tq - 1), 1, stride=1, stride_axis=0)
    return full[:, :tk]


def _diff_flash_kernel(q_ref, k_ref, v_ref, gate_ref, bvec_ref, lam_ref, sub_ref, y_ref, bias_ref, *scratch,
                       lambda_init):
    tq = q_ref.shape[0]
    tk = bias_ref.shape[-1]
    qi = pl.program_id(2)

    @pl.when(qi == 0)
    def _():
        for d in range(bias_ref.shape[0]):
            bias_ref[d] = _toeplitz(bvec_ref[d:d + 1, :], tq, tk)

    q = q_ref[...]
    lane = lax.broadcasted_iota(jnp.int32, (1, LANES), 1)
    zero = jnp.zeros_like(q)
    q0 = jnp.where(lane < DIFF_DQK, q, zero)
    q1 = jnp.where(lane < DIFF_DQK, zero, q)

    def bias_of(j):
        return bias_ref.at[jnp.clip(j - qi, -2, 2) + 2]

    (a0, l0), (a1, l1) = _flash_core([q0, q1], k_ref, v_ref, bias_of, scratch)

    lam = lam_ref[...]
    lam_full = (jnp.exp(jnp.sum(lam[0:1] * lam[1:2], axis=-1, keepdims=True))
                - jnp.exp(jnp.sum(lam[2:3] * lam[3:4], axis=-1, keepdims=True)) + lambda_init)
    o = a0 / l0 - lam_full * (a1 / l1)
    o = _rms(o, sub_ref[...]) * (1.0 - lambda_init)
    y_ref[...] = (o * _silu(gate_ref[...])).astype(y_ref.dtype)


def _diff_flash(dq, dk, dv, h, bvec, lam, subln, lambda_init, bsz, seq):
    tq, tk = ATT_TQ, ATT_TK
    nq = seq // tq
    nh = DIFF_HEADS
    return pl.pallas_call(
        functools.partial(_diff_flash_kernel, lambda_init=lambda_init),
        grid=(bsz, nh, nq),
        in_specs=[
            pl.BlockSpec((tq, LANES), lambda b, hh, i: (b * nq + i, hh)),
            pl.BlockSpec((seq, LANES), lambda b, hh, i: (b, hh)),
            pl.BlockSpec((seq, LANES), lambda b, hh, i: (b, hh)),
            pl.BlockSpec((tq, LANES), lambda b, hh, i: (b * nq + i, EV_DG // LANES + hh)),
            pl.BlockSpec((None,) + bvec.shape[1:], lambda b, hh, i: (hh, 0, 0)),
            pl.BlockSpec((4, DIFF_DQK), lambda b, hh, i: (0, 0)),
            pl.BlockSpec((1, LANES), lambda b, hh, i: (0, 0)),
        ],
        out_specs=pl.BlockSpec((tq, LANES), lambda b, hh, i: (b * nq + i, hh)),
        out_shape=jax.ShapeDtypeStruct((bsz * seq, nh * DIFF_DV), BF16),
        scratch_shapes=[pltpu.VMEM((5, tq, tk), F32)] + _flash_scratch(2, tq, tk, DIFF_DV),
        compiler_params=_cparams(("parallel", "parallel", "arbitrary")),
        name="diff_flash",
    )(dq, dk, dv, h, bvec, lam, subln.reshape(1, LANES))


def _mem_attn_kernel(q_ref, gate_ref, mk_ref, mv_ref, qg_ref, kg_ref, y_ref):
    q = (_rms(q_ref[...], qg_ref[...]) * (MEM_DH ** -0.5)).astype(BF16)
    mk = _rms(mk_ref[...], kg_ref[...]).astype(BF16)
    s = _dot_nt(q, mk)
    p = jnp.exp(s - jnp.max(s, axis=-1, keepdims=True))
    o = _dot(p.astype(BF16), mv_ref[...].astype(BF16)) / jnp.sum(p, axis=-1, keepdims=True)
    y_ref[...] = (o * _silu(gate_ref[...])).astype(y_ref.dtype)


def _mem_attn(h, memkv, qg, kg, q_off, g_off, bsz, seq, tq):
    nq = seq // tq
    nh = MEM_HEADS
    vec = pl.BlockSpec((1, LANES), lambda b, hh, i: (0, 0))
    return pl.pallas_call(
        _mem_attn_kernel,
        grid=(bsz, nh, nq),
        in_specs=[
            pl.BlockSpec((tq, LANES), lambda b, hh, i: (b * nq + i, q_off // LANES + hh)),
            pl.BlockSpec((tq, LANES), lambda b, hh, i: (b * nq + i, g_off // LANES + hh)),
            pl.BlockSpec((MEM_LEN, LANES), lambda b, hh, i: (b, hh)),
            pl.BlockSpec((MEM_LEN, LANES), lambda b, hh, i: (b, nh + hh)),
            vec, vec,
        ],
        out_specs=pl.BlockSpec((tq, LANES), lambda b, hh, i: (b * nq + i, hh)),
        out_shape=jax.ShapeDtypeStruct((bsz * seq, nh * MEM_DH), BF16),
        compiler_params=_cparams(("parallel", "parallel", "parallel")),
        name="mem_attn",
    )(h, h, memkv, memkv, qg.reshape(1, LANES), kg.reshape(1, LANES))


def _swa_kernel(sink_ref, q_ref, k_ref, v_ref, gate_ref, bias_ref, qg_ref, kg_ref, y_ref, tile_ref):
    tq = q_ref.shape[0]
    seq = k_ref.shape[0]
    grp = SWA_HEADS // SWA_KV_HEADS
    kvh = pl.program_id(1)
    qi = pl.program_id(2)
    nq = pl.num_programs(2)

    @pl.when(jnp.logical_or(qi <= 1, qi == nq - 1))
    def _():
        for g in range(grp):
            tile_ref[g] = _toeplitz(bias_ref[g:g + 1, :], tq, SWA_TW)

    ws = pl.multiple_of(jnp.clip(qi * tq - WINDOW, 0, seq - SWA_TW), WINDOW)
    kw = _rms(k_ref[pl.ds(ws, SWA_TW), :], kg_ref[...]).astype(BF16)
    vw = v_ref[pl.ds(ws, SWA_TW), :].astype(BF16)
    heads = [slice(g * SWA_DH, (g + 1) * SWA_DH) for g in range(grp)]
    qs = [(_rms(q_ref[:, sl], qg_ref[...]) * (SWA_DH ** -0.5)).astype(BF16) for sl in heads]
    ss = [_dot_nt(q, kw) + tile_ref[g] for g, q in enumerate(qs)]
    ps, dens = [], []
    for g, s in enumerate(ss):
        sink = sink_ref[kvh * grp + g]
        mx = jnp.maximum(jnp.max(s, axis=-1, keepdims=True), sink)
        p = jnp.exp(s - mx)
        dens.append(jnp.sum(p, axis=-1, keepdims=True) + jnp.exp(sink - mx))
        ps.append(p.astype(BF16))
    for sl, p, den in zip(heads, ps, dens):
        y_ref[:, sl] = (_dot(p, vw) / den * _silu(gate_ref[:, sl])).astype(y_ref.dtype)


def _swa(h, bias, sink, qg, kg, bsz, seq):
    tq = SWA_TQ
    nq = seq // tq
    grp = SWA_HEADS // SWA_KV_HEADS
    gw = grp * SWA_DH
    assert seq >= SWA_TW and nq >= 2
    vec = pl.BlockSpec((1, LANES), lambda b, kv, i: (0, 0))

    def bias_idx(b, kv, i):
        return (jnp.where(i == 0, 0, jnp.where(i == nq - 1, 2, 1)), kv, 0, 0)

    return pl.pallas_call(
        _swa_kernel,
        grid=(bsz, SWA_KV_HEADS, nq),
        in_specs=[
            pl.BlockSpec(memory_space=pltpu.SMEM),
            pl.BlockSpec((tq, gw), lambda b, kv, i: (b * nq + i, OD_SQ // gw + kv)),
            pl.BlockSpec((seq, LANES), lambda b, kv, i: (b, OD_SK // LANES + kv)),
            pl.BlockSpec((seq, LANES), lambda b, kv, i: (b, OD_SV // LANES + kv)),
            pl.BlockSpec((tq, gw), lambda b, kv, i: (b * nq + i, OD_SG // gw + kv)),
            pl.BlockSpec((None, None, grp, bias.shape[-1]), bias_idx),
            vec, vec,
        ],
        out_specs=pl.BlockSpec((tq, gw), lambda b, kv, i: (b * nq + i, kv)),
        out_shape=jax.ShapeDtypeStruct((bsz * seq, SWA_HEADS * SWA_DH), BF16),
        scratch_shapes=[pltpu.VMEM((grp, tq, SWA_TW), F32)],
        compiler_params=_cparams(("parallel", "parallel", "arbitrary")),
        name="swa",
    )(sink, h, h, h, h, bias, qg.reshape(1, LANES), kg.reshape(1, LANES))


def _mla_prep_kernel(q_ref, ckv_ref, kr_ref, cos_ref, sin_ref, kvg_ref, wup_ref,
                     qgn_ref, qgr_ref, kgn_ref, kgr_ref, qo_ref, ko_ref, vo_ref):
    dqk = MLA_NOPE + MLA_ROPE
    lane = lax.broadcasted_iota(jnp.int32, (1, LANES), 1)
    lo = lane < MLA_ROPE
    first = lane < MLA_ROPE // 2
    cos = cos_ref[...]
    sin = sin_ref[...]

    def rope(t):
        rot = jnp.where(first, pltpu.roll(t, LANES - MLA_ROPE // 2, 1), pltpu.roll(t, MLA_ROPE // 2, 1))
        return t * cos + rot * sin

    ckv = _rms(ckv_ref[...], kvg_ref[...]).astype(BF16)
    kv = _dot(ckv, wup_ref[...])
    kr = kr_ref[...]
    kr2 = kr * kr
    c = dqk ** -0.5 * LOG2E
    hw = MLA_NOPE + MLA_DV
    for hh in range(MLA_HEADS):
        kn = kv[:, hh * hw:hh * hw + MLA_NOPE]
        inv = lax.rsqrt(_lane_sum(kn * kn + kr2, 1.0 / dqk) + EPS)
        ko_ref[:, hh * 2 * LANES:hh * 2 * LANES + LANES] = (kn * inv * kgn_ref[...]).astype(BF16)
        ko_ref[:, hh * 2 * LANES + LANES:(hh + 1) * 2 * LANES] = rope(kr * inv * kgr_ref[...]).astype(BF16)
        vo_ref[:, hh * LANES:(hh + 1) * LANES] = kv[:, hh * hw + MLA_NOPE:(hh + 1) * hw].astype(BF16)

        qn = q_ref[:, hh * LANES:(hh + 1) * LANES].astype(F32)
        pair = q_ref[:, OD_MLQ_ROPE + (hh // 2) * LANES:OD_MLQ_ROPE + (hh // 2 + 1) * LANES].astype(F32)
        if hh % 2 == 1:
            pair = pltpu.roll(pair, MLA_ROPE, 1)
        qr = jnp.where(lo, pair, 0.0)
        inv = lax.rsqrt(_lane_sum(qn * qn + qr * qr, 1.0 / dqk) + EPS)
        qo_ref[:, hh * 2 * LANES:hh * 2 * LANES + LANES] = (qn * inv * qgn_ref[...] * c).astype(BF16)
        qo_ref[:, hh * 2 * LANES + LANES:(hh + 1) * 2 * LANES] = (rope(qr * inv * qgr_ref[...]) * c).astype(BF16)


def _mla_prep(h, side, cos, sin, kv_gain, w_up, qg, kg, tm):
    m = h.shape[0]
    nh = MLA_HEADS
    qw = nh * (MLA_NOPE + MLA_ROPE)
    pad = lambda g: jnp.pad(g[MLA_NOPE:], (0, LANES - MLA_ROPE)).reshape(1, LANES)
    vec = pl.BlockSpec((1, LANES), lambda i: (0, 0))
    row = lambda i: (i, 0)
    return pl.pallas_call(
        _mla_prep_kernel,
        grid=(m // tm,),
        in_specs=[
            pl.BlockSpec((tm, qw), row),
            pl.BlockSpec((tm, MLA_KV_RANK), lambda i: (i, OD_CKV // MLA_KV_RANK)),
            pl.BlockSpec((tm, LANES), row),
            pl.BlockSpec((tm, LANES), row),
            pl.BlockSpec((tm, LANES), row),
            pl.BlockSpec((1, MLA_KV_RANK), lambda i: (0, 0)),
            pl.BlockSpec(w_up.shape, lambda i: (0, 0)),
            vec, vec, vec, vec,
        ],
        out_specs=[
            pl.BlockSpec((tm, nh * 2 * LANES), row),
            pl.BlockSpec((tm, nh * 2 * LANES), row),
            pl.BlockSpec((tm, nh * MLA_DV), row),
        ],
        out_shape=[
            jax.ShapeDtypeStruct((m, nh * 2 * LANES), BF16),
            jax.ShapeDtypeStruct((m, nh * 2 * LANES), BF16),
            jax.ShapeDtypeStruct((m, nh * MLA_DV), BF16),
        ],
        compiler_params=_cparams(("parallel",)),
        name="mla_prep",
    )(h, h, side, cos, sin, kv_gain.reshape(1, MLA_KV_RANK), w_up.astype(BF16),
      qg[:MLA_NOPE].reshape(1, LANES), pad(qg), kg[:MLA_NOPE].reshape(1, LANES), pad(kg))


def _mla_flash_kernel(q_ref, k_ref, v_ref, gate_ref, y_ref, *scratch):
    (acc, l), = _flash_core([q_ref[...]], k_ref, v_ref, lambda j: None, scratch)
    y_ref[...] = (acc / l * _silu(gate_ref[...])).astype(y_ref.dtype)


def _mla_flash(qm, km, vm, h, bsz, seq):
    tq = ATT_TQ
    nq = seq // tq
    nh = MLA_HEADS
    return pl.pallas_call(
        _mla_flash_kernel,
        grid=(bsz, nh, nq),
        in_specs=[
            pl.BlockSpec((tq, 2 * LANES), lambda b, hh, i: (b * nq + i, hh)),
            pl.BlockSpec((seq, 2 * LANES), lambda b, hh, i: (b, hh)),
            pl.BlockSpec((seq, LANES), lambda b, hh, i: (b, hh)),
            pl.BlockSpec((tq, LANES), lambda b, hh, i: (b * nq + i, OD_MLG // LANES + hh)),
        ],
        out_specs=pl.BlockSpec((tq, LANES), lambda b, hh, i: (b * nq + i, hh)),
        out_shape=jax.ShapeDtypeStruct((bsz * seq, nh * MLA_DV), BF16),
        scratch_shapes=_flash_scratch(1, tq, ATT_TK, MLA_DV),
        compiler_params=_cparams(("parallel", "parallel", "arbitrary")),
        name="mla_flash",
    )(qm, km, vm, h)


def _t5_bucket(rel):
    half = REL_BUCKETS // 2
    max_exact = half // 2
    ret = (rel > 0).astype(jnp.int32) * half
    n = jnp.abs(rel)
    nf = jnp.maximum(n, 1).astype(F32)
    large = max_exact + (jnp.log(nf / max_exact) / math.log(REL_MAX_DIST / max_exact)
                         * (half - max_exact)).astype(jnp.int32)
    large = jnp.minimum(large, half - 1)
    return ret + jnp.where(n < max_exact, n, large)


def _round_up(n, m):
    return (n + m - 1) // m * m


def _diff_bias_vecs(rel_table, tq, tk):
    assert tq == tk and tk + 1 >= REL_MAX_DIST
    m = jnp.arange(_round_up(tq + tk - 1, LANES))
    rows = [rel_table[_t5_bucket(m - (tq - 1) + d * tk)] for d in (-2, -1, 0, 1, 2)]
    return jnp.pad(jnp.stack(rows, axis=0).transpose(2, 0, 1) * LOG2E, ((0, 0), (0, 3), (0, 0))).astype(F32)


def _swa_bias_vecs(rel_table):
    m = jnp.arange(_round_up(SWA_TQ + SWA_TW - 1, LANES))
    rows = []
    for shift in (0, WINDOW, SWA_TW - SWA_TQ):
        rel = m - (SWA_TQ - 1) - shift
        rows.append(jnp.where((jnp.abs(rel) <= WINDOW)[None], rel_table[_t5_bucket(rel)].T, NEG))
    return jnp.stack(rows, axis=0).reshape(3, SWA_KV_HEADS, SWA_HEADS // SWA_KV_HEADS, -1).astype(F32)


def _od_perm():
    heads = np.arange(MLA_HEADS)[:, None] * (MLA_NOPE + MLA_ROPE)
    mlq = 2560
    nope = (mlq + heads + np.arange(MLA_NOPE)[None, :]).reshape(-1)
    rope = (mlq + heads + MLA_NOPE + np.arange(MLA_ROPE)[None, :]).reshape(-1)
    rng = lambda a, b: np.arange(a, b)
    return np.concatenate([nope, rope, rng(0, 2560), rng(4096, OD_KR_LO), rng(OD_KR_HI, 6720)])


def _rope_tables(positions):
    half = MLA_ROPE // 2
    inv_freq = ROPE_BASE ** (-jnp.arange(half, dtype=F32) / half)
    ang = positions.astype(F32).reshape(-1, 1) * inv_freq
    cos, sin = jnp.cos(ang), jnp.sin(ang)
    z = jnp.zeros((ang.shape[0], LANES - MLA_ROPE), F32)
    return jnp.concatenate([cos, cos, z], axis=1), jnp.concatenate([-sin, sin, z], axis=1)


def _side_weight(w, lo, hi):
    return jnp.pad(w[:, lo:hi], ((0, 0), (0, LANES - (hi - lo)))).astype(BF16)


def _mem_kv(mem2, gain, w_kv):
    zero_side = jnp.zeros((D_MODEL, LANES), BF16)
    memkv, _ = _norm_proj(mem2, gain, w_kv.astype(BF16), zero_side, tm=mem2.shape[0] // 2, tn=512, out_dtype=F32)
    return memkv


def _even_layer(x2, mem2, rel_bias, norm_g, w_in, conv_w, a_log, dt_bias, gdn_gain, dq_gain, dk_gain, lam, subln,
                mem_norm, mem_w_kv, mem_qn, mem_kn, w_out, lambda_init, bsz, seq):
    w_main = jnp.concatenate([w_in[:, :EV_SIDE_LO], w_in[:, EV_SIDE_HI:]], axis=1).astype(BF16)
    h, side = _norm_proj(x2, norm_g, w_main, _side_weight(w_in, EV_SIDE_LO, EV_SIDE_HI), tm=1024, tn=512,
                         out_dtype=BF16)

    nh = GDN_HEADS
    bar = side[:, :4 * nh].reshape(bsz, seq, 4, nh).transpose(0, 3, 2, 1)
    bar = jnp.pad(bar, ((0, 0), (0, 0), (0, 4), (0, 0)))
    prr = jnp.pad(jnp.stack([a_log.T, dt_bias.T], axis=-1), ((0, 0), (2, 4), (0, 0)))
    ya = _gdn(h, bar, conv_w, prr, gdn_gain, bsz, seq)

    dq, dk, dv = _diff_prep(h, dq_gain, dk_gain, tm=512)
    bias = _diff_bias_vecs(rel_bias, ATT_TQ, ATT_TK)
    yb = _diff_flash(dq, dk, dv, h, bias, lam, subln, lambda_init, bsz, seq)

    memkv = _mem_kv(mem2, mem_norm, mem_w_kv)
    ym = _mem_attn(h, memkv, mem_qn, mem_kn, EV_MQ, EV_MG, bsz, seq, tq=1024)
    return _out_proj(x2, ya, yb, ym, w_out, tm=512)


def _odd_layer(x2, mem2, positions, rel_bias, norm_g, w_in, swa_qn, swa_kn, sink, kv_norm, w_kv_up, mla_qn, mla_kn,
               mem_norm, mem_w_kv, mem_qn, mem_kn, w_out, bsz, seq):
    w_main = w_in[:, _od_perm()].astype(BF16)
    h, side = _norm_proj(x2, norm_g, w_main, _side_weight(w_in, OD_KR_LO, OD_KR_HI), tm=1024, tn=512,
                         out_dtype=BF16)

    ya = _swa(h, _swa_bias_vecs(rel_bias), sink, swa_qn, swa_kn, bsz, seq)

    cos, sin = _rope_tables(positions)
    qm, km, vm = _mla_prep(h, side, cos, sin, kv_norm, w_kv_up, mla_qn, mla_kn, tm=256)
    yb = _mla_flash(qm, km, vm, h, bsz, seq)

    memkv = _mem_kv(mem2, mem_norm, mem_w_kv)
    ym = _mem_attn(h, memkv, mem_qn, mem_kn, OD_MQ, OD_MG, bsz, seq, tq=1024)
    return _out_proj(x2, ya, yb, ym, w_out, tm=512)


def kernel(x, mem, positions, rel_bias, ev_norm, ev_w_in, ev_conv, ev_a_log, ev_dt_bias, ev_gdn_norm, ev_diff_qnorm, ev_diff_knorm, ev_diff_lambda, ev_diff_subln, ev_mem_norm, ev_mem_w_kv, ev_mem_qnorm, ev_mem_knorm, ev_w_out, od_norm, od_w_in, od_swa_qnorm, od_swa_knorm, od_swa_sink, od_mla_kv_norm, od_mla_w_kv_up, od_mla_qnorm, od_mla_knorm, od_mem_norm, od_mem_w_kv, od_mem_qnorm, od_mem_knorm, od_w_out):
    bsz, seq, d = x.shape
    depth = ev_norm.shape[0] + od_norm.shape[0]
    x2 = x.reshape(bsz * seq, d)
    mem2 = mem.reshape(bsz * mem.shape[1], d)
    for layer in range(depth):
        i = layer // 2
        if layer % 2 == 0:
            lambda_init = 0.8 - 0.6 * math.exp(-0.3 * layer)
            x2 = _even_layer(x2, mem2, rel_bias, ev_norm[i], ev_w_in[i], ev_conv[i], ev_a_log[i], ev_dt_bias[i],
                             ev_gdn_norm[i], ev_diff_qnorm[i], ev_diff_knorm[i], ev_diff_lambda[i], ev_diff_subln[i],
                             ev_mem_norm[i], ev_mem_w_kv[i], ev_mem_qnorm[i], ev_mem_knorm[i], ev_w_out[i],
                             lambda_init, bsz, seq)
        else:
            x2 = _odd_layer(x2, mem2, positions, rel_bias, od_norm[i], od_w_in[i], od_swa_qnorm[i], od_swa_knorm[i],
                            od_swa_sink[i], od_mla_kv_norm[i], od_mla_w_kv_up[i], od_mla_qnorm[i], od_mla_knorm[i],
                            od_mem_norm[i], od_mem_w_kv[i], od_mem_qnorm[i], od_mem_knorm[i], od_w_out[i], bsz, seq)
    return x2.reshape(bsz, seq, d)
```

```python
import functools
import math

import jax
import jax.numpy as jnp
from jax import lax
from jax.experimental import pallas as pl
from jax.experimental.pallas import tpu as pltpu

F32 = jnp.float32
BF16 = jnp.bfloat16
EPS = 1e-6
NEG = -1e30

V7X_VMEM_BYTES = 64 * 1024 * 1024
VMEM_LIMIT = V7X_VMEM_BYTES - 8 * 1024 * 1024
LANES = 128

D_MODEL = 2048
MEM_LEN = 256
GDN_HEADS, GDN_DK, GDN_DV, GDN_CONV = 8, 128, 128, 5
DIFF_HEADS, DIFF_DQK, DIFF_DV = 8, 64, 128
SWA_HEADS, SWA_KV_HEADS, SWA_DH, WINDOW = 8, 2, 128, 128
MLA_HEADS, MLA_NOPE, MLA_ROPE, MLA_DV, MLA_KV_RANK = 8, 128, 64, 128, 512
ROPE_BASE = 10000.0
MEM_HEADS, MEM_DH = 4, 128
REL_BUCKETS, REL_MAX_DIST = 32, 128

GDN_CHUNK = 256
GDN_LEVELS = (GDN_CHUNK // 2).bit_length() - 1
GDN_PREP_CHUNKS = 2
ATT_TQ = 512
ATT_TK = 512
ATT_RB = 32
LOG2E = math.log2(math.e)
SWA_TQ = 256
SWA_TW = SWA_TQ + 2 * WINDOW

EV_GQ, EV_GK, EV_GV, EV_GG = 0, 1024, 2048, 3072
EV_DQ, EV_DK, EV_DV, EV_DG = 4096, 5120, 6144, 7168
EV_MQ, EV_MG = 8192, 8704
EV_MAIN = 9216
EV_SIDE_LO, EV_SIDE_HI = 3072, 3104

OD_MLQ_NOPE, OD_MLQ_ROPE = 0, 1024
OD_SQ, OD_SK, OD_SV, OD_SG = 1536, 2560, 2816, 3072
OD_CKV, OD_MLG, OD_MQ, OD_MG = 4096, 4608, 5632, 6144
OD_MAIN = 6656
OD_KR_LO, OD_KR_HI = 4608, 4672


def _cparams(sem):
    return pltpu.CompilerParams(dimension_semantics=sem, vmem_limit_bytes=VMEM_LIMIT)


def _dot(a, b):
    return jnp.dot(a, b, preferred_element_type=F32)


def _dot_nt(a, b):
    return lax.dot_general(a, b, (((1,), (1,)), ((), ())), preferred_element_type=F32)


def _silu(x):
    x = x.astype(F32)
    return x * jax.nn.sigmoid(x)


def _softplus(x):
    return jnp.maximum(x, 0.0) + jnp.log(1.0 + jnp.exp(-jnp.abs(x)))


def _lane_sum(x, scale=1.0):
    return _dot(x.astype(BF16), jnp.ones((LANES, LANES), BF16)) * scale


def _rms(x, gain):
    x = x.astype(F32)
    if x.shape[-1] == LANES:
        ms = _lane_sum(x * x, 1.0 / LANES)
    else:
        ms = jnp.mean(x * x, axis=-1, keepdims=True)
    return x * lax.rsqrt(ms + EPS) * gain


def _norm_proj_kernel(x_ref, g_ref, w_ref, ws_ref, o_ref, os_ref, xn_ref):
    @pl.when(pl.program_id(1) == 0)
    def _():
        xn = _rms(x_ref[...], g_ref[...]).astype(BF16)
        xn_ref[...] = xn
        os_ref[...] = _dot_nt(xn, ws_ref[...])

    o_ref[...] = _dot_nt(xn_ref[...], w_ref[...]).astype(o_ref.dtype)


def _norm_proj(x, gain, w_main, w_side, tm, tn, out_dtype):
    m, k = x.shape
    n = w_main.shape[0]
    ns = w_side.shape[0]
    assert m % tm == 0 and n % tn == 0
    return pl.pallas_call(
        _norm_proj_kernel,
        grid=(m // tm, n // tn),
        in_specs=[
            pl.BlockSpec((tm, k), lambda i, j: (i, 0)),
            pl.BlockSpec((1, k), lambda i, j: (0, 0)),
            pl.BlockSpec((tn, k), lambda i, j: (j, 0)),
            pl.BlockSpec((ns, k), lambda i, j: (0, 0)),
        ],
        out_specs=[
            pl.BlockSpec((tm, tn), lambda i, j: (i, j)),
            pl.BlockSpec((tm, ns), lambda i, j: (i, 0)),
        ],
        out_shape=[jax.ShapeDtypeStruct((m, n), out_dtype), jax.ShapeDtypeStruct((m, ns), F32)],
        scratch_shapes=[pltpu.VMEM((tm, k), BF16)],
        compiler_params=_cparams(("parallel", "arbitrary")),
        name="norm_proj",
    )(x, gain.reshape(1, k), w_main, w_side)


def _out_proj_kernel(x_ref, ya_ref, yb_ref, ym_ref, wa_ref, wb_ref, wm_ref, o_ref):
    acc = _dot(ya_ref[...], wa_ref[...])
    acc = acc + _dot(yb_ref[...], wb_ref[...])
    acc = acc + _dot(ym_ref[...], wm_ref[...])
    o_ref[...] = x_ref[...] + acc


def _out_proj(x, ya, yb, ym, w_out, tm):
    m, d = x.shape
    na, nb, nm = ya.shape[1], yb.shape[1], ym.shape[1]
    wa = w_out[:na].astype(BF16)
    wb = w_out[na:na + nb].astype(BF16)
    wm = w_out[na + nb:].astype(BF16)
    row = lambda i: (i, 0)
    fixed = lambda i: (0, 0)
    return pl.pallas_call(
        _out_proj_kernel,
        grid=(m // tm,),
        in_specs=[
            pl.BlockSpec((tm, d), row), pl.BlockSpec((tm, na), row), pl.BlockSpec((tm, nb), row),
            pl.BlockSpec((tm, nm), row),
            pl.BlockSpec((na, d), fixed), pl.BlockSpec((nb, d), fixed), pl.BlockSpec((nm, d), fixed),
        ],
        out_specs=pl.BlockSpec((tm, d), row),
        out_shape=jax.ShapeDtypeStruct((m, d), F32),
        compiler_params=_cparams(("parallel",)),
        name="out_proj",
    )(x, ya, yb, ym, wa, wb, wm)


def _gdn_kernel(q_ref, k_ref, v_ref, gate_ref, bar_ref, cwq_ref, cwk_ref, cwv_ref, prr_ref, gain_ref, y_ref,
                xp_ref, qd_ref, kw_ref, b_ref, egl_ref, o_ref, lvl_ref, tri_ref):
    seq = q_ref.shape[0]
    c = GDN_CHUNK
    nc = seq // c
    pad = 8
    scale = GDN_DK ** -0.5

    for i, src in enumerate((q_ref, k_ref, v_ref)):
        xp_ref[i, 0:pad, :] = jnp.zeros((pad, LANES), F32)
        xp_ref[i, pad + seq:2 * pad + seq, :] = jnp.zeros((pad, LANES), F32)
        xp_ref[i, pad:pad + seq, :] = src[...].astype(F32)

    hc = c // 2
    row_h = lax.broadcasted_iota(jnp.int32, (hc, hc), 0)
    col_h = lax.broadcasted_iota(jnp.int32, (hc, hc), 1)
    for bit in range(GDN_LEVELS):
        lvl_ref[bit] = (((row_h ^ col_h) >> bit) == 1).astype(BF16)
    lvl_ref[GDN_LEVELS] = (row_h == col_h).astype(BF16)
    tri_ref[0] = jnp.where(row_h > col_h, 0.0, NEG)
    tri_ref[1] = jnp.where(row_h < col_h, 0.0, NEG)
    tri_ref[2] = (row_h == col_h).astype(F32)

    lane_c = lax.broadcasted_iota(jnp.int32, (8, c), 1)
    sub_c = lax.broadcasted_iota(jnp.int32, (8, c), 0)

    def prefix(x):
        s = 1
        while s < c:
            x = x + jnp.where(lane_c >= s, pltpu.roll(x, s, 1), 0.0)
            s *= 2
        return x

    def suffix(x):
        s = 1
        while s < c:
            x = x + jnp.where(lane_c < c - s, pltpu.roll(x, c - s, 1), 0.0)
            s *= 2
        return x

    def conv(i, cw_ref, t0):
        half = (GDN_CONV - 1) // 2
        acc = None
        for j in range(GDN_CONV):
            tap = xp_ref[i, pl.ds(t0 + (pad - half + j), c), :] * cw_ref[j:j + 1, :]
            acc = tap if acc is None else acc + tap
        return _silu(acc)

    def l2n(x):
        return x * lax.rsqrt(_lane_sum(x * x) + EPS)

    half = lambda i: slice(i * hc, (i + 1) * hc)


    def prologue(ns, chains):
        st = []
        for n in ns:
            t0 = pl.multiple_of(n * c, c)
            st.append(dict(n=n, t0=t0, rows=pl.ds(t0, c)))
        for name, idx, cw_ref in (("q", 0, cwq_ref), ("k", 1, cwk_ref), ("v", 2, cwv_ref)):
            for s in st:
                s[name] = conv(idx, cw_ref, s["t0"])
            yield
        for s in st:
            s["q"], s["k"] = l2n(s["q"]), l2n(s["k"])
        yield
        for s in st:
            s["qs"] = s["q"] * scale
            s["kb"] = s["k"].astype(BF16)
            s["qk"] = _dot_nt(s["qs"].astype(BF16), s["kb"])
            bar = bar_ref[:, s["rows"]]
            g_r = (-LOG2E * jnp.exp(prr_ref[:, 0:1])) * _softplus(bar + prr_ref[:, 1:2])
            s["pre"], s["suf"] = prefix(g_r), suffix(g_r)
            s["tot"] = jnp.sum(g_r, axis=1, keepdims=True)
            packed = jnp.where(sub_c < 2, jax.nn.sigmoid(bar), jnp.where(sub_c == 2, s["pre"], s["suf"]))
            s["cols"] = jnp.concatenate([packed, jnp.zeros((LANES - 8, c), F32)], axis=0).T
            o_ref[s["rows"], :] = jnp.zeros((c, LANES), F32)
        yield
        for d in range(2):
            for s in st:
                s["kkb", d] = _dot_nt((s["k"] * s["cols"][:, d:d + 1]).astype(BF16), s["kb"])
            yield
        for d in range(2):
            first, second = (0, 1) if d == 0 else (1, 0)
            for s in st:
                gr = s["pre"][2:3, :] if d == 0 else s["suf"][3:4, :]
                gc = s["cols"][:, 2 + d:3 + d]
                blocks, attn = {}, {}
                for r, t in ((first, first), (second, second), (second, first)):
                    e = gc[half(r)] - gr[:, half(t)]
                    if r == t:
                        e = e + tri_ref[d]
                    dec = jnp.exp2(e)
                    blocks[r, t] = (s["kkb", d][half(r), half(t)] * dec).astype(BF16)
                    if r == t:
                        dec = dec + tri_ref[2]
                    attn[r, t] = (s["qk"][half(r), half(t)] * dec).astype(BF16)
                attn[first, second] = jnp.zeros((hc, hc), BF16)
                a_full = jnp.concatenate([jnp.concatenate([attn[r, 0], attn[r, 1]], axis=1) for r in range(2)], axis=0)
                diag = [blocks[0, 0], blocks[1, 1]]
                chains.append(dict(n=s["n"], rows=s["rows"], d=d, k=s["k"], v=s["v"],
                                   qs=s["qs"], beta=s["cols"][:, d:d + 1], gc=gc, tot=s["tot"][2 + d:3 + d, :],
                                   m=diag, off=blocks[second, first], a=a_full,
                                   p=[lvl_ref[GDN_LEVELS] - blk * lvl_ref[0] for blk in diag]))
            yield

    def levels(chains):
        for bit in range(1, GDN_LEVELS):
            lvl = lvl_ref[bit]
            for ch in chains:
                ch["x"] = [_dot(p, m * lvl).astype(BF16) for p, m in zip(ch["p"], ch["m"])]
            yield
            for ch in chains:
                ch["p"] = [p - _dot(x, p).astype(BF16) for p, x in zip(ch["p"], ch["x"])]
            yield

    def epilogue(chains):
        order = lambda ch: (0, 1) if ch["d"] == 0 else (1, 0)
        for ch in chains:
            ch["eg"] = jnp.exp2(ch["gc"])
            ch["rhs"] = jnp.concatenate([ch["v"] * ch["beta"], ch["k"] * (ch["beta"] * ch["eg"])], axis=1)
            ch["x1"] = _dot(ch["p"][order(ch)[0]], ch["rhs"][half(order(ch)[0])].astype(BF16))
        yield
        for ch in chains:
            ch["cross"] = _dot(ch["off"], ch["x1"].astype(BF16))
        yield
        for ch in chains:
            second = order(ch)[1]
            ch["x2"] = _dot(ch["p"][second], (ch["rhs"][half(second)] - ch["cross"]).astype(BF16))
        yield
        for ch in chains:
            xs = (ch["x1"], ch["x2"]) if ch["d"] == 0 else (ch["x2"], ch["x1"])
            ch["xb"] = jnp.concatenate(xs, axis=0).astype(BF16)
            kd = ch["k"] * jnp.exp2(ch["tot"] - ch["gc"])
            ch["kx"] = _dot(kd.T.astype(BF16), ch["xb"])
            ch["ax"] = _dot(ch["a"], ch["xb"])
        yield
        for ch in chains:
            d, rows = ch["d"], ch["rows"]
            blk = pl.ds(pl.multiple_of(ch["n"] * GDN_DK, GDN_DK), GDN_DK)
            b_ref[d, blk, :] = ch["kx"][:, :GDN_DV]
            kw_ref[d, blk, :] = ch["kx"][:, GDN_DV:].astype(BF16)
            o_ref[rows, :] = o_ref[rows, :] + ch["ax"][:, :GDN_DV]
            qd_ref[d, rows, :] = (ch["qs"] * ch["eg"] - ch["ax"][:, GDN_DV:]).astype(BF16)
            egl_ref[d, pl.ds(pl.multiple_of(ch["n"] * 8, 8), 8), :] = jnp.broadcast_to(jnp.exp2(ch["tot"]), (8, LANES))
        yield

    def interleave(*gens):
        gens = list(gens)
        while gens:
            for g in list(gens):
                if next(g, StopIteration) is StopIteration:
                    gens.remove(g)

    def prep(n, carry):
        base = n * 2 * GDN_PREP_CHUNKS
        group_a = [base + i for i in range(GDN_PREP_CHUNKS)]
        group_b = [base + GDN_PREP_CHUNKS + i for i in range(GDN_PREP_CHUNKS)]
        chains_a, chains_b = [], []
        interleave(prologue(group_a, chains_a))
        interleave(levels(chains_a), prologue(group_b, chains_b))
        interleave(levels(chains_b), epilogue(chains_a))
        interleave(epilogue(chains_b))
        return carry

    lax.fori_loop(0, nc // (2 * GDN_PREP_CHUNKS), prep, 0)

    def scan(n, carry):
        states = list(carry)
        for d in range(2):
            idx = n if d == 0 else nc - 1 - n
            t0 = pl.multiple_of(idx * c, c)
            rows = pl.ds(t0, c)
            blk = pl.ds(pl.multiple_of(idx * GDN_DK, GDN_DK), GDN_DK)
            s = states[d]
            sb = s.astype(BF16)
            o_ref[rows, :] = o_ref[rows, :] + _dot(qd_ref[d, rows, :], sb)
            egl = egl_ref[d, pl.ds(pl.multiple_of(idx * 8, 8), 8), :][0:1, :]
            states[d] = s * egl - _dot(kw_ref[d, blk, :], sb) + b_ref[d, blk, :]
        return tuple(states)

    zero = jnp.zeros((GDN_DK, GDN_DV), F32)
    lax.fori_loop(0, nc, scan, (zero, zero))

    def fin(n, carry):
        rows = pl.ds(pl.multiple_of(n * c, c), c)
        y = _rms(o_ref[rows, :], gain_ref[...]) * _silu(gate_ref[rows, :])
        y_ref[rows, :] = y.astype(y_ref.dtype)
        return carry

    lax.fori_loop(0, nc, fin, 0)


def _gdn(h, bar, conv_w, prr, gain, bsz, seq):
    nh = GDN_HEADS
    blk = lambda off: pl.BlockSpec((seq, LANES), lambda b, hh, off=off: (b, off // LANES + hh))
    cw = lambda off: pl.BlockSpec((GDN_CONV, LANES), lambda b, hh, off=off: (0, off // LANES + hh))
    c = GDN_CHUNK
    assert seq % (2 * GDN_PREP_CHUNKS * c) == 0
    return pl.pallas_call(
        _gdn_kernel,
        grid=(bsz, nh),
        in_specs=[
            blk(EV_GQ), blk(EV_GK), blk(EV_GV), blk(EV_GG),
            pl.BlockSpec((None, None, 8, seq), lambda b, hh: (b, hh, 0, 0)),
            cw(0), cw(GDN_HEADS * GDN_DK), cw(2 * GDN_HEADS * GDN_DK),
            pl.BlockSpec((None, 8, 2), lambda b, hh: (hh, 0, 0)),
            pl.BlockSpec((1, LANES), lambda b, hh: (0, 0)),
        ],
        out_specs=pl.BlockSpec((seq, LANES), lambda b, hh: (b, hh)),
        out_shape=jax.ShapeDtypeStruct((bsz * seq, nh * GDN_DV), BF16),
        scratch_shapes=[
            pltpu.VMEM((3, seq + 16, LANES), F32),
            pltpu.VMEM((2, seq, LANES), BF16),
            pltpu.VMEM((2, (seq // c) * GDN_DK, GDN_DV), BF16),
            pltpu.VMEM((2, (seq // c) * GDN_DK, GDN_DV), F32),
            pltpu.VMEM((2, (seq // c) * 8, LANES), F32),
            pltpu.VMEM((seq, LANES), F32),
            pltpu.VMEM((GDN_LEVELS + 1, c // 2, c // 2), BF16),
            pltpu.VMEM((3, c // 2, c // 2), F32),
        ],
        compiler_params=_cparams(("parallel", "parallel")),
        name="gdn",
    )(h, h, h, h, bar, conv_w, conv_w, conv_w, prr, gain.reshape(1, LANES))


def _flash_scratch(nmaps, tq, tk, dv):
    per_map = [pltpu.VMEM((tq, tk), F32), pltpu.VMEM((tq, tk), F32), pltpu.VMEM((tq, tk), BF16),
               pltpu.VMEM((tq, LANES), F32), pltpu.VMEM((tq, LANES), F32), pltpu.VMEM((tq, dv), F32)]
    return per_map * nmaps


def _flash_core(qs, k_ref, v_ref, bias_of, scratch):
    nmaps = len(qs)
    maps = [scratch[6 * i:6 * i + 6] for i in range(nmaps)]
    tq, tk = maps[0][0].shape
    nk = k_ref.shape[0] // tk
    assert nk % 2 == 0 and tk % LANES == 0 and tq % ATT_RB == 0
    nlb = tk // LANES

    for s0, s1, p, m, l, acc in maps:
        p[...] = jnp.zeros(p.shape, p.dtype)
        m[...] = jnp.full(m.shape, NEG, F32)
        l[...] = jnp.zeros(l.shape, F32)
        acc[...] = jnp.zeros(acc.shape, F32)

    def chunk(ref, j):
        return ref[pl.ds(pl.multiple_of(j * tk, tk), tk), :]

    def qk(i, j, slot):
        maps[i][slot][...] = _dot_nt(qs[i], chunk(k_ref, j))

    def pv(i, j):
        acc = maps[i][5]
        acc[...] = acc[...] + _dot(maps[i][2][...], chunk(v_ref, j))

    def softmax(i, j, slot):
        s_ref, p_ref, m_ref, l_ref, acc_ref = maps[i][slot], maps[i][2], maps[i][3], maps[i][4], maps[i][5]
        bias = bias_of(j)
        for rb in range(tq // ATT_RB):
            r = slice(rb * ATT_RB, (rb + 1) * ATT_RB)
            s = s_ref[r, :]
            if bias is not None:
                s = s + bias[r, :]
            blocks = [s[:, b * LANES:(b + 1) * LANES] for b in range(nlb)]
            mx = functools.reduce(jnp.maximum, blocks)
            m_old = m_ref[r, :]
            m_new = jnp.maximum(m_old, jnp.broadcast_to(jnp.max(mx, axis=-1, keepdims=True), m_old.shape))
            alpha = jnp.exp2(m_old - m_new)
            ps = [jnp.exp2(b - m_new) for b in blocks]
            row_sum = jnp.sum(functools.reduce(jnp.add, ps), axis=-1, keepdims=True)
            l_ref[r, :] = alpha * l_ref[r, :] + jnp.broadcast_to(row_sum, m_old.shape)
            m_ref[r, :] = m_new
            acc_ref[r, :] = acc_ref[r, :] * alpha
            p_ref[r, :] = jnp.concatenate(ps, axis=1).astype(BF16)

    for i in range(nmaps):
        qk(i, 0, 0)

    def step(j, slot):
        nxt = jnp.minimum(j + 1, nk - 1)
        for i in range(nmaps):
            if i == 0:
                pv(nmaps - 1, jnp.maximum(j - 1, 0))
            else:
                pv(i - 1, j)
            qk(i, nxt, 1 - slot)
            softmax(i, j, slot)

    def body(jj, carry):
        step(2 * jj, 0)
        step(2 * jj + 1, 1)
        return carry

    lax.fori_loop(0, nk // 2, body, 0)
    pv(nmaps - 1, nk - 1)
    return [(mp[5][...], mp[4][...]) for mp in maps]


def _diff_prep_kernel(q_ref, k_ref, v_ref, qg_ref, kg_ref, qo_ref, ko_ref, vo_ref):
    r = lax.broadcasted_iota(jnp.int32, (LANES, LANES), 0) < DIFF_DQK
    cc = lax.broadcasted_iota(jnp.int32, (LANES, LANES), 1) < DIFF_DQK
    half_mean = jnp.where(r == cc, 1.0 / DIFF_DQK, 0.0).astype(BF16)

    def halfnorm(x, gain):
        x = x.astype(F32)
        return x * lax.rsqrt(_dot((x * x).astype(BF16), half_mean) + EPS) * gain

    for hh in range(DIFF_HEADS):
        sl = slice(hh * LANES, (hh + 1) * LANES)
        qo_ref[:, sl] = (halfnorm(q_ref[:, sl], qg_ref[...]) * (DIFF_DQK ** -0.5 * LOG2E)).astype(BF16)
        ko_ref[:, sl] = halfnorm(k_ref[:, sl], kg_ref[...]).astype(BF16)
    vo_ref[...] = v_ref[...].astype(BF16)


def _diff_prep(h, qg, kg, tm):
    m = h.shape[0]
    w = DIFF_HEADS * LANES
    spec = lambda off: pl.BlockSpec((tm, w), lambda i, off=off: (i, off // w))
    vec = pl.BlockSpec((1, LANES), lambda i: (0, 0))
    out = jax.ShapeDtypeStruct((m, w), BF16)
    return pl.pallas_call(
        _diff_prep_kernel,
        grid=(m // tm,),
        in_specs=[spec(EV_DQ), spec(EV_DK), spec(EV_DV), vec, vec],
        out_specs=[pl.BlockSpec((tm, w), lambda i: (i, 0))] * 3,
        out_shape=[out, out, out],
        compiler_params=_cparams(("parallel",)),
        name="diff_prep",
    )(h, h, h, jnp.tile(qg, 2).reshape(1, LANES), jnp.tile(kg, 2).reshape(1, LANES))


def _toeplitz(vec, tq, tk):
    w = vec.shape[-1]
    full = pltpu.roll(jnp.broadcast_to(vec, (tq, w)), w - (tq - 1), 1, stride=1, stride_axis=0)
    return full[:, :tk]


def _diff_flash_kernel(q_ref, k_ref, v_ref, gate_ref, bvec_ref, lam_ref, sub_ref, y_ref, bias_ref, *scratch,
                       lambda_init):
    tq = q_ref.shape[0]
    tk = bias_ref.shape[-1]
    qi = pl.program_id(2)

    @pl.when(qi == 0)
    def _():
        for d in range(bias_ref.shape[0]):
            bias_ref[d] = _toeplitz(bvec_ref[d:d + 1, :], tq, tk)

    q = q_ref[...]
    lane = lax.broadcasted_iota(jnp.int32, (1, LANES), 1)
    zero = jnp.zeros_like(q)
    q0 = jnp.where(lane < DIFF_DQK, q, zero)
    q1 = jnp.where(lane < DIFF_DQK, zero, q)

    def bias_of(j):
        return bias_ref.at[jnp.clip(j - qi, -2, 2) + 2]

    (a0, l0), (a1, l1) = _flash_core([q0, q1], k_ref, v_ref, bias_of, scratch)

    lam = lam_ref[...]
    lam_full = (jnp.exp(jnp.sum(lam[0:1] * lam[1:2], axis=-1, keepdims=True))
                - jnp.exp(jnp.sum(lam[2:3] * lam[3:4], axis=-1, keepdims=True)) + lambda_init)
    o = a0 / l0 - lam_full * (a1 / l1)
    o = _rms(o, sub_ref[...]) * (1.0 - lambda_init)
    y_ref[...] = (o * _silu(gate_ref[...])).astype(y_ref.dtype)


def _diff_flash(dq, dk, dv, h, bvec, lam, subln, lambda_init, bsz, seq):
    tq, tk = ATT_TQ, ATT_TK
    nq = seq // tq
    nh = DIFF_HEADS
    return pl.pallas_call(
        functools.partial(_diff_flash_kernel, lambda_init=lambda_init),
        grid=(bsz, nh, nq),
        in_specs=[
            pl.BlockSpec((tq, LANES), lambda b, hh, i: (b * nq + i, hh)),
            pl.BlockSpec((seq, LANES), lambda b, hh, i: (b, hh)),
            pl.BlockSpec((seq, LANES), lambda b, hh, i: (b, hh)),
            pl.BlockSpec((tq, LANES), lambda b, hh, i: (b * nq + i, EV_DG // LANES + hh)),
            pl.BlockSpec((None,) + bvec.shape[1:], lambda b, hh, i: (hh, 0, 0)),
            pl.BlockSpec((4, DIFF_DQK), lambda b, hh, i: (0, 0)),
            pl.BlockSpec((1, LANES), lambda b, hh, i: (0, 0)),
        ],
        out_specs=pl.BlockSpec((tq, LANES), lambda b, hh, i: (b * nq + i, hh)),
        out_shape=jax.ShapeDtypeStruct((bsz * seq, nh * DIFF_DV), BF16),
        scratch_shapes=[pltpu.VMEM((5, tq, tk), F32)] + _flash_scratch(2, tq, tk, DIFF_DV),
        compiler_params=_cparams(("parallel", "parallel", "arbitrary")),
        name="diff_flash",
    )(dq, dk, dv, h, bvec, lam, subln.reshape(1, LANES))


def _mem_attn_kernel(q_ref, gate_ref, mk_ref, mv_ref, qg_ref, kg_ref, y_ref):
    q = (_rms(q_ref[...], qg_ref[...]) * (MEM_DH ** -0.5)).astype(BF16)
    mk = _rms(mk_ref[...], kg_ref[...]).astype(BF16)
    s = _dot_nt(q, mk)
    p = jnp.exp(s - jnp.max(s, axis=-1, keepdims=True))
    o = _dot(p.astype(BF16), mv_ref[...].astype(BF16)) / jnp.sum(p, axis=-1, keepdims=True)
    y_ref[...] = (o * _silu(gate_ref[...])).astype(y_ref.dtype)


def _mem_attn(h, memkv, qg, kg, q_off, g_off, bsz, seq, tq):
    nq = seq // tq
    nh = MEM_HEADS
    vec = pl.BlockSpec((1, LANES), lambda b, hh, i: (0, 0))
    return pl.pallas_call(
        _mem_attn_kernel,
        grid=(bsz, nh, nq),
        in_specs=[
            pl.BlockSpec((tq, LANES), lambda b, hh, i: (b * nq + i, q_off // LANES + hh)),
            pl.BlockSpec((tq, LANES), lambda b, hh, i: (b * nq + i, g_off // LANES + hh)),
            pl.BlockSpec((MEM_LEN, LANES), lambda b, hh, i: (b, hh)),
            pl.BlockSpec((MEM_LEN, LANES), lambda b, hh, i: (b, nh + hh)),
            vec, vec,
        ],
        out_specs=pl.BlockSpec((tq, LANES), lambda b, hh, i: (b * nq + i, hh)),
        out_shape=jax.ShapeDtypeStruct((bsz * seq, nh * MEM_DH), BF16),
        compiler_params=_cparams(("parallel", "parallel", "parallel")),
        name="mem_attn",
    )(h, h, memkv, memkv, qg.reshape(1, LANES), kg.reshape(1, LANES))


def _swa_kernel(sink_ref, q_ref, k_ref, v_ref, gate_ref, bias_ref, qg_ref, kg_ref, y_ref, tile_ref):
    tq = q_ref.shape[0]
    seq = k_ref.shape[0]
    grp = SWA_HEADS // SWA_KV_HEADS
    kvh = pl.program_id(1)
    qi = pl.program_id(2)
    nq = pl.num_programs(2)

    @pl.when(jnp.logical_or(qi <= 1, qi == nq - 1))
    def _():
        for g in range(grp):
            tile_ref[g] = _toeplitz(bias_ref[g:g + 1, :], tq, SWA_TW)

    ws = pl.multiple_of(jnp.clip(qi * tq - WINDOW, 0, seq - SWA_TW), WINDOW)
    kw = _rms(k_ref[pl.ds(ws, SWA_TW), :], kg_ref[...]).astype(BF16)
    vw = v_ref[pl.ds(ws, SWA_TW), :].astype(BF16)
    heads = [slice(g * SWA_DH, (g + 1) * SWA_DH) for g in range(grp)]
    qs = [(_rms(q_ref[:, sl], qg_ref[...]) * (SWA_DH ** -0.5)).astype(BF16) for sl in heads]
    ss = [_dot_nt(q, kw) + tile_ref[g] for g, q in enumerate(qs)]
    ps, dens = [], []
    for g, s in enumerate(ss):
        sink = sink_ref[kvh * grp + g]
        mx = jnp.maximum(jnp.max(s, axis=-1, keepdims=True), sink)
        p = jnp.exp(s - mx)
        dens.append(jnp.sum(p, axis=-1, keepdims=True) + jnp.exp(sink - mx))
        ps.append(p.astype(BF16))
    for sl, p, den in zip(heads, ps, dens):
        y_ref[:, sl] = (_dot(p, vw) / den * _silu(gate_ref[:, sl])).astype(y_ref.dtype)


def _swa(h, bias, sink, qg, kg, bsz, seq):
    tq = SWA_TQ
    nq = seq // tq
    grp = SWA_HEADS // SWA_KV_HEADS
    gw = grp * SWA_DH
    assert seq >= SWA_TW and nq >= 2
    vec = pl.BlockSpec((1, LANES), lambda b, kv, i: (0, 0))

    def bias_idx(b, kv, i):
        return (jnp.where(i == 0, 0, jnp.where(i == nq - 1, 2, 1)), kv, 0, 0)

    return pl.pallas_call(
        _swa_kernel,
        grid=(bsz, SWA_KV_HEADS, nq),
        in_specs=[
            pl.BlockSpec(memory_space=pltpu.SMEM),
            pl.BlockSpec((tq, gw), lambda b, kv, i: (b * nq + i, OD_SQ // gw + kv)),
            pl.BlockSpec((seq, LANES), lambda b, kv, i: (b, OD_SK // LANES + kv)),
            pl.BlockSpec((seq, LANES), lambda b, kv, i: (b, OD_SV // LANES + kv)),
            pl.BlockSpec((tq, gw), lambda b, kv, i: (b * nq + i, OD_SG // gw + kv)),
            pl.BlockSpec((None, None, grp, bias.shape[-1]), bias_idx),
            vec, vec,
        ],
        out_specs=pl.BlockSpec((tq, gw), lambda b, kv, i: (b * nq + i, kv)),
        out_shape=jax.ShapeDtypeStruct((bsz * seq, SWA_HEADS * SWA_DH), BF16),
        scratch_shapes=[pltpu.VMEM((grp, tq, SWA_TW), F32)],
        compiler_params=_cparams(("parallel", "parallel", "arbitrary")),
        name="swa",
    )(sink, h, h, h, h, bias, qg.reshape(1, LANES), kg.reshape(1, LANES))


def _mla_prep_kernel(q_ref, ckv_ref, kr_ref, cos_ref, sin_ref, kvg_ref, wup_ref,
                     qgn_ref, qgr_ref, kgn_ref, kgr_ref, qo_ref, ko_ref, vo_ref):
    dqk = MLA_NOPE + MLA_ROPE
    lane = lax.broadcasted_iota(jnp.int32, (1, LANES), 1)
    lo = lane < MLA_ROPE
    first = lane < MLA_ROPE // 2
    cos = cos_ref[...]
    sin = sin_ref[...]

    def rope(t):
        rot = jnp.where(first, pltpu.roll(t, LANES - MLA_ROPE // 2, 1), pltpu.roll(t, MLA_ROPE // 2, 1))
        return t * cos + rot * sin

    ckv = _rms(ckv_ref[...], kvg_ref[...]).astype(BF16)
    kv = _dot(ckv, wup_ref[...])
    kr = kr_ref[...]
    kr2 = kr * kr
    c = dqk ** -0.5 * LOG2E
    hw = MLA_NOPE + MLA_DV
    for hh in range(MLA_HEADS):
        kn = kv[:, hh * hw:hh * hw + MLA_NOPE]
        inv = lax.rsqrt(_lane_sum(kn * kn + kr2, 1.0 / dqk) + EPS)
        ko_ref[:, hh * 2 * LANES:hh * 2 * LANES + LANES] = (kn * inv * kgn_ref[...]).astype(BF16)
        ko_ref[:, hh * 2 * LANES + LANES:(hh + 1) * 2 * LANES] = rope(kr * inv * kgr_ref[...]).astype(BF16)
        vo_ref[:, hh * LANES:(hh + 1) * LANES] = kv[:, hh * hw + MLA_NOPE:(hh + 1) * hw].astype(BF16)

        qn = q_ref[:, hh * LANES:(hh + 1) * LANES].astype(F32)
        pair = q_ref[:, OD_MLQ_ROPE + (hh // 2) * LANES:OD_MLQ_ROPE + (hh // 2 + 1) * LANES].astype(F32)
        if hh % 2 == 1:
            pair = pltpu.roll(pair, MLA_ROPE, 1)
        qr = jnp.where(lo, pair, 0.0)
        inv = lax.rsqrt(_lane_sum(qn * qn + qr * qr, 1.0 / dqk) + EPS)
        qo_ref[:, hh * 2 * LANES:hh * 2 * LANES + LANES] = (qn * inv * qgn_ref[...] * c).astype(BF16)
        qo_ref[:, hh * 2 * LANES + LANES:(hh + 1) * 2 * LANES] = (rope(qr * inv * qgr_ref[...]) * c).astype(BF16)


def _mla_prep(h, side, cos, sin, kv_gain, w_up, qg, kg, tm):
    m = h.shape[0]
    nh = MLA_HEADS
    qw = nh * (MLA_NOPE + MLA_ROPE)
    pad = lambda g: jnp.pad(g[MLA_NOPE:], (0, LANES - MLA_ROPE)).reshape(1, LANES)
    vec = pl.BlockSpec((1, LANES), lambda i: (0, 0))
    row = lambda i: (i, 0)
    return pl.pallas_call(
        _mla_prep_kernel,
        grid=(m // tm,),
        in_specs=[
            pl.BlockSpec((tm, qw), row),
            pl.BlockSpec((tm, MLA_KV_RANK), lambda i: (i, OD_CKV // MLA_KV_RANK)),
            pl.BlockSpec((tm, LANES), row),
            pl.BlockSpec((tm, LANES), row),
            pl.BlockSpec((tm, LANES), row),
            pl.BlockSpec((1, MLA_KV_RANK), lambda i: (0, 0)),
            pl.BlockSpec(w_up.shape, lambda i: (0, 0)),
            vec, vec, vec, vec,
        ],
        out_specs=[
            pl.BlockSpec((tm, nh * 2 * LANES), row),
            pl.BlockSpec((tm, nh * 2 * LANES), row),
            pl.BlockSpec((tm, nh * MLA_DV), row),
        ],
        out_shape=[
            jax.ShapeDtypeStruct((m, nh * 2 * LANES), BF16),
            jax.ShapeDtypeStruct((m, nh * 2 * LANES), BF16),
            jax.ShapeDtypeStruct((m, nh * MLA_DV), BF16),
        ],
        compiler_params=_cparams(("parallel",)),
        name="mla_prep",
    )(h, h, side, cos, sin, kv_gain.reshape(1, MLA_KV_RANK), w_up.astype(BF16),
      qg[:MLA_NOPE].reshape(1, LANES), pad(qg), kg[:MLA_NOPE].reshape(1, LANES), pad(kg))


def _mla_flash_kernel(q_ref, k_ref, v_ref, gate_ref, y_ref, *scratch):
    (acc, l), = _flash_core([q_ref[...]], k_ref, v_ref, lambda j: None, scratch)
    y_ref[...] = (acc / l * _silu(gate_ref[...])).astype(y_ref.dtype)


def _mla_flash(qm, km, vm, h, bsz, seq):
    tq = ATT_TQ
    nq = seq // tq
    nh = MLA_HEADS
    return pl.pallas_call(
        _mla_flash_kernel,
        grid=(bsz, nh, nq),
        in_specs=[
            pl.BlockSpec((tq, 2 * LANES), lambda b, hh, i: (b * nq + i, hh)),
            pl.BlockSpec((seq, 2 * LANES), lambda b, hh, i: (b, hh)),
            pl.BlockSpec((seq, LANES), lambda b, hh, i: (b, hh)),
            pl.BlockSpec((tq, LANES), lambda b, hh, i: (b * nq + i, OD_MLG // LANES + hh)),
        ],
        out_specs=pl.BlockSpec((tq, LANES), lambda b, hh, i: (b * nq + i, hh)),
        out_shape=jax.ShapeDtypeStruct((bsz * seq, nh * MLA_DV), BF16),
        scratch_shapes=_flash_scratch(1, tq, ATT_TK, MLA_DV),
        compiler_params=_cparams(("parallel", "parallel", "arbitrary")),
        name="mla_flash",
    )(qm, km, vm, h)


def _t5_bucket(rel):
    half = REL_BUCKETS // 2
    max_exact = half // 2
    ret = (rel > 0).astype(jnp.int32) * half
    n = jnp.abs(rel)
    nf = jnp.maximum(n, 1).astype(F32)
    large = max_exact + (jnp.log(nf / max_exact) / math.log(REL_MAX_DIST / max_exact)
                         * (half - max_exact)).astype(jnp.int32)
    large = jnp.minimum(large, half - 1)
    return ret + jnp.where(n < max_exact, n, large)


def _round_up(n, m):
    return (n + m - 1) // m * m


def _diff_bias_vecs(rel_table, tq, tk):
    assert tq == tk and tk + 1 >= REL_MAX_DIST
    m = jnp.arange(_round_up(tq + tk - 1, LANES))
    rows = [rel_table[_t5_bucket(m - (tq - 1) + d * tk)] for d in (-2, -1, 0, 1, 2)]
    return jnp.pad(jnp.stack(rows, axis=0).transpose(2, 0, 1) * LOG2E, ((0, 0), (0, 3), (0, 0))).astype(F32)


def _swa_bias_vecs(rel_table):
    m = jnp.arange(_round_up(SWA_TQ + SWA_TW - 1, LANES))
    rows = []
    for shift in (0, WINDOW, SWA_TW - SWA_TQ):
        rel = m - (SWA_TQ - 1) - shift
        rows.append(jnp.where((jnp.abs(rel) <= WINDOW)[None], rel_table[_t5_bucket(rel)].T, NEG))
    return jnp.stack(rows, axis=0).reshape(3, SWA_KV_HEADS, SWA_HEADS // SWA_KV_HEADS, -1).astype(F32)


def _od_main_weight(wt):
    mlq_lo, mlq_hi = 2560, 4096
    mlq = wt[mlq_lo:mlq_hi].reshape(MLA_HEADS, MLA_NOPE + MLA_ROPE, -1)
    nope = mlq[:, :MLA_NOPE].reshape(MLA_HEADS * MLA_NOPE, -1)
    rope = mlq[:, MLA_NOPE:].reshape(MLA_HEADS * MLA_ROPE, -1)
    return jnp.concatenate([nope, rope, wt[:mlq_lo], wt[mlq_hi:OD_KR_LO], wt[OD_KR_HI:]], axis=0).astype(BF16)


def _rope_tables(positions):
    half = MLA_ROPE // 2
    inv_freq = ROPE_BASE ** (-jnp.arange(half, dtype=F32) / half)
    ang = positions.astype(F32).reshape(-1, 1) * inv_freq
    cos, sin = jnp.cos(ang), jnp.sin(ang)
    z = jnp.zeros((ang.shape[0], LANES - MLA_ROPE), F32)
    return jnp.concatenate([cos, cos, z], axis=1), jnp.concatenate([-sin, sin, z], axis=1)


def _side_weight(wt, lo, hi):
    return jnp.pad(wt[lo:hi], ((0, LANES - (hi - lo)), (0, 0))).astype(BF16)


def _mem_kv(mem2, gain, w_kv):
    zero_side = jnp.zeros((LANES, D_MODEL), BF16)
    memkv, _ = _norm_proj(mem2, gain, jnp.swapaxes(w_kv, 0, 1).astype(BF16), zero_side, tm=mem2.shape[0] // 2,
                          tn=512, out_dtype=F32)
    return memkv


def _even_layer(x2, mem2, rel_bias, norm_g, w_in, conv_w, a_log, dt_bias, gdn_gain, dq_gain, dk_gain, lam, subln,
                mem_norm, mem_w_kv, mem_qn, mem_kn, w_out, lambda_init, bsz, seq):
    wt = jnp.swapaxes(w_in, 0, 1)
    w_main = jnp.concatenate([wt[:EV_SIDE_LO], wt[EV_SIDE_HI:]], axis=0).astype(BF16)
    h, side = _norm_proj(x2, norm_g, w_main, _side_weight(wt, EV_SIDE_LO, EV_SIDE_HI), tm=1024, tn=EV_MAIN // 6,
                         out_dtype=BF16)

    nh = GDN_HEADS
    bar = side[:, :4 * nh].reshape(bsz, seq, 4, nh).transpose(0, 3, 2, 1)
    bar = jnp.pad(bar, ((0, 0), (0, 0), (0, 4), (0, 0)))
    prr = jnp.pad(jnp.stack([a_log.T, dt_bias.T], axis=-1), ((0, 0), (2, 4), (0, 0)))
    ya = _gdn(h, bar, conv_w, prr, gdn_gain, bsz, seq)

    dq, dk, dv = _diff_prep(h, dq_gain, dk_gain, tm=512)
    bias = _diff_bias_vecs(rel_bias, ATT_TQ, ATT_TK)
    yb = _diff_flash(dq, dk, dv, h, bias, lam, subln, lambda_init, bsz, seq)

    memkv = _mem_kv(mem2, mem_norm, mem_w_kv)
    ym = _mem_attn(h, memkv, mem_qn, mem_kn, EV_MQ, EV_MG, bsz, seq, tq=1024)
    return _out_proj(x2, ya, yb, ym, w_out, tm=512)


def _odd_layer(x2, mem2, positions, rel_bias, norm_g, w_in, swa_qn, swa_kn, sink, kv_norm, w_kv_up, mla_qn, mla_kn,
               mem_norm, mem_w_kv, mem_qn, mem_kn, w_out, bsz, seq):
    wt = jnp.swapaxes(w_in, 0, 1)
    h, side = _norm_proj(x2, norm_g, _od_main_weight(wt), _side_weight(wt, OD_KR_LO, OD_KR_HI), tm=1024,
                         tn=OD_MAIN // 4, out_dtype=BF16)

    ya = _swa(h, _swa_bias_vecs(rel_bias), sink, swa_qn, swa_kn, bsz, seq)

    cos, sin = _rope_tables(positions)
    qm, km, vm = _mla_prep(h, side, cos, sin, kv_norm, w_kv_up, mla_qn, mla_kn, tm=256)
    yb = _mla_flash(qm, km, vm, h, bsz, seq)

    memkv = _mem_kv(mem2, mem_norm, mem_w_kv)
    ym = _mem_attn(h, memkv, mem_qn, mem_kn, OD_MQ, OD_MG, bsz, seq, tq=1024)
    return _out_proj(x2, ya, yb, ym, w_out, tm=512)


def kernel(x, mem, positions, rel_bias, ev_norm, ev_w_in, ev_conv, ev_a_log, ev_dt_bias, ev_gdn_norm, ev_diff_qnorm, ev_diff_knorm, ev_diff_lambda, ev_diff_subln, ev_mem_norm, ev_mem_w_kv, ev_mem_qnorm, ev_mem_knorm, ev_w_out, od_norm, od_w_in, od_swa_qnorm, od_swa_knorm, od_swa_sink, od_mla_kv_norm, od_mla_w_kv_up, od_mla_qnorm, od_mla_knorm, od_mem_norm, od_mem_w_kv, od_mem_qnorm, od_mem_knorm, od_w_out):
    bsz, seq, d = x.shape
    depth = ev_norm.shape[0] + od_norm.shape[0]
    x2 = x.reshape(bsz * seq, d)
    mem2 = mem.reshape(bsz * mem.shape[1], d)
    for layer in range(depth):
        i = layer // 2
        if layer % 2 == 0:
            lambda_init = 0.8 - 0.6 * math.exp(-0.3 * layer)
            x2 = _even_layer(x2, mem2, rel_bias, ev_norm[i], ev_w_in[i], ev_conv[i], ev_a_log[i], ev_dt_bias[i],
                             ev_gdn_norm[i], ev_diff_qnorm[i], ev_diff_knorm[i], ev_diff_lambda[i], ev_diff_subln[i],
                             ev_mem_norm[i], ev_mem_w_kv[i], ev_mem_qnorm[i], ev_mem_knorm[i], ev_w_out[i],
                             lambda_init, bsz, seq)
        else:
            x2 = _odd_layer(x2, mem2, positions, rel_bias, od_norm[i], od_w_in[i], od_swa_qnorm[i], od_swa_knorm[i],
                            od_swa_sink[i], od_mla_kv_norm[i], od_mla_w_kv_up[i], od_mla_qnorm[i], od_mla_knorm[i],
                            od_mem_norm[i], od_mem_w_kv[i], od_mem_qnorm[i], od_mem_knorm[i], od_w_out[i], bsz, seq)
    return x2.reshape(bsz, seq, d)
```

```python
import functools
import math

import jax
import jax.numpy as jnp
from jax import lax
from jax.experimental import pallas as pl
from jax.experimental.pallas import tpu as pltpu

F32 = jnp.float32
BF16 = jnp.bfloat16
EPS = 1e-6
NEG = -1e30

V7X_VMEM_BYTES = 64 * 1024 * 1024
VMEM_LIMIT = V7X_VMEM_BYTES - 8 * 1024 * 1024
LANES = 128
MXU_COLS = 256

D_MODEL = 2048
MEM_LEN = 256
GDN_HEADS, GDN_DK, GDN_DV, GDN_CONV = 8, 128, 128, 5
DIFF_HEADS, DIFF_DQK, DIFF_DV = 8, 64, 128
SWA_HEADS, SWA_KV_HEADS, SWA_DH, WINDOW = 8, 2, 128, 128
MLA_HEADS, MLA_NOPE, MLA_ROPE, MLA_DV, MLA_KV_RANK = 8, 128, 64, 128, 512
ROPE_BASE = 10000.0
MEM_HEADS, MEM_DH = 4, 128
REL_BUCKETS, REL_MAX_DIST = 32, 128

GDN_CHUNK = 256
GDN_LEVELS = (GDN_CHUNK // 2).bit_length() - 1
GDN_PREP_CHUNKS = 2
ATT_TQ = 512
ATT_TK = 512
ATT_RB = 32
LOG2E = math.log2(math.e)
SWA_TQ = 256
SWA_TW = SWA_TQ + 2 * WINDOW

EV_GQ, EV_GK, EV_GV, EV_GG = 0, 1024, 2048, 3072
EV_DQ, EV_DK, EV_DV, EV_DG = 4096, 5120, 6144, 7168
EV_MQ, EV_MG = 8192, 8704
EV_MAIN = 9216
EV_SIDE_LO, EV_SIDE_HI = 3072, 3104

OD_MLQ_NOPE, OD_MLQ_ROPE = 0, 1024
OD_SQ, OD_SK, OD_SV, OD_SG = 1536, 2560, 2816, 3072
OD_CKV, OD_MLG, OD_MQ, OD_MG = 4096, 4608, 5632, 6144
OD_MAIN = 6656
OD_KR_LO, OD_KR_HI = 4608, 4672


def _cparams(sem):
    return pltpu.CompilerParams(dimension_semantics=sem, vmem_limit_bytes=VMEM_LIMIT)


def _dot(a, b):
    return jnp.dot(a, b, preferred_element_type=F32)


def _dot_nt(a, b):
    return lax.dot_general(a, b, (((1,), (1,)), ((), ())), preferred_element_type=F32)


def _silu(x):
    x = x.astype(F32)
    return x * jax.nn.sigmoid(x)


def _softplus(x):
    return jnp.maximum(x, 0.0) + jnp.log(1.0 + jnp.exp(-jnp.abs(x)))


def _lane_sum(x, scale=1.0):
    return _dot(x.astype(BF16), jnp.ones((LANES, LANES), BF16)) * scale


def _rms(x, gain):
    x = x.astype(F32)
    if x.shape[-1] == LANES:
        ms = _lane_sum(x * x, 1.0 / LANES)
    else:
        ms = jnp.mean(x * x, axis=-1, keepdims=True)
    return x * lax.rsqrt(ms + EPS) * gain


def _norm_proj_kernel(x_ref, g_ref, w_ref, ws_ref, o_ref, os_ref, xn_ref):
    @pl.when(pl.program_id(1) == 0)
    def _():
        xn = _rms(x_ref[...], g_ref[...]).astype(BF16)
        xn_ref[...] = xn
        os_ref[...] = _dot_nt(xn, ws_ref[...])

    o_ref[...] = _dot_nt(xn_ref[...], w_ref[...]).astype(o_ref.dtype)


def _norm_proj(x, gain, w_main, w_side, tm, tn, out_dtype):
    m, k = x.shape
    n = w_main.shape[0]
    ns = w_side.shape[0]
    assert m % tm == 0 and n % tn == 0
    return pl.pallas_call(
        _norm_proj_kernel,
        grid=(m // tm, n // tn),
        in_specs=[
            pl.BlockSpec((tm, k), lambda i, j: (i, 0)),
            pl.BlockSpec((1, k), lambda i, j: (0, 0)),
            pl.BlockSpec((tn, k), lambda i, j: (j, 0)),
            pl.BlockSpec((ns, k), lambda i, j: (0, 0)),
        ],
        out_specs=[
            pl.BlockSpec((tm, tn), lambda i, j: (i, j)),
            pl.BlockSpec((tm, ns), lambda i, j: (i, 0)),
        ],
        out_shape=[jax.ShapeDtypeStruct((m, n), out_dtype), jax.ShapeDtypeStruct((m, ns), F32)],
        scratch_shapes=[pltpu.VMEM((tm, k), BF16)],
        compiler_params=_cparams(("parallel", "arbitrary")),
        name="norm_proj",
    )(x, gain.reshape(1, k), w_main, w_side)


def _out_proj_kernel(x_ref, ya_ref, yb_ref, ym_ref, wa_ref, wb_ref, wm_ref, o_ref):
    acc = _dot(ya_ref[...], wa_ref[...])
    acc = acc + _dot(yb_ref[...], wb_ref[...])
    acc = acc + _dot(ym_ref[...], wm_ref[...])
    o_ref[...] = x_ref[...] + acc


def _out_proj(x, ya, yb, ym, w_out, tm):
    m, d = x.shape
    na, nb, nm = ya.shape[1], yb.shape[1], ym.shape[1]
    wa = w_out[:na].astype(BF16)
    wb = w_out[na:na + nb].astype(BF16)
    wm = w_out[na + nb:].astype(BF16)
    row = lambda i: (i, 0)
    fixed = lambda i: (0, 0)
    return pl.pallas_call(
        _out_proj_kernel,
        grid=(m // tm,),
        in_specs=[
            pl.BlockSpec((tm, d), row), pl.BlockSpec((tm, na), row), pl.BlockSpec((tm, nb), row),
            pl.BlockSpec((tm, nm), row),
            pl.BlockSpec((na, d), fixed), pl.BlockSpec((nb, d), fixed), pl.BlockSpec((nm, d), fixed),
        ],
        out_specs=pl.BlockSpec((tm, d), row),
        out_shape=jax.ShapeDtypeStruct((m, d), F32),
        compiler_params=_cparams(("parallel",)),
        name="out_proj",
    )(x, ya, yb, ym, wa, wb, wm)


def _gdn_kernel(q_ref, k_ref, v_ref, gate_ref, bar_ref, cwq_ref, cwk_ref, cwv_ref, prr_ref, gain_ref, y_ref,
                xp_ref, qd_ref, kw_ref, b_ref, egl_ref, o_ref, lvl_ref, tri_ref):
    seq = q_ref.shape[0]
    c = GDN_CHUNK
    nc = seq // c
    pad = 8
    scale = GDN_DK ** -0.5

    for i, src in enumerate((q_ref, k_ref, v_ref)):
        xp_ref[i, 0:pad, :] = jnp.zeros((pad, LANES), F32)
        xp_ref[i, pad + seq:2 * pad + seq, :] = jnp.zeros((pad, LANES), F32)
        xp_ref[i, pad:pad + seq, :] = src[...].astype(F32)

    hc = c // 2
    row_h = lax.broadcasted_iota(jnp.int32, (hc, hc), 0)
    col_h = lax.broadcasted_iota(jnp.int32, (hc, hc), 1)
    for bit in range(GDN_LEVELS):
        lvl_ref[bit] = (((row_h ^ col_h) >> bit) == 1).astype(BF16)
    lvl_ref[GDN_LEVELS] = (row_h == col_h).astype(BF16)
    tri_ref[0] = jnp.where(row_h > col_h, 0.0, NEG)
    tri_ref[1] = jnp.where(row_h < col_h, 0.0, NEG)
    tri_ref[2] = (row_h == col_h).astype(F32)

    lane_c = lax.broadcasted_iota(jnp.int32, (8, c), 1)
    sub_c = lax.broadcasted_iota(jnp.int32, (8, c), 0)

    def prefix(x):
        s = 1
        while s < c:
            x = x + jnp.where(lane_c >= s, pltpu.roll(x, s, 1), 0.0)
            s *= 2
        return x

    def suffix(x):
        s = 1
        while s < c:
            x = x + jnp.where(lane_c < c - s, pltpu.roll(x, c - s, 1), 0.0)
            s *= 2
        return x

    def conv(i, cw_ref, t0):
        half = (GDN_CONV - 1) // 2
        acc = None
        for j in range(GDN_CONV):
            tap = xp_ref[i, pl.ds(t0 + (pad - half + j), c), :] * cw_ref[j:j + 1, :]
            acc = tap if acc is None else acc + tap
        return _silu(acc)

    def l2n(x):
        return x * lax.rsqrt(_lane_sum(x * x) + EPS)

    half = lambda i: slice(i * hc, (i + 1) * hc)


    def prologue(ns, chains):
        st = []
        for n in ns:
            t0 = pl.multiple_of(n * c, c)
            st.append(dict(n=n, t0=t0, rows=pl.ds(t0, c)))
        for name, idx, cw_ref in (("q", 0, cwq_ref), ("k", 1, cwk_ref), ("v", 2, cwv_ref)):
            for s in st:
                s[name] = conv(idx, cw_ref, s["t0"])
            yield
        for s in st:
            s["q"], s["k"] = l2n(s["q"]), l2n(s["k"])
        yield
        for s in st:
            s["qs"] = s["q"] * scale
            s["kb"] = s["k"].astype(BF16)
            s["qk"] = _dot_nt(s["qs"].astype(BF16), s["kb"])
            bar = bar_ref[:, s["rows"]]
            g_r = (-LOG2E * jnp.exp(prr_ref[:, 0:1])) * _softplus(bar + prr_ref[:, 1:2])
            s["pre"], s["suf"] = prefix(g_r), suffix(g_r)
            s["tot"] = jnp.sum(g_r, axis=1, keepdims=True)
            packed = jnp.where(sub_c < 2, jax.nn.sigmoid(bar), jnp.where(sub_c == 2, s["pre"], s["suf"]))
            s["cols"] = jnp.concatenate([packed, jnp.zeros((LANES - 8, c), F32)], axis=0).T
            o_ref[s["rows"], :] = jnp.zeros((c, LANES), F32)
        yield
        for d in range(2):
            for s in st:
                s["kkb", d] = _dot_nt((s["k"] * s["cols"][:, d:d + 1]).astype(BF16), s["kb"])
            yield
        for d in range(2):
            first, second = (0, 1) if d == 0 else (1, 0)
            for s in st:
                gr = s["pre"][2:3, :] if d == 0 else s["suf"][3:4, :]
                gc = s["cols"][:, 2 + d:3 + d]
                blocks, attn = {}, {}
                for r, t in ((first, first), (second, second), (second, first)):
                    e = gc[half(r)] - gr[:, half(t)]
                    if r == t:
                        e = e + tri_ref[d]
                    dec = jnp.exp2(e)
                    blocks[r, t] = (s["kkb", d][half(r), half(t)] * dec).astype(BF16)
                    if r == t:
                        dec = dec + tri_ref[2]
                    attn[r, t] = (s["qk"][half(r), half(t)] * dec).astype(BF16)
                attn[first, second] = jnp.zeros((hc, hc), BF16)
                a_full = jnp.concatenate([jnp.concatenate([attn[r, 0], attn[r, 1]], axis=1) for r in range(2)], axis=0)
                diag = [blocks[0, 0], blocks[1, 1]]
                chains.append(dict(n=s["n"], rows=s["rows"], d=d, k=s["k"], v=s["v"],
                                   qs=s["qs"], beta=s["cols"][:, d:d + 1], gc=gc, tot=s["tot"][2 + d:3 + d, :],
                                   m=diag, off=blocks[second, first], a=a_full,
                                   p=[lvl_ref[GDN_LEVELS] - blk * lvl_ref[0] for blk in diag]))
            yield

    def levels(chains):
        for bit in range(1, GDN_LEVELS):
            lvl = lvl_ref[bit]
            for ch in chains:
                ch["x"] = [_dot(p, m * lvl).astype(BF16) for p, m in zip(ch["p"], ch["m"])]
            yield
            for ch in chains:
                ch["p"] = [p - _dot(x, p).astype(BF16) for p, x in zip(ch["p"], ch["x"])]
            yield

    def epilogue(chains):
        order = lambda ch: (0, 1) if ch["d"] == 0 else (1, 0)
        for ch in chains:
            ch["eg"] = jnp.exp2(ch["gc"])
            ch["rhs"] = jnp.concatenate([ch["v"] * ch["beta"], ch["k"] * (ch["beta"] * ch["eg"])], axis=1)
            ch["x1"] = _dot(ch["p"][order(ch)[0]], ch["rhs"][half(order(ch)[0])].astype(BF16))
        yield
        for ch in chains:
            ch["cross"] = _dot(ch["off"], ch["x1"].astype(BF16))
        yield
        for ch in chains:
            second = order(ch)[1]
            ch["x2"] = _dot(ch["p"][second], (ch["rhs"][half(second)] - ch["cross"]).astype(BF16))
        yield
        for ch in chains:
            xs = (ch["x1"], ch["x2"]) if ch["d"] == 0 else (ch["x2"], ch["x1"])
            ch["xb"] = jnp.concatenate(xs, axis=0).astype(BF16)
            kd = ch["k"] * jnp.exp2(ch["tot"] - ch["gc"])
            ch["kx"] = _dot(kd.T.astype(BF16), ch["xb"])
            ch["ax"] = _dot(ch["a"], ch["xb"])
        yield
        for ch in chains:
            d, rows = ch["d"], ch["rows"]
            blk = pl.ds(pl.multiple_of(ch["n"] * GDN_DK, GDN_DK), GDN_DK)
            b_ref[d, blk, :] = ch["kx"][:, :GDN_DV]
            kw_ref[d, blk, :] = ch["kx"][:, GDN_DV:].astype(BF16)
            o_ref[rows, :] = o_ref[rows, :] + ch["ax"][:, :GDN_DV]
            qd_ref[d, rows, :] = (ch["qs"] * ch["eg"] - ch["ax"][:, GDN_DV:]).astype(BF16)
            egl_ref[d, pl.ds(pl.multiple_of(ch["n"] * 8, 8), 8), :] = jnp.broadcast_to(jnp.exp2(ch["tot"]), (8, LANES))
        yield

    def interleave(*gens):
        gens = list(gens)
        while gens:
            for g in list(gens):
                if next(g, StopIteration) is StopIteration:
                    gens.remove(g)

    def prep(n, carry):
        base = n * 2 * GDN_PREP_CHUNKS
        group_a = [base + i for i in range(GDN_PREP_CHUNKS)]
        group_b = [base + GDN_PREP_CHUNKS + i for i in range(GDN_PREP_CHUNKS)]
        chains_a, chains_b = [], []
        interleave(prologue(group_a, chains_a))
        interleave(levels(chains_a), prologue(group_b, chains_b))
        interleave(levels(chains_b), epilogue(chains_a))
        interleave(epilogue(chains_b))
        return carry

    lax.fori_loop(0, nc // (2 * GDN_PREP_CHUNKS), prep, 0)

    def scan(n, carry):
        states = list(carry)
        for d in range(2):
            idx = n if d == 0 else nc - 1 - n
            t0 = pl.multiple_of(idx * c, c)
            rows = pl.ds(t0, c)
            blk = pl.ds(pl.multiple_of(idx * GDN_DK, GDN_DK), GDN_DK)
            s = states[d]
            sb = s.astype(BF16)
            o_ref[rows, :] = o_ref[rows, :] + _dot(qd_ref[d, rows, :], sb)
            egl = egl_ref[d, pl.ds(pl.multiple_of(idx * 8, 8), 8), :][0:1, :]
            states[d] = s * egl - _dot(kw_ref[d, blk, :], sb) + b_ref[d, blk, :]
        return tuple(states)

    zero = jnp.zeros((GDN_DK, GDN_DV), F32)
    lax.fori_loop(0, nc, scan, (zero, zero))

    def fin(n, carry):
        rows = pl.ds(pl.multiple_of(n * c, c), c)
        y = _rms(o_ref[rows, :], gain_ref[...]) * _silu(gate_ref[rows, :])
        y_ref[rows, :] = y.astype(y_ref.dtype)
        return carry

    lax.fori_loop(0, nc, fin, 0)


def _gdn(h, bar, conv_w, prr, gain, bsz, seq):
    nh = GDN_HEADS
    blk = lambda off: pl.BlockSpec((seq, LANES), lambda b, hh, off=off: (b, off // LANES + hh))
    cw = lambda off: pl.BlockSpec((GDN_CONV, LANES), lambda b, hh, off=off: (0, off // LANES + hh))
    c = GDN_CHUNK
    assert seq % (2 * GDN_PREP_CHUNKS * c) == 0
    return pl.pallas_call(
        _gdn_kernel,
        grid=(bsz, nh),
        in_specs=[
            blk(EV_GQ), blk(EV_GK), blk(EV_GV), blk(EV_GG),
            pl.BlockSpec((None, None, 8, seq), lambda b, hh: (b, hh, 0, 0)),
            cw(0), cw(GDN_HEADS * GDN_DK), cw(2 * GDN_HEADS * GDN_DK),
            pl.BlockSpec((None, 8, 2), lambda b, hh: (hh, 0, 0)),
            pl.BlockSpec((1, LANES), lambda b, hh: (0, 0)),
        ],
        out_specs=pl.BlockSpec((seq, LANES), lambda b, hh: (b, hh)),
        out_shape=jax.ShapeDtypeStruct((bsz * seq, nh * GDN_DV), BF16),
        scratch_shapes=[
            pltpu.VMEM((3, seq + 16, LANES), F32),
            pltpu.VMEM((2, seq, LANES), BF16),
            pltpu.VMEM((2, (seq // c) * GDN_DK, GDN_DV), BF16),
            pltpu.VMEM((2, (seq // c) * GDN_DK, GDN_DV), F32),
            pltpu.VMEM((2, (seq // c) * 8, LANES), F32),
            pltpu.VMEM((seq, LANES), F32),
            pltpu.VMEM((GDN_LEVELS + 1, c // 2, c // 2), BF16),
            pltpu.VMEM((3, c // 2, c // 2), F32),
        ],
        compiler_params=_cparams(("parallel", "parallel")),
        name="gdn",
    )(h, h, h, h, bar, conv_w, conv_w, conv_w, prr, gain.reshape(1, LANES))


def _flash_scratch(nmaps, tq, tk, dv):
    stat = pltpu.VMEM((tq, LANES), F32)
    per_map = [pltpu.VMEM((tq, tk), F32), pltpu.VMEM((tq, tk), F32), pltpu.VMEM((tq, tk), BF16),
               stat, stat, pltpu.VMEM((tq, dv), F32), stat, stat, stat]
    return per_map * nmaps


def _flash_core(qs, k_ref, v_ref, bias_of, scratch):
    nmaps = len(qs)
    maps = [scratch[9 * i:9 * i + 9] for i in range(nmaps)]
    tq, tk = maps[0][0].shape
    nk = k_ref.shape[0] // tk
    assert nk % 2 == 0 and tk % LANES == 0 and tq % ATT_RB == 0
    lane_blocks = lambda x: [x[:, b * LANES:(b + 1) * LANES] for b in range(tk // LANES)]
    row_blocks = [slice(rb * ATT_RB, (rb + 1) * ATT_RB) for rb in range(tq // ATT_RB)]

    for s0, s1, p, m, l, acc, mx0, mx1, al in maps:
        p[...] = jnp.zeros(p.shape, p.dtype)
        m[...] = jnp.full(m.shape, NEG, F32)
        l[...] = jnp.zeros(l.shape, F32)
        acc[...] = jnp.zeros(acc.shape, F32)

    def chunk(ref, j):
        return ref[pl.ds(pl.multiple_of(j * tk, tk), tk), :]

    def qk(i, j, slot):
        bias = bias_of(j)
        mx_ref = maps[i][6 + slot]
        for t in range(tk // MXU_COLS):
            cols = slice(t * MXU_COLS, (t + 1) * MXU_COLS)
            keys = k_ref[pl.ds(pl.multiple_of(j * tk + t * MXU_COLS, MXU_COLS), MXU_COLS), :]
            s = _dot_nt(qs[i], keys)
            if bias is not None:
                s = s + bias[:, cols]
            maps[i][slot][:, cols] = s
            part = functools.reduce(jnp.maximum, [s[:, b * LANES:(b + 1) * LANES] for b in range(MXU_COLS // LANES)])
            mx_ref[...] = part if t == 0 else jnp.maximum(mx_ref[...], part)

    def pv(i, j):
        acc = maps[i][5]
        acc[...] = acc[...] + _dot(maps[i][2][...], chunk(v_ref, j))

    def softmax(i, slot):
        s_ref, p_ref, m_ref, l_ref, acc_ref = maps[i][slot], maps[i][2], maps[i][3], maps[i][4], maps[i][5]
        mx_ref, al_ref = maps[i][6 + slot], maps[i][8]
        for r in row_blocks:
            m_old = m_ref[r, :]
            m_new = jnp.maximum(m_old, jnp.broadcast_to(jnp.max(mx_ref[r, :], axis=-1, keepdims=True), m_old.shape))
            al_ref[r, :] = jnp.exp2(m_old - m_new)
            m_ref[r, :] = m_new
        for r in row_blocks:
            m_new, alpha = m_ref[r, :], al_ref[r, :]
            ps = [jnp.exp2(b - m_new) for b in lane_blocks(s_ref[r, :])]
            l_ref[r, :] = alpha * l_ref[r, :] + functools.reduce(jnp.add, ps)
            acc_ref[r, :] = acc_ref[r, :] * alpha
            p_ref[r, :] = jnp.concatenate(ps, axis=1).astype(BF16)

    for i in range(nmaps):
        qk(i, 0, 0)

    def step(j, slot):
        nxt = jnp.minimum(j + 1, nk - 1)
        for i in range(nmaps):
            if i == 0:
                pv(nmaps - 1, jnp.maximum(j - 1, 0))
            else:
                pv(i - 1, j)
            qk(i, nxt, 1 - slot)
            softmax(i, slot)

    def body(jj, carry):
        step(2 * jj, 0)
        step(2 * jj + 1, 1)
        return carry

    lax.fori_loop(0, nk // 2, body, 0)
    pv(nmaps - 1, nk - 1)
    return [(mp[5][...], jnp.sum(mp[4][...], axis=-1, keepdims=True)) for mp in maps]


def _diff_prep_kernel(q_ref, k_ref, v_ref, qg_ref, kg_ref, qo_ref, ko_ref, vo_ref):
    r = lax.broadcasted_iota(jnp.int32, (LANES, LANES), 0) < DIFF_DQK
    cc = lax.broadcasted_iota(jnp.int32, (LANES, LANES), 1) < DIFF_DQK
    half_mean = jnp.where(r == cc, 1.0 / DIFF_DQK, 0.0).astype(BF16)

    def halfnorm(x, gain):
        x = x.astype(F32)
        return x * lax.rsqrt(_dot((x * x).astype(BF16), half_mean) + EPS) * gain

    for hh in range(DIFF_HEADS):
        sl = slice(hh * LANES, (hh + 1) * LANES)
        qo_ref[:, sl] = (halfnorm(q_ref[:, sl], qg_ref[...]) * (DIFF_DQK ** -0.5 * LOG2E)).astype(BF16)
        ko_ref[:, sl] = halfnorm(k_ref[:, sl], kg_ref[...]).astype(BF16)
    vo_ref[...] = v_ref[...].astype(BF16)


def _diff_prep(h, qg, kg, tm):
    m = h.shape[0]
    w = DIFF_HEADS * LANES
    spec = lambda off: pl.BlockSpec((tm, w), lambda i, off=off: (i, off // w))
    vec = pl.BlockSpec((1, LANES), lambda i: (0, 0))
    out = jax.ShapeDtypeStruct((m, w), BF16)
    return pl.pallas_call(
        _diff_prep_kernel,
        grid=(m // tm,),
        in_specs=[spec(EV_DQ), spec(EV_DK), spec(EV_DV), vec, vec],
        out_specs=[pl.BlockSpec((tm, w), lambda i: (i, 0))] * 3,
        out_shape=[out, out, out],
        compiler_params=_cparams(("parallel",)),
        name="diff_prep",
    )(h, h, h, jnp.tile(qg, 2).reshape(1, LANES), jnp.tile(kg, 2).reshape(1, LANES))


def _toeplitz(vec, tq, tk):
    w = vec.shape[-1]
    full = pltpu.roll(jnp.broadcast_to(vec, (tq, w)), w - (tq - 1), 1, stride=1, stride_axis=0)
    return full[:, :tk]


def _diff_flash_kernel(q_ref, k_ref, v_ref, gate_ref, bvec_ref, lam_ref, sub_ref, y_ref, bias_ref, *scratch,
                       lambda_init):
    tq = q_ref.shape[0]
    tk = bias_ref.shape[-1]
    qi = pl.program_id(2)

    @pl.when(qi == 0)
    def _():
        for d in range(bias_ref.shape[0]):
            bias_ref[d] = _toeplitz(bvec_ref[d:d + 1, :], tq, tk)

    q = q_ref[...]
    lane = lax.broadcasted_iota(jnp.int32, (1, LANES), 1)
    zero = jnp.zeros_like(q)
    q0 = jnp.where(lane < DIFF_DQK, q, zero)
    q1 = jnp.where(lane < DIFF_DQK, zero, q)

    def bias_of(j):
        return bias_ref.at[jnp.clip(j - qi, -2, 2) + 2]

    (a0, l0), (a1, l1) = _flash_core([q0, q1], k_ref, v_ref, bias_of, scratch)

    lam = lam_ref[...]
    lam_full = (jnp.exp(jnp.sum(lam[0:1] * lam[1:2], axis=-1, keepdims=True))
                - jnp.exp(jnp.sum(lam[2:3] * lam[3:4], axis=-1, keepdims=True)) + lambda_init)
    o = a0 / l0 - lam_full * (a1 / l1)
    o = _rms(o, sub_ref[...]) * (1.0 - lambda_init)
    y_ref[...] = (o * _silu(gate_ref[...])).astype(y_ref.dtype)


def _diff_flash(dq, dk, dv, h, bvec, lam, subln, lambda_init, bsz, seq):
    tq, tk = ATT_TQ, ATT_TK
    nq = seq // tq
    nh = DIFF_HEADS
    return pl.pallas_call(
        functools.partial(_diff_flash_kernel, lambda_init=lambda_init),
        grid=(bsz, nh, nq),
        in_specs=[
            pl.BlockSpec((tq, LANES), lambda b, hh, i: (b * nq + i, hh)),
            pl.BlockSpec((seq, LANES), lambda b, hh, i: (b, hh)),
            pl.BlockSpec((seq, LANES), lambda b, hh, i: (b, hh)),
            pl.BlockSpec((tq, LANES), lambda b, hh, i: (b * nq + i, EV_DG // LANES + hh)),
            pl.BlockSpec((None,) + bvec.shape[1:], lambda b, hh, i: (hh, 0, 0)),
            pl.BlockSpec((4, DIFF_DQK), lambda b, hh, i: (0, 0)),
            pl.BlockSpec((1, LANES), lambda b, hh, i: (0, 0)),
        ],
        out_specs=pl.BlockSpec((tq, LANES), lambda b, hh, i: (b * nq + i, hh)),
        out_shape=jax.ShapeDtypeStruct((bsz * seq, nh * DIFF_DV), BF16),
        scratch_shapes=[pltpu.VMEM((5, tq, tk), F32)] + _flash_scratch(2, tq, tk, DIFF_DV),
        compiler_params=_cparams(("parallel", "parallel", "arbitrary")),
        name="diff_flash",
    )(dq, dk, dv, h, bvec, lam, subln.reshape(1, LANES))


def _mem_attn_kernel(q_ref, gate_ref, mk_ref, mv_ref, qg_ref, kg_ref, y_ref):
    q = (_rms(q_ref[...], qg_ref[...]) * (MEM_DH ** -0.5)).astype(BF16)
    mk = _rms(mk_ref[...], kg_ref[...]).astype(BF16)
    s = _dot_nt(q, mk)
    p = jnp.exp(s - jnp.max(s, axis=-1, keepdims=True))
    o = _dot(p.astype(BF16), mv_ref[...].astype(BF16)) / jnp.sum(p, axis=-1, keepdims=True)
    y_ref[...] = (o * _silu(gate_ref[...])).astype(y_ref.dtype)


def _mem_attn(h, memkv, qg, kg, q_off, g_off, bsz, seq, tq):
    nq = seq // tq
    nh = MEM_HEADS
    vec = pl.BlockSpec((1, LANES), lambda b, hh, i: (0, 0))
    return pl.pallas_call(
        _mem_attn_kernel,
        grid=(bsz, nh, nq),
        in_specs=[
            pl.BlockSpec((tq, LANES), lambda b, hh, i: (b * nq + i, q_off // LANES + hh)),
            pl.BlockSpec((tq, LANES), lambda b, hh, i: (b * nq + i, g_off // LANES + hh)),
            pl.BlockSpec((MEM_LEN, LANES), lambda b, hh, i: (b, hh)),
            pl.BlockSpec((MEM_LEN, LANES), lambda b, hh, i: (b, nh + hh)),
            vec, vec,
        ],
        out_specs=pl.BlockSpec((tq, LANES), lambda b, hh, i: (b * nq + i, hh)),
        out_shape=jax.ShapeDtypeStruct((bsz * seq, nh * MEM_DH), BF16),
        compiler_params=_cparams(("parallel", "parallel", "parallel")),
        name="mem_attn",
    )(h, h, memkv, memkv, qg.reshape(1, LANES), kg.reshape(1, LANES))


def _swa_kernel(sink_ref, q_ref, k_ref, v_ref, gate_ref, bias_ref, qg_ref, kg_ref, y_ref, tile_ref):
    tq = q_ref.shape[0]
    seq = k_ref.shape[0]
    grp = SWA_HEADS // SWA_KV_HEADS
    kvh = pl.program_id(1)
    qi = pl.program_id(2)
    nq = pl.num_programs(2)

    @pl.when(jnp.logical_or(qi <= 1, qi == nq - 1))
    def _():
        for g in range(grp):
            tile_ref[g] = _toeplitz(bias_ref[g:g + 1, :], tq, SWA_TW)

    ws = pl.multiple_of(jnp.clip(qi * tq - WINDOW, 0, seq - SWA_TW), WINDOW)
    kw = _rms(k_ref[pl.ds(ws, SWA_TW), :], kg_ref[...]).astype(BF16)
    vw = v_ref[pl.ds(ws, SWA_TW), :].astype(BF16)
    heads = [slice(g * SWA_DH, (g + 1) * SWA_DH) for g in range(grp)]
    qs = [(_rms(q_ref[:, sl], qg_ref[...]) * (SWA_DH ** -0.5)).astype(BF16) for sl in heads]
    ss = [_dot_nt(q, kw) + tile_ref[g] for g, q in enumerate(qs)]
    ps, dens = [], []
    for g, s in enumerate(ss):
        sink = sink_ref[kvh * grp + g]
        mx = jnp.maximum(jnp.max(s, axis=-1, keepdims=True), sink)
        p = jnp.exp(s - mx)
        dens.append(jnp.sum(p, axis=-1, keepdims=True) + jnp.exp(sink - mx))
        ps.append(p.astype(BF16))
    for sl, p, den in zip(heads, ps, dens):
        y_ref[:, sl] = (_dot(p, vw) / den * _silu(gate_ref[:, sl])).astype(y_ref.dtype)


def _swa(h, bias, sink, qg, kg, bsz, seq):
    tq = SWA_TQ
    nq = seq // tq
    grp = SWA_HEADS // SWA_KV_HEADS
    gw = grp * SWA_DH
    assert seq >= SWA_TW and nq >= 2
    vec = pl.BlockSpec((1, LANES), lambda b, kv, i: (0, 0))

    def bias_idx(b, kv, i):
        return (jnp.where(i == 0, 0, jnp.where(i == nq - 1, 2, 1)), kv, 0, 0)

    return pl.pallas_call(
        _swa_kernel,
        grid=(bsz, SWA_KV_HEADS, nq),
        in_specs=[
            pl.BlockSpec(memory_space=pltpu.SMEM),
            pl.BlockSpec((tq, gw), lambda b, kv, i: (b * nq + i, OD_SQ // gw + kv)),
            pl.BlockSpec((seq, LANES), lambda b, kv, i: (b, OD_SK // LANES + kv)),
            pl.BlockSpec((seq, LANES), lambda b, kv, i: (b, OD_SV // LANES + kv)),
            pl.BlockSpec((tq, gw), lambda b, kv, i: (b * nq + i, OD_SG // gw + kv)),
            pl.BlockSpec((None, None, grp, bias.shape[-1]), bias_idx),
            vec, vec,
        ],
        out_specs=pl.BlockSpec((tq, gw), lambda b, kv, i: (b * nq + i, kv)),
        out_shape=jax.ShapeDtypeStruct((bsz * seq, SWA_HEADS * SWA_DH), BF16),
        scratch_shapes=[pltpu.VMEM((grp, tq, SWA_TW), F32)],
        compiler_params=_cparams(("parallel", "parallel", "arbitrary")),
        name="swa",
    )(sink, h, h, h, h, bias, qg.reshape(1, LANES), kg.reshape(1, LANES))


def _mla_prep_kernel(q_ref, ckv_ref, kr_ref, cos_ref, sin_ref, kvg_ref, wup_ref,
                     qgn_ref, qgr_ref, kgn_ref, kgr_ref, qo_ref, ko_ref, vo_ref):
    dqk = MLA_NOPE + MLA_ROPE
    lane = lax.broadcasted_iota(jnp.int32, (1, LANES), 1)
    lo = lane < MLA_ROPE
    first = lane < MLA_ROPE // 2
    cos = cos_ref[...]
    sin = sin_ref[...]

    def rope(t):
        rot = jnp.where(first, pltpu.roll(t, LANES - MLA_ROPE // 2, 1), pltpu.roll(t, MLA_ROPE // 2, 1))
        return t * cos + rot * sin

    ckv = _rms(ckv_ref[...], kvg_ref[...]).astype(BF16)
    kv = _dot(ckv, wup_ref[...])
    kr = kr_ref[...]
    kr2 = kr * kr
    c = dqk ** -0.5 * LOG2E
    hw = MLA_NOPE + MLA_DV
    for hh in range(MLA_HEADS):
        kn = kv[:, hh * hw:hh * hw + MLA_NOPE]
        inv = lax.rsqrt(_lane_sum(kn * kn + kr2, 1.0 / dqk) + EPS)
        ko_ref[:, hh * 2 * LANES:hh * 2 * LANES + LANES] = (kn * inv * kgn_ref[...]).astype(BF16)
        ko_ref[:, hh * 2 * LANES + LANES:(hh + 1) * 2 * LANES] = rope(kr * inv * kgr_ref[...]).astype(BF16)
        vo_ref[:, hh * LANES:(hh + 1) * LANES] = kv[:, hh * hw + MLA_NOPE:(hh + 1) * hw].astype(BF16)

        qn = q_ref[:, hh * LANES:(hh + 1) * LANES].astype(F32)
        pair = q_ref[:, OD_MLQ_ROPE + (hh // 2) * LANES:OD_MLQ_ROPE + (hh // 2 + 1) * LANES].astype(F32)
        if hh % 2 == 1:
            pair = pltpu.roll(pair, MLA_ROPE, 1)
        qr = jnp.where(lo, pair, 0.0)
        inv = lax.rsqrt(_lane_sum(qn * qn + qr * qr, 1.0 / dqk) + EPS)
        qo_ref[:, hh * 2 * LANES:hh * 2 * LANES + LANES] = (qn * inv * qgn_ref[...] * c).astype(BF16)
        qo_ref[:, hh * 2 * LANES + LANES:(hh + 1) * 2 * LANES] = (rope(qr * inv * qgr_ref[...]) * c).astype(BF16)


def _mla_prep(h, side, cos, sin, kv_gain, w_up, qg, kg, tm):
    m = h.shape[0]
    nh = MLA_HEADS
    qw = nh * (MLA_NOPE + MLA_ROPE)
    pad = lambda g: jnp.pad(g[MLA_NOPE:], (0, LANES - MLA_ROPE)).reshape(1, LANES)
    vec = pl.BlockSpec((1, LANES), lambda i: (0, 0))
    row = lambda i: (i, 0)
    return pl.pallas_call(
        _mla_prep_kernel,
        grid=(m // tm,),
        in_specs=[
            pl.BlockSpec((tm, qw), row),
            pl.BlockSpec((tm, MLA_KV_RANK), lambda i: (i, OD_CKV // MLA_KV_RANK)),
            pl.BlockSpec((tm, LANES), row),
            pl.BlockSpec((tm, LANES), row),
            pl.BlockSpec((tm, LANES), row),
            pl.BlockSpec((1, MLA_KV_RANK), lambda i: (0, 0)),
            pl.BlockSpec(w_up.shape, lambda i: (0, 0)),
            vec, vec, vec, vec,
        ],
        out_specs=[
            pl.BlockSpec((tm, nh * 2 * LANES), row),
            pl.BlockSpec((tm, nh * 2 * LANES), row),
            pl.BlockSpec((tm, nh * MLA_DV), row),
        ],
        out_shape=[
            jax.ShapeDtypeStruct((m, nh * 2 * LANES), BF16),
            jax.ShapeDtypeStruct((m, nh * 2 * LANES), BF16),
            jax.ShapeDtypeStruct((m, nh * MLA_DV), BF16),
        ],
        compiler_params=_cparams(("parallel",)),
        name="mla_prep",
    )(h, h, side, cos, sin, kv_gain.reshape(1, MLA_KV_RANK), w_up.astype(BF16),
      qg[:MLA_NOPE].reshape(1, LANES), pad(qg), kg[:MLA_NOPE].reshape(1, LANES), pad(kg))


def _mla_flash_kernel(q_ref, k_ref, v_ref, gate_ref, y_ref, *scratch):
    (acc, l), = _flash_core([q_ref[...]], k_ref, v_ref, lambda j: None, scratch)
    y_ref[...] = (acc / l * _silu(gate_ref[...])).astype(y_ref.dtype)


def _mla_flash(qm, km, vm, h, bsz, seq):
    tq = ATT_TQ
    nq = seq // tq
    nh = MLA_HEADS
    return pl.pallas_call(
        _mla_flash_kernel,
        grid=(bsz, nh, nq),
        in_specs=[
            pl.BlockSpec((tq, 2 * LANES), lambda b, hh, i: (b * nq + i, hh)),
            pl.BlockSpec((seq, 2 * LANES), lambda b, hh, i: (b, hh)),
            pl.BlockSpec((seq, LANES), lambda b, hh, i: (b, hh)),
            pl.BlockSpec((tq, LANES), lambda b, hh, i: (b * nq + i, OD_MLG // LANES + hh)),
        ],
        out_specs=pl.BlockSpec((tq, LANES), lambda b, hh, i: (b * nq + i, hh)),
        out_shape=jax.ShapeDtypeStruct((bsz * seq, nh * MLA_DV), BF16),
        scratch_shapes=_flash_scratch(1, tq, ATT_TK, MLA_DV),
        compiler_params=_cparams(("parallel", "parallel", "arbitrary")),
        name="mla_flash",
    )(qm, km, vm, h)


def _t5_bucket(rel):
    half = REL_BUCKETS // 2
    max_exact = half // 2
    ret = (rel > 0).astype(jnp.int32) * half
    n = jnp.abs(rel)
    nf = jnp.maximum(n, 1).astype(F32)
    large = max_exact + (jnp.log(nf / max_exact) / math.log(REL_MAX_DIST / max_exact)
                         * (half - max_exact)).astype(jnp.int32)
    large = jnp.minimum(large, half - 1)
    return ret + jnp.where(n < max_exact, n, large)


def _round_up(n, m):
    return (n + m - 1) // m * m


def _diff_bias_vecs(rel_table, tq, tk):
    assert tq == tk and tk + 1 >= REL_MAX_DIST
    m = jnp.arange(_round_up(tq + tk - 1, LANES))
    rows = [rel_table[_t5_bucket(m - (tq - 1) + d * tk)] for d in (-2, -1, 0, 1, 2)]
    return jnp.pad(jnp.stack(rows, axis=0).transpose(2, 0, 1) * LOG2E, ((0, 0), (0, 3), (0, 0))).astype(F32)


def _swa_bias_vecs(rel_table):
    m = jnp.arange(_round_up(SWA_TQ + SWA_TW - 1, LANES))
    rows = []
    for shift in (0, WINDOW, SWA_TW - SWA_TQ):
        rel = m - (SWA_TQ - 1) - shift
        rows.append(jnp.where((jnp.abs(rel) <= WINDOW)[None], rel_table[_t5_bucket(rel)].T, NEG))
    return jnp.stack(rows, axis=0).reshape(3, SWA_KV_HEADS, SWA_HEADS // SWA_KV_HEADS, -1).astype(F32)


def _od_main_weight(wt):
    mlq_lo, mlq_hi = 2560, 4096
    mlq = wt[mlq_lo:mlq_hi].reshape(MLA_HEADS, MLA_NOPE + MLA_ROPE, -1)
    nope = mlq[:, :MLA_NOPE].reshape(MLA_HEADS * MLA_NOPE, -1)
    rope = mlq[:, MLA_NOPE:].reshape(MLA_HEADS * MLA_ROPE, -1)
    return jnp.concatenate([nope, rope, wt[:mlq_lo], wt[mlq_hi:OD_KR_LO], wt[OD_KR_HI:]], axis=0).astype(BF16)


def _rope_tables(positions):
    half = MLA_ROPE // 2
    inv_freq = ROPE_BASE ** (-jnp.arange(half, dtype=F32) / half)
    ang = positions.astype(F32).reshape(-1, 1) * inv_freq
    cos, sin = jnp.cos(ang), jnp.sin(ang)
    z = jnp.zeros((ang.shape[0], LANES - MLA_ROPE), F32)
    return jnp.concatenate([cos, cos, z], axis=1), jnp.concatenate([-sin, sin, z], axis=1)


def _side_weight(wt, lo, hi):
    return jnp.pad(wt[lo:hi], ((0, LANES - (hi - lo)), (0, 0))).astype(BF16)


def _mem_kv(mem2, gain, w_kv):
    zero_side = jnp.zeros((LANES, D_MODEL), BF16)
    memkv, _ = _norm_proj(mem2, gain, jnp.swapaxes(w_kv, 0, 1).astype(BF16), zero_side, tm=mem2.shape[0] // 2,
                          tn=512, out_dtype=F32)
    return memkv


def _even_layer(x2, mem2, rel_bias, norm_g, w_in, conv_w, a_log, dt_bias, gdn_gain, dq_gain, dk_gain, lam, subln,
                mem_norm, mem_w_kv, mem_qn, mem_kn, w_out, lambda_init, bsz, seq):
    wt = jnp.swapaxes(w_in, 0, 1)
    w_main = jnp.concatenate([wt[:EV_SIDE_LO], wt[EV_SIDE_HI:]], axis=0).astype(BF16)
    h, side = _norm_proj(x2, norm_g, w_main, _side_weight(wt, EV_SIDE_LO, EV_SIDE_HI), tm=1024, tn=EV_MAIN // 6,
                         out_dtype=BF16)

    nh = GDN_HEADS
    bar = side[:, :4 * nh].reshape(bsz, seq, 4, nh).transpose(0, 3, 2, 1)
    bar = jnp.pad(bar, ((0, 0), (0, 0), (0, 4), (0, 0)))
    prr = jnp.pad(jnp.stack([a_log.T, dt_bias.T], axis=-1), ((0, 0), (2, 4), (0, 0)))
    ya = _gdn(h, bar, conv_w, prr, gdn_gain, bsz, seq)

    dq, dk, dv = _diff_prep(h, dq_gain, dk_gain, tm=512)
    bias = _diff_bias_vecs(rel_bias, ATT_TQ, ATT_TK)
    yb = _diff_flash(dq, dk, dv, h, bias, lam, subln, lambda_init, bsz, seq)

    memkv = _mem_kv(mem2, mem_norm, mem_w_kv)
    ym = _mem_attn(h, memkv, mem_qn, mem_kn, EV_MQ, EV_MG, bsz, seq, tq=1024)
    return _out_proj(x2, ya, yb, ym, w_out, tm=512)


def _odd_layer(x2, mem2, positions, rel_bias, norm_g, w_in, swa_qn, swa_kn, sink, kv_norm, w_kv_up, mla_qn, mla_kn,
               mem_norm, mem_w_kv, mem_qn, mem_kn, w_out, bsz, seq):
    wt = jnp.swapaxes(w_in, 0, 1)
    h, side = _norm_proj(x2, norm_g, _od_main_weight(wt), _side_weight(wt, OD_KR_LO, OD_KR_HI), tm=1024,
                         tn=OD_MAIN // 4, out_dtype=BF16)

    ya = _swa(h, _swa_bias_vecs(rel_bias), sink, swa_qn, swa_kn, bsz, seq)

    cos, sin = _rope_tables(positions)
    qm, km, vm = _mla_prep(h, side, cos, sin, kv_norm, w_kv_up, mla_qn, mla_kn, tm=256)
    yb = _mla_flash(qm, km, vm, h, bsz, seq)

    memkv = _mem_kv(mem2, mem_norm, mem_w_kv)
    ym = _mem_attn(h, memkv, mem_qn, mem_kn, OD_MQ, OD_MG, bsz, seq, tq=1024)
    return _out_proj(x2, ya, yb, ym, w_out, tm=512)


def kernel(x, mem, positions, rel_bias, ev_norm, ev_w_in, ev_conv, ev_a_log, ev_dt_bias, ev_gdn_norm, ev_diff_qnorm, ev_diff_knorm, ev_diff_lambda, ev_diff_subln, ev_mem_norm, ev_mem_w_kv, ev_mem_qnorm, ev_mem_knorm, ev_w_out, od_norm, od_w_in, od_swa_qnorm, od_swa_knorm, od_swa_sink, od_mla_kv_norm, od_mla_w_kv_up, od_mla_qnorm, od_mla_knorm, od_mem_norm, od_mem_w_kv, od_mem_qnorm, od_mem_knorm, od_w_out):
    bsz, seq, d = x.shape
    depth = ev_norm.shape[0] + od_norm.shape[0]
    x2 = x.reshape(bsz * seq, d)
    mem2 = mem.reshape(bsz * mem.shape[1], d)
    for layer in range(depth):
        i = layer // 2
        if layer % 2 == 0:
            lambda_init = 0.8 - 0.6 * math.exp(-0.3 * layer)
            x2 = _even_layer(x2, mem2, rel_bias, ev_norm[i], ev_w_in[i], ev_conv[i], ev_a_log[i], ev_dt_bias[i],
                             ev_gdn_norm[i], ev_diff_qnorm[i], ev_diff_knorm[i], ev_diff_lambda[i], ev_diff_subln[i],
                             ev_mem_norm[i], ev_mem_w_kv[i], ev_mem_qnorm[i], ev_mem_knorm[i], ev_w_out[i],
                             lambda_init, bsz, seq)
        else:
            x2 = _odd_layer(x2, mem2, positions, rel_bias, od_norm[i], od_w_in[i], od_swa_qnorm[i], od_swa_knorm[i],
                            od_swa_sink[i], od_mla_kv_norm[i], od_mla_w_kv_up[i], od_mla_qnorm[i], od_mla_knorm[i],
                            od_mem_norm[i], od_mem_w_kv[i], od_mem_qnorm[i], od_mem_knorm[i], od_w_out[i], bsz, seq)
    return x2.reshape(bsz, seq, d)
```

```python
import functools
import math

import jax
import jax.numpy as jnp
from jax import lax
from jax.experimental import pallas as pl
from jax.experimental.pallas import tpu as pltpu

F32 = jnp.float32
BF16 = jnp.bfloat16
EPS = 1e-6
NEG = -1e30

V7X_VMEM_BYTES = 64 * 1024 * 1024
VMEM_LIMIT = V7X_VMEM_BYTES - 8 * 1024 * 1024
LANES = 128

D_MODEL = 2048
MEM_LEN = 256
GDN_HEADS, GDN_DK, GDN_DV, GDN_CONV = 8, 128, 128, 5
DIFF_HEADS, DIFF_DQK, DIFF_DV = 8, 64, 128
SWA_HEADS, SWA_KV_HEADS, SWA_DH, WINDOW = 8, 2, 128, 128
MLA_HEADS, MLA_NOPE, MLA_ROPE, MLA_DV, MLA_KV_RANK = 8, 128, 64, 128, 512
ROPE_BASE = 10000.0
MEM_HEADS, MEM_DH = 4, 128
REL_BUCKETS, REL_MAX_DIST = 32, 128

GDN_CHUNK = 256
GDN_LEVELS = (GDN_CHUNK // 2).bit_length() - 1
GDN_PREP_CHUNKS = 2
ATT_TQ = 512
ATT_TK = 512
ATT_RB = 32
LOG2E = math.log2(math.e)
SWA_TQ = 256
SWA_TW = SWA_TQ + 2 * WINDOW

EV_GQ, EV_GK, EV_GV, EV_GG = 0, 1024, 2048, 3072
EV_DQ, EV_DK, EV_DV, EV_DG = 4096, 5120, 6144, 7168
EV_MQ, EV_MG = 8192, 8704
EV_MAIN = 9216
EV_SIDE_LO, EV_SIDE_HI = 3072, 3104

OD_MLQ_NOPE, OD_MLQ_ROPE = 0, 1024
OD_SQ, OD_SK, OD_SV, OD_SG = 1536, 2560, 2816, 3072
OD_CKV, OD_MLG, OD_MQ, OD_MG = 4096, 4608, 5632, 6144
OD_MAIN = 6656
OD_KR_LO, OD_KR_HI = 4608, 4672


def _cparams(sem):
    return pltpu.CompilerParams(dimension_semantics=sem, vmem_limit_bytes=VMEM_LIMIT)


def _dot(a, b):
    return jnp.dot(a, b, preferred_element_type=F32)


def _dot_nt(a, b):
    return lax.dot_general(a, b, (((1,), (1,)), ((), ())), preferred_element_type=F32)


def _silu(x):
    x = x.astype(F32)
    return x * jax.nn.sigmoid(x)


def _softplus(x):
    return jnp.maximum(x, 0.0) + jnp.log(1.0 + jnp.exp(-jnp.abs(x)))


def _lane_sum(x, scale=1.0):
    return _dot(x.astype(BF16), jnp.ones((LANES, LANES), BF16)) * scale


def _rms(x, gain):
    x = x.astype(F32)
    if x.shape[-1] == LANES:
        ms = _lane_sum(x * x, 1.0 / LANES)
    else:
        ms = jnp.mean(x * x, axis=-1, keepdims=True)
    return x * lax.rsqrt(ms + EPS) * gain


def _norm_proj_kernel(x_ref, g_ref, w_ref, ws_ref, o_ref, os_ref, xn_ref):
    @pl.when(pl.program_id(1) == 0)
    def _():
        xn = _rms(x_ref[...], g_ref[...]).astype(BF16)
        xn_ref[...] = xn
        os_ref[...] = _dot_nt(xn, ws_ref[...])

    o_ref[...] = _dot_nt(xn_ref[...], w_ref[...]).astype(o_ref.dtype)


def _norm_proj(x, gain, w_main, w_side, tm, tn, out_dtype):
    m, k = x.shape
    n = w_main.shape[0]
    ns = w_side.shape[0]
    assert m % tm == 0 and n % tn == 0
    return pl.pallas_call(
        _norm_proj_kernel,
        grid=(m // tm, n // tn),
        in_specs=[
            pl.BlockSpec((tm, k), lambda i, j: (i, 0)),
            pl.BlockSpec((1, k), lambda i, j: (0, 0)),
            pl.BlockSpec((tn, k), lambda i, j: (j, 0)),
            pl.BlockSpec((ns, k), lambda i, j: (0, 0)),
        ],
        out_specs=[
            pl.BlockSpec((tm, tn), lambda i, j: (i, j)),
            pl.BlockSpec((tm, ns), lambda i, j: (i, 0)),
        ],
        out_shape=[jax.ShapeDtypeStruct((m, n), out_dtype), jax.ShapeDtypeStruct((m, ns), F32)],
        scratch_shapes=[pltpu.VMEM((tm, k), BF16)],
        compiler_params=_cparams(("parallel", "arbitrary")),
        name="norm_proj",
    )(x, gain.reshape(1, k), w_main, w_side)


def _out_proj_kernel(x_ref, ya_ref, yb_ref, ym_ref, wa_ref, wb_ref, wm_ref, o_ref):
    acc = _dot(ya_ref[...], wa_ref[...])
    acc = acc + _dot(yb_ref[...], wb_ref[...])
    acc = acc + _dot(ym_ref[...], wm_ref[...])
    o_ref[...] = x_ref[...] + acc


def _out_proj(x, ya, yb, ym, w_out, tm):
    m, d = x.shape
    na, nb, nm = ya.shape[1], yb.shape[1], ym.shape[1]
    wa = w_out[:na].astype(BF16)
    wb = w_out[na:na + nb].astype(BF16)
    wm = w_out[na + nb:].astype(BF16)
    row = lambda i: (i, 0)
    fixed = lambda i: (0, 0)
    return pl.pallas_call(
        _out_proj_kernel,
        grid=(m // tm,),
        in_specs=[
            pl.BlockSpec((tm, d), row), pl.BlockSpec((tm, na), row), pl.BlockSpec((tm, nb), row),
            pl.BlockSpec((tm, nm), row),
            pl.BlockSpec((na, d), fixed), pl.BlockSpec((nb, d), fixed), pl.BlockSpec((nm, d), fixed),
        ],
        out_specs=pl.BlockSpec((tm, d), row),
        out_shape=jax.ShapeDtypeStruct((m, d), F32),
        compiler_params=_cparams(("parallel",)),
        name="out_proj",
    )(x, ya, yb, ym, wa, wb, wm)


def _gdn_kernel(q_ref, k_ref, v_ref, gate_ref, bar_ref, cwq_ref, cwk_ref, cwv_ref, prr_ref, gain_ref, y_ref,
                xp_ref, qd_ref, kw_ref, b_ref, egl_ref, o_ref, lvl_ref, tri_ref):
    seq = q_ref.shape[0]
    c = GDN_CHUNK
    nc = seq // c
    pad = 8
    scale = GDN_DK ** -0.5

    for i, src in enumerate((q_ref, k_ref, v_ref)):
        xp_ref[i, 0:pad, :] = jnp.zeros((pad, LANES), F32)
        xp_ref[i, pad + seq:2 * pad + seq, :] = jnp.zeros((pad, LANES), F32)
        xp_ref[i, pad:pad + seq, :] = src[...].astype(F32)

    hc = c // 2
    row_h = lax.broadcasted_iota(jnp.int32, (hc, hc), 0)
    col_h = lax.broadcasted_iota(jnp.int32, (hc, hc), 1)
    for bit in range(GDN_LEVELS):
        lvl_ref[bit] = (((row_h ^ col_h) >> bit) == 1).astype(BF16)
    lvl_ref[GDN_LEVELS] = (row_h == col_h).astype(BF16)
    tri_ref[0] = jnp.where(row_h > col_h, 0.0, NEG)
    tri_ref[1] = jnp.where(row_h < col_h, 0.0, NEG)
    tri_ref[2] = (row_h == col_h).astype(F32)

    lane_c = lax.broadcasted_iota(jnp.int32, (8, c), 1)
    sub_c = lax.broadcasted_iota(jnp.int32, (8, c), 0)

    def prefix(x):
        s = 1
        while s < c:
            x = x + jnp.where(lane_c >= s, pltpu.roll(x, s, 1), 0.0)
            s *= 2
        return x

    def suffix(x):
        s = 1
        while s < c:
            x = x + jnp.where(lane_c < c - s, pltpu.roll(x, c - s, 1), 0.0)
            s *= 2
        return x

    def conv(i, cw_ref, t0):
        half = (GDN_CONV - 1) // 2
        acc = None
        for j in range(GDN_CONV):
            tap = xp_ref[i, pl.ds(t0 + (pad - half + j), c), :] * cw_ref[j:j + 1, :]
            acc = tap if acc is None else acc + tap
        return _silu(acc)

    def l2n(x):
        return x * lax.rsqrt(_lane_sum(x * x) + EPS)

    half = lambda i: slice(i * hc, (i + 1) * hc)


    def prologue(ns, chains):
        st = []
        for n in ns:
            t0 = pl.multiple_of(n * c, c)
            st.append(dict(n=n, t0=t0, rows=pl.ds(t0, c)))
        for name, idx, cw_ref in (("q", 0, cwq_ref), ("k", 1, cwk_ref), ("v", 2, cwv_ref)):
            for s in st:
                s[name] = conv(idx, cw_ref, s["t0"])
            yield
        for s in st:
            s["q"], s["k"] = l2n(s["q"]), l2n(s["k"])
        yield
        for s in st:
            s["qs"] = s["q"] * scale
            s["kb"] = s["k"].astype(BF16)
            s["qk"] = _dot_nt(s["qs"].astype(BF16), s["kb"])
            bar = bar_ref[:, s["rows"]]
            g_r = (-LOG2E * jnp.exp(prr_ref[:, 0:1])) * _softplus(bar + prr_ref[:, 1:2])
            s["pre"], s["suf"] = prefix(g_r), suffix(g_r)
            s["tot"] = jnp.sum(g_r, axis=1, keepdims=True)
            packed = jnp.where(sub_c < 2, jax.nn.sigmoid(bar), jnp.where(sub_c == 2, s["pre"], s["suf"]))
            s["cols"] = jnp.concatenate([packed, jnp.zeros((LANES - 8, c), F32)], axis=0).T
            o_ref[s["rows"], :] = jnp.zeros((c, LANES), F32)
        yield
        for d in range(2):
            for s in st:
                s["kkb", d] = _dot_nt((s["k"] * s["cols"][:, d:d + 1]).astype(BF16), s["kb"])
            yield
        for d in range(2):
            first, second = (0, 1) if d == 0 else (1, 0)
            for s in st:
                gr = s["pre"][2:3, :] if d == 0 else s["suf"][3:4, :]
                gc = s["cols"][:, 2 + d:3 + d]
                blocks, attn = {}, {}
                for r, t in ((first, first), (second, second), (second, first)):
                    e = gc[half(r)] - gr[:, half(t)]
                    if r == t:
                        e = e + tri_ref[d]
                    dec = jnp.exp2(e)
                    blocks[r, t] = (s["kkb", d][half(r), half(t)] * dec).astype(BF16)
                    if r == t:
                        dec = dec + tri_ref[2]
                    attn[r, t] = (s["qk"][half(r), half(t)] * dec).astype(BF16)
                attn[first, second] = jnp.zeros((hc, hc), BF16)
                a_full = jnp.concatenate([jnp.concatenate([attn[r, 0], attn[r, 1]], axis=1) for r in range(2)], axis=0)
                diag = [blocks[0, 0], blocks[1, 1]]
                chains.append(dict(n=s["n"], rows=s["rows"], d=d, k=s["k"], v=s["v"],
                                   qs=s["qs"], beta=s["cols"][:, d:d + 1], gc=gc, tot=s["tot"][2 + d:3 + d, :],
                                   m=diag, off=blocks[second, first], a=a_full,
                                   p=[lvl_ref[GDN_LEVELS] - blk * lvl_ref[0] for blk in diag]))
            yield

    def levels(chains):
        for bit in range(1, GDN_LEVELS):
            lvl = lvl_ref[bit]
            for ch in chains:
                ch["x"] = [_dot(p, m * lvl).astype(BF16) for p, m in zip(ch["p"], ch["m"])]
            yield
            for ch in chains:
                ch["p"] = [p - _dot(x, p).astype(BF16) for p, x in zip(ch["p"], ch["x"])]
            yield

    def epilogue(chains):
        order = lambda ch: (0, 1) if ch["d"] == 0 else (1, 0)
        for ch in chains:
            ch["eg"] = jnp.exp2(ch["gc"])
            ch["rhs"] = jnp.concatenate([ch["v"] * ch["beta"], ch["k"] * (ch["beta"] * ch["eg"])], axis=1)
            ch["x1"] = _dot(ch["p"][order(ch)[0]], ch["rhs"][half(order(ch)[0])].astype(BF16))
        yield
        for ch in chains:
            ch["cross"] = _dot(ch["off"], ch["x1"].astype(BF16))
        yield
        for ch in chains:
            second = order(ch)[1]
            ch["x2"] = _dot(ch["p"][second], (ch["rhs"][half(second)] - ch["cross"]).astype(BF16))
        yield
        for ch in chains:
            xs = (ch["x1"], ch["x2"]) if ch["d"] == 0 else (ch["x2"], ch["x1"])
            ch["xb"] = jnp.concatenate(xs, axis=0).astype(BF16)
            kd = ch["k"] * jnp.exp2(ch["tot"] - ch["gc"])
            ch["kx"] = _dot(kd.T.astype(BF16), ch["xb"])
            ch["ax"] = _dot(ch["a"], ch["xb"])
        yield
        for ch in chains:
            d, rows = ch["d"], ch["rows"]
            blk = pl.ds(pl.multiple_of(ch["n"] * GDN_DK, GDN_DK), GDN_DK)
            b_ref[d, blk, :] = ch["kx"][:, :GDN_DV]
            kw_ref[d, blk, :] = ch["kx"][:, GDN_DV:].astype(BF16)
            o_ref[rows, :] = o_ref[rows, :] + ch["ax"][:, :GDN_DV]
            qd_ref[d, rows, :] = (ch["qs"] * ch["eg"] - ch["ax"][:, GDN_DV:]).astype(BF16)
            egl_ref[d, pl.ds(pl.multiple_of(ch["n"] * 8, 8), 8), :] = jnp.broadcast_to(jnp.exp2(ch["tot"]), (8, LANES))
        yield

    def interleave(*gens):
        gens = list(gens)
        while gens:
            for g in list(gens):
                if next(g, StopIteration) is StopIteration:
                    gens.remove(g)

    def prep(n, carry):
        base = n * 2 * GDN_PREP_CHUNKS
        group_a = [base + i for i in range(GDN_PREP_CHUNKS)]
        group_b = [base + GDN_PREP_CHUNKS + i for i in range(GDN_PREP_CHUNKS)]
        chains_a, chains_b = [], []
        interleave(prologue(group_a, chains_a))
        interleave(levels(chains_a), prologue(group_b, chains_b))
        interleave(levels(chains_b), epilogue(chains_a))
        interleave(epilogue(chains_b))
        return carry

    lax.fori_loop(0, nc // (2 * GDN_PREP_CHUNKS), prep, 0)

    def scan(n, carry):
        states = list(carry)
        for d in range(2):
            idx = n if d == 0 else nc - 1 - n
            t0 = pl.multiple_of(idx * c, c)
            rows = pl.ds(t0, c)
            blk = pl.ds(pl.multiple_of(idx * GDN_DK, GDN_DK), GDN_DK)
            s = states[d]
            sb = s.astype(BF16)
            o_ref[rows, :] = o_ref[rows, :] + _dot(qd_ref[d, rows, :], sb)
            egl = egl_ref[d, pl.ds(pl.multiple_of(idx * 8, 8), 8), :][0:1, :]
            states[d] = s * egl - _dot(kw_ref[d, blk, :], sb) + b_ref[d, blk, :]
        return tuple(states)

    zero = jnp.zeros((GDN_DK, GDN_DV), F32)
    lax.fori_loop(0, nc, scan, (zero, zero))

    def fin(n, carry):
        rows = pl.ds(pl.multiple_of(n * c, c), c)
        y = _rms(o_ref[rows, :], gain_ref[...]) * _silu(gate_ref[rows, :])
        y_ref[rows, :] = y.astype(y_ref.dtype)
        return carry

    lax.fori_loop(0, nc, fin, 0)


def _gdn(h, bar, conv_w, prr, gain, bsz, seq):
    nh = GDN_HEADS
    blk = lambda off: pl.BlockSpec((seq, LANES), lambda b, hh, off=off: (b, off // LANES + hh))
    cw = lambda off: pl.BlockSpec((GDN_CONV, LANES), lambda b, hh, off=off: (0, off // LANES + hh))
    c = GDN_CHUNK
    assert seq % (2 * GDN_PREP_CHUNKS * c) == 0
    return pl.pallas_call(
        _gdn_kernel,
        grid=(bsz, nh),
        in_specs=[
            blk(EV_GQ), blk(EV_GK), blk(EV_GV), blk(EV_GG),
            pl.BlockSpec((None, None, 8, seq), lambda b, hh: (b, hh, 0, 0)),
            cw(0), cw(GDN_HEADS * GDN_DK), cw(2 * GDN_HEADS * GDN_DK),
            pl.BlockSpec((None, 8, 2), lambda b, hh: (hh, 0, 0)),
            pl.BlockSpec((1, LANES), lambda b, hh: (0, 0)),
        ],
        out_specs=pl.BlockSpec((seq, LANES), lambda b, hh: (b, hh)),
        out_shape=jax.ShapeDtypeStruct((bsz * seq, nh * GDN_DV), BF16),
        scratch_shapes=[
            pltpu.VMEM((3, seq + 16, LANES), F32),
            pltpu.VMEM((2, seq, LANES), BF16),
            pltpu.VMEM((2, (seq // c) * GDN_DK, GDN_DV), BF16),
            pltpu.VMEM((2, (seq // c) * GDN_DK, GDN_DV), F32),
            pltpu.VMEM((2, (seq // c) * 8, LANES), F32),
            pltpu.VMEM((seq, LANES), F32),
            pltpu.VMEM((GDN_LEVELS + 1, c // 2, c // 2), BF16),
            pltpu.VMEM((3, c // 2, c // 2), F32),
        ],
        compiler_params=_cparams(("parallel", "parallel")),
        name="gdn",
    )(h, h, h, h, bar, conv_w, conv_w, conv_w, prr, gain.reshape(1, LANES))


def _flash_scratch(nmaps, tq, tk, dv):
    per_map = [pltpu.VMEM((tq, tk), F32), pltpu.VMEM((tq, tk), F32), pltpu.VMEM((tq, tk), BF16),
               pltpu.VMEM((tq, LANES), F32), pltpu.VMEM((tq, LANES), F32), pltpu.VMEM((tq, dv), F32)]
    return per_map * nmaps


def _flash_core(qs, k_ref, v_ref, nsteps, step_of, loop, scratch):
    nmaps = len(qs)
    maps = [scratch[6 * i:6 * i + 6] for i in range(nmaps)]
    tq, tk = maps[0][0].shape
    lo, hi = loop
    assert tk % LANES == 0 and tq % ATT_RB == 0 and 0 < lo <= hi < nsteps and (hi - lo) % 2 == 0
    nlb = tk // LANES

    for s0, s1, p, m, l, acc in maps:
        m[...] = jnp.full(m.shape, NEG, F32)
        l[...] = jnp.zeros(l.shape, F32)
        acc[...] = jnp.zeros(acc.shape, F32)

    def chunk(ref, t):
        j = step_of(t)[0]
        start = j * tk if isinstance(j, int) else pl.multiple_of(j * tk, tk)
        return ref[pl.ds(start, tk), :]

    def qk(i, t, slot):
        maps[i][slot][...] = _dot_nt(qs[i], chunk(k_ref, t))

    def pv(i, t):
        acc = maps[i][5]
        acc[...] = acc[...] + _dot(maps[i][2][...], chunk(v_ref, t))

    def softmax(i, t, slot):
        s_ref, p_ref, m_ref, l_ref, acc_ref = maps[i][slot], maps[i][2], maps[i][3], maps[i][4], maps[i][5]
        _, tile, const = step_of(t)
        for rb in range(tq // ATT_RB):
            r = slice(rb * ATT_RB, (rb + 1) * ATT_RB)
            s = s_ref[r, :]
            if tile is not None:
                s = s + tile[r, :]
            blocks = [s[:, b * LANES:(b + 1) * LANES] for b in range(nlb)]
            mx = functools.reduce(jnp.maximum, blocks)
            m_old = m_ref[r, :]
            mx = jnp.broadcast_to(jnp.max(mx, axis=-1, keepdims=True), m_old.shape)
            if const is not None:
                mx = mx + const
            m_new = jnp.maximum(m_old, mx)
            alpha = jnp.exp2(m_old - m_new)
            shift = m_new if const is None else m_new - const
            ps = [jnp.exp2(b - shift) for b in blocks]
            row_sum = jnp.sum(functools.reduce(jnp.add, ps), axis=-1, keepdims=True)
            l_ref[r, :] = alpha * l_ref[r, :] + jnp.broadcast_to(row_sum, m_old.shape)
            m_ref[r, :] = m_new
            acc_ref[r, :] = acc_ref[r, :] * alpha
            p_ref[r, :] = jnp.concatenate(ps, axis=1).astype(BF16)

    def stage(t, slot, first=False, last=False):
        for i in range(nmaps):
            if i > 0:
                pv(i - 1, t)
            elif not first:
                pv(nmaps - 1, t - 1)
            if not last:
                qk(i, t + 1, 1 - slot)
            softmax(i, t, slot)

    def pair(tt, carry):
        for u in range(2):
            stage(lo + 2 * tt + u, (lo + u) % 2)
        return carry

    for i in range(nmaps):
        qk(i, 0, 0)
    for t in range(lo):
        stage(t, t % 2, first=t == 0)
    lax.fori_loop(0, (hi - lo) // 2, pair, 0)
    for t in range(hi, nsteps):
        stage(t, t % 2, last=t == nsteps - 1)
    pv(nmaps - 1, nsteps - 1)
    return [(mp[5][...], mp[4][...]) for mp in maps]


def _diff_prep_kernel(q_ref, k_ref, v_ref, qg_ref, kg_ref, qo_ref, ko_ref, vo_ref):
    r = lax.broadcasted_iota(jnp.int32, (LANES, LANES), 0) < DIFF_DQK
    cc = lax.broadcasted_iota(jnp.int32, (LANES, LANES), 1) < DIFF_DQK
    half_mean = jnp.where(r == cc, 1.0 / DIFF_DQK, 0.0).astype(BF16)

    def halfnorm(x, gain):
        x = x.astype(F32)
        return x * lax.rsqrt(_dot((x * x).astype(BF16), half_mean) + EPS) * gain

    for hh in range(DIFF_HEADS):
        sl = slice(hh * LANES, (hh + 1) * LANES)
        qo_ref[:, sl] = (halfnorm(q_ref[:, sl], qg_ref[...]) * (DIFF_DQK ** -0.5 * LOG2E)).astype(BF16)
        ko_ref[:, sl] = halfnorm(k_ref[:, sl], kg_ref[...]).astype(BF16)
    vo_ref[...] = v_ref[...].astype(BF16)


def _diff_prep(h, qg, kg, tm):
    m = h.shape[0]
    w = DIFF_HEADS * LANES
    spec = lambda off: pl.BlockSpec((tm, w), lambda i, off=off: (i, off // w))
    vec = pl.BlockSpec((1, LANES), lambda i: (0, 0))
    out = jax.ShapeDtypeStruct((m, w), BF16)
    return pl.pallas_call(
        _diff_prep_kernel,
        grid=(m // tm,),
        in_specs=[spec(EV_DQ), spec(EV_DK), spec(EV_DV), vec, vec],
        out_specs=[pl.BlockSpec((tm, w), lambda i: (i, 0))] * 3,
        out_shape=[out, out, out],
        compiler_params=_cparams(("parallel",)),
        name="diff_prep",
    )(h, h, h, jnp.tile(qg, 2).reshape(1, LANES), jnp.tile(kg, 2).reshape(1, LANES))


def _toeplitz(vec, tq, tk):
    w = vec.shape[-1]
    full = pltpu.roll(jnp.broadcast_to(vec, (tq, w)), w - (tq - 1), 1, stride=1, stride_axis=0)
    return full[:, :tk]


def _diff_flash_kernel(q_ref, k_ref, v_ref, gate_ref, bvec_ref, lam_ref, sub_ref, y_ref, bias_ref, *scratch,
                       lambda_init):
    tq = q_ref.shape[0]
    tk = bias_ref.shape[-1]
    qi = pl.program_id(2)

    @pl.when(qi == 0)
    def _():
        for d in range(bias_ref.shape[0]):
            bias_ref[d] = _toeplitz(bvec_ref[d:d + 1, :], tq, tk)

    q = q_ref[...]
    lane = lax.broadcasted_iota(jnp.int32, (1, LANES), 1)
    zero = jnp.zeros_like(q)
    q0 = jnp.where(lane < DIFF_DQK, q, zero)
    q1 = jnp.where(lane < DIFF_DQK, zero, q)

    nk = k_ref.shape[0] // tk
    near = 3
    n0 = jnp.clip(qi - 1, 0, nk - near)
    c_before, c_after = bvec_ref[0:1, 0:1], bvec_ref[4:5, 0:1]
    nfar = nk - near

    def step_of(t):
        if isinstance(t, int) and t >= nfar:
            j = n0 + (t - nfar)
            return j, bias_ref.at[jnp.clip(j - qi, -2, 2) + 2], None
        j = jnp.where(t < n0, t, t + near)
        j = jnp.where(t < nfar, j, n0 + (t - nfar))
        return j, None, jnp.where(j < qi, c_before, c_after)

    loop = (1, nfar - (nfar - 1) % 2)
    (a0, l0), (a1, l1) = _flash_core([q0, q1], k_ref, v_ref, nk, step_of, loop, scratch)

    lam = lam_ref[...]
    lam_full = (jnp.exp(jnp.sum(lam[0:1] * lam[1:2], axis=-1, keepdims=True))
                - jnp.exp(jnp.sum(lam[2:3] * lam[3:4], axis=-1, keepdims=True)) + lambda_init)
    o = a0 / l0 - lam_full * (a1 / l1)
    o = _rms(o, sub_ref[...]) * (1.0 - lambda_init)
    y_ref[...] = (o * _silu(gate_ref[...])).astype(y_ref.dtype)


def _diff_flash(dq, dk, dv, h, bvec, lam, subln, lambda_init, bsz, seq):
    tq, tk = ATT_TQ, ATT_TK
    nq = seq // tq
    nh = DIFF_HEADS
    return pl.pallas_call(
        functools.partial(_diff_flash_kernel, lambda_init=lambda_init),
        grid=(bsz, nh, nq),
        in_specs=[
            pl.BlockSpec((tq, LANES), lambda b, hh, i: (b * nq + i, hh)),
            pl.BlockSpec((seq, LANES), lambda b, hh, i: (b, hh)),
            pl.BlockSpec((seq, LANES), lambda b, hh, i: (b, hh)),
            pl.BlockSpec((tq, LANES), lambda b, hh, i: (b * nq + i, EV_DG // LANES + hh)),
            pl.BlockSpec((None,) + bvec.shape[1:], lambda b, hh, i: (hh, 0, 0)),
            pl.BlockSpec((4, DIFF_DQK), lambda b, hh, i: (0, 0)),
            pl.BlockSpec((1, LANES), lambda b, hh, i: (0, 0)),
        ],
        out_specs=pl.BlockSpec((tq, LANES), lambda b, hh, i: (b * nq + i, hh)),
        out_shape=jax.ShapeDtypeStruct((bsz * seq, nh * DIFF_DV), BF16),
        scratch_shapes=[pltpu.VMEM((5, tq, tk), F32)] + _flash_scratch(2, tq, tk, DIFF_DV),
        compiler_params=_cparams(("parallel", "parallel", "arbitrary")),
        name="diff_flash",
    )(dq, dk, dv, h, bvec, lam, subln.reshape(1, LANES))


def _mem_attn_kernel(q_ref, gate_ref, mk_ref, mv_ref, qg_ref, kg_ref, y_ref):
    q = (_rms(q_ref[...], qg_ref[...]) * (MEM_DH ** -0.5)).astype(BF16)
    mk = _rms(mk_ref[...], kg_ref[...]).astype(BF16)
    s = _dot_nt(q, mk)
    p = jnp.exp(s - jnp.max(s, axis=-1, keepdims=True))
    o = _dot(p.astype(BF16), mv_ref[...].astype(BF16)) / jnp.sum(p, axis=-1, keepdims=True)
    y_ref[...] = (o * _silu(gate_ref[...])).astype(y_ref.dtype)


def _mem_attn(h, memkv, qg, kg, q_off, g_off, bsz, seq, tq):
    nq = seq // tq
    nh = MEM_HEADS
    vec = pl.BlockSpec((1, LANES), lambda b, hh, i: (0, 0))
    return pl.pallas_call(
        _mem_attn_kernel,
        grid=(bsz, nh, nq),
        in_specs=[
            pl.BlockSpec((tq, LANES), lambda b, hh, i: (b * nq + i, q_off // LANES + hh)),
            pl.BlockSpec((tq, LANES), lambda b, hh, i: (b * nq + i, g_off // LANES + hh)),
            pl.BlockSpec((MEM_LEN, LANES), lambda b, hh, i: (b, hh)),
            pl.BlockSpec((MEM_LEN, LANES), lambda b, hh, i: (b, nh + hh)),
            vec, vec,
        ],
        out_specs=pl.BlockSpec((tq, LANES), lambda b, hh, i: (b * nq + i, hh)),
        out_shape=jax.ShapeDtypeStruct((bsz * seq, nh * MEM_DH), BF16),
        compiler_params=_cparams(("parallel", "parallel", "parallel")),
        name="mem_attn",
    )(h, h, memkv, memkv, qg.reshape(1, LANES), kg.reshape(1, LANES))


def _swa_kernel(sink_ref, q_ref, k_ref, v_ref, gate_ref, bias_ref, qg_ref, kg_ref, y_ref, tile_ref):
    tq = q_ref.shape[0]
    seq = k_ref.shape[0]
    grp = SWA_HEADS // SWA_KV_HEADS
    kvh = pl.program_id(1)
    qi = pl.program_id(2)
    nq = pl.num_programs(2)

    @pl.when(jnp.logical_or(qi <= 1, qi == nq - 1))
    def _():
        for g in range(grp):
            tile_ref[g] = _toeplitz(bias_ref[g:g + 1, :], tq, SWA_TW)

    ws = pl.multiple_of(jnp.clip(qi * tq - WINDOW, 0, seq - SWA_TW), WINDOW)
    kw = _rms(k_ref[pl.ds(ws, SWA_TW), :], kg_ref[...]).astype(BF16)
    vw = v_ref[pl.ds(ws, SWA_TW), :].astype(BF16)
    heads = [slice(g * SWA_DH, (g + 1) * SWA_DH) for g in range(grp)]
    qs = [(_rms(q_ref[:, sl], qg_ref[...]) * (SWA_DH ** -0.5)).astype(BF16) for sl in heads]
    ss = [_dot_nt(q, kw) + tile_ref[g] for g, q in enumerate(qs)]
    ps, dens = [], []
    for g, s in enumerate(ss):
        sink = sink_ref[kvh * grp + g]
        mx = jnp.maximum(jnp.max(s, axis=-1, keepdims=True), sink)
        p = jnp.exp(s - mx)
        dens.append(jnp.sum(p, axis=-1, keepdims=True) + jnp.exp(sink - mx))
        ps.append(p.astype(BF16))
    for sl, p, den in zip(heads, ps, dens):
        y_ref[:, sl] = (_dot(p, vw) / den * _silu(gate_ref[:, sl])).astype(y_ref.dtype)


def _swa(h, bias, sink, qg, kg, bsz, seq):
    tq = SWA_TQ
    nq = seq // tq
    grp = SWA_HEADS // SWA_KV_HEADS
    gw = grp * SWA_DH
    assert seq >= SWA_TW and nq >= 2
    vec = pl.BlockSpec((1, LANES), lambda b, kv, i: (0, 0))

    def bias_idx(b, kv, i):
        return (jnp.where(i == 0, 0, jnp.where(i == nq - 1, 2, 1)), kv, 0, 0)

    return pl.pallas_call(
        _swa_kernel,
        grid=(bsz, SWA_KV_HEADS, nq),
        in_specs=[
            pl.BlockSpec(memory_space=pltpu.SMEM),
            pl.BlockSpec((tq, gw), lambda b, kv, i: (b * nq + i, OD_SQ // gw + kv)),
            pl.BlockSpec((seq, LANES), lambda b, kv, i: (b, OD_SK // LANES + kv)),
            pl.BlockSpec((seq, LANES), lambda b, kv, i: (b, OD_SV // LANES + kv)),
            pl.BlockSpec((tq, gw), lambda b, kv, i: (b * nq + i, OD_SG // gw + kv)),
            pl.BlockSpec((None, None, grp, bias.shape[-1]), bias_idx),
            vec, vec,
        ],
        out_specs=pl.BlockSpec((tq, gw), lambda b, kv, i: (b * nq + i, kv)),
        out_shape=jax.ShapeDtypeStruct((bsz * seq, SWA_HEADS * SWA_DH), BF16),
        scratch_shapes=[pltpu.VMEM((grp, tq, SWA_TW), F32)],
        compiler_params=_cparams(("parallel", "parallel", "arbitrary")),
        name="swa",
    )(sink, h, h, h, h, bias, qg.reshape(1, LANES), kg.reshape(1, LANES))


def _mla_prep_kernel(q_ref, ckv_ref, kr_ref, cos_ref, sin_ref, kvg_ref, wup_ref,
                     qgn_ref, qgr_ref, kgn_ref, kgr_ref, qo_ref, ko_ref, vo_ref):
    dqk = MLA_NOPE + MLA_ROPE
    lane = lax.broadcasted_iota(jnp.int32, (1, LANES), 1)
    lo = lane < MLA_ROPE
    first = lane < MLA_ROPE // 2
    cos = cos_ref[...]
    sin = sin_ref[...]

    def rope(t):
        rot = jnp.where(first, pltpu.roll(t, LANES - MLA_ROPE // 2, 1), pltpu.roll(t, MLA_ROPE // 2, 1))
        return t * cos + rot * sin

    ckv = _rms(ckv_ref[...], kvg_ref[...]).astype(BF16)
    kv = _dot(ckv, wup_ref[...])
    kr = kr_ref[...]
    kr2 = kr * kr
    c = dqk ** -0.5 * LOG2E
    hw = MLA_NOPE + MLA_DV
    for hh in range(MLA_HEADS):
        kn = kv[:, hh * hw:hh * hw + MLA_NOPE]
        inv = lax.rsqrt(_lane_sum(kn * kn + kr2, 1.0 / dqk) + EPS)
        ko_ref[:, hh * 2 * LANES:hh * 2 * LANES + LANES] = (kn * inv * kgn_ref[...]).astype(BF16)
        ko_ref[:, hh * 2 * LANES + LANES:(hh + 1) * 2 * LANES] = rope(kr * inv * kgr_ref[...]).astype(BF16)
        vo_ref[:, hh * LANES:(hh + 1) * LANES] = kv[:, hh * hw + MLA_NOPE:(hh + 1) * hw].astype(BF16)

        qn = q_ref[:, hh * LANES:(hh + 1) * LANES].astype(F32)
        pair = q_ref[:, OD_MLQ_ROPE + (hh // 2) * LANES:OD_MLQ_ROPE + (hh // 2 + 1) * LANES].astype(F32)
        if hh % 2 == 1:
            pair = pltpu.roll(pair, MLA_ROPE, 1)
        qr = jnp.where(lo, pair, 0.0)
        inv = lax.rsqrt(_lane_sum(qn * qn + qr * qr, 1.0 / dqk) + EPS)
        qo_ref[:, hh * 2 * LANES:hh * 2 * LANES + LANES] = (qn * inv * qgn_ref[...] * c).astype(BF16)
        qo_ref[:, hh * 2 * LANES + LANES:(hh + 1) * 2 * LANES] = (rope(qr * inv * qgr_ref[...]) * c).astype(BF16)


def _mla_prep(h, side, cos, sin, kv_gain, w_up, qg, kg, tm):
    m = h.shape[0]
    nh = MLA_HEADS
    qw = nh * (MLA_NOPE + MLA_ROPE)
    pad = lambda g: jnp.pad(g[MLA_NOPE:], (0, LANES - MLA_ROPE)).reshape(1, LANES)
    vec = pl.BlockSpec((1, LANES), lambda i: (0, 0))
    row = lambda i: (i, 0)
    return pl.pallas_call(
        _mla_prep_kernel,
        grid=(m // tm,),
        in_specs=[
            pl.BlockSpec((tm, qw), row),
            pl.BlockSpec((tm, MLA_KV_RANK), lambda i: (i, OD_CKV // MLA_KV_RANK)),
            pl.BlockSpec((tm, LANES), row),
            pl.BlockSpec((tm, LANES), row),
            pl.BlockSpec((tm, LANES), row),
            pl.BlockSpec((1, MLA_KV_RANK), lambda i: (0, 0)),
            pl.BlockSpec(w_up.shape, lambda i: (0, 0)),
            vec, vec, vec, vec,
        ],
        out_specs=[
            pl.BlockSpec((tm, nh * 2 * LANES), row),
            pl.BlockSpec((tm, nh * 2 * LANES), row),
            pl.BlockSpec((tm, nh * MLA_DV), row),
        ],
        out_shape=[
            jax.ShapeDtypeStruct((m, nh * 2 * LANES), BF16),
            jax.ShapeDtypeStruct((m, nh * 2 * LANES), BF16),
            jax.ShapeDtypeStruct((m, nh * MLA_DV), BF16),
        ],
        compiler_params=_cparams(("parallel",)),
        name="mla_prep",
    )(h, h, side, cos, sin, kv_gain.reshape(1, MLA_KV_RANK), w_up.astype(BF16),
      qg[:MLA_NOPE].reshape(1, LANES), pad(qg), kg[:MLA_NOPE].reshape(1, LANES), pad(kg))


def _mla_flash_kernel(q_ref, k_ref, v_ref, gate_ref, y_ref, *scratch):
    nk = k_ref.shape[0] // ATT_TK
    (acc, l), = _flash_core([q_ref[...]], k_ref, v_ref, nk, lambda t: (t, None, None), (1, nk - 1 - nk % 2), scratch)
    y_ref[...] = (acc / l * _silu(gate_ref[...])).astype(y_ref.dtype)


def _mla_flash(qm, km, vm, h, bsz, seq):
    tq = ATT_TQ
    nq = seq // tq
    nh = MLA_HEADS
    return pl.pallas_call(
        _mla_flash_kernel,
        grid=(bsz, nh, nq),
        in_specs=[
            pl.BlockSpec((tq, 2 * LANES), lambda b, hh, i: (b * nq + i, hh)),
            pl.BlockSpec((seq, 2 * LANES), lambda b, hh, i: (b, hh)),
            pl.BlockSpec((seq, LANES), lambda b, hh, i: (b, hh)),
            pl.BlockSpec((tq, LANES), lambda b, hh, i: (b * nq + i, OD_MLG // LANES + hh)),
        ],
        out_specs=pl.BlockSpec((tq, LANES), lambda b, hh, i: (b * nq + i, hh)),
        out_shape=jax.ShapeDtypeStruct((bsz * seq, nh * MLA_DV), BF16),
        scratch_shapes=_flash_scratch(1, tq, ATT_TK, MLA_DV),
        compiler_params=_cparams(("parallel", "parallel", "arbitrary")),
        name="mla_flash",
    )(qm, km, vm, h)


def _t5_bucket(rel):
    half = REL_BUCKETS // 2
    max_exact = half // 2
    ret = (rel > 0).astype(jnp.int32) * half
    n = jnp.abs(rel)
    nf = jnp.maximum(n, 1).astype(F32)
    large = max_exact + (jnp.log(nf / max_exact) / math.log(REL_MAX_DIST / max_exact)
                         * (half - max_exact)).astype(jnp.int32)
    large = jnp.minimum(large, half - 1)
    return ret + jnp.where(n < max_exact, n, large)


def _round_up(n, m):
    return (n + m - 1) // m * m


def _diff_bias_vecs(rel_table, tq, tk):
    assert tq == tk and tk + 1 >= REL_MAX_DIST
    m = jnp.arange(_round_up(tq + tk - 1, LANES))
    rows = [rel_table[_t5_bucket(m - (tq - 1) + d * tk)] for d in (-2, -1, 0, 1, 2)]
    return jnp.pad(jnp.stack(rows, axis=0).transpose(2, 0, 1) * LOG2E, ((0, 0), (0, 3), (0, 0))).astype(F32)


def _swa_bias_vecs(rel_table):
    m = jnp.arange(_round_up(SWA_TQ + SWA_TW - 1, LANES))
    rows = []
    for shift in (0, WINDOW, SWA_TW - SWA_TQ):
        rel = m - (SWA_TQ - 1) - shift
        rows.append(jnp.where((jnp.abs(rel) <= WINDOW)[None], rel_table[_t5_bucket(rel)].T, NEG))
    return jnp.stack(rows, axis=0).reshape(3, SWA_KV_HEADS, SWA_HEADS // SWA_KV_HEADS, -1).astype(F32)


def _od_main_weight(wt):
    mlq_lo, mlq_hi = 2560, 4096
    mlq = wt[mlq_lo:mlq_hi].reshape(MLA_HEADS, MLA_NOPE + MLA_ROPE, -1)
    nope = mlq[:, :MLA_NOPE].reshape(MLA_HEADS * MLA_NOPE, -1)
    rope = mlq[:, MLA_NOPE:].reshape(MLA_HEADS * MLA_ROPE, -1)
    return jnp.concatenate([nope, rope, wt[:mlq_lo], wt[mlq_hi:OD_KR_LO], wt[OD_KR_HI:]], axis=0).astype(BF16)


def _rope_tables(positions):
    half = MLA_ROPE // 2
    inv_freq = ROPE_BASE ** (-jnp.arange(half, dtype=F32) / half)
    ang = positions.astype(F32).reshape(-1, 1) * inv_freq
    cos, sin = jnp.cos(ang), jnp.sin(ang)
    z = jnp.zeros((ang.shape[0], LANES - MLA_ROPE), F32)
    return jnp.concatenate([cos, cos, z], axis=1), jnp.concatenate([-sin, sin, z], axis=1)


def _side_weight(wt, lo, hi):
    return jnp.pad(wt[lo:hi], ((0, LANES - (hi - lo)), (0, 0))).astype(BF16)


def _mem_kv(mem2, gain, w_kv):
    zero_side = jnp.zeros((LANES, D_MODEL), BF16)
    memkv, _ = _norm_proj(mem2, gain, jnp.swapaxes(w_kv, 0, 1).astype(BF16), zero_side, tm=mem2.shape[0] // 2,
                          tn=512, out_dtype=F32)
    return memkv


def _even_layer(x2, mem2, rel_bias, norm_g, w_in, conv_w, a_log, dt_bias, gdn_gain, dq_gain, dk_gain, lam, subln,
                mem_norm, mem_w_kv, mem_qn, mem_kn, w_out, lambda_init, bsz, seq):
    wt = jnp.swapaxes(w_in, 0, 1)
    w_main = jnp.concatenate([wt[:EV_SIDE_LO], wt[EV_SIDE_HI:]], axis=0).astype(BF16)
    h, side = _norm_proj(x2, norm_g, w_main, _side_weight(wt, EV_SIDE_LO, EV_SIDE_HI), tm=1024, tn=EV_MAIN // 6,
                         out_dtype=BF16)

    nh = GDN_HEADS
    bar = side[:, :4 * nh].reshape(bsz, seq, 4, nh).transpose(0, 3, 2, 1)
    bar = jnp.pad(bar, ((0, 0), (0, 0), (0, 4), (0, 0)))
    prr = jnp.pad(jnp.stack([a_log.T, dt_bias.T], axis=-1), ((0, 0), (2, 4), (0, 0)))
    ya = _gdn(h, bar, conv_w, prr, gdn_gain, bsz, seq)

    dq, dk, dv = _diff_prep(h, dq_gain, dk_gain, tm=512)
    bias = _diff_bias_vecs(rel_bias, ATT_TQ, ATT_TK)
    yb = _diff_flash(dq, dk, dv, h, bias, lam, subln, lambda_init, bsz, seq)

    memkv = _mem_kv(mem2, mem_norm, mem_w_kv)
    ym = _mem_attn(h, memkv, mem_qn, mem_kn, EV_MQ, EV_MG, bsz, seq, tq=1024)
    return _out_proj(x2, ya, yb, ym, w_out, tm=512)


def _odd_layer(x2, mem2, positions, rel_bias, norm_g, w_in, swa_qn, swa_kn, sink, kv_norm, w_kv_up, mla_qn, mla_kn,
               mem_norm, mem_w_kv, mem_qn, mem_kn, w_out, bsz, seq):
    wt = jnp.swapaxes(w_in, 0, 1)
    h, side = _norm_proj(x2, norm_g, _od_main_weight(wt), _side_weight(wt, OD_KR_LO, OD_KR_HI), tm=1024,
                         tn=OD_MAIN // 4, out_dtype=BF16)

    ya = _swa(h, _swa_bias_vecs(rel_bias), sink, swa_qn, swa_kn, bsz, seq)

    cos, sin = _rope_tables(positions)
    qm, km, vm = _mla_prep(h, side, cos, sin, kv_norm, w_kv_up, mla_qn, mla_kn, tm=256)
    yb = _mla_flash(qm, km, vm, h, bsz, seq)

    memkv = _mem_kv(mem2, mem_norm, mem_w_kv)
    ym = _mem_attn(h, memkv, mem_qn, mem_kn, OD_MQ, OD_MG, bsz, seq, tq=1024)
    return _out_proj(x2, ya, yb, ym, w_out, tm=512)


def kernel(x, mem, positions, rel_bias, ev_norm, ev_w_in, ev_conv, ev_a_log, ev_dt_bias, ev_gdn_norm, ev_diff_qnorm, ev_diff_knorm, ev_diff_lambda, ev_diff_subln, ev_mem_norm, ev_mem_w_kv, ev_mem_qnorm, ev_mem_knorm, ev_w_out, od_norm, od_w_in, od_swa_qnorm, od_swa_knorm, od_swa_sink, od_mla_kv_norm, od_mla_w_kv_up, od_mla_qnorm, od_mla_knorm, od_mem_norm, od_mem_w_kv, od_mem_qnorm, od_mem_knorm, od_w_out):
    bsz, seq, d = x.shape
    depth = ev_norm.shape[0] + od_norm.shape[0]
    x2 = x.reshape(bsz * seq, d)
    mem2 = mem.reshape(bsz * mem.shape[1], d)
    for layer in range(depth):
        i = layer // 2
        if layer % 2 == 0:
            lambda_init = 0.8 - 0.6 * math.exp(-0.3 * layer)
            x2 = _even_layer(x2, mem2, rel_bias, ev_norm[i], ev_w_in[i], ev_conv[i], ev_a_log[i], ev_dt_bias[i],
                             ev_gdn_norm[i], ev_diff_qnorm[i], ev_diff_knorm[i], ev_diff_lambda[i], ev_diff_subln[i],
                             ev_mem_norm[i], ev_mem_w_kv[i], ev_mem_qnorm[i], ev_mem_knorm[i], ev_w_out[i],
                             lambda_init, bsz, seq)
        else:
            x2 = _odd_layer(x2, mem2, positions, rel_bias, od_norm[i], od_w_in[i], od_swa_qnorm[i], od_swa_knorm[i],
                            od_swa_sink[i], od_mla_kv_norm[i], od_mla_w_kv_up[i], od_mla_qnorm[i], od_mla_knorm[i],
                            od_mem_norm[i], od_mem_w_kv[i], od_mem_qnorm[i], od_mem_knorm[i], od_w_out[i], bsz, seq)
    return x2.reshape(bsz, seq, d)
```

```python
import functools
import math

import jax
import jax.numpy as jnp
from jax import lax
from jax.experimental import pallas as pl
from jax.experimental.pallas import tpu as pltpu

F32 = jnp.float32
BF16 = jnp.bfloat16
EPS = 1e-6
NEG = -1e30

V7X_VMEM_BYTES = 64 * 1024 * 1024
VMEM_LIMIT = V7X_VMEM_BYTES - 8 * 1024 * 1024
LANES = 128

D_MODEL = 2048
MEM_LEN = 256
GDN_HEADS, GDN_DK, GDN_DV, GDN_CONV = 8, 128, 128, 5
DIFF_HEADS, DIFF_DQK, DIFF_DV = 8, 64, 128
SWA_HEADS, SWA_KV_HEADS, SWA_DH, WINDOW = 8, 2, 128, 128
MLA_HEADS, MLA_NOPE, MLA_ROPE, MLA_DV, MLA_KV_RANK = 8, 128, 64, 128, 512
ROPE_BASE = 10000.0
MEM_HEADS, MEM_DH = 4, 128
REL_BUCKETS, REL_MAX_DIST = 32, 128

GDN_CHUNK = 256
GDN_LEVELS = (GDN_CHUNK // 2).bit_length() - 1
GDN_PREP_CHUNKS = 2
ATT_TQ = 512
ATT_TK = 512
ATT_RB = 32
LOG2E = math.log2(math.e)
SWA_TQ = 256
SWA_TW = SWA_TQ + 2 * WINDOW

EV_GQ, EV_GK, EV_GV, EV_GG = 0, 1024, 2048, 3072
EV_DQ, EV_DK, EV_DV, EV_DG = 4096, 5120, 6144, 7168
EV_MQ, EV_MG = 8192, 8704
EV_MAIN = 9216
EV_SIDE_LO, EV_SIDE_HI = 3072, 3104

OD_MLQ_NOPE, OD_MLQ_ROPE = 0, 1024
OD_SQ, OD_SK, OD_SV, OD_SG = 1536, 2560, 2816, 3072
OD_CKV, OD_MLG, OD_MQ, OD_MG = 4096, 4608, 5632, 6144
OD_MAIN = 6656
OD_KR_LO, OD_KR_HI = 4608, 4672


def _cparams(sem):
    return pltpu.CompilerParams(dimension_semantics=sem, vmem_limit_bytes=VMEM_LIMIT)


def _dot(a, b):
    return jnp.dot(a, b, preferred_element_type=F32)


def _dot_nt(a, b):
    return lax.dot_general(a, b, (((1,), (1,)), ((), ())), preferred_element_type=F32)


def _silu(x):
    x = x.astype(F32)
    return x * jax.nn.sigmoid(x)


def _softplus(x):
    return jnp.maximum(x, 0.0) + jnp.log(1.0 + jnp.exp(-jnp.abs(x)))


def _lane_sum(x, scale=1.0):
    return _dot(x.astype(BF16), jnp.ones((LANES, LANES), BF16)) * scale


def _rms(x, gain):
    x = x.astype(F32)
    if x.shape[-1] == LANES:
        ms = _lane_sum(x * x, 1.0 / LANES)
    else:
        ms = jnp.mean(x * x, axis=-1, keepdims=True)
    return x * lax.rsqrt(ms + EPS) * gain


def _norm_proj_kernel(x_ref, g_ref, w_ref, ws_ref, o_ref, os_ref, xn_ref):
    @pl.when(pl.program_id(1) == 0)
    def _():
        xn = _rms(x_ref[...], g_ref[...]).astype(BF16)
        xn_ref[...] = xn
        os_ref[...] = _dot_nt(xn, ws_ref[...])

    o_ref[...] = _dot_nt(xn_ref[...], w_ref[...]).astype(o_ref.dtype)


def _norm_proj(x, gain, w_main, w_side, tm, tn, out_dtype):
    m, k = x.shape
    n = w_main.shape[0]
    ns = w_side.shape[0]
    assert m % tm == 0 and n % tn == 0
    return pl.pallas_call(
        _norm_proj_kernel,
        grid=(m // tm, n // tn),
        in_specs=[
            pl.BlockSpec((tm, k), lambda i, j: (i, 0)),
            pl.BlockSpec((1, k), lambda i, j: (0, 0)),
            pl.BlockSpec((tn, k), lambda i, j: (j, 0)),
            pl.BlockSpec((ns, k), lambda i, j: (0, 0)),
        ],
        out_specs=[
            pl.BlockSpec((tm, tn), lambda i, j: (i, j)),
            pl.BlockSpec((tm, ns), lambda i, j: (i, 0)),
        ],
        out_shape=[jax.ShapeDtypeStruct((m, n), out_dtype), jax.ShapeDtypeStruct((m, ns), F32)],
        scratch_shapes=[pltpu.VMEM((tm, k), BF16)],
        compiler_params=_cparams(("parallel", "arbitrary")),
        name="norm_proj",
    )(x, gain.reshape(1, k), w_main, w_side)


def _out_proj_kernel(x_ref, ya_ref, yb_ref, ym_ref, wa_ref, wb_ref, wm_ref, o_ref):
    acc = _dot(ya_ref[...], wa_ref[...])
    acc = acc + _dot(yb_ref[...], wb_ref[...])
    acc = acc + _dot(ym_ref[...], wm_ref[...])
    o_ref[...] = x_ref[...] + acc


def _out_proj(x, ya, yb, ym, w_out, tm):
    m, d = x.shape
    na, nb, nm = ya.shape[1], yb.shape[1], ym.shape[1]
    wa = w_out[:na].astype(BF16)
    wb = w_out[na:na + nb].astype(BF16)
    wm = w_out[na + nb:].astype(BF16)
    row = lambda i: (i, 0)
    fixed = lambda i: (0, 0)
    return pl.pallas_call(
        _out_proj_kernel,
        grid=(m // tm,),
        in_specs=[
            pl.BlockSpec((tm, d), row), pl.BlockSpec((tm, na), row), pl.BlockSpec((tm, nb), row),
            pl.BlockSpec((tm, nm), row),
            pl.BlockSpec((na, d), fixed), pl.BlockSpec((nb, d), fixed), pl.BlockSpec((nm, d), fixed),
        ],
        out_specs=pl.BlockSpec((tm, d), row),
        out_shape=jax.ShapeDtypeStruct((m, d), F32),
        compiler_params=_cparams(("parallel",)),
        name="out_proj",
    )(x, ya, yb, ym, wa, wb, wm)


def _gdn_kernel(q_ref, k_ref, v_ref, gate_ref, bar_ref, cwq_ref, cwk_ref, cwv_ref, prr_ref, gain_ref, y_ref,
                xp_ref, qd_ref, kw_ref, b_ref, egl_ref, o_ref, lvl_ref, tri_ref):
    seq = q_ref.shape[0]
    c = GDN_CHUNK
    nc = seq // c
    pad = 8
    scale = GDN_DK ** -0.5

    for i, src in enumerate((q_ref, k_ref, v_ref)):
        xp_ref[i, 0:pad, :] = jnp.zeros((pad, LANES), F32)
        xp_ref[i, pad + seq:2 * pad + seq, :] = jnp.zeros((pad, LANES), F32)
        xp_ref[i, pad:pad + seq, :] = src[...].astype(F32)

    hc = c // 2
    row_h = lax.broadcasted_iota(jnp.int32, (hc, hc), 0)
    col_h = lax.broadcasted_iota(jnp.int32, (hc, hc), 1)
    for bit in range(GDN_LEVELS):
        lvl_ref[bit] = (((row_h ^ col_h) >> bit) == 1).astype(BF16)
    lvl_ref[GDN_LEVELS] = (row_h == col_h).astype(BF16)
    tri_ref[0] = jnp.where(row_h > col_h, 0.0, NEG)
    tri_ref[1] = jnp.where(row_h < col_h, 0.0, NEG)
    tri_ref[2] = (row_h == col_h).astype(F32)

    lane_c = lax.broadcasted_iota(jnp.int32, (8, c), 1)
    sub_c = lax.broadcasted_iota(jnp.int32, (8, c), 0)

    def prefix(x):
        s = 1
        while s < c:
            x = x + jnp.where(lane_c >= s, pltpu.roll(x, s, 1), 0.0)
            s *= 2
        return x

    def suffix(x):
        s = 1
        while s < c:
            x = x + jnp.where(lane_c < c - s, pltpu.roll(x, c - s, 1), 0.0)
            s *= 2
        return x

    def conv(i, cw_ref, t0):
        half = (GDN_CONV - 1) // 2
        acc = None
        for j in range(GDN_CONV):
            tap = xp_ref[i, pl.ds(t0 + (pad - half + j), c), :] * cw_ref[j:j + 1, :]
            acc = tap if acc is None else acc + tap
        return _silu(acc)

    def l2n(x):
        return x * lax.rsqrt(_lane_sum(x * x) + EPS)

    half = lambda i: slice(i * hc, (i + 1) * hc)


    def prologue(ns, chains):
        st = []
        for n in ns:
            t0 = pl.multiple_of(n * c, c)
            st.append(dict(n=n, t0=t0, rows=pl.ds(t0, c)))
        for name, idx, cw_ref in (("q", 0, cwq_ref), ("k", 1, cwk_ref), ("v", 2, cwv_ref)):
            for s in st:
                s[name] = conv(idx, cw_ref, s["t0"])
            yield
        for s in st:
            s["q"], s["k"] = l2n(s["q"]), l2n(s["k"])
        yield
        for s in st:
            s["qs"] = s["q"] * scale
            s["kb"] = s["k"].astype(BF16)
            s["qk"] = _dot_nt(s["qs"].astype(BF16), s["kb"])
            bar = bar_ref[:, s["rows"]]
            g_r = (-LOG2E * jnp.exp(prr_ref[:, 0:1])) * _softplus(bar + prr_ref[:, 1:2])
            s["pre"], s["suf"] = prefix(g_r), suffix(g_r)
            s["tot"] = jnp.sum(g_r, axis=1, keepdims=True)
            packed = jnp.where(sub_c < 2, jax.nn.sigmoid(bar), jnp.where(sub_c == 2, s["pre"], s["suf"]))
            s["cols"] = jnp.concatenate([packed, jnp.zeros((LANES - 8, c), F32)], axis=0).T
            o_ref[s["rows"], :] = jnp.zeros((c, LANES), F32)
        yield
        for d in range(2):
            for s in st:
                s["kkb", d] = _dot_nt((s["k"] * s["cols"][:, d:d + 1]).astype(BF16), s["kb"])
            yield
        for d in range(2):
            first, second = (0, 1) if d == 0 else (1, 0)
            for s in st:
                gr = s["pre"][2:3, :] if d == 0 else s["suf"][3:4, :]
                gc = s["cols"][:, 2 + d:3 + d]
                blocks, attn = {}, {}
                for r, t in ((first, first), (second, second), (second, first)):
                    e = gc[half(r)] - gr[:, half(t)]
                    if r == t:
                        e = e + tri_ref[d]
                    dec = jnp.exp2(e)
                    blocks[r, t] = (s["kkb", d][half(r), half(t)] * dec).astype(BF16)
                    if r == t:
                        dec = dec + tri_ref[2]
                    attn[r, t] = (s["qk"][half(r), half(t)] * dec).astype(BF16)
                attn[first, second] = jnp.zeros((hc, hc), BF16)
                a_full = jnp.concatenate([jnp.concatenate([attn[r, 0], attn[r, 1]], axis=1) for r in range(2)], axis=0)
                diag = [blocks[0, 0], blocks[1, 1]]
                chains.append(dict(n=s["n"], rows=s["rows"], d=d, k=s["k"], v=s["v"],
                                   qs=s["qs"], beta=s["cols"][:, d:d + 1], gc=gc, tot=s["tot"][2 + d:3 + d, :],
                                   m=diag, off=blocks[second, first], a=a_full,
                                   p=[lvl_ref[GDN_LEVELS] - blk * lvl_ref[0] for blk in diag]))
            yield

    def levels(chains):
        for bit in range(1, GDN_LEVELS):
            lvl = lvl_ref[bit]
            for ch in chains:
                ch["x"] = [_dot(p, m * lvl).astype(BF16) for p, m in zip(ch["p"], ch["m"])]
            yield
            for ch in chains:
                ch["p"] = [p - _dot(x, p).astype(BF16) for p, x in zip(ch["p"], ch["x"])]
            yield

    def epilogue(chains):
        order = lambda ch: (0, 1) if ch["d"] == 0 else (1, 0)
        for ch in chains:
            ch["eg"] = jnp.exp2(ch["gc"])
            ch["rhs"] = jnp.concatenate([ch["v"] * ch["beta"], ch["k"] * (ch["beta"] * ch["eg"])], axis=1)
            ch["x1"] = _dot(ch["p"][order(ch)[0]], ch["rhs"][half(order(ch)[0])].astype(BF16))
        yield
        for ch in chains:
            ch["cross"] = _dot(ch["off"], ch["x1"].astype(BF16))
        yield
        for ch in chains:
            second = order(ch)[1]
            ch["x2"] = _dot(ch["p"][second], (ch["rhs"][half(second)] - ch["cross"]).astype(BF16))
        yield
        for ch in chains:
            xs = (ch["x1"], ch["x2"]) if ch["d"] == 0 else (ch["x2"], ch["x1"])
            ch["xb"] = jnp.concatenate(xs, axis=0).astype(BF16)
            kd = ch["k"] * jnp.exp2(ch["tot"] - ch["gc"])
            ch["kx"] = _dot(kd.T.astype(BF16), ch["xb"])
            ch["ax"] = _dot(ch["a"], ch["xb"])
        yield
        for ch in chains:
            d, rows = ch["d"], ch["rows"]
            blk = pl.ds(pl.multiple_of(ch["n"] * GDN_DK, GDN_DK), GDN_DK)
            b_ref[d, blk, :] = ch["kx"][:, :GDN_DV]
            kw_ref[d, blk, :] = ch["kx"][:, GDN_DV:].astype(BF16)
            o_ref[rows, :] = o_ref[rows, :] + ch["ax"][:, :GDN_DV]
            qd_ref[d, rows, :] = (ch["qs"] * ch["eg"] - ch["ax"][:, GDN_DV:]).astype(BF16)
            egl_ref[d, pl.ds(pl.multiple_of(ch["n"] * 8, 8), 8), :] = jnp.broadcast_to(jnp.exp2(ch["tot"]), (8, LANES))
        yield

    def interleave(*gens):
        gens = list(gens)
        while gens:
            for g in list(gens):
                if next(g, StopIteration) is StopIteration:
                    gens.remove(g)

    def prep(n, carry):
        base = n * 2 * GDN_PREP_CHUNKS
        group_a = [base + i for i in range(GDN_PREP_CHUNKS)]
        group_b = [base + GDN_PREP_CHUNKS + i for i in range(GDN_PREP_CHUNKS)]
        chains_a, chains_b = [], []
        interleave(prologue(group_a, chains_a))
        interleave(levels(chains_a), prologue(group_b, chains_b))
        interleave(levels(chains_b), epilogue(chains_a))
        interleave(epilogue(chains_b))
        return carry

    lax.fori_loop(0, nc // (2 * GDN_PREP_CHUNKS), prep, 0)

    def scan(n, carry):
        states = list(carry)
        for d in range(2):
            idx = n if d == 0 else nc - 1 - n
            t0 = pl.multiple_of(idx * c, c)
            rows = pl.ds(t0, c)
            blk = pl.ds(pl.multiple_of(idx * GDN_DK, GDN_DK), GDN_DK)
            s = states[d]
            sb = s.astype(BF16)
            o_ref[rows, :] = o_ref[rows, :] + _dot(qd_ref[d, rows, :], sb)
            egl = egl_ref[d, pl.ds(pl.multiple_of(idx * 8, 8), 8), :][0:1, :]
            states[d] = s * egl - _dot(kw_ref[d, blk, :], sb) + b_ref[d, blk, :]
        return tuple(states)

    zero = jnp.zeros((GDN_DK, GDN_DV), F32)
    lax.fori_loop(0, nc, scan, (zero, zero))

    def fin(n, carry):
        rows = pl.ds(pl.multiple_of(n * c, c), c)
        y = _rms(o_ref[rows, :], gain_ref[...]) * _silu(gate_ref[rows, :])
        y_ref[rows, :] = y.astype(y_ref.dtype)
        return carry

    lax.fori_loop(0, nc, fin, 0)


def _gdn(h, bar, conv_w, prr, gain, bsz, seq):
    nh = GDN_HEADS
    blk = lambda off: pl.BlockSpec((seq, LANES), lambda b, hh, off=off: (b, off // LANES + hh))
    cw = lambda off: pl.BlockSpec((GDN_CONV, LANES), lambda b, hh, off=off: (0, off // LANES + hh))
    c = GDN_CHUNK
    assert seq % (2 * GDN_PREP_CHUNKS * c) == 0
    return pl.pallas_call(
        _gdn_kernel,
        grid=(bsz, nh),
        in_specs=[
            blk(EV_GQ), blk(EV_GK), blk(EV_GV), blk(EV_GG),
            pl.BlockSpec((None, None, 8, seq), lambda b, hh: (b, hh, 0, 0)),
            cw(0), cw(GDN_HEADS * GDN_DK), cw(2 * GDN_HEADS * GDN_DK),
            pl.BlockSpec((None, 8, 2), lambda b, hh: (hh, 0, 0)),
            pl.BlockSpec((1, LANES), lambda b, hh: (0, 0)),
        ],
        out_specs=pl.BlockSpec((seq, LANES), lambda b, hh: (b, hh)),
        out_shape=jax.ShapeDtypeStruct((bsz * seq, nh * GDN_DV), BF16),
        scratch_shapes=[
            pltpu.VMEM((3, seq + 16, LANES), F32),
            pltpu.VMEM((2, seq, LANES), BF16),
            pltpu.VMEM((2, (seq // c) * GDN_DK, GDN_DV), BF16),
            pltpu.VMEM((2, (seq // c) * GDN_DK, GDN_DV), F32),
            pltpu.VMEM((2, (seq // c) * 8, LANES), F32),
            pltpu.VMEM((seq, LANES), F32),
            pltpu.VMEM((GDN_LEVELS + 1, c // 2, c // 2), BF16),
            pltpu.VMEM((3, c // 2, c // 2), F32),
        ],
        compiler_params=_cparams(("parallel", "parallel")),
        name="gdn",
    )(h, h, h, h, bar, conv_w, conv_w, conv_w, prr, gain.reshape(1, LANES))


def _flash_scratch(nmaps, tq, tk, dv):
    per_map = [pltpu.VMEM((tq, tk), F32), pltpu.VMEM((tq, tk), F32), pltpu.VMEM((tq, tk), BF16),
               pltpu.VMEM((tq, LANES), F32), pltpu.VMEM((tq, LANES), F32), pltpu.VMEM((tq, dv), F32)]
    return per_map * nmaps


def _flash_core(qs, k_ref, v_ref, nsteps, step_of, loop, scratch):
    nmaps = len(qs)
    maps = [scratch[6 * i:6 * i + 6] for i in range(nmaps)]
    tq, tk = maps[0][0].shape
    lo, hi = loop
    assert tk % LANES == 0 and tq % ATT_RB == 0 and 0 < lo <= hi < nsteps and (hi - lo) % 2 == 0
    nlb = tk // LANES

    for s0, s1, p, m, l, acc in maps:
        m[...] = jnp.full(m.shape, NEG, F32)
        l[...] = jnp.zeros(l.shape, F32)
        acc[...] = jnp.zeros(acc.shape, F32)

    def chunk(ref, t):
        j = step_of(t)[0]
        start = j * tk if isinstance(j, int) else pl.multiple_of(j * tk, tk)
        return ref[pl.ds(start, tk), :]

    def qk(i, t, slot):
        maps[i][slot][...] = _dot_nt(qs[i], chunk(k_ref, t))

    def pv(i, t):
        acc = maps[i][5]
        acc[...] = acc[...] + _dot(maps[i][2][...], chunk(v_ref, t))

    def softmax(i, t, slot):
        s_ref, p_ref, m_ref, l_ref, acc_ref = maps[i][slot], maps[i][2], maps[i][3], maps[i][4], maps[i][5]
        _, tile, const = step_of(t)
        for rb in range(tq // ATT_RB):
            r = slice(rb * ATT_RB, (rb + 1) * ATT_RB)
            s = s_ref[r, :]
            if tile is not None:
                s = s + tile[r, :]
            blocks = [s[:, b * LANES:(b + 1) * LANES] for b in range(nlb)]
            mx = functools.reduce(jnp.maximum, blocks)
            m_old = m_ref[r, :]
            mx = jnp.broadcast_to(jnp.max(mx, axis=-1, keepdims=True), m_old.shape)
            if const is not None:
                mx = mx + const
            m_new = jnp.maximum(m_old, mx)
            alpha = jnp.exp2(m_old - m_new)
            shift = m_new if const is None else m_new - const
            ps = [jnp.exp2((b - shift).astype(BF16)) for b in blocks]
            row_sum = jnp.sum(functools.reduce(jnp.add, [p.astype(F32) for p in ps]), axis=-1, keepdims=True)
            l_ref[r, :] = alpha * l_ref[r, :] + jnp.broadcast_to(row_sum, m_old.shape)
            m_ref[r, :] = m_new
            acc_ref[r, :] = acc_ref[r, :] * alpha
            p_ref[r, :] = jnp.concatenate(ps, axis=1)

    def stage(t, slot, first=False, last=False):
        for i in range(nmaps):
            if i > 0:
                pv(i - 1, t)
            elif not first:
                pv(nmaps - 1, t - 1)
            if not last:
                qk(i, t + 1, 1 - slot)
            softmax(i, t, slot)

    def pair(tt, carry):
        for u in range(2):
            stage(lo + 2 * tt + u, (lo + u) % 2)
        return carry

    for i in range(nmaps):
        qk(i, 0, 0)
    for t in range(lo):
        stage(t, t % 2, first=t == 0)
    lax.fori_loop(0, (hi - lo) // 2, pair, 0)
    for t in range(hi, nsteps):
        stage(t, t % 2, last=t == nsteps - 1)
    pv(nmaps - 1, nsteps - 1)
    return [(mp[5][...], mp[4][...]) for mp in maps]


def _diff_prep_kernel(q_ref, k_ref, v_ref, qg_ref, kg_ref, qo_ref, ko_ref, vo_ref):
    r = lax.broadcasted_iota(jnp.int32, (LANES, LANES), 0) < DIFF_DQK
    cc = lax.broadcasted_iota(jnp.int32, (LANES, LANES), 1) < DIFF_DQK
    half_mean = jnp.where(r == cc, 1.0 / DIFF_DQK, 0.0).astype(BF16)

    def halfnorm(x, gain):
        x = x.astype(F32)
        return x * lax.rsqrt(_dot((x * x).astype(BF16), half_mean) + EPS) * gain

    for hh in range(DIFF_HEADS):
        sl = slice(hh * LANES, (hh + 1) * LANES)
        qo_ref[:, sl] = (halfnorm(q_ref[:, sl], qg_ref[...]) * (DIFF_DQK ** -0.5 * LOG2E)).astype(BF16)
        ko_ref[:, sl] = halfnorm(k_ref[:, sl], kg_ref[...]).astype(BF16)
    vo_ref[...] = v_ref[...].astype(BF16)


def _diff_prep(h, qg, kg, tm):
    m = h.shape[0]
    w = DIFF_HEADS * LANES
    spec = lambda off: pl.BlockSpec((tm, w), lambda i, off=off: (i, off // w))
    vec = pl.BlockSpec((1, LANES), lambda i: (0, 0))
    out = jax.ShapeDtypeStruct((m, w), BF16)
    return pl.pallas_call(
        _diff_prep_kernel,
        grid=(m // tm,),
        in_specs=[spec(EV_DQ), spec(EV_DK), spec(EV_DV), vec, vec],
        out_specs=[pl.BlockSpec((tm, w), lambda i: (i, 0))] * 3,
        out_shape=[out, out, out],
        compiler_params=_cparams(("parallel",)),
        name="diff_prep",
    )(h, h, h, jnp.tile(qg, 2).reshape(1, LANES), jnp.tile(kg, 2).reshape(1, LANES))


def _toeplitz(vec, tq, tk):
    w = vec.shape[-1]
    full = pltpu.roll(jnp.broadcast_to(vec, (tq, w)), w - (tq - 1), 1, stride=1, stride_axis=0)
    return full[:, :tk]


def _diff_flash_kernel(q_ref, k_ref, v_ref, gate_ref, bvec_ref, lam_ref, sub_ref, y_ref, bias_ref, *scratch,
                       lambda_init):
    tq = q_ref.shape[0]
    tk = bias_ref.shape[-1]
    qi = pl.program_id(2)

    @pl.when(qi == 0)
    def _():
        for d in range(bias_ref.shape[0]):
            bias_ref[d] = _toeplitz(bvec_ref[d:d + 1, :], tq, tk)

    q = q_ref[...]
    lane = lax.broadcasted_iota(jnp.int32, (1, LANES), 1)
    zero = jnp.zeros_like(q)
    q0 = jnp.where(lane < DIFF_DQK, q, zero)
    q1 = jnp.where(lane < DIFF_DQK, zero, q)

    nk = k_ref.shape[0] // tk
    near = 3
    n0 = jnp.clip(qi - 1, 0, nk - near)
    c_before, c_after = bvec_ref[0:1, 0:1], bvec_ref[4:5, 0:1]
    nfar = nk - near

    def step_of(t):
        if isinstance(t, int) and t >= nfar:
            j = n0 + (t - nfar)
            return j, bias_ref.at[jnp.clip(j - qi, -2, 2) + 2], None
        j = jnp.where(t < n0, t, t + near)
        j = jnp.where(t < nfar, j, n0 + (t - nfar))
        return j, None, jnp.where(j < qi, c_before, c_after)

    loop = (1, nfar - (nfar - 1) % 2)
    (a0, l0), (a1, l1) = _flash_core([q0, q1], k_ref, v_ref, nk, step_of, loop, scratch)

    lam = lam_ref[...]
    lam_full = (jnp.exp(jnp.sum(lam[0:1] * lam[1:2], axis=-1, keepdims=True))
                - jnp.exp(jnp.sum(lam[2:3] * lam[3:4], axis=-1, keepdims=True)) + lambda_init)
    o = a0 / l0 - lam_full * (a1 / l1)
    o = _rms(o, sub_ref[...]) * (1.0 - lambda_init)
    y_ref[...] = (o * _silu(gate_ref[...])).astype(y_ref.dtype)


def _diff_flash(dq, dk, dv, h, bvec, lam, subln, lambda_init, bsz, seq):
    tq, tk = ATT_TQ, ATT_TK
    nq = seq // tq
    nh = DIFF_HEADS
    return pl.pallas_call(
        functools.partial(_diff_flash_kernel, lambda_init=lambda_init),
        grid=(bsz, nh, nq),
        in_specs=[
            pl.BlockSpec((tq, LANES), lambda b, hh, i: (b * nq + i, hh)),
            pl.BlockSpec((seq, LANES), lambda b, hh, i: (b, hh)),
            pl.BlockSpec((seq, LANES), lambda b, hh, i: (b, hh)),
            pl.BlockSpec((tq, LANES), lambda b, hh, i: (b * nq + i, EV_DG // LANES + hh)),
            pl.BlockSpec((None,) + bvec.shape[1:], lambda b, hh, i: (hh, 0, 0)),
            pl.BlockSpec((4, DIFF_DQK), lambda b, hh, i: (0, 0)),
            pl.BlockSpec((1, LANES), lambda b, hh, i: (0, 0)),
        ],
        out_specs=pl.BlockSpec((tq, LANES), lambda b, hh, i: (b * nq + i, hh)),
        out_shape=jax.ShapeDtypeStruct((bsz * seq, nh * DIFF_DV), BF16),
        scratch_shapes=[pltpu.VMEM((5, tq, tk), F32)] + _flash_scratch(2, tq, tk, DIFF_DV),
        compiler_params=_cparams(("parallel", "parallel", "arbitrary")),
        name="diff_flash",
    )(dq, dk, dv, h, bvec, lam, subln.reshape(1, LANES))


def _mem_attn_kernel(q_ref, gate_ref, mk_ref, mv_ref, qg_ref, kg_ref, y_ref):
    q = (_rms(q_ref[...], qg_ref[...]) * (MEM_DH ** -0.5)).astype(BF16)
    mk = _rms(mk_ref[...], kg_ref[...]).astype(BF16)
    s = _dot_nt(q, mk)
    p = jnp.exp(s - jnp.max(s, axis=-1, keepdims=True))
    o = _dot(p.astype(BF16), mv_ref[...].astype(BF16)) / jnp.sum(p, axis=-1, keepdims=True)
    y_ref[...] = (o * _silu(gate_ref[...])).astype(y_ref.dtype)


def _mem_attn(h, memkv, qg, kg, q_off, g_off, bsz, seq, tq):
    nq = seq // tq
    nh = MEM_HEADS
    vec = pl.BlockSpec((1, LANES), lambda b, hh, i: (0, 0))
    return pl.pallas_call(
        _mem_attn_kernel,
        grid=(bsz, nh, nq),
        in_specs=[
            pl.BlockSpec((tq, LANES), lambda b, hh, i: (b * nq + i, q_off // LANES + hh)),
            pl.BlockSpec((tq, LANES), lambda b, hh, i: (b * nq + i, g_off // LANES + hh)),
            pl.BlockSpec((MEM_LEN, LANES), lambda b, hh, i: (b, hh)),
            pl.BlockSpec((MEM_LEN, LANES), lambda b, hh, i: (b, nh + hh)),
            vec, vec,
        ],
        out_specs=pl.BlockSpec((tq, LANES), lambda b, hh, i: (b * nq + i, hh)),
        out_shape=jax.ShapeDtypeStruct((bsz * seq, nh * MEM_DH), BF16),
        compiler_params=_cparams(("parallel", "parallel", "parallel")),
        name="mem_attn",
    )(h, h, memkv, memkv, qg.reshape(1, LANES), kg.reshape(1, LANES))


def _swa_kernel(sink_ref, q_ref, k_ref, v_ref, gate_ref, bias_ref, qg_ref, kg_ref, y_ref, tile_ref):
    tq = q_ref.shape[0]
    seq = k_ref.shape[0]
    grp = SWA_HEADS // SWA_KV_HEADS
    kvh = pl.program_id(1)
    qi = pl.program_id(2)
    nq = pl.num_programs(2)

    @pl.when(jnp.logical_or(qi <= 1, qi == nq - 1))
    def _():
        for g in range(grp):
            tile_ref[g] = _toeplitz(bias_ref[g:g + 1, :], tq, SWA_TW)

    ws = pl.multiple_of(jnp.clip(qi * tq - WINDOW, 0, seq - SWA_TW), WINDOW)
    kw = _rms(k_ref[pl.ds(ws, SWA_TW), :], kg_ref[...]).astype(BF16)
    vw = v_ref[pl.ds(ws, SWA_TW), :].astype(BF16)
    heads = [slice(g * SWA_DH, (g + 1) * SWA_DH) for g in range(grp)]
    qs = [(_rms(q_ref[:, sl], qg_ref[...]) * (SWA_DH ** -0.5)).astype(BF16) for sl in heads]
    ss = [_dot_nt(q, kw) + tile_ref[g] for g, q in enumerate(qs)]
    ps, dens = [], []
    for g, s in enumerate(ss):
        sink = sink_ref[kvh * grp + g]
        mx = jnp.maximum(jnp.max(s, axis=-1, keepdims=True), sink)
        p = jnp.exp(s - mx)
        dens.append(jnp.sum(p, axis=-1, keepdims=True) + jnp.exp(sink - mx))
        ps.append(p.astype(BF16))
    for sl, p, den in zip(heads, ps, dens):
        y_ref[:, sl] = (_dot(p, vw) / den * _silu(gate_ref[:, sl])).astype(y_ref.dtype)


def _swa(h, bias, sink, qg, kg, bsz, seq):
    tq = SWA_TQ
    nq = seq // tq
    grp = SWA_HEADS // SWA_KV_HEADS
    gw = grp * SWA_DH
    assert seq >= SWA_TW and nq >= 2
    vec = pl.BlockSpec((1, LANES), lambda b, kv, i: (0, 0))

    def bias_idx(b, kv, i):
        return (jnp.where(i == 0, 0, jnp.where(i == nq - 1, 2, 1)), kv, 0, 0)

    return pl.pallas_call(
        _swa_kernel,
        grid=(bsz, SWA_KV_HEADS, nq),
        in_specs=[
            pl.BlockSpec(memory_space=pltpu.SMEM),
            pl.BlockSpec((tq, gw), lambda b, kv, i: (b * nq + i, OD_SQ // gw + kv)),
            pl.BlockSpec((seq, LANES), lambda b, kv, i: (b, OD_SK // LANES + kv)),
            pl.BlockSpec((seq, LANES), lambda b, kv, i: (b, OD_SV // LANES + kv)),
            pl.BlockSpec((tq, gw), lambda b, kv, i: (b * nq + i, OD_SG // gw + kv)),
            pl.BlockSpec((None, None, grp, bias.shape[-1]), bias_idx),
            vec, vec,
        ],
        out_specs=pl.BlockSpec((tq, gw), lambda b, kv, i: (b * nq + i, kv)),
        out_shape=jax.ShapeDtypeStruct((bsz * seq, SWA_HEADS * SWA_DH), BF16),
        scratch_shapes=[pltpu.VMEM((grp, tq, SWA_TW), F32)],
        compiler_params=_cparams(("parallel", "parallel", "arbitrary")),
        name="swa",
    )(sink, h, h, h, h, bias, qg.reshape(1, LANES), kg.reshape(1, LANES))


def _mla_prep_kernel(q_ref, ckv_ref, kr_ref, cos_ref, sin_ref, kvg_ref, wup_ref,
                     qgn_ref, qgr_ref, kgn_ref, kgr_ref, qo_ref, ko_ref, vo_ref):
    dqk = MLA_NOPE + MLA_ROPE
    lane = lax.broadcasted_iota(jnp.int32, (1, LANES), 1)
    lo = lane < MLA_ROPE
    first = lane < MLA_ROPE // 2
    cos = cos_ref[...]
    sin = sin_ref[...]

    def rope(t):
        rot = jnp.where(first, pltpu.roll(t, LANES - MLA_ROPE // 2, 1), pltpu.roll(t, MLA_ROPE // 2, 1))
        return t * cos + rot * sin

    ckv = _rms(ckv_ref[...], kvg_ref[...]).astype(BF16)
    kv = _dot(ckv, wup_ref[...])
    kr = kr_ref[...]
    kr2 = kr * kr
    c = dqk ** -0.5 * LOG2E
    hw = MLA_NOPE + MLA_DV
    for hh in range(MLA_HEADS):
        kn = kv[:, hh * hw:hh * hw + MLA_NOPE]
        inv = lax.rsqrt(_lane_sum(kn * kn + kr2, 1.0 / dqk) + EPS)
        ko_ref[:, hh * 2 * LANES:hh * 2 * LANES + LANES] = (kn * inv * kgn_ref[...]).astype(BF16)
        ko_ref[:, hh * 2 * LANES + LANES:(hh + 1) * 2 * LANES] = rope(kr * inv * kgr_ref[...]).astype(BF16)
        vo_ref[:, hh * LANES:(hh + 1) * LANES] = kv[:, hh * hw + MLA_NOPE:(hh + 1) * hw].astype(BF16)

        qn = q_ref[:, hh * LANES:(hh + 1) * LANES].astype(F32)
        pair = q_ref[:, OD_MLQ_ROPE + (hh // 2) * LANES:OD_MLQ_ROPE + (hh // 2 + 1) * LANES].astype(F32)
        if hh % 2 == 1:
            pair = pltpu.roll(pair, MLA_ROPE, 1)
        qr = jnp.where(lo, pair, 0.0)
        inv = lax.rsqrt(_lane_sum(qn * qn + qr * qr, 1.0 / dqk) + EPS)
        qo_ref[:, hh * 2 * LANES:hh * 2 * LANES + LANES] = (qn * inv * qgn_ref[...] * c).astype(BF16)
        qo_ref[:, hh * 2 * LANES + LANES:(hh + 1) * 2 * LANES] = (rope(qr * inv * qgr_ref[...]) * c).astype(BF16)


def _mla_prep(h, side, cos, sin, kv_gain, w_up, qg, kg, tm):
    m = h.shape[0]
    nh = MLA_HEADS
    qw = nh * (MLA_NOPE + MLA_ROPE)
    pad = lambda g: jnp.pad(g[MLA_NOPE:], (0, LANES - MLA_ROPE)).reshape(1, LANES)
    vec = pl.BlockSpec((1, LANES), lambda i: (0, 0))
    row = lambda i: (i, 0)
    return pl.pallas_call(
        _mla_prep_kernel,
        grid=(m // tm,),
        in_specs=[
            pl.BlockSpec((tm, qw), row),
            pl.BlockSpec((tm, MLA_KV_RANK), lambda i: (i, OD_CKV // MLA_KV_RANK)),
            pl.BlockSpec((tm, LANES), row),
            pl.BlockSpec((tm, LANES), row),
            pl.BlockSpec((tm, LANES), row),
            pl.BlockSpec((1, MLA_KV_RANK), lambda i: (0, 0)),
            pl.BlockSpec(w_up.shape, lambda i: (0, 0)),
            vec, vec, vec, vec,
        ],
        out_specs=[
            pl.BlockSpec((tm, nh * 2 * LANES), row),
            pl.BlockSpec((tm, nh * 2 * LANES), row),
            pl.BlockSpec((tm, nh * MLA_DV), row),
        ],
        out_shape=[
            jax.ShapeDtypeStruct((m, nh * 2 * LANES), BF16),
            jax.ShapeDtypeStruct((m, nh * 2 * LANES), BF16),
            jax.ShapeDtypeStruct((m, nh * MLA_DV), BF16),
        ],
        compiler_params=_cparams(("parallel",)),
        name="mla_prep",
    )(h, h, side, cos, sin, kv_gain.reshape(1, MLA_KV_RANK), w_up.astype(BF16),
      qg[:MLA_NOPE].reshape(1, LANES), pad(qg), kg[:MLA_NOPE].reshape(1, LANES), pad(kg))


def _mla_flash_kernel(q_ref, k_ref, v_ref, gate_ref, y_ref, *scratch):
    nk = k_ref.shape[0] // ATT_TK
    (acc, l), = _flash_core([q_ref[...]], k_ref, v_ref, nk, lambda t: (t, None, None), (1, nk - 1 - nk % 2), scratch)
    y_ref[...] = (acc / l * _silu(gate_ref[...])).astype(y_ref.dtype)


def _mla_flash(qm, km, vm, h, bsz, seq):
    tq = ATT_TQ
    nq = seq // tq
    nh = MLA_HEADS
    return pl.pallas_call(
        _mla_flash_kernel,
        grid=(bsz, nh, nq),
        in_specs=[
            pl.BlockSpec((tq, 2 * LANES), lambda b, hh, i: (b * nq + i, hh)),
            pl.BlockSpec((seq, 2 * LANES), lambda b, hh, i: (b, hh)),
            pl.BlockSpec((seq, LANES), lambda b, hh, i: (b, hh)),
            pl.BlockSpec((tq, LANES), lambda b, hh, i: (b * nq + i, OD_MLG // LANES + hh)),
        ],
        out_specs=pl.BlockSpec((tq, LANES), lambda b, hh, i: (b * nq + i, hh)),
        out_shape=jax.ShapeDtypeStruct((bsz * seq, nh * MLA_DV), BF16),
        scratch_shapes=_flash_scratch(1, tq, ATT_TK, MLA_DV),
        compiler_params=_cparams(("parallel", "parallel", "arbitrary")),
        name="mla_flash",
    )(qm, km, vm, h)


def _t5_bucket(rel):
    half = REL_BUCKETS // 2
    max_exact = half // 2
    ret = (rel > 0).astype(jnp.int32) * half
    n = jnp.abs(rel)
    nf = jnp.maximum(n, 1).astype(F32)
    large = max_exact + (jnp.log(nf / max_exact) / math.log(REL_MAX_DIST / max_exact)
                         * (half - max_exact)).astype(jnp.int32)
    large = jnp.minimum(large, half - 1)
    return ret + jnp.where(n < max_exact, n, large)


def _round_up(n, m):
    return (n + m - 1) // m * m


def _diff_bias_vecs(rel_table, tq, tk):
    assert tq == tk and tk + 1 >= REL_MAX_DIST
    m = jnp.arange(_round_up(tq + tk - 1, LANES))
    rows = [rel_table[_t5_bucket(m - (tq - 1) + d * tk)] for d in (-2, -1, 0, 1, 2)]
    return jnp.pad(jnp.stack(rows, axis=0).transpose(2, 0, 1) * LOG2E, ((0, 0), (0, 3), (0, 0))).astype(F32)


def _swa_bias_vecs(rel_table):
    m = jnp.arange(_round_up(SWA_TQ + SWA_TW - 1, LANES))
    rows = []
    for shift in (0, WINDOW, SWA_TW - SWA_TQ):
        rel = m - (SWA_TQ - 1) - shift
        rows.append(jnp.where((jnp.abs(rel) <= WINDOW)[None], rel_table[_t5_bucket(rel)].T, NEG))
    return jnp.stack(rows, axis=0).reshape(3, SWA_KV_HEADS, SWA_HEADS // SWA_KV_HEADS, -1).astype(F32)


def _od_main_weight(wt):
    mlq_lo, mlq_hi = 2560, 4096
    mlq = wt[mlq_lo:mlq_hi].reshape(MLA_HEADS, MLA_NOPE + MLA_ROPE, -1)
    nope = mlq[:, :MLA_NOPE].reshape(MLA_HEADS * MLA_NOPE, -1)
    rope = mlq[:, MLA_NOPE:].reshape(MLA_HEADS * MLA_ROPE, -1)
    return jnp.concatenate([nope, rope, wt[:mlq_lo], wt[mlq_hi:OD_KR_LO], wt[OD_KR_HI:]], axis=0).astype(BF16)


def _rope_tables(positions):
    half = MLA_ROPE // 2
    inv_freq = ROPE_BASE ** (-jnp.arange(half, dtype=F32) / half)
    ang = positions.astype(F32).reshape(-1, 1) * inv_freq
    cos, sin = jnp.cos(ang), jnp.sin(ang)
    z = jnp.zeros((ang.shape[0], LANES - MLA_ROPE), F32)
    return jnp.concatenate([cos, cos, z], axis=1), jnp.concatenate([-sin, sin, z], axis=1)


def _side_weight(wt, lo, hi):
    return jnp.pad(wt[lo:hi], ((0, LANES - (hi - lo)), (0, 0))).astype(BF16)


def _mem_kv(mem2, gain, w_kv):
    zero_side = jnp.zeros((LANES, D_MODEL), BF16)
    memkv, _ = _norm_proj(mem2, gain, jnp.swapaxes(w_kv, 0, 1).astype(BF16), zero_side, tm=mem2.shape[0] // 2,
                          tn=512, out_dtype=F32)
    return memkv


def _even_layer(x2, mem2, rel_bias, norm_g, w_in, conv_w, a_log, dt_bias, gdn_gain, dq_gain, dk_gain, lam, subln,
                mem_norm, mem_w_kv, mem_qn, mem_kn, w_out, lambda_init, bsz, seq):
    wt = jnp.swapaxes(w_in, 0, 1)
    w_main = jnp.concatenate([wt[:EV_SIDE_LO], wt[EV_SIDE_HI:]], axis=0).astype(BF16)
    h, side = _norm_proj(x2, norm_g, w_main, _side_weight(wt, EV_SIDE_LO, EV_SIDE_HI), tm=1024, tn=EV_MAIN // 6,
                         out_dtype=BF16)

    nh = GDN_HEADS
    bar = side[:, :4 * nh].reshape(bsz, seq, 4, nh).transpose(0, 3, 2, 1)
    bar = jnp.pad(bar, ((0, 0), (0, 0), (0, 4), (0, 0)))
    prr = jnp.pad(jnp.stack([a_log.T, dt_bias.T], axis=-1), ((0, 0), (2, 4), (0, 0)))
    ya = _gdn(h, bar, conv_w, prr, gdn_gain, bsz, seq)

    dq, dk, dv = _diff_prep(h, dq_gain, dk_gain, tm=512)
    bias = _diff_bias_vecs(rel_bias, ATT_TQ, ATT_TK)
    yb = _diff_flash(dq, dk, dv, h, bias, lam, subln, lambda_init, bsz, seq)

    memkv = _mem_kv(mem2, mem_norm, mem_w_kv)
    ym = _mem_attn(h, memkv, mem_qn, mem_kn, EV_MQ, EV_MG, bsz, seq, tq=1024)
    return _out_proj(x2, ya, yb, ym, w_out, tm=512)


def _odd_layer(x2, mem2, positions, rel_bias, norm_g, w_in, swa_qn, swa_kn, sink, kv_norm, w_kv_up, mla_qn, mla_kn,
               mem_norm, mem_w_kv, mem_qn, mem_kn, w_out, bsz, seq):
    wt = jnp.swapaxes(w_in, 0, 1)
    h, side = _norm_proj(x2, norm_g, _od_main_weight(wt), _side_weight(wt, OD_KR_LO, OD_KR_HI), tm=1024,
                         tn=OD_MAIN // 4, out_dtype=BF16)

    ya = _swa(h, _swa_bias_vecs(rel_bias), sink, swa_qn, swa_kn, bsz, seq)

    cos, sin = _rope_tables(positions)
    qm, km, vm = _mla_prep(h, side, cos, sin, kv_norm, w_kv_up, mla_qn, mla_kn, tm=256)
    yb = _mla_flash(qm, km, vm, h, bsz, seq)

    memkv = _mem_kv(mem2, mem_norm, mem_w_kv)
    ym = _mem_attn(h, memkv, mem_qn, mem_kn, OD_MQ, OD_MG, bsz, seq, tq=1024)
    return _out_proj(x2, ya, yb, ym, w_out, tm=512)


def kernel(x, mem, positions, rel_bias, ev_norm, ev_w_in, ev_conv, ev_a_log, ev_dt_bias, ev_gdn_norm, ev_diff_qnorm, ev_diff_knorm, ev_diff_lambda, ev_diff_subln, ev_mem_norm, ev_mem_w_kv, ev_mem_qnorm, ev_mem_knorm, ev_w_out, od_norm, od_w_in, od_swa_qnorm, od_swa_knorm, od_swa_sink, od_mla_kv_norm, od_mla_w_kv_up, od_mla_qnorm, od_mla_knorm, od_mem_norm, od_mem_w_kv, od_mem_qnorm, od_mem_knorm, od_w_out):
    bsz, seq, d = x.shape
    depth = ev_norm.shape[0] + od_norm.shape[0]
    x2 = x.reshape(bsz * seq, d)
    mem2 = mem.reshape(bsz * mem.shape[1], d)
    for layer in range(depth):
        i = layer // 2
        if layer % 2 == 0:
            lambda_init = 0.8 - 0.6 * math.exp(-0.3 * layer)
            x2 = _even_layer(x2, mem2, rel_bias, ev_norm[i], ev_w_in[i], ev_conv[i], ev_a_log[i], ev_dt_bias[i],
                             ev_gdn_norm[i], ev_diff_qnorm[i], ev_diff_knorm[i], ev_diff_lambda[i], ev_diff_subln[i],
                             ev_mem_norm[i], ev_mem_w_kv[i], ev_mem_qnorm[i], ev_mem_knorm[i], ev_w_out[i],
                             lambda_init, bsz, seq)
        else:
            x2 = _odd_layer(x2, mem2, positions, rel_bias, od_norm[i], od_w_in[i], od_swa_qnorm[i], od_swa_knorm[i],
                            od_swa_sink[i], od_mla_kv_norm[i], od_mla_w_kv_up[i], od_mla_qnorm[i], od_mla_knorm[i],
                            od_mem_norm[i], od_mem_w_kv[i], od_mem_qnorm[i], od_mem_knorm[i], od_w_out[i], bsz, seq)
    return x2.reshape(bsz, seq, d)
```

```python
import functools
import math

import jax
import jax.numpy as jnp
from jax import lax
from jax.experimental import pallas as pl
from jax.experimental.pallas import tpu as pltpu

F32 = jnp.float32
BF16 = jnp.bfloat16
EPS = 1e-6
NEG = -1e30

V7X_VMEM_BYTES = 64 * 1024 * 1024
VMEM_LIMIT = V7X_VMEM_BYTES - 8 * 1024 * 1024
LANES = 128

D_MODEL = 2048
MEM_LEN = 256
GDN_HEADS, GDN_DK, GDN_DV, GDN_CONV = 8, 128, 128, 5
DIFF_HEADS, DIFF_DQK, DIFF_DV = 8, 64, 128
SWA_HEADS, SWA_KV_HEADS, SWA_DH, WINDOW = 8, 2, 128, 128
MLA_HEADS, MLA_NOPE, MLA_ROPE, MLA_DV, MLA_KV_RANK = 8, 128, 64, 128, 512
ROPE_BASE = 10000.0
MEM_HEADS, MEM_DH = 4, 128
REL_BUCKETS, REL_MAX_DIST = 32, 128

GDN_CHUNK = 256
GDN_LEVELS = (GDN_CHUNK // 2).bit_length() - 1
GDN_PREP_CHUNKS = 2
ATT_TQ = 512
ATT_TK = 512
ATT_RB = 32
LOG2E = math.log2(math.e)
SWA_TQ = 256
SWA_TW = SWA_TQ + 2 * WINDOW

EV_GQ, EV_GK, EV_GV, EV_GG = 0, 1024, 2048, 3072
EV_DQ, EV_DK, EV_DV, EV_DG = 4096, 5120, 6144, 7168
EV_MQ, EV_MG = 8192, 8704
EV_MAIN = 9216
EV_SIDE_LO, EV_SIDE_HI = 3072, 3104

OD_MLQ_NOPE, OD_MLQ_ROPE = 0, 1024
OD_SQ, OD_SK, OD_SV, OD_SG = 1536, 2560, 2816, 3072
OD_CKV, OD_MLG, OD_MQ, OD_MG = 4096, 4608, 5632, 6144
OD_MAIN = 6656
OD_KR_LO, OD_KR_HI = 4608, 4672


def _cparams(sem):
    return pltpu.CompilerParams(dimension_semantics=sem, vmem_limit_bytes=VMEM_LIMIT)


def _dot(a, b):
    return jnp.dot(a, b, preferred_element_type=F32)


def _dot_nt(a, b):
    return lax.dot_general(a, b, (((1,), (1,)), ((), ())), preferred_element_type=F32)


def _silu(x):
    x = x.astype(F32)
    return x * jax.nn.sigmoid(x)


def _softplus(x):
    return jnp.maximum(x, 0.0) + jnp.log(1.0 + jnp.exp(-jnp.abs(x)))


def _lane_sum(x, scale=1.0):
    return _dot(x.astype(BF16), jnp.ones((LANES, LANES), BF16)) * scale


def _rms(x, gain):
    x = x.astype(F32)
    if x.shape[-1] == LANES:
        ms = _lane_sum(x * x, 1.0 / LANES)
    else:
        ms = jnp.mean(x * x, axis=-1, keepdims=True)
    return x * lax.rsqrt(ms + EPS) * gain


def _norm_proj_kernel(x_ref, g_ref, w_ref, ws_ref, o_ref, os_ref, xn_ref):
    @pl.when(pl.program_id(1) == 0)
    def _():
        xn = _rms(x_ref[...], g_ref[...]).astype(BF16)
        xn_ref[...] = xn
        os_ref[...] = _dot_nt(xn, ws_ref[...])

    o_ref[...] = _dot_nt(xn_ref[...], w_ref[...]).astype(o_ref.dtype)


def _norm_proj(x, gain, w_main, w_side, tm, tn, out_dtype):
    m, k = x.shape
    n = w_main.shape[0]
    ns = w_side.shape[0]
    assert m % tm == 0 and n % tn == 0
    return pl.pallas_call(
        _norm_proj_kernel,
        grid=(m // tm, n // tn),
        in_specs=[
            pl.BlockSpec((tm, k), lambda i, j: (i, 0)),
            pl.BlockSpec((1, k), lambda i, j: (0, 0)),
            pl.BlockSpec((tn, k), lambda i, j: (j, 0)),
            pl.BlockSpec((ns, k), lambda i, j: (0, 0)),
        ],
        out_specs=[
            pl.BlockSpec((tm, tn), lambda i, j: (i, j)),
            pl.BlockSpec((tm, ns), lambda i, j: (i, 0)),
        ],
        out_shape=[jax.ShapeDtypeStruct((m, n), out_dtype), jax.ShapeDtypeStruct((m, ns), F32)],
        scratch_shapes=[pltpu.VMEM((tm, k), BF16)],
        compiler_params=_cparams(("parallel", "arbitrary")),
        name="norm_proj",
    )(x, gain.reshape(1, k), w_main, w_side)


def _out_proj_kernel(x_ref, ya_ref, yb_ref, ym_ref, wa_ref, wb_ref, wm_ref, o_ref):
    acc = _dot(ya_ref[...], wa_ref[...])
    acc = acc + _dot(yb_ref[...], wb_ref[...])
    acc = acc + _dot(ym_ref[...], wm_ref[...])
    o_ref[...] = x_ref[...] + acc


def _out_proj(x, ya, yb, ym, w_out, tm):
    m, d = x.shape
    na, nb, nm = ya.shape[1], yb.shape[1], ym.shape[1]
    wa = w_out[:na].astype(BF16)
    wb = w_out[na:na + nb].astype(BF16)
    wm = w_out[na + nb:].astype(BF16)
    row = lambda i: (i, 0)
    fixed = lambda i: (0, 0)
    return pl.pallas_call(
        _out_proj_kernel,
        grid=(m // tm,),
        in_specs=[
            pl.BlockSpec((tm, d), row), pl.BlockSpec((tm, na), row), pl.BlockSpec((tm, nb), row),
            pl.BlockSpec((tm, nm), row),
            pl.BlockSpec((na, d), fixed), pl.BlockSpec((nb, d), fixed), pl.BlockSpec((nm, d), fixed),
        ],
        out_specs=pl.BlockSpec((tm, d), row),
        out_shape=jax.ShapeDtypeStruct((m, d), F32),
        compiler_params=_cparams(("parallel",)),
        name="out_proj",
    )(x, ya, yb, ym, wa, wb, wm)


def _gdn_kernel(q_ref, k_ref, v_ref, gate_ref, bar_ref, cwq_ref, cwk_ref, cwv_ref, prr_ref, gain_ref, y_ref,
                xp_ref, qd_ref, kw_ref, b_ref, egl_ref, o_ref, lvl_ref, tri_ref):
    seq = q_ref.shape[0]
    c = GDN_CHUNK
    nc = seq // c
    pad = 8
    scale = GDN_DK ** -0.5

    for i, src in enumerate((q_ref, k_ref, v_ref)):
        xp_ref[i, 0:pad, :] = jnp.zeros((pad, LANES), F32)
        xp_ref[i, pad + seq:2 * pad + seq, :] = jnp.zeros((pad, LANES), F32)
        xp_ref[i, pad:pad + seq, :] = src[...].astype(F32)

    hc = c // 2
    row_h = lax.broadcasted_iota(jnp.int32, (hc, hc), 0)
    col_h = lax.broadcasted_iota(jnp.int32, (hc, hc), 1)
    for bit in range(GDN_LEVELS):
        lvl_ref[bit] = (((row_h ^ col_h) >> bit) == 1).astype(BF16)
    lvl_ref[GDN_LEVELS] = (row_h == col_h).astype(BF16)
    tri_ref[0] = jnp.where(row_h > col_h, 0.0, NEG)
    tri_ref[1] = jnp.where(row_h < col_h, 0.0, NEG)
    tri_ref[2] = (row_h == col_h).astype(F32)

    lane_c = lax.broadcasted_iota(jnp.int32, (8, c), 1)
    sub_c = lax.broadcasted_iota(jnp.int32, (8, c), 0)

    def prefix(x):
        s = 1
        while s < c:
            x = x + jnp.where(lane_c >= s, pltpu.roll(x, s, 1), 0.0)
            s *= 2
        return x

    def suffix(x):
        s = 1
        while s < c:
            x = x + jnp.where(lane_c < c - s, pltpu.roll(x, c - s, 1), 0.0)
            s *= 2
        return x

    def conv(i, cw_ref, t0):
        half = (GDN_CONV - 1) // 2
        acc = None
        for j in range(GDN_CONV):
            tap = xp_ref[i, pl.ds(t0 + (pad - half + j), c), :] * cw_ref[j:j + 1, :]
            acc = tap if acc is None else acc + tap
        return _silu(acc)

    def l2n(x):
        return x * lax.rsqrt(_lane_sum(x * x) + EPS)

    half = lambda i: slice(i * hc, (i + 1) * hc)


    def prologue(ns, chains):
        st = []
        for n in ns:
            t0 = pl.multiple_of(n * c, c)
            st.append(dict(n=n, t0=t0, rows=pl.ds(t0, c)))
        for name, idx, cw_ref in (("q", 0, cwq_ref), ("k", 1, cwk_ref), ("v", 2, cwv_ref)):
            for s in st:
                s[name] = conv(idx, cw_ref, s["t0"])
            yield
        for s in st:
            s["q"], s["k"] = l2n(s["q"]), l2n(s["k"])
        yield
        for s in st:
            s["qs"] = s["q"] * scale
            s["kb"] = s["k"].astype(BF16)
            s["qk"] = _dot_nt(s["qs"].astype(BF16), s["kb"])
            bar = bar_ref[:, s["rows"]]
            g_r = (-LOG2E * jnp.exp(prr_ref[:, 0:1])) * _softplus(bar + prr_ref[:, 1:2])
            s["pre"], s["suf"] = prefix(g_r), suffix(g_r)
            s["tot"] = jnp.sum(g_r, axis=1, keepdims=True)
            packed = jnp.where(sub_c < 2, jax.nn.sigmoid(bar), jnp.where(sub_c == 2, s["pre"], s["suf"]))
            s["cols"] = jnp.concatenate([packed, jnp.zeros((LANES - 8, c), F32)], axis=0).T
            o_ref[s["rows"], :] = jnp.zeros((c, LANES), F32)
        yield
        for d in range(2):
            for s in st:
                s["kkb", d] = _dot_nt((s["k"] * s["cols"][:, d:d + 1]).astype(BF16), s["kb"])
            yield
        for d in range(2):
            first, second = (0, 1) if d == 0 else (1, 0)
            for s in st:
                gr = s["pre"][2:3, :] if d == 0 else s["suf"][3:4, :]
                gc = s["cols"][:, 2 + d:3 + d]
                blocks, attn = {}, {}
                for r, t in ((first, first), (second, second), (second, first)):
                    e = gc[half(r)] - gr[:, half(t)]
                    if r == t:
                        e = e + tri_ref[d]
                    dec = jnp.exp2(e)
                    blocks[r, t] = (s["kkb", d][half(r), half(t)] * dec).astype(BF16)
                    if r == t:
                        dec = dec + tri_ref[2]
                    attn[r, t] = (s["qk"][half(r), half(t)] * dec).astype(BF16)
                attn[first, second] = jnp.zeros((hc, hc), BF16)
                a_full = jnp.concatenate([jnp.concatenate([attn[r, 0], attn[r, 1]], axis=1) for r in range(2)], axis=0)
                diag = [blocks[0, 0], blocks[1, 1]]
                chains.append(dict(n=s["n"], rows=s["rows"], d=d, k=s["k"], v=s["v"],
                                   qs=s["qs"], beta=s["cols"][:, d:d + 1], gc=gc, tot=s["tot"][2 + d:3 + d, :],
                                   m=diag, off=blocks[second, first], a=a_full,
                                   p=[lvl_ref[GDN_LEVELS] - blk * lvl_ref[0] for blk in diag]))
            yield

    def levels(chains):
        for bit in range(1, GDN_LEVELS):
            lvl = lvl_ref[bit]
            for ch in chains:
                ch["x"] = [_dot(p, m * lvl).astype(BF16) for p, m in zip(ch["p"], ch["m"])]
            yield
            for ch in chains:
                ch["p"] = [p - _dot(x, p).astype(BF16) for p, x in zip(ch["p"], ch["x"])]
            yield

    def epilogue(chains):
        order = lambda ch: (0, 1) if ch["d"] == 0 else (1, 0)
        for ch in chains:
            ch["eg"] = jnp.exp2(ch["gc"])
            ch["rhs"] = jnp.concatenate([ch["v"] * ch["beta"], ch["k"] * (ch["beta"] * ch["eg"])], axis=1)
            ch["x1"] = _dot(ch["p"][order(ch)[0]], ch["rhs"][half(order(ch)[0])].astype(BF16))
        yield
        for ch in chains:
            ch["cross"] = _dot(ch["off"], ch["x1"].astype(BF16))
        yield
        for ch in chains:
            second = order(ch)[1]
            ch["x2"] = _dot(ch["p"][second], (ch["rhs"][half(second)] - ch["cross"]).astype(BF16))
        yield
        for ch in chains:
            xs = (ch["x1"], ch["x2"]) if ch["d"] == 0 else (ch["x2"], ch["x1"])
            ch["xb"] = jnp.concatenate(xs, axis=0).astype(BF16)
            kd = ch["k"] * jnp.exp2(ch["tot"] - ch["gc"])
            ch["kx"] = _dot(kd.T.astype(BF16), ch["xb"])
            ch["ax"] = _dot(ch["a"], ch["xb"])
        yield
        for ch in chains:
            d, rows = ch["d"], ch["rows"]
            blk = pl.ds(pl.multiple_of(ch["n"] * GDN_DK, GDN_DK), GDN_DK)
            b_ref[d, blk, :] = ch["kx"][:, :GDN_DV]
            kw_ref[d, blk, :] = ch["kx"][:, GDN_DV:].astype(BF16)
            o_ref[rows, :] = o_ref[rows, :] + ch["ax"][:, :GDN_DV]
            qd_ref[d, rows, :] = (ch["qs"] * ch["eg"] - ch["ax"][:, GDN_DV:]).astype(BF16)
            egl_ref[d, pl.ds(pl.multiple_of(ch["n"] * 8, 8), 8), :] = jnp.broadcast_to(jnp.exp2(ch["tot"]), (8, LANES))
        yield

    def interleave(*gens):
        gens = list(gens)
        while gens:
            for g in list(gens):
                if next(g, StopIteration) is StopIteration:
                    gens.remove(g)

    ngroups = nc // GDN_PREP_CHUNKS
    groups = [[g * GDN_PREP_CHUNKS + i for i in range(GDN_PREP_CHUNKS)] for g in range(ngroups)]
    chains = [[] for _ in range(ngroups)]
    interleave(prologue(groups[0], chains[0]))
    for g in range(ngroups):
        gens = [levels(chains[g])]
        if g + 1 < ngroups:
            gens.append(prologue(groups[g + 1], chains[g + 1]))
        if g >= 1:
            gens.append(epilogue(chains[g - 1]))
        interleave(*gens)
    interleave(epilogue(chains[ngroups - 1]))

    def scan(n, carry):
        states = list(carry)
        for d in range(2):
            idx = n if d == 0 else nc - 1 - n
            t0 = pl.multiple_of(idx * c, c)
            rows = pl.ds(t0, c)
            blk = pl.ds(pl.multiple_of(idx * GDN_DK, GDN_DK), GDN_DK)
            s = states[d]
            sb = s.astype(BF16)
            o_ref[rows, :] = o_ref[rows, :] + _dot(qd_ref[d, rows, :], sb)
            egl = egl_ref[d, pl.ds(pl.multiple_of(idx * 8, 8), 8), :][0:1, :]
            states[d] = s * egl - _dot(kw_ref[d, blk, :], sb) + b_ref[d, blk, :]
        return tuple(states)

    zero = jnp.zeros((GDN_DK, GDN_DV), F32)
    lax.fori_loop(0, nc, scan, (zero, zero))

    def fin(n, carry):
        rows = pl.ds(pl.multiple_of(n * c, c), c)
        y = _rms(o_ref[rows, :], gain_ref[...]) * _silu(gate_ref[rows, :])
        y_ref[rows, :] = y.astype(y_ref.dtype)
        return carry

    lax.fori_loop(0, nc, fin, 0)


def _gdn(h, bar, conv_w, prr, gain, bsz, seq):
    nh = GDN_HEADS
    blk = lambda off: pl.BlockSpec((seq, LANES), lambda b, hh, off=off: (b, off // LANES + hh))
    cw = lambda off: pl.BlockSpec((GDN_CONV, LANES), lambda b, hh, off=off: (0, off // LANES + hh))
    c = GDN_CHUNK
    assert seq % (GDN_PREP_CHUNKS * c) == 0
    return pl.pallas_call(
        _gdn_kernel,
        grid=(bsz, nh),
        in_specs=[
            blk(EV_GQ), blk(EV_GK), blk(EV_GV), blk(EV_GG),
            pl.BlockSpec((None, None, 8, seq), lambda b, hh: (b, hh, 0, 0)),
            cw(0), cw(GDN_HEADS * GDN_DK), cw(2 * GDN_HEADS * GDN_DK),
            pl.BlockSpec((None, 8, 2), lambda b, hh: (hh, 0, 0)),
            pl.BlockSpec((1, LANES), lambda b, hh: (0, 0)),
        ],
        out_specs=pl.BlockSpec((seq, LANES), lambda b, hh: (b, hh)),
        out_shape=jax.ShapeDtypeStruct((bsz * seq, nh * GDN_DV), BF16),
        scratch_shapes=[
            pltpu.VMEM((3, seq + 16, LANES), F32),
            pltpu.VMEM((2, seq, LANES), BF16),
            pltpu.VMEM((2, (seq // c) * GDN_DK, GDN_DV), BF16),
            pltpu.VMEM((2, (seq // c) * GDN_DK, GDN_DV), F32),
            pltpu.VMEM((2, (seq // c) * 8, LANES), F32),
            pltpu.VMEM((seq, LANES), F32),
            pltpu.VMEM((GDN_LEVELS + 1, c // 2, c // 2), BF16),
            pltpu.VMEM((3, c // 2, c // 2), F32),
        ],
        compiler_params=_cparams(("parallel", "parallel")),
        name="gdn",
    )(h, h, h, h, bar, conv_w, conv_w, conv_w, prr, gain.reshape(1, LANES))


def _flash_scratch(nmaps, tq, tk, dv):
    per_map = [pltpu.VMEM((tq, tk), F32), pltpu.VMEM((tq, tk), F32), pltpu.VMEM((tq, tk), BF16),
               pltpu.VMEM((tq, LANES), F32), pltpu.VMEM((tq, LANES), F32), pltpu.VMEM((tq, dv), F32)]
    return per_map * nmaps


def _flash_core(qs, k_ref, v_ref, nsteps, step_of, loop, scratch):
    nmaps = len(qs)
    maps = [scratch[6 * i:6 * i + 6] for i in range(nmaps)]
    tq, tk = maps[0][0].shape
    lo, hi = loop
    assert tk % LANES == 0 and tq % ATT_RB == 0 and 0 < lo <= hi < nsteps and (hi - lo) % 2 == 0
    nlb = tk // LANES

    def chunk(ref, t):
        j = step_of(t)[0]
        start = j * tk if isinstance(j, int) else pl.multiple_of(j * tk, tk)
        return ref[pl.ds(start, tk), :]

    def qk(i, t, slot):
        maps[i][slot][...] = _dot_nt(qs[i], chunk(k_ref, t))

    maps[nmaps - 1][5][...] = jnp.zeros(maps[nmaps - 1][5].shape, F32)

    def pv(i, t):
        acc = maps[i][5]
        out = _dot(maps[i][2][...], chunk(v_ref, t))
        acc[...] = out if i < nmaps - 1 and isinstance(t, int) and t == 0 else acc[...] + out

    def softmax(i, t, slot, first):
        s_ref, p_ref, m_ref, l_ref, acc_ref = maps[i][slot], maps[i][2], maps[i][3], maps[i][4], maps[i][5]
        _, tile, const = step_of(t)
        for rb in range(tq // ATT_RB):
            r = slice(rb * ATT_RB, (rb + 1) * ATT_RB)
            s = s_ref[r, :]
            if tile is not None:
                s = s + tile[r, :]
            blocks = [s[:, b * LANES:(b + 1) * LANES] for b in range(nlb)]
            mx = functools.reduce(jnp.maximum, blocks)
            mx = jnp.broadcast_to(jnp.max(mx, axis=-1, keepdims=True), (ATT_RB, LANES))
            if const is not None:
                mx = mx + const
            m_new = mx if first else jnp.maximum(m_ref[r, :], mx)
            shift = m_new if const is None else m_new - const
            ps = [jnp.exp2(b - shift) for b in blocks]
            row_sum = jnp.broadcast_to(jnp.sum(functools.reduce(jnp.add, ps), axis=-1, keepdims=True), m_new.shape)
            if first:
                l_ref[r, :] = row_sum
            else:
                alpha = jnp.exp2(m_ref[r, :] - m_new)
                l_ref[r, :] = alpha * l_ref[r, :] + row_sum
                acc_ref[r, :] = acc_ref[r, :] * alpha
            m_ref[r, :] = m_new
            p_ref[r, :] = jnp.concatenate(ps, axis=1).astype(BF16)

    def stage(t, slot, first=False, last=False):
        for i in range(nmaps):
            if i > 0:
                pv(i - 1, t)
            elif not first:
                pv(nmaps - 1, t - 1)
            if not last:
                qk(i, t + 1, 1 - slot)
            softmax(i, t, slot, first)

    def pair(tt, carry):
        for u in range(2):
            stage(lo + 2 * tt + u, (lo + u) % 2)
        return carry

    for i in range(nmaps):
        qk(i, 0, 0)
    for t in range(lo):
        stage(t, t % 2, first=t == 0)
    lax.fori_loop(0, (hi - lo) // 2, pair, 0)
    for t in range(hi, nsteps):
        stage(t, t % 2, last=t == nsteps - 1)
    pv(nmaps - 1, nsteps - 1)
    return [(mp[5][...], mp[4][...]) for mp in maps]


def _diff_prep_kernel(q_ref, k_ref, v_ref, qg_ref, kg_ref, qo_ref, ko_ref, vo_ref):
    r = lax.broadcasted_iota(jnp.int32, (LANES, LANES), 0) < DIFF_DQK
    cc = lax.broadcasted_iota(jnp.int32, (LANES, LANES), 1) < DIFF_DQK
    half_mean = jnp.where(r == cc, 1.0 / DIFF_DQK, 0.0).astype(BF16)

    def halfnorm(x, gain):
        x = x.astype(F32)
        return x * lax.rsqrt(_dot((x * x).astype(BF16), half_mean) + EPS) * gain

    for hh in range(DIFF_HEADS):
        sl = slice(hh * LANES, (hh + 1) * LANES)
        qo_ref[:, sl] = (halfnorm(q_ref[:, sl], qg_ref[...]) * (DIFF_DQK ** -0.5 * LOG2E)).astype(BF16)
        ko_ref[:, sl] = halfnorm(k_ref[:, sl], kg_ref[...]).astype(BF16)
    vo_ref[...] = v_ref[...].astype(BF16)


def _diff_prep(h, qg, kg, tm):
    m = h.shape[0]
    w = DIFF_HEADS * LANES
    spec = lambda off: pl.BlockSpec((tm, w), lambda i, off=off: (i, off // w))
    vec = pl.BlockSpec((1, LANES), lambda i: (0, 0))
    out = jax.ShapeDtypeStruct((m, w), BF16)
    return pl.pallas_call(
        _diff_prep_kernel,
        grid=(m // tm,),
        in_specs=[spec(EV_DQ), spec(EV_DK), spec(EV_DV), vec, vec],
        out_specs=[pl.BlockSpec((tm, w), lambda i: (i, 0))] * 3,
        out_shape=[out, out, out],
        compiler_params=_cparams(("parallel",)),
        name="diff_prep",
    )(h, h, h, jnp.tile(qg, 2).reshape(1, LANES), jnp.tile(kg, 2).reshape(1, LANES))


def _toeplitz(vec, tq, tk):
    w = vec.shape[-1]
    full = pltpu.roll(jnp.broadcast_to(vec, (tq, w)), w - (tq - 1), 1, stride=1, stride_axis=0)
    return full[:, :tk]


def _diff_flash_kernel(q_ref, k_ref, v_ref, gate_ref, bvec_ref, lam_ref, sub_ref, y_ref, bias_ref, *scratch,
                       lambda_init):
    tq = q_ref.shape[0]
    tk = bias_ref.shape[-1]
    qi = pl.program_id(2)

    @pl.when(qi == 0)
    def _():
        for d in range(bias_ref.shape[0]):
            bias_ref[d] = _toeplitz(bvec_ref[d:d + 1, :], tq, tk)

    q = q_ref[...]
    lane = lax.broadcasted_iota(jnp.int32, (1, LANES), 1)
    zero = jnp.zeros_like(q)
    q0 = jnp.where(lane < DIFF_DQK, q, zero)
    q1 = jnp.where(lane < DIFF_DQK, zero, q)

    nk = k_ref.shape[0] // tk
    near = 3
    n0 = jnp.clip(qi - 1, 0, nk - near)
    c_before, c_after = bvec_ref[0:1, 0:1], bvec_ref[4:5, 0:1]
    nfar = nk - near

    def step_of(t):
        if isinstance(t, int) and t >= nfar:
            j = n0 + (t - nfar)
            return j, bias_ref.at[jnp.clip(j - qi, -2, 2) + 2], None
        j = jnp.where(t < n0, t, t + near)
        j = jnp.where(t < nfar, j, n0 + (t - nfar))
        return j, None, jnp.where(j < qi, c_before, c_after)

    loop = (1, nfar - (nfar - 1) % 2)
    (a0, l0), (a1, l1) = _flash_core([q0, q1], k_ref, v_ref, nk, step_of, loop, scratch)

    lam = lam_ref[...]
    lam_full = (jnp.exp(jnp.sum(lam[0:1] * lam[1:2], axis=-1, keepdims=True))
                - jnp.exp(jnp.sum(lam[2:3] * lam[3:4], axis=-1, keepdims=True)) + lambda_init)
    o = a0 / l0 - lam_full * (a1 / l1)
    o = _rms(o, sub_ref[...]) * (1.0 - lambda_init)
    y_ref[...] = (o * _silu(gate_ref[...])).astype(y_ref.dtype)


def _diff_flash(dq, dk, dv, h, bvec, lam, subln, lambda_init, bsz, seq):
    tq, tk = ATT_TQ, ATT_TK
    nq = seq // tq
    nh = DIFF_HEADS
    return pl.pallas_call(
        functools.partial(_diff_flash_kernel, lambda_init=lambda_init),
        grid=(bsz, nh, nq),
        in_specs=[
            pl.BlockSpec((tq, LANES), lambda b, hh, i: (b * nq + i, hh)),
            pl.BlockSpec((seq, LANES), lambda b, hh, i: (b, hh)),
            pl.BlockSpec((seq, LANES), lambda b, hh, i: (b, hh)),
            pl.BlockSpec((tq, LANES), lambda b, hh, i: (b * nq + i, EV_DG // LANES + hh)),
            pl.BlockSpec((None,) + bvec.shape[1:], lambda b, hh, i: (hh, 0, 0)),
            pl.BlockSpec((4, DIFF_DQK), lambda b, hh, i: (0, 0)),
            pl.BlockSpec((1, LANES), lambda b, hh, i: (0, 0)),
        ],
        out_specs=pl.BlockSpec((tq, LANES), lambda b, hh, i: (b * nq + i, hh)),
        out_shape=jax.ShapeDtypeStruct((bsz * seq, nh * DIFF_DV), BF16),
        scratch_shapes=[pltpu.VMEM((5, tq, tk), F32)] + _flash_scratch(2, tq, tk, DIFF_DV),
        compiler_params=_cparams(("parallel", "parallel", "arbitrary")),
        name="diff_flash",
    )(dq, dk, dv, h, bvec, lam, subln.reshape(1, LANES))


def _mem_attn_kernel(q_ref, gate_ref, mk_ref, mv_ref, qg_ref, kg_ref, y_ref):
    q = (_rms(q_ref[...], qg_ref[...]) * (MEM_DH ** -0.5)).astype(BF16)
    mk = _rms(mk_ref[...], kg_ref[...]).astype(BF16)
    s = _dot_nt(q, mk)
    p = jnp.exp(s - jnp.max(s, axis=-1, keepdims=True))
    o = _dot(p.astype(BF16), mv_ref[...].astype(BF16)) / jnp.sum(p, axis=-1, keepdims=True)
    y_ref[...] = (o * _silu(gate_ref[...])).astype(y_ref.dtype)


def _mem_attn(h, memkv, qg, kg, q_off, g_off, bsz, seq, tq):
    nq = seq // tq
    nh = MEM_HEADS
    vec = pl.BlockSpec((1, LANES), lambda b, hh, i: (0, 0))
    return pl.pallas_call(
        _mem_attn_kernel,
        grid=(bsz, nh, nq),
        in_specs=[
            pl.BlockSpec((tq, LANES), lambda b, hh, i: (b * nq + i, q_off // LANES + hh)),
            pl.BlockSpec((tq, LANES), lambda b, hh, i: (b * nq + i, g_off // LANES + hh)),
            pl.BlockSpec((MEM_LEN, LANES), lambda b, hh, i: (b, hh)),
            pl.BlockSpec((MEM_LEN, LANES), lambda b, hh, i: (b, nh + hh)),
            vec, vec,
        ],
        out_specs=pl.BlockSpec((tq, LANES), lambda b, hh, i: (b * nq + i, hh)),
        out_shape=jax.ShapeDtypeStruct((bsz * seq, nh * MEM_DH), BF16),
        compiler_params=_cparams(("parallel", "parallel", "parallel")),
        name="mem_attn",
    )(h, h, memkv, memkv, qg.reshape(1, LANES), kg.reshape(1, LANES))


def _swa_kernel(sink_ref, q_ref, k_ref, v_ref, gate_ref, bias_ref, qg_ref, kg_ref, y_ref, tile_ref):
    tq = q_ref.shape[0]
    seq = k_ref.shape[0]
    grp = SWA_HEADS // SWA_KV_HEADS
    kvh = pl.program_id(1)
    qi = pl.program_id(2)
    nq = pl.num_programs(2)

    @pl.when(jnp.logical_or(qi <= 1, qi == nq - 1))
    def _():
        for g in range(grp):
            tile_ref[g] = _toeplitz(bias_ref[g:g + 1, :], tq, SWA_TW)

    ws = pl.multiple_of(jnp.clip(qi * tq - WINDOW, 0, seq - SWA_TW), WINDOW)
    kw = _rms(k_ref[pl.ds(ws, SWA_TW), :], kg_ref[...]).astype(BF16)
    vw = v_ref[pl.ds(ws, SWA_TW), :].astype(BF16)
    heads = [slice(g * SWA_DH, (g + 1) * SWA_DH) for g in range(grp)]
    qs = [(_rms(q_ref[:, sl], qg_ref[...]) * (SWA_DH ** -0.5)).astype(BF16) for sl in heads]
    ss = [_dot_nt(q, kw) + tile_ref[g] for g, q in enumerate(qs)]
    ps, dens = [], []
    for g, s in enumerate(ss):
        sink = sink_ref[kvh * grp + g]
        mx = jnp.maximum(jnp.max(s, axis=-1, keepdims=True), sink)
        p = jnp.exp(s - mx)
        dens.append(jnp.sum(p, axis=-1, keepdims=True) + jnp.exp(sink - mx))
        ps.append(p.astype(BF16))
    for sl, p, den in zip(heads, ps, dens):
        y_ref[:, sl] = (_dot(p, vw) / den * _silu(gate_ref[:, sl])).astype(y_ref.dtype)


def _swa(h, bias, sink, qg, kg, bsz, seq):
    tq = SWA_TQ
    nq = seq // tq
    grp = SWA_HEADS // SWA_KV_HEADS
    gw = grp * SWA_DH
    assert seq >= SWA_TW and nq >= 2
    vec = pl.BlockSpec((1, LANES), lambda b, kv, i: (0, 0))

    def bias_idx(b, kv, i):
        return (jnp.where(i == 0, 0, jnp.where(i == nq - 1, 2, 1)), kv, 0, 0)

    return pl.pallas_call(
        _swa_kernel,
        grid=(bsz, SWA_KV_HEADS, nq),
        in_specs=[
            pl.BlockSpec(memory_space=pltpu.SMEM),
            pl.BlockSpec((tq, gw), lambda b, kv, i: (b * nq + i, OD_SQ // gw + kv)),
            pl.BlockSpec((seq, LANES), lambda b, kv, i: (b, OD_SK // LANES + kv)),
            pl.BlockSpec((seq, LANES), lambda b, kv, i: (b, OD_SV // LANES + kv)),
            pl.BlockSpec((tq, gw), lambda b, kv, i: (b * nq + i, OD_SG // gw + kv)),
            pl.BlockSpec((None, None, grp, bias.shape[-1]), bias_idx),
            vec, vec,
        ],
        out_specs=pl.BlockSpec((tq, gw), lambda b, kv, i: (b * nq + i, kv)),
        out_shape=jax.ShapeDtypeStruct((bsz * seq, SWA_HEADS * SWA_DH), BF16),
        scratch_shapes=[pltpu.VMEM((grp, tq, SWA_TW), F32)],
        compiler_params=_cparams(("parallel", "parallel", "arbitrary")),
        name="swa",
    )(sink, h, h, h, h, bias, qg.reshape(1, LANES), kg.reshape(1, LANES))


def _mla_prep_kernel(q_ref, ckv_ref, kr_ref, cos_ref, sin_ref, kvg_ref, wup_ref,
                     qgn_ref, qgr_ref, kgn_ref, kgr_ref, qo_ref, ko_ref, vo_ref):
    dqk = MLA_NOPE + MLA_ROPE
    lane = lax.broadcasted_iota(jnp.int32, (1, LANES), 1)
    lo = lane < MLA_ROPE
    first = lane < MLA_ROPE // 2
    cos = cos_ref[...]
    sin = sin_ref[...]

    def rope(t):
        rot = jnp.where(first, pltpu.roll(t, LANES - MLA_ROPE // 2, 1), pltpu.roll(t, MLA_ROPE // 2, 1))
        return t * cos + rot * sin

    ckv = _rms(ckv_ref[...], kvg_ref[...]).astype(BF16)
    kv = _dot(ckv, wup_ref[...])
    kr = kr_ref[...]
    kr2 = kr * kr
    c = dqk ** -0.5 * LOG2E
    hw = MLA_NOPE + MLA_DV
    for hh in range(MLA_HEADS):
        kn = kv[:, hh * hw:hh * hw + MLA_NOPE]
        inv = lax.rsqrt(_lane_sum(kn * kn + kr2, 1.0 / dqk) + EPS)
        ko_ref[:, hh * 2 * LANES:hh * 2 * LANES + LANES] = (kn * inv * kgn_ref[...]).astype(BF16)
        ko_ref[:, hh * 2 * LANES + LANES:(hh + 1) * 2 * LANES] = rope(kr * inv * kgr_ref[...]).astype(BF16)
        vo_ref[:, hh * LANES:(hh + 1) * LANES] = kv[:, hh * hw + MLA_NOPE:(hh + 1) * hw].astype(BF16)

        qn = q_ref[:, hh * LANES:(hh + 1) * LANES].astype(F32)
        pair = q_ref[:, OD_MLQ_ROPE + (hh // 2) * LANES:OD_MLQ_ROPE + (hh // 2 + 1) * LANES].astype(F32)
        if hh % 2 == 1:
            pair = pltpu.roll(pair, MLA_ROPE, 1)
        qr = jnp.where(lo, pair, 0.0)
        inv = lax.rsqrt(_lane_sum(qn * qn + qr * qr, 1.0 / dqk) + EPS)
        qo_ref[:, hh * 2 * LANES:hh * 2 * LANES + LANES] = (qn * inv * qgn_ref[...] * c).astype(BF16)
        qo_ref[:, hh * 2 * LANES + LANES:(hh + 1) * 2 * LANES] = (rope(qr * inv * qgr_ref[...]) * c).astype(BF16)


def _mla_prep(h, side, cos, sin, kv_gain, w_up, qg, kg, tm):
    m = h.shape[0]
    nh = MLA_HEADS
    qw = nh * (MLA_NOPE + MLA_ROPE)
    pad = lambda g: jnp.pad(g[MLA_NOPE:], (0, LANES - MLA_ROPE)).reshape(1, LANES)
    vec = pl.BlockSpec((1, LANES), lambda i: (0, 0))
    row = lambda i: (i, 0)
    return pl.pallas_call(
        _mla_prep_kernel,
        grid=(m // tm,),
        in_specs=[
            pl.BlockSpec((tm, qw), row),
            pl.BlockSpec((tm, MLA_KV_RANK), lambda i: (i, OD_CKV // MLA_KV_RANK)),
            pl.BlockSpec((tm, LANES), row),
            pl.BlockSpec((tm, LANES), row),
            pl.BlockSpec((tm, LANES), row),
            pl.BlockSpec((1, MLA_KV_RANK), lambda i: (0, 0)),
            pl.BlockSpec(w_up.shape, lambda i: (0, 0)),
            vec, vec, vec, vec,
        ],
        out_specs=[
            pl.BlockSpec((tm, nh * 2 * LANES), row),
            pl.BlockSpec((tm, nh * 2 * LANES), row),
            pl.BlockSpec((tm, nh * MLA_DV), row),
        ],
        out_shape=[
            jax.ShapeDtypeStruct((m, nh * 2 * LANES), BF16),
            jax.ShapeDtypeStruct((m, nh * 2 * LANES), BF16),
            jax.ShapeDtypeStruct((m, nh * MLA_DV), BF16),
        ],
        compiler_params=_cparams(("parallel",)),
        name="mla_prep",
    )(h, h, side, cos, sin, kv_gain.reshape(1, MLA_KV_RANK), w_up.astype(BF16),
      qg[:MLA_NOPE].reshape(1, LANES), pad(qg), kg[:MLA_NOPE].reshape(1, LANES), pad(kg))


def _mla_flash_kernel(q_ref, k_ref, v_ref, gate_ref, y_ref, *scratch):
    nk = k_ref.shape[0] // ATT_TK
    (acc, l), = _flash_core([q_ref[...]], k_ref, v_ref, nk, lambda t: (t, None, None), (1, nk - 1 - nk % 2), scratch)
    y_ref[...] = (acc / l * _silu(gate_ref[...])).astype(y_ref.dtype)


def _mla_flash(qm, km, vm, h, bsz, seq):
    tq = ATT_TQ
    nq = seq // tq
    nh = MLA_HEADS
    return pl.pallas_call(
        _mla_flash_kernel,
        grid=(bsz, nh, nq),
        in_specs=[
            pl.BlockSpec((tq, 2 * LANES), lambda b, hh, i: (b * nq + i, hh)),
            pl.BlockSpec((seq, 2 * LANES), lambda b, hh, i: (b, hh)),
            pl.BlockSpec((seq, LANES), lambda b, hh, i: (b, hh)),
            pl.BlockSpec((tq, LANES), lambda b, hh, i: (b * nq + i, OD_MLG // LANES + hh)),
        ],
        out_specs=pl.BlockSpec((tq, LANES), lambda b, hh, i: (b * nq + i, hh)),
        out_shape=jax.ShapeDtypeStruct((bsz * seq, nh * MLA_DV), BF16),
        scratch_shapes=_flash_scratch(1, tq, ATT_TK, MLA_DV),
        compiler_params=_cparams(("parallel", "parallel", "arbitrary")),
        name="mla_flash",
    )(qm, km, vm, h)


def _t5_bucket(rel):
    half = REL_BUCKETS // 2
    max_exact = half // 2
    ret = (rel > 0).astype(jnp.int32) * half
    n = jnp.abs(rel)
    nf = jnp.maximum(n, 1).astype(F32)
    large = max_exact + (jnp.log(nf / max_exact) / math.log(REL_MAX_DIST / max_exact)
                         * (half - max_exact)).astype(jnp.int32)
    large = jnp.minimum(large, half - 1)
    return ret + jnp.where(n < max_exact, n, large)


def _round_up(n, m):
    return (n + m - 1) // m * m


def _diff_bias_vecs(rel_table, tq, tk):
    assert tq == tk and tk + 1 >= REL_MAX_DIST
    m = jnp.arange(_round_up(tq + tk - 1, LANES))
    rows = [rel_table[_t5_bucket(m - (tq - 1) + d * tk)] for d in (-2, -1, 0, 1, 2)]
    return jnp.pad(jnp.stack(rows, axis=0).transpose(2, 0, 1) * LOG2E, ((0, 0), (0, 3), (0, 0))).astype(F32)


def _swa_bias_vecs(rel_table):
    m = jnp.arange(_round_up(SWA_TQ + SWA_TW - 1, LANES))
    rows = []
    for shift in (0, WINDOW, SWA_TW - SWA_TQ):
        rel = m - (SWA_TQ - 1) - shift
        rows.append(jnp.where((jnp.abs(rel) <= WINDOW)[None], rel_table[_t5_bucket(rel)].T, NEG))
    return jnp.stack(rows, axis=0).reshape(3, SWA_KV_HEADS, SWA_HEADS // SWA_KV_HEADS, -1).astype(F32)


def _od_main_weight(wt):
    mlq_lo = 2 * SWA_HEADS * SWA_DH + 2 * SWA_KV_HEADS * SWA_DH
    mlq_hi = mlq_lo + MLA_HEADS * (MLA_NOPE + MLA_ROPE)
    mlq = wt[mlq_lo:mlq_hi].reshape(MLA_HEADS, MLA_NOPE + MLA_ROPE, -1)
    nope = mlq[:, :MLA_NOPE].reshape(MLA_HEADS * MLA_NOPE, -1)
    rope = mlq[:, MLA_NOPE:].reshape(MLA_HEADS * MLA_ROPE, -1)
    return jnp.concatenate([nope, rope, wt[:mlq_lo], wt[mlq_hi:OD_KR_LO], wt[OD_KR_HI:]], axis=0).astype(BF16)


def _rope_tables(positions):
    half = MLA_ROPE // 2
    inv_freq = ROPE_BASE ** (-jnp.arange(half, dtype=F32) / half)
    ang = positions.astype(F32).reshape(-1, 1) * inv_freq
    cos, sin = jnp.cos(ang), jnp.sin(ang)
    z = jnp.zeros((ang.shape[0], LANES - MLA_ROPE), F32)
    return jnp.concatenate([cos, cos, z], axis=1), jnp.concatenate([-sin, sin, z], axis=1)


def _side_weight(wt, lo, hi):
    return jnp.pad(wt[lo:hi], ((0, LANES - (hi - lo)), (0, 0))).astype(BF16)


def _mem_kv(mem2, gain, w_kv):
    zero_side = jnp.zeros((LANES, D_MODEL), BF16)
    memkv, _ = _norm_proj(mem2, gain, jnp.swapaxes(w_kv, 0, 1).astype(BF16), zero_side, tm=mem2.shape[0] // 2,
                          tn=512, out_dtype=F32)
    return memkv


def _even_layer(x2, mem2, rel_bias, norm_g, w_in, conv_w, a_log, dt_bias, gdn_gain, dq_gain, dk_gain, lam, subln,
                mem_norm, mem_w_kv, mem_qn, mem_kn, w_out, lambda_init, bsz, seq):
    wt = jnp.swapaxes(w_in, 0, 1)
    w_main = jnp.concatenate([wt[:EV_SIDE_LO], wt[EV_SIDE_HI:]], axis=0).astype(BF16)
    h, side = _norm_proj(x2, norm_g, w_main, _side_weight(wt, EV_SIDE_LO, EV_SIDE_HI), tm=1024, tn=EV_MAIN // 6,
                         out_dtype=BF16)

    nh = GDN_HEADS
    bar = side[:, :4 * nh].reshape(bsz, seq, 4, nh).transpose(0, 3, 2, 1)
    bar = jnp.pad(bar, ((0, 0), (0, 0), (0, 4), (0, 0)))
    prr = jnp.pad(jnp.stack([a_log.T, dt_bias.T], axis=-1), ((0, 0), (2, 4), (0, 0)))
    ya = _gdn(h, bar, conv_w, prr, gdn_gain, bsz, seq)

    dq, dk, dv = _diff_prep(h, dq_gain, dk_gain, tm=512)
    bias = _diff_bias_vecs(rel_bias, ATT_TQ, ATT_TK)
    yb = _diff_flash(dq, dk, dv, h, bias, lam, subln, lambda_init, bsz, seq)

    memkv = _mem_kv(mem2, mem_norm, mem_w_kv)
    ym = _mem_attn(h, memkv, mem_qn, mem_kn, EV_MQ, EV_MG, bsz, seq, tq=2048)
    return _out_proj(x2, ya, yb, ym, w_out, tm=512)


def _odd_layer(x2, mem2, positions, rel_bias, norm_g, w_in, swa_qn, swa_kn, sink, kv_norm, w_kv_up, mla_qn, mla_kn,
               mem_norm, mem_w_kv, mem_qn, mem_kn, w_out, bsz, seq):
    wt = jnp.swapaxes(w_in, 0, 1)
    h, side = _norm_proj(x2, norm_g, _od_main_weight(wt), _side_weight(wt, OD_KR_LO, OD_KR_HI), tm=1024,
                         tn=OD_MAIN // 4, out_dtype=BF16)

    ya = _swa(h, _swa_bias_vecs(rel_bias), sink, swa_qn, swa_kn, bsz, seq)

    cos, sin = _rope_tables(positions)
    qm, km, vm = _mla_prep(h, side, cos, sin, kv_norm, w_kv_up, mla_qn, mla_kn, tm=256)
    yb = _mla_flash(qm, km, vm, h, bsz, seq)

    memkv = _mem_kv(mem2, mem_norm, mem_w_kv)
    ym = _mem_attn(h, memkv, mem_qn, mem_kn, OD_MQ, OD_MG, bsz, seq, tq=2048)
    return _out_proj(x2, ya, yb, ym, w_out, tm=512)


def kernel(x, mem, positions, rel_bias, ev_norm, ev_w_in, ev_conv, ev_a_log, ev_dt_bias, ev_gdn_norm, ev_diff_qnorm, ev_diff_knorm, ev_diff_lambda, ev_diff_subln, ev_mem_norm, ev_mem_w_kv, ev_mem_qnorm, ev_mem_knorm, ev_w_out, od_norm, od_w_in, od_swa_qnorm, od_swa_knorm, od_swa_sink, od_mla_kv_norm, od_mla_w_kv_up, od_mla_qnorm, od_mla_knorm, od_mem_norm, od_mem_w_kv, od_mem_qnorm, od_mem_knorm, od_w_out):
    bsz, seq, d = x.shape
    depth = ev_norm.shape[0] + od_norm.shape[0]
    x2 = x.reshape(bsz * seq, d)
    mem2 = mem.reshape(bsz * mem.shape[1], d)
    for layer in range(depth):
        i = layer // 2
        if layer % 2 == 0:
            lambda_init = 0.8 - 0.6 * math.exp(-0.3 * layer)
            x2 = _even_layer(x2, mem2, rel_bias, ev_norm[i], ev_w_in[i], ev_conv[i], ev_a_log[i], ev_dt_bias[i],
                             ev_gdn_norm[i], ev_diff_qnorm[i], ev_diff_knorm[i], ev_diff_lambda[i], ev_diff_subln[i],
                             ev_mem_norm[i], ev_mem_w_kv[i], ev_mem_qnorm[i], ev_mem_knorm[i], ev_w_out[i],
                             lambda_init, bsz, seq)
        else:
            x2 = _odd_layer(x2, mem2, positions, rel_bias, od_norm[i], od_w_in[i], od_swa_qnorm[i], od_swa_knorm[i],
                            od_swa_sink[i], od_mla_kv_norm[i], od_mla_w_kv_up[i], od_mla_qnorm[i], od_mla_knorm[i],
                            od_mem_norm[i], od_mem_w_kv[i], od_mem_qnorm[i], od_mem_knorm[i], od_w_out[i], bsz, seq)
    return x2.reshape(bsz, seq, d)
```

```python
import functools
import math

import jax
import jax.numpy as jnp
from jax import lax
from jax.experimental import pallas as pl
from jax.experimental.pallas import tpu as pltpu

F32 = jnp.float32
BF16 = jnp.bfloat16
EPS = 1e-6
NEG = -1e30

V7X_VMEM_BYTES = 64 * 1024 * 1024
VMEM_LIMIT = V7X_VMEM_BYTES - 8 * 1024 * 1024
LANES = 128

D_MODEL = 2048
MEM_LEN = 256
GDN_HEADS, GDN_DK, GDN_DV, GDN_CONV = 8, 128, 128, 5
DIFF_HEADS, DIFF_DQK, DIFF_DV = 8, 64, 128
SWA_HEADS, SWA_KV_HEADS, SWA_DH, WINDOW = 8, 2, 128, 128
MLA_HEADS, MLA_NOPE, MLA_ROPE, MLA_DV, MLA_KV_RANK = 8, 128, 64, 128, 512
ROPE_BASE = 10000.0
MEM_HEADS, MEM_DH = 4, 128
REL_BUCKETS, REL_MAX_DIST = 32, 128

GDN_CHUNK = 256
GDN_LEVELS = (GDN_CHUNK // 2).bit_length() - 1
GDN_PREP_CHUNKS = 2
ATT_TQ = 512
ATT_TK = 512
ATT_RB = 32
LOG2E = math.log2(math.e)
SWA_TQ = 256
SWA_TW = SWA_TQ + 2 * WINDOW

EV_GQ, EV_GK, EV_GV, EV_GG = 0, 1024, 2048, 3072
EV_DQ, EV_DK, EV_DV, EV_DG = 4096, 5120, 6144, 7168
EV_MQ, EV_MG = 8192, 8704
EV_MAIN = 9216
EV_SIDE_LO, EV_SIDE_HI = 3072, 3104

OD_MLQ_NOPE, OD_MLQ_ROPE = 0, 1024
OD_SQ, OD_SK, OD_SV, OD_SG = 1536, 2560, 2816, 3072
OD_CKV, OD_MLG, OD_MQ, OD_MG = 4096, 4608, 5632, 6144
OD_MAIN = 6656
OD_KR_LO, OD_KR_HI = 4608, 4672


def _cparams(sem):
    return pltpu.CompilerParams(dimension_semantics=sem, vmem_limit_bytes=VMEM_LIMIT)


def _dot(a, b):
    return jnp.dot(a, b, preferred_element_type=F32)


def _dot_nt(a, b):
    return lax.dot_general(a, b, (((1,), (1,)), ((), ())), preferred_element_type=F32)


def _silu(x):
    x = x.astype(F32)
    return x * jax.nn.sigmoid(x)


def _softplus(x):
    return jnp.maximum(x, 0.0) + jnp.log(1.0 + jnp.exp(-jnp.abs(x)))


def _lane_sum(x, scale=1.0):
    return _dot(x.astype(BF16), jnp.ones((LANES, LANES), BF16)) * scale


def _rms(x, gain):
    x = x.astype(F32)
    if x.shape[-1] == LANES:
        ms = _lane_sum(x * x, 1.0 / LANES)
    else:
        ms = jnp.mean(x * x, axis=-1, keepdims=True)
    return x * lax.rsqrt(ms + EPS) * gain


def _norm_proj_kernel(x_ref, g_ref, w_ref, ws_ref, o_ref, os_ref, xn_ref):
    @pl.when(pl.program_id(1) == 0)
    def _():
        xn = _rms(x_ref[...], g_ref[...]).astype(BF16)
        xn_ref[...] = xn
        os_ref[...] = _dot_nt(xn, ws_ref[...])

    o_ref[...] = _dot_nt(xn_ref[...], w_ref[...]).astype(o_ref.dtype)


def _norm_proj(x, gain, w_main, w_side, tm, tn, out_dtype):
    m, k = x.shape
    n = w_main.shape[0]
    ns = w_side.shape[0]
    assert m % tm == 0 and n % tn == 0
    return pl.pallas_call(
        _norm_proj_kernel,
        grid=(m // tm, n // tn),
        in_specs=[
            pl.BlockSpec((tm, k), lambda i, j: (i, 0)),
            pl.BlockSpec((1, k), lambda i, j: (0, 0)),
            pl.BlockSpec((tn, k), lambda i, j: (j, 0)),
            pl.BlockSpec((ns, k), lambda i, j: (0, 0)),
        ],
        out_specs=[
            pl.BlockSpec((tm, tn), lambda i, j: (i, j)),
            pl.BlockSpec((tm, ns), lambda i, j: (i, 0)),
        ],
        out_shape=[jax.ShapeDtypeStruct((m, n), out_dtype), jax.ShapeDtypeStruct((m, ns), F32)],
        scratch_shapes=[pltpu.VMEM((tm, k), BF16)],
        compiler_params=_cparams(("parallel", "arbitrary")),
        name="norm_proj",
    )(x, gain.reshape(1, k), w_main, w_side)


def _out_proj_kernel(x_ref, ya_ref, yb_ref, ym_ref, wa_ref, wb_ref, wm_ref, o_ref):
    acc = _dot(ya_ref[...], wa_ref[...])
    acc = acc + _dot(yb_ref[...], wb_ref[...])
    acc = acc + _dot(ym_ref[...], wm_ref[...])
    o_ref[...] = x_ref[...] + acc


def _out_proj(x, ya, yb, ym, w_out, tm):
    m, d = x.shape
    na, nb, nm = ya.shape[1], yb.shape[1], ym.shape[1]
    wa = w_out[:na].astype(BF16)
    wb = w_out[na:na + nb].astype(BF16)
    wm = w_out[na + nb:].astype(BF16)
    row = lambda i: (i, 0)
    fixed = lambda i: (0, 0)
    return pl.pallas_call(
        _out_proj_kernel,
        grid=(m // tm,),
        in_specs=[
            pl.BlockSpec((tm, d), row), pl.BlockSpec((tm, na), row), pl.BlockSpec((tm, nb), row),
            pl.BlockSpec((tm, nm), row),
            pl.BlockSpec((na, d), fixed), pl.BlockSpec((nb, d), fixed), pl.BlockSpec((nm, d), fixed),
        ],
        out_specs=pl.BlockSpec((tm, d), row),
        out_shape=jax.ShapeDtypeStruct((m, d), F32),
        compiler_params=_cparams(("parallel",)),
        name="out_proj",
    )(x, ya, yb, ym, wa, wb, wm)


def _gdn_kernel(q_ref, k_ref, v_ref, gate_ref, bar_ref, cwq_ref, cwk_ref, cwv_ref, prr_ref, gain_ref, y_ref,
                xp_ref, qd_ref, kw_ref, b_ref, egl_ref, o_ref, lvl_ref, tri_ref):
    seq = q_ref.shape[0]
    c = GDN_CHUNK
    nc = seq // c
    pad = 8
    scale = GDN_DK ** -0.5

    for i, src in enumerate((q_ref, k_ref, v_ref)):
        xp_ref[i, 0:pad, :] = jnp.zeros((pad, LANES), F32)
        xp_ref[i, pad + seq:2 * pad + seq, :] = jnp.zeros((pad, LANES), F32)
        xp_ref[i, pad:pad + seq, :] = src[...].astype(F32)

    hc = c // 2
    row_h = lax.broadcasted_iota(jnp.int32, (hc, hc), 0)
    col_h = lax.broadcasted_iota(jnp.int32, (hc, hc), 1)
    for bit in range(GDN_LEVELS):
        lvl_ref[bit] = (((row_h ^ col_h) >> bit) == 1).astype(BF16)
    lvl_ref[GDN_LEVELS] = (row_h == col_h).astype(BF16)
    tri_ref[0] = jnp.where(row_h > col_h, 0.0, NEG)
    tri_ref[1] = jnp.where(row_h < col_h, 0.0, NEG)
    tri_ref[2] = (row_h == col_h).astype(F32)

    lane_c = lax.broadcasted_iota(jnp.int32, (8, c), 1)
    sub_c = lax.broadcasted_iota(jnp.int32, (8, c), 0)

    def prefix(x):
        s = 1
        while s < c:
            x = x + jnp.where(lane_c >= s, pltpu.roll(x, s, 1), 0.0)
            s *= 2
        return x

    def suffix(x):
        s = 1
        while s < c:
            x = x + jnp.where(lane_c < c - s, pltpu.roll(x, c - s, 1), 0.0)
            s *= 2
        return x

    def conv(i, cw_ref, t0):
        half = (GDN_CONV - 1) // 2
        acc = None
        for j in range(GDN_CONV):
            tap = xp_ref[i, pl.ds(t0 + (pad - half + j), c), :] * cw_ref[j:j + 1, :]
            acc = tap if acc is None else acc + tap
        return _silu(acc)

    def l2n(x):
        return x * lax.rsqrt(_lane_sum(x * x) + EPS)

    half = lambda i: slice(i * hc, (i + 1) * hc)


    def prologue(ns, chains):
        st = []
        for n in ns:
            t0 = pl.multiple_of(n * c, c)
            st.append(dict(n=n, t0=t0, rows=pl.ds(t0, c)))
        for name, idx, cw_ref in (("q", 0, cwq_ref), ("k", 1, cwk_ref), ("v", 2, cwv_ref)):
            for s in st:
                s[name] = conv(idx, cw_ref, s["t0"])
            yield
        for s in st:
            s["q"], s["k"] = l2n(s["q"]), l2n(s["k"])
        yield
        for s in st:
            s["qs"] = s["q"] * scale
            s["kb"] = s["k"].astype(BF16)
            s["qk"] = _dot_nt(s["qs"].astype(BF16), s["kb"])
            bar = bar_ref[:, s["rows"]]
            g_r = (-LOG2E * jnp.exp(prr_ref[:, 0:1])) * _softplus(bar + prr_ref[:, 1:2])
            s["pre"], s["suf"] = prefix(g_r), suffix(g_r)
            s["tot"] = jnp.sum(g_r, axis=1, keepdims=True)
            packed = jnp.where(sub_c < 2, jax.nn.sigmoid(bar), jnp.where(sub_c == 2, s["pre"], s["suf"]))
            s["cols"] = jnp.concatenate([packed, jnp.zeros((LANES - 8, c), F32)], axis=0).T
            o_ref[s["rows"], :] = jnp.zeros((c, LANES), F32)
        yield
        for d in range(2):
            for s in st:
                s["kkb", d] = _dot_nt((s["k"] * s["cols"][:, d:d + 1]).astype(BF16), s["kb"])
            yield
        for d in range(2):
            first, second = (0, 1) if d == 0 else (1, 0)
            for s in st:
                gr = s["pre"][2:3, :] if d == 0 else s["suf"][3:4, :]
                gc = s["cols"][:, 2 + d:3 + d]
                blocks, attn = {}, {}
                for r, t in ((first, first), (second, second), (second, first)):
                    e = gc[half(r)] - gr[:, half(t)]
                    if r == t:
                        e = e + tri_ref[d]
                    dec = jnp.exp2(e)
                    blocks[r, t] = (s["kkb", d][half(r), half(t)] * dec).astype(BF16)
                    if r == t:
                        dec = dec + tri_ref[2]
                    attn[r, t] = (s["qk"][half(r), half(t)] * dec).astype(BF16)
                attn[first, second] = jnp.zeros((hc, hc), BF16)
                a_full = jnp.concatenate([jnp.concatenate([attn[r, 0], attn[r, 1]], axis=1) for r in range(2)], axis=0)
                diag = [blocks[0, 0], blocks[1, 1]]
                chains.append(dict(n=s["n"], rows=s["rows"], d=d, k=s["k"], v=s["v"],
                                   qs=s["qs"], beta=s["cols"][:, d:d + 1], gc=gc, tot=s["tot"][2 + d:3 + d, :],
                                   m=diag, off=blocks[second, first], a=a_full,
                                   p=[lvl_ref[GDN_LEVELS] - blk * lvl_ref[0] for blk in diag]))
            yield

    def levels(chains):
        for bit in range(1, GDN_LEVELS):
            lvl = lvl_ref[bit]
            for ch in chains:
                ch["x"] = [_dot(p, m * lvl).astype(BF16) for p, m in zip(ch["p"], ch["m"])]
            yield
            for ch in chains:
                ch["p"] = [p - _dot(x, p).astype(BF16) for p, x in zip(ch["p"], ch["x"])]
            yield

    def epilogue(chains):
        order = lambda ch: (0, 1) if ch["d"] == 0 else (1, 0)
        for ch in chains:
            ch["eg"] = jnp.exp2(ch["gc"])
            ch["rhs"] = jnp.concatenate([ch["v"] * ch["beta"], ch["k"] * (ch["beta"] * ch["eg"])], axis=1)
            ch["x1"] = _dot(ch["p"][order(ch)[0]], ch["rhs"][half(order(ch)[0])].astype(BF16))
        yield
        for ch in chains:
            ch["cross"] = _dot(ch["off"], ch["x1"].astype(BF16))
        yield
        for ch in chains:
            second = order(ch)[1]
            ch["x2"] = _dot(ch["p"][second], (ch["rhs"][half(second)] - ch["cross"]).astype(BF16))
        yield
        for ch in chains:
            xs = (ch["x1"], ch["x2"]) if ch["d"] == 0 else (ch["x2"], ch["x1"])
            ch["xb"] = jnp.concatenate(xs, axis=0).astype(BF16)
            kd = ch["k"] * jnp.exp2(ch["tot"] - ch["gc"])
            ch["kx"] = _dot(kd.T.astype(BF16), ch["xb"])
            ch["ax"] = _dot(ch["a"], ch["xb"])
        yield
        for ch in chains:
            d, rows = ch["d"], ch["rows"]
            blk = pl.ds(pl.multiple_of(ch["n"] * GDN_DK, GDN_DK), GDN_DK)
            b_ref[d, blk, :] = ch["kx"][:, :GDN_DV]
            kw_ref[d, blk, :] = ch["kx"][:, GDN_DV:].astype(BF16)
            o_ref[rows, :] = o_ref[rows, :] + ch["ax"][:, :GDN_DV]
            qd_ref[d, rows, :] = (ch["qs"] * ch["eg"] - ch["ax"][:, GDN_DV:]).astype(BF16)
            egl_ref[d, pl.ds(pl.multiple_of(ch["n"] * 8, 8), 8), :] = jnp.broadcast_to(jnp.exp2(ch["tot"]), (8, LANES))
        yield

    def interleave(*gens):
        gens = list(gens)
        while gens:
            for g in list(gens):
                if next(g, StopIteration) is StopIteration:
                    gens.remove(g)

    ngroups = nc // GDN_PREP_CHUNKS
    groups = [[g * GDN_PREP_CHUNKS + i for i in range(GDN_PREP_CHUNKS)] for g in range(ngroups)]
    chains = [[] for _ in range(ngroups)]
    interleave(prologue(groups[0], chains[0]))
    for g in range(ngroups):
        gens = [levels(chains[g])]
        if g + 1 < ngroups:
            gens.append(prologue(groups[g + 1], chains[g + 1]))
        if g >= 1:
            gens.append(epilogue(chains[g - 1]))
        interleave(*gens)
    interleave(epilogue(chains[ngroups - 1]))

    def scan(n, carry):
        states = list(carry)
        for d in range(2):
            idx = n if d == 0 else nc - 1 - n
            t0 = pl.multiple_of(idx * c, c)
            rows = pl.ds(t0, c)
            blk = pl.ds(pl.multiple_of(idx * GDN_DK, GDN_DK), GDN_DK)
            s = states[d]
            sb = s.astype(BF16)
            o_ref[rows, :] = o_ref[rows, :] + _dot(qd_ref[d, rows, :], sb)
            egl = egl_ref[d, pl.ds(pl.multiple_of(idx * 8, 8), 8), :][0:1, :]
            states[d] = s * egl - _dot(kw_ref[d, blk, :], sb) + b_ref[d, blk, :]
        return tuple(states)

    zero = jnp.zeros((GDN_DK, GDN_DV), F32)
    lax.fori_loop(0, nc, scan, (zero, zero))

    def fin(n, carry):
        rows = pl.ds(pl.multiple_of(n * c, c), c)
        y = _rms(o_ref[rows, :], gain_ref[...]) * _silu(gate_ref[rows, :])
        y_ref[rows, :] = y.astype(y_ref.dtype)
        return carry

    lax.fori_loop(0, nc, fin, 0)


def _gdn(h, bar, conv_w, prr, gain, bsz, seq):
    nh = GDN_HEADS
    blk = lambda off: pl.BlockSpec((seq, LANES), lambda b, hh, off=off: (b, off // LANES + hh))
    cw = lambda off: pl.BlockSpec((GDN_CONV, LANES), lambda b, hh, off=off: (0, off // LANES + hh))
    c = GDN_CHUNK
    assert seq % (GDN_PREP_CHUNKS * c) == 0
    return pl.pallas_call(
        _gdn_kernel,
        grid=(bsz, nh),
        in_specs=[
            blk(EV_GQ), blk(EV_GK), blk(EV_GV), blk(EV_GG),
            pl.BlockSpec((None, None, 8, seq), lambda b, hh: (b, hh, 0, 0)),
            cw(0), cw(GDN_HEADS * GDN_DK), cw(2 * GDN_HEADS * GDN_DK),
            pl.BlockSpec((None, 8, 2), lambda b, hh: (hh, 0, 0)),
            pl.BlockSpec((1, LANES), lambda b, hh: (0, 0)),
        ],
        out_specs=pl.BlockSpec((seq, LANES), lambda b, hh: (b, hh)),
        out_shape=jax.ShapeDtypeStruct((bsz * seq, nh * GDN_DV), BF16),
        scratch_shapes=[
            pltpu.VMEM((3, seq + 16, LANES), F32),
            pltpu.VMEM((2, seq, LANES), BF16),
            pltpu.VMEM((2, (seq // c) * GDN_DK, GDN_DV), BF16),
            pltpu.VMEM((2, (seq // c) * GDN_DK, GDN_DV), F32),
            pltpu.VMEM((2, (seq // c) * 8, LANES), F32),
            pltpu.VMEM((seq, LANES), F32),
            pltpu.VMEM((GDN_LEVELS + 1, c // 2, c // 2), BF16),
            pltpu.VMEM((3, c // 2, c // 2), F32),
        ],
        compiler_params=_cparams(("parallel", "parallel")),
        name="gdn",
    )(h, h, h, h, bar, conv_w, conv_w, conv_w, prr, gain.reshape(1, LANES))


def _flash_scratch(nmaps, tq, tk, dv):
    per_map = [pltpu.VMEM((tq, tk), F32), pltpu.VMEM((tq, tk), F32), pltpu.VMEM((tq, tk), BF16),
               pltpu.VMEM((tq, LANES), F32), pltpu.VMEM((tq, LANES), F32), pltpu.VMEM((tq, dv), F32)]
    return per_map * nmaps


def _flash_core(qs, k_ref, v_ref, nsteps, step_of, loop, scratch):
    nmaps = len(qs)
    maps = [scratch[6 * i:6 * i + 6] for i in range(nmaps)]
    tq, tk = maps[0][0].shape
    lo, hi = loop
    assert tk % LANES == 0 and tq % ATT_RB == 0 and 0 < lo <= hi < nsteps and (hi - lo) % 2 == 0
    nlb = tk // LANES

    def chunk(ref, t):
        j = step_of(t)[0]
        start = j * tk if isinstance(j, int) else pl.multiple_of(j * tk, tk)
        return ref[pl.ds(start, tk), :]

    def qk(i, t, slot):
        maps[i][slot][...] = _dot_nt(qs[i], chunk(k_ref, t))

    maps[nmaps - 1][5][...] = jnp.zeros(maps[nmaps - 1][5].shape, F32)

    def pv(i, t):
        acc = maps[i][5]
        out = _dot(maps[i][2][...], chunk(v_ref, t))
        acc[...] = out if i < nmaps - 1 and isinstance(t, int) and t == 0 else acc[...] + out

    def softmax(i, t, slot, first):
        s_ref, p_ref, m_ref, l_ref, acc_ref = maps[i][slot], maps[i][2], maps[i][3], maps[i][4], maps[i][5]
        _, tile, const = step_of(t)
        for rb in range(tq // ATT_RB):
            r = slice(rb * ATT_RB, (rb + 1) * ATT_RB)
            s = s_ref[r, :]
            if tile is not None:
                s = s + tile[r, :]
            blocks = [s[:, b * LANES:(b + 1) * LANES] for b in range(nlb)]
            mx = functools.reduce(jnp.maximum, blocks)
            mx = jnp.broadcast_to(jnp.max(mx, axis=-1, keepdims=True), (ATT_RB, LANES))
            if const is not None:
                mx = mx + const
            m_new = mx if first else jnp.maximum(m_ref[r, :], mx)
            shift = m_new if const is None else m_new - const
            ps = [jnp.exp2(b - shift) for b in blocks]
            row_sum = jnp.broadcast_to(jnp.sum(functools.reduce(jnp.add, ps), axis=-1, keepdims=True), m_new.shape)
            if first:
                l_ref[r, :] = row_sum
            else:
                alpha = jnp.exp2(m_ref[r, :] - m_new)
                l_ref[r, :] = alpha * l_ref[r, :] + row_sum
                acc_ref[r, :] = acc_ref[r, :] * alpha
            m_ref[r, :] = m_new
            p_ref[r, :] = jnp.concatenate(ps, axis=1).astype(BF16)

    def stage(t, slot, first=False, last=False):
        for i in range(nmaps):
            if i > 0:
                pv(i - 1, t)
            elif not first:
                pv(nmaps - 1, t - 1)
            if not last:
                qk(i, t + 1, 1 - slot)
            softmax(i, t, slot, first)

    def pair(tt, carry):
        for u in range(2):
            stage(lo + 2 * tt + u, (lo + u) % 2)
        return carry

    for i in range(nmaps):
        qk(i, 0, 0)
    for t in range(lo):
        stage(t, t % 2, first=t == 0)
    lax.fori_loop(0, (hi - lo) // 2, pair, 0)
    for t in range(hi, nsteps):
        stage(t, t % 2, last=t == nsteps - 1)
    pv(nmaps - 1, nsteps - 1)
    return [(mp[5][...], mp[4][...]) for mp in maps]


def _diff_prep_kernel(q_ref, k_ref, v_ref, qg_ref, kg_ref, qo_ref, ko_ref, vo_ref):
    r = lax.broadcasted_iota(jnp.int32, (LANES, LANES), 0) < DIFF_DQK
    cc = lax.broadcasted_iota(jnp.int32, (LANES, LANES), 1) < DIFF_DQK
    half_mean = jnp.where(r == cc, 1.0 / DIFF_DQK, 0.0).astype(BF16)

    def halfnorm(x, gain):
        x = x.astype(F32)
        return x * lax.rsqrt(_dot((x * x).astype(BF16), half_mean) + EPS) * gain

    for hh in range(DIFF_HEADS):
        sl = slice(hh * LANES, (hh + 1) * LANES)
        qo_ref[:, sl] = (halfnorm(q_ref[:, sl], qg_ref[...]) * (DIFF_DQK ** -0.5 * LOG2E)).astype(BF16)
        ko_ref[:, sl] = halfnorm(k_ref[:, sl], kg_ref[...]).astype(BF16)
    vo_ref[...] = v_ref[...].astype(BF16)


def _diff_prep(h, qg, kg, tm):
    m = h.shape[0]
    w = DIFF_HEADS * LANES
    spec = lambda off: pl.BlockSpec((tm, w), lambda i, off=off: (i, off // w))
    vec = pl.BlockSpec((1, LANES), lambda i: (0, 0))
    out = jax.ShapeDtypeStruct((m, w), BF16)
    return pl.pallas_call(
        _diff_prep_kernel,
        grid=(m // tm,),
        in_specs=[spec(EV_DQ), spec(EV_DK), spec(EV_DV), vec, vec],
        out_specs=[pl.BlockSpec((tm, w), lambda i: (i, 0))] * 3,
        out_shape=[out, out, out],
        compiler_params=_cparams(("parallel",)),
        name="diff_prep",
    )(h, h, h, jnp.tile(qg, 2).reshape(1, LANES), jnp.tile(kg, 2).reshape(1, LANES))


def _toeplitz(vec, tq, tk):
    w = vec.shape[-1]
    full = pltpu.roll(jnp.broadcast_to(vec, (tq, w)), w - (tq - 1), 1, stride=1, stride_axis=0)
    return full[:, :tk]


def _diff_flash_kernel(q_ref, k_ref, v_ref, gate_ref, bvec_ref, lam_ref, sub_ref, y_ref, bias_ref, *scratch,
                       lambda_init):
    tq = q_ref.shape[0]
    tk = bias_ref.shape[-1]
    qi = pl.program_id(2)

    @pl.when(qi == 0)
    def _():
        for d in range(bias_ref.shape[0]):
            bias_ref[d] = _toeplitz(bvec_ref[d:d + 1, :], tq, tk)

    q = q_ref[...]
    lane = lax.broadcasted_iota(jnp.int32, (1, LANES), 1)
    zero = jnp.zeros_like(q)
    q0 = jnp.where(lane < DIFF_DQK, q, zero)
    q1 = jnp.where(lane < DIFF_DQK, zero, q)

    nk = k_ref.shape[0] // tk
    near = 3
    n0 = jnp.clip(qi - 1, 0, nk - near)
    c_before, c_after = bvec_ref[0:1, 0:1], bvec_ref[4:5, 0:1]
    nfar = nk - near

    def step_of(t):
        if isinstance(t, int) and t >= nfar:
            j = n0 + (t - nfar)
            return j, bias_ref.at[jnp.clip(j - qi, -2, 2) + 2], None
        j = jnp.where(t < n0, t, t + near)
        j = jnp.where(t < nfar, j, n0 + (t - nfar))
        return j, None, jnp.where(j < qi, c_before, c_after)

    loop = (1, nfar - (nfar - 1) % 2)
    (a0, l0), (a1, l1) = _flash_core([q0, q1], k_ref, v_ref, nk, step_of, loop, scratch)

    lam = lam_ref[...]
    lam_full = (jnp.exp(jnp.sum(lam[0:1] * lam[1:2], axis=-1, keepdims=True))
                - jnp.exp(jnp.sum(lam[2:3] * lam[3:4], axis=-1, keepdims=True)) + lambda_init)
    o = a0 / l0 - lam_full * (a1 / l1)
    o = _rms(o, sub_ref[...]) * (1.0 - lambda_init)
    y_ref[...] = (o * _silu(gate_ref[...])).astype(y_ref.dtype)


def _diff_flash(dq, dk, dv, h, bvec, lam, subln, lambda_init, bsz, seq):
    tq, tk = ATT_TQ, ATT_TK
    nq = seq // tq
    nh = DIFF_HEADS
    return pl.pallas_call(
        functools.partial(_diff_flash_kernel, lambda_init=lambda_init),
        grid=(bsz, nh, nq),
        in_specs=[
            pl.BlockSpec((tq, LANES), lambda b, hh, i: (b * nq + i, hh)),
            pl.BlockSpec((seq, LANES), lambda b, hh, i: (b, hh)),
            pl.BlockSpec((seq, LANES), lambda b, hh, i: (b, hh)),
            pl.BlockSpec((tq, LANES), lambda b, hh, i: (b * nq + i, EV_DG // LANES + hh)),
            pl.BlockSpec((None,) + bvec.shape[1:], lambda b, hh, i: (hh, 0, 0)),
            pl.BlockSpec((4, DIFF_DQK), lambda b, hh, i: (0, 0)),
            pl.BlockSpec((1, LANES), lambda b, hh, i: (0, 0)),
        ],
        out_specs=pl.BlockSpec((tq, LANES), lambda b, hh, i: (b * nq + i, hh)),
        out_shape=jax.ShapeDtypeStruct((bsz * seq, nh * DIFF_DV), BF16),
        scratch_shapes=[pltpu.VMEM((5, tq, tk), F32)] + _flash_scratch(2, tq, tk, DIFF_DV),
        compiler_params=_cparams(("parallel", "parallel", "arbitrary")),
        name="diff_flash",
    )(dq, dk, dv, h, bvec, lam, subln.reshape(1, LANES))


def _mem_attn_kernel(q_ref, gate_ref, mk_ref, mv_ref, qg_ref, kg_ref, y_ref):
    q = (_rms(q_ref[...], qg_ref[...]) * (MEM_DH ** -0.5)).astype(BF16)
    mk = _rms(mk_ref[...], kg_ref[...]).astype(BF16)
    s = _dot_nt(q, mk)
    p = jnp.exp(s - jnp.max(s, axis=-1, keepdims=True))
    o = _dot(p.astype(BF16), mv_ref[...].astype(BF16)) / jnp.sum(p, axis=-1, keepdims=True)
    y_ref[...] = (o * _silu(gate_ref[...])).astype(y_ref.dtype)


def _mem_attn(h, memkv, qg, kg, q_off, g_off, bsz, seq, tq):
    nq = seq // tq
    nh = MEM_HEADS
    vec = pl.BlockSpec((1, LANES), lambda b, hh, i: (0, 0))
    return pl.pallas_call(
        _mem_attn_kernel,
        grid=(bsz, nh, nq),
        in_specs=[
            pl.BlockSpec((tq, LANES), lambda b, hh, i: (b * nq + i, q_off // LANES + hh)),
            pl.BlockSpec((tq, LANES), lambda b, hh, i: (b * nq + i, g_off // LANES + hh)),
            pl.BlockSpec((MEM_LEN, LANES), lambda b, hh, i: (b, hh)),
            pl.BlockSpec((MEM_LEN, LANES), lambda b, hh, i: (b, nh + hh)),
            vec, vec,
        ],
        out_specs=pl.BlockSpec((tq, LANES), lambda b, hh, i: (b * nq + i, hh)),
        out_shape=jax.ShapeDtypeStruct((bsz * seq, nh * MEM_DH), BF16),
        compiler_params=_cparams(("parallel", "parallel", "parallel")),
        name="mem_attn",
    )(h, h, memkv, memkv, qg.reshape(1, LANES), kg.reshape(1, LANES))


def _swa_kernel(sink_ref, q_ref, k_ref, v_ref, gate_ref, bias_ref, qg_ref, kg_ref, y_ref, tile_ref):
    tq = q_ref.shape[0]
    seq = k_ref.shape[0]
    grp = SWA_HEADS // SWA_KV_HEADS
    kvh = pl.program_id(1)
    qi = pl.program_id(2)
    nq = pl.num_programs(2)

    @pl.when(jnp.logical_or(qi <= 1, qi == nq - 1))
    def _():
        for g in range(grp):
            tile_ref[g] = _toeplitz(bias_ref[g:g + 1, :], tq, SWA_TW)

    ws = pl.multiple_of(jnp.clip(qi * tq - WINDOW, 0, seq - SWA_TW), WINDOW)
    kw = _rms(k_ref[pl.ds(ws, SWA_TW), :], kg_ref[...]).astype(BF16)
    vw = v_ref[pl.ds(ws, SWA_TW), :].astype(BF16)
    heads = [slice(g * SWA_DH, (g + 1) * SWA_DH) for g in range(grp)]
    qs = [(_rms(q_ref[:, sl], qg_ref[...]) * (SWA_DH ** -0.5)).astype(BF16) for sl in heads]
    ss = [_dot_nt(q, kw) + tile_ref[g] for g, q in enumerate(qs)]
    ps, dens = [], []
    for g, s in enumerate(ss):
        sink = sink_ref[kvh * grp + g]
        mx = jnp.maximum(jnp.max(s, axis=-1, keepdims=True), sink)
        p = jnp.exp(s - mx)
        dens.append(jnp.sum(p, axis=-1, keepdims=True) + jnp.exp(sink - mx))
        ps.append(p.astype(BF16))
    for sl, p, den in zip(heads, ps, dens):
        y_ref[:, sl] = (_dot(p, vw) / den * _silu(gate_ref[:, sl])).astype(y_ref.dtype)


def _swa(h, bias, sink, qg, kg, bsz, seq):
    tq = SWA_TQ
    nq = seq // tq
    grp = SWA_HEADS // SWA_KV_HEADS
    gw = grp * SWA_DH
    assert seq >= SWA_TW and nq >= 2
    vec = pl.BlockSpec((1, LANES), lambda b, kv, i: (0, 0))

    def bias_idx(b, kv, i):
        return (jnp.where(i == 0, 0, jnp.where(i == nq - 1, 2, 1)), kv, 0, 0)

    return pl.pallas_call(
        _swa_kernel,
        grid=(bsz, SWA_KV_HEADS, nq),
        in_specs=[
            pl.BlockSpec(memory_space=pltpu.SMEM),
            pl.BlockSpec((tq, gw), lambda b, kv, i: (b * nq + i, OD_SQ // gw + kv)),
            pl.BlockSpec((seq, LANES), lambda b, kv, i: (b, OD_SK // LANES + kv)),
            pl.BlockSpec((seq, LANES), lambda b, kv, i: (b, OD_SV // LANES + kv)),
            pl.BlockSpec((tq, gw), lambda b, kv, i: (b * nq + i, OD_SG // gw + kv)),
            pl.BlockSpec((None, None, grp, bias.shape[-1]), bias_idx),
            vec, vec,
        ],
        out_specs=pl.BlockSpec((tq, gw), lambda b, kv, i: (b * nq + i, kv)),
        out_shape=jax.ShapeDtypeStruct((bsz * seq, SWA_HEADS * SWA_DH), BF16),
        scratch_shapes=[pltpu.VMEM((grp, tq, SWA_TW), F32)],
        compiler_params=_cparams(("parallel", "parallel", "arbitrary")),
        name="swa",
    )(sink, h, h, h, h, bias, qg.reshape(1, LANES), kg.reshape(1, LANES))


def _mla_prep_kernel(q_ref, ckv_ref, kr_ref, cos_ref, sin_ref, kvg_ref, wup_ref,
                     qgn_ref, qgr_ref, kgn_ref, kgr_ref, qo_ref, ko_ref, vo_ref):
    dqk = MLA_NOPE + MLA_ROPE
    lane = lax.broadcasted_iota(jnp.int32, (1, LANES), 1)
    lo = lane < MLA_ROPE
    first = lane < MLA_ROPE // 2
    cos = cos_ref[...]
    sin = sin_ref[...]

    def rope(t):
        rot = jnp.where(first, pltpu.roll(t, LANES - MLA_ROPE // 2, 1), pltpu.roll(t, MLA_ROPE // 2, 1))
        return t * cos + rot * sin

    ckv = _rms(ckv_ref[...], kvg_ref[...]).astype(BF16)
    kv = _dot(ckv, wup_ref[...])
    kr = kr_ref[...]
    kr2 = kr * kr
    c = dqk ** -0.5 * LOG2E
    hw = MLA_NOPE + MLA_DV
    for hh in range(MLA_HEADS):
        kn = kv[:, hh * hw:hh * hw + MLA_NOPE]
        inv = lax.rsqrt(_lane_sum(kn * kn + kr2, 1.0 / dqk) + EPS)
        ko_ref[:, hh * 2 * LANES:hh * 2 * LANES + LANES] = (kn * inv * kgn_ref[...]).astype(BF16)
        ko_ref[:, hh * 2 * LANES + LANES:(hh + 1) * 2 * LANES] = rope(kr * inv * kgr_ref[...]).astype(BF16)
        vo_ref[:, hh * LANES:(hh + 1) * LANES] = kv[:, hh * hw + MLA_NOPE:(hh + 1) * hw].astype(BF16)

        qn = q_ref[:, hh * LANES:(hh + 1) * LANES].astype(F32)
        pair = q_ref[:, OD_MLQ_ROPE + (hh // 2) * LANES:OD_MLQ_ROPE + (hh // 2 + 1) * LANES].astype(F32)
        if hh % 2 == 1:
            pair = pltpu.roll(pair, MLA_ROPE, 1)
        qr = jnp.where(lo, pair, 0.0)
        inv = lax.rsqrt(_lane_sum(qn * qn + qr * qr, 1.0 / dqk) + EPS)
        qo_ref[:, hh * 2 * LANES:hh * 2 * LANES + LANES] = (qn * inv * qgn_ref[...] * c).astype(BF16)
        qo_ref[:, hh * 2 * LANES + LANES:(hh + 1) * 2 * LANES] = (rope(qr * inv * qgr_ref[...]) * c).astype(BF16)


def _mla_prep(h, side, cos, sin, kv_gain, w_up, qg, kg, tm):
    m = h.shape[0]
    nh = MLA_HEADS
    qw = nh * (MLA_NOPE + MLA_ROPE)
    pad = lambda g: jnp.pad(g[MLA_NOPE:], (0, LANES - MLA_ROPE)).reshape(1, LANES)
    vec = pl.BlockSpec((1, LANES), lambda i: (0, 0))
    row = lambda i: (i, 0)
    return pl.pallas_call(
        _mla_prep_kernel,
        grid=(m // tm,),
        in_specs=[
            pl.BlockSpec((tm, qw), row),
            pl.BlockSpec((tm, MLA_KV_RANK), lambda i: (i, OD_CKV // MLA_KV_RANK)),
            pl.BlockSpec((tm, LANES), row),
            pl.BlockSpec((tm, LANES), row),
            pl.BlockSpec((tm, LANES), row),
            pl.BlockSpec((1, MLA_KV_RANK), lambda i: (0, 0)),
            pl.BlockSpec(w_up.shape, lambda i: (0, 0)),
            vec, vec, vec, vec,
        ],
        out_specs=[
            pl.BlockSpec((tm, nh * 2 * LANES), row),
            pl.BlockSpec((tm, nh * 2 * LANES), row),
            pl.BlockSpec((tm, nh * MLA_DV), row),
        ],
        out_shape=[
            jax.ShapeDtypeStruct((m, nh * 2 * LANES), BF16),
            jax.ShapeDtypeStruct((m, nh * 2 * LANES), BF16),
            jax.ShapeDtypeStruct((m, nh * MLA_DV), BF16),
        ],
        compiler_params=_cparams(("parallel",)),
        name="mla_prep",
    )(h, h, side, cos, sin, kv_gain.reshape(1, MLA_KV_RANK), w_up.astype(BF16),
      qg[:MLA_NOPE].reshape(1, LANES), pad(qg), kg[:MLA_NOPE].reshape(1, LANES), pad(kg))


def _mla_flash_kernel(q_ref, k_ref, v_ref, gate_ref, y_ref, *scratch):
    nk = k_ref.shape[0] // ATT_TK
    (acc, l), = _flash_core([q_ref[...]], k_ref, v_ref, nk, lambda t: (t, None, None), (1, nk - 1 - nk % 2), scratch)
    y_ref[...] = (acc / l * _silu(gate_ref[...])).astype(y_ref.dtype)


def _mla_flash(qm, km, vm, h, bsz, seq):
    tq = ATT_TQ
    nq = seq // tq
    nh = MLA_HEADS
    return pl.pallas_call(
        _mla_flash_kernel,
        grid=(bsz, nh, nq),
        in_specs=[
            pl.BlockSpec((tq, 2 * LANES), lambda b, hh, i: (b * nq + i, hh)),
            pl.BlockSpec((seq, 2 * LANES), lambda b, hh, i: (b, hh)),
            pl.BlockSpec((seq, LANES), lambda b, hh, i: (b, hh)),
            pl.BlockSpec((tq, LANES), lambda b, hh, i: (b * nq + i, OD_MLG // LANES + hh)),
        ],
        out_specs=pl.BlockSpec((tq, LANES), lambda b, hh, i: (b * nq + i, hh)),
        out_shape=jax.ShapeDtypeStruct((bsz * seq, nh * MLA_DV), BF16),
        scratch_shapes=_flash_scratch(1, tq, ATT_TK, MLA_DV),
        compiler_params=_cparams(("parallel", "parallel", "arbitrary")),
        name="mla_flash",
    )(qm, km, vm, h)


def _t5_bucket(rel):
    half = REL_BUCKETS // 2
    max_exact = half // 2
    ret = (rel > 0).astype(jnp.int32) * half
    n = jnp.abs(rel)
    nf = jnp.maximum(n, 1).astype(F32)
    large = max_exact + (jnp.log(nf / max_exact) / math.log(REL_MAX_DIST / max_exact)
                         * (half - max_exact)).astype(jnp.int32)
    large = jnp.minimum(large, half - 1)
    return ret + jnp.where(n < max_exact, n, large)


def _round_up(n, m):
    return (n + m - 1) // m * m


def _diff_bias_vecs(rel_table, tq, tk):
    assert tq == tk and tk + 1 >= REL_MAX_DIST
    m = jnp.arange(_round_up(tq + tk - 1, LANES))
    rows = [rel_table[_t5_bucket(m - (tq - 1) + d * tk)] for d in (-2, -1, 0, 1, 2)]
    return jnp.pad(jnp.stack(rows, axis=0).transpose(2, 0, 1) * LOG2E, ((0, 0), (0, 3), (0, 0))).astype(F32)


def _swa_bias_vecs(rel_table):
    m = jnp.arange(_round_up(SWA_TQ + SWA_TW - 1, LANES))
    rows = []
    for shift in (0, WINDOW, SWA_TW - SWA_TQ):
        rel = m - (SWA_TQ - 1) - shift
        rows.append(jnp.where((jnp.abs(rel) <= WINDOW)[None], rel_table[_t5_bucket(rel)].T, NEG))
    return jnp.stack(rows, axis=0).reshape(3, SWA_KV_HEADS, SWA_HEADS // SWA_KV_HEADS, -1).astype(F32)


def _od_main_weight(wt):
    mlq_lo = 2 * SWA_HEADS * SWA_DH + 2 * SWA_KV_HEADS * SWA_DH
    mlq_hi = mlq_lo + MLA_HEADS * (MLA_NOPE + MLA_ROPE)
    mlq = wt[mlq_lo:mlq_hi].reshape(MLA_HEADS, MLA_NOPE + MLA_ROPE, -1)
    nope = mlq[:, :MLA_NOPE].reshape(MLA_HEADS * MLA_NOPE, -1)
    rope = mlq[:, MLA_NOPE:].reshape(MLA_HEADS * MLA_ROPE, -1)
    return jnp.concatenate([nope, rope, wt[:mlq_lo], wt[mlq_hi:OD_KR_LO], wt[OD_KR_HI:]], axis=0).astype(BF16)


def _rope_tables(positions):
    half = MLA_ROPE // 2
    inv_freq = ROPE_BASE ** (-jnp.arange(half, dtype=F32) / half)
    ang = positions.astype(F32).reshape(-1, 1) * inv_freq
    cos, sin = jnp.cos(ang), jnp.sin(ang)
    z = jnp.zeros((ang.shape[0], LANES - MLA_ROPE), F32)
    return jnp.concatenate([cos, cos, z], axis=1), jnp.concatenate([-sin, sin, z], axis=1)


def _side_weight(wt, lo, hi):
    return jnp.pad(wt[lo:hi], ((0, LANES - (hi - lo)), (0, 0))).astype(BF16)


def _mem_kv(mem2, gain, w_kv):
    zero_side = jnp.zeros((LANES, D_MODEL), BF16)
    memkv, _ = _norm_proj(mem2, gain, jnp.swapaxes(w_kv, 0, 1).astype(BF16), zero_side, tm=mem2.shape[0] // 2,
                          tn=512, out_dtype=F32)
    return memkv


def _even_layer(x2, mem2, rel_bias, norm_g, w_in, conv_w, a_log, dt_bias, gdn_gain, dq_gain, dk_gain, lam, subln,
                mem_norm, mem_w_kv, mem_qn, mem_kn, w_out, lambda_init, bsz, seq):
    wt = jnp.swapaxes(w_in, 0, 1)
    w_main = jnp.concatenate([wt[:EV_SIDE_LO], wt[EV_SIDE_HI:]], axis=0).astype(BF16)
    h, side = _norm_proj(x2, norm_g, w_main, _side_weight(wt, EV_SIDE_LO, EV_SIDE_HI), tm=1024, tn=EV_MAIN // 6,
                         out_dtype=BF16)

    nh = GDN_HEADS
    bar = side[:, :4 * nh].reshape(bsz, seq, 4, nh).transpose(0, 3, 2, 1)
    bar = jnp.pad(bar, ((0, 0), (0, 0), (0, 4), (0, 0)))
    prr = jnp.pad(jnp.stack([a_log.T, dt_bias.T], axis=-1), ((0, 0), (2, 4), (0, 0)))
    ya = _gdn(h, bar, conv_w, prr, gdn_gain, bsz, seq)

    dq, dk, dv = _diff_prep(h, dq_gain, dk_gain, tm=1024)
    bias = _diff_bias_vecs(rel_bias, ATT_TQ, ATT_TK)
    yb = _diff_flash(dq, dk, dv, h, bias, lam, subln, lambda_init, bsz, seq)

    memkv = _mem_kv(mem2, mem_norm, mem_w_kv)
    ym = _mem_attn(h, memkv, mem_qn, mem_kn, EV_MQ, EV_MG, bsz, seq, tq=seq)
    return _out_proj(x2, ya, yb, ym, w_out, tm=512)


def _odd_layer(x2, mem2, positions, rel_bias, norm_g, w_in, swa_qn, swa_kn, sink, kv_norm, w_kv_up, mla_qn, mla_kn,
               mem_norm, mem_w_kv, mem_qn, mem_kn, w_out, bsz, seq):
    wt = jnp.swapaxes(w_in, 0, 1)
    h, side = _norm_proj(x2, norm_g, _od_main_weight(wt), _side_weight(wt, OD_KR_LO, OD_KR_HI), tm=1024,
                         tn=OD_MAIN // 4, out_dtype=BF16)

    ya = _swa(h, _swa_bias_vecs(rel_bias), sink, swa_qn, swa_kn, bsz, seq)

    cos, sin = _rope_tables(positions)
    qm, km, vm = _mla_prep(h, side, cos, sin, kv_norm, w_kv_up, mla_qn, mla_kn, tm=512)
    yb = _mla_flash(qm, km, vm, h, bsz, seq)

    memkv = _mem_kv(mem2, mem_norm, mem_w_kv)
    ym = _mem_attn(h, memkv, mem_qn, mem_kn, OD_MQ, OD_MG, bsz, seq, tq=seq)
    return _out_proj(x2, ya, yb, ym, w_out, tm=512)


def kernel(x, mem, positions, rel_bias, ev_norm, ev_w_in, ev_conv, ev_a_log, ev_dt_bias, ev_gdn_norm, ev_diff_qnorm, ev_diff_knorm, ev_diff_lambda, ev_diff_subln, ev_mem_norm, ev_mem_w_kv, ev_mem_qnorm, ev_mem_knorm, ev_w_out, od_norm, od_w_in, od_swa_qnorm, od_swa_knorm, od_swa_sink, od_mla_kv_norm, od_mla_w_kv_up, od_mla_qnorm, od_mla_knorm, od_mem_norm, od_mem_w_kv, od_mem_qnorm, od_mem_knorm, od_w_out):
    bsz, seq, d = x.shape
    depth = ev_norm.shape[0] + od_norm.shape[0]
    x2 = x.reshape(bsz * seq, d)
    mem2 = mem.reshape(bsz * mem.shape[1], d)
    for layer in range(depth):
        i = layer // 2
        if layer % 2 == 0:
            lambda_init = 0.8 - 0.6 * math.exp(-0.3 * layer)
            x2 = _even_layer(x2, mem2, rel_bias, ev_norm[i], ev_w_in[i], ev_conv[i], ev_a_log[i], ev_dt_bias[i],
                             ev_gdn_norm[i], ev_diff_qnorm[i], ev_diff_knorm[i], ev_diff_lambda[i], ev_diff_subln[i],
                             ev_mem_norm[i], ev_mem_w_kv[i], ev_mem_qnorm[i], ev_mem_knorm[i], ev_w_out[i],
                             lambda_init, bsz, seq)
        else:
            x2 = _odd_layer(x2, mem2, positions, rel_bias, od_norm[i], od_w_in[i], od_swa_qnorm[i], od_swa_knorm[i],
                            od_swa_sink[i], od_mla_kv_norm[i], od_mla_w_kv_up[i], od_mla_qnorm[i], od_mla_knorm[i],
                            od_mem_norm[i], od_mem_w_kv[i], od_mem_qnorm[i], od_mem_knorm[i], od_w_out[i], bsz, seq)
    return x2.reshape(bsz, seq, d)
```

```python
import functools
import math

import jax
import jax.numpy as jnp
from jax import lax
from jax.experimental import pallas as pl
from jax.experimental.pallas import tpu as pltpu

F32 = jnp.float32
BF16 = jnp.bfloat16
EPS = 1e-6
NEG = -1e30

V7X_VMEM_BYTES = 64 * 1024 * 1024
VMEM_LIMIT = V7X_VMEM_BYTES - 8 * 1024 * 1024
LANES = 128

D_MODEL = 2048
MEM_LEN = 256
GDN_HEADS, GDN_DK, GDN_DV, GDN_CONV = 8, 128, 128, 5
DIFF_HEADS, DIFF_DQK, DIFF_DV = 8, 64, 128
SWA_HEADS, SWA_KV_HEADS, SWA_DH, WINDOW = 8, 2, 128, 128
MLA_HEADS, MLA_NOPE, MLA_ROPE, MLA_DV, MLA_KV_RANK = 8, 128, 64, 128, 512
ROPE_BASE = 10000.0
MEM_HEADS, MEM_DH = 4, 128
REL_BUCKETS, REL_MAX_DIST = 32, 128

GDN_CHUNK = 256
GDN_LEVELS = (GDN_CHUNK // 2).bit_length() - 1
GDN_PREP_CHUNKS = 2
ATT_TQ = 512
ATT_TK = 512
ATT_RB = 32
ATT_QBLOCKS = 2
LOG2E = math.log2(math.e)
SWA_TQ = 256
SWA_TW = SWA_TQ + 2 * WINDOW

EV_GQ, EV_GK, EV_GV, EV_GG = 0, 1024, 2048, 3072
EV_DQ, EV_DK, EV_DV, EV_DG = 4096, 5120, 6144, 7168
EV_MQ, EV_MG = 8192, 8704
EV_MAIN = 9216
EV_SIDE_LO, EV_SIDE_HI = 3072, 3104

OD_MLQ_NOPE, OD_MLQ_ROPE = 0, 1024
OD_SQ, OD_SK, OD_SV, OD_SG = 1536, 2560, 2816, 3072
OD_CKV, OD_MLG, OD_MQ, OD_MG = 4096, 4608, 5632, 6144
OD_MAIN = 6656
OD_KR_LO, OD_KR_HI = 4608, 4672


def _cparams(sem):
    return pltpu.CompilerParams(dimension_semantics=sem, vmem_limit_bytes=VMEM_LIMIT)


def _dot(a, b):
    return jnp.dot(a, b, preferred_element_type=F32)


def _dot_nt(a, b):
    return lax.dot_general(a, b, (((1,), (1,)), ((), ())), preferred_element_type=F32)


def _silu(x):
    x = x.astype(F32)
    return x * jax.nn.sigmoid(x)


def _softplus(x):
    return jnp.maximum(x, 0.0) + jnp.log(1.0 + jnp.exp(-jnp.abs(x)))


def _lane_sum(x, scale=1.0):
    return _dot(x.astype(BF16), jnp.ones((LANES, LANES), BF16)) * scale


def _rms(x, gain):
    x = x.astype(F32)
    if x.shape[-1] == LANES:
        ms = _lane_sum(x * x, 1.0 / LANES)
    else:
        ms = jnp.mean(x * x, axis=-1, keepdims=True)
    return x * lax.rsqrt(ms + EPS) * gain


def _norm_proj_kernel(x_ref, g_ref, w_ref, ws_ref, o_ref, os_ref, xn_ref):
    @pl.when(pl.program_id(1) == 0)
    def _():
        xn = _rms(x_ref[...], g_ref[...]).astype(BF16)
        xn_ref[...] = xn
        os_ref[...] = _dot_nt(xn, ws_ref[...])

    o_ref[...] = _dot_nt(xn_ref[...], w_ref[...]).astype(o_ref.dtype)


def _norm_proj(x, gain, w_main, w_side, tm, tn, out_dtype):
    m, k = x.shape
    n = w_main.shape[0]
    ns = w_side.shape[0]
    assert m % tm == 0 and n % tn == 0
    return pl.pallas_call(
        _norm_proj_kernel,
        grid=(m // tm, n // tn),
        in_specs=[
            pl.BlockSpec((tm, k), lambda i, j: (i, 0)),
            pl.BlockSpec((1, k), lambda i, j: (0, 0)),
            pl.BlockSpec((tn, k), lambda i, j: (j, 0)),
            pl.BlockSpec((ns, k), lambda i, j: (0, 0)),
        ],
        out_specs=[
            pl.BlockSpec((tm, tn), lambda i, j: (i, j)),
            pl.BlockSpec((tm, ns), lambda i, j: (i, 0)),
        ],
        out_shape=[jax.ShapeDtypeStruct((m, n), out_dtype), jax.ShapeDtypeStruct((m, ns), F32)],
        scratch_shapes=[pltpu.VMEM((tm, k), BF16)],
        compiler_params=_cparams(("parallel", "arbitrary")),
        name="norm_proj",
    )(x, gain.reshape(1, k), w_main, w_side)


def _out_proj_kernel(x_ref, ya_ref, yb_ref, ym_ref, wa_ref, wb_ref, wm_ref, o_ref):
    acc = _dot(ya_ref[...], wa_ref[...])
    acc = acc + _dot(yb_ref[...], wb_ref[...])
    acc = acc + _dot(ym_ref[...], wm_ref[...])
    o_ref[...] = x_ref[...] + acc


def _out_proj(x, ya, yb, ym, w_out, tm):
    m, d = x.shape
    na, nb, nm = ya.shape[1], yb.shape[1], ym.shape[1]
    wa = w_out[:na].astype(BF16)
    wb = w_out[na:na + nb].astype(BF16)
    wm = w_out[na + nb:].astype(BF16)
    row = lambda i: (i, 0)
    fixed = lambda i: (0, 0)
    return pl.pallas_call(
        _out_proj_kernel,
        grid=(m // tm,),
        in_specs=[
            pl.BlockSpec((tm, d), row), pl.BlockSpec((tm, na), row), pl.BlockSpec((tm, nb), row),
            pl.BlockSpec((tm, nm), row),
            pl.BlockSpec((na, d), fixed), pl.BlockSpec((nb, d), fixed), pl.BlockSpec((nm, d), fixed),
        ],
        out_specs=pl.BlockSpec((tm, d), row),
        out_shape=jax.ShapeDtypeStruct((m, d), F32),
        compiler_params=_cparams(("parallel",)),
        name="out_proj",
    )(x, ya, yb, ym, wa, wb, wm)


def _gdn_kernel(q_ref, k_ref, v_ref, gate_ref, bar_ref, cwq_ref, cwk_ref, cwv_ref, prr_ref, gain_ref, y_ref,
                xp_ref, qd_ref, kw_ref, b_ref, egl_ref, o_ref, lvl_ref, tri_ref):
    seq = q_ref.shape[0]
    c = GDN_CHUNK
    nc = seq // c
    pad = 8
    scale = GDN_DK ** -0.5

    for i, src in enumerate((q_ref, k_ref, v_ref)):
        xp_ref[i, 0:pad, :] = jnp.zeros((pad, LANES), F32)
        xp_ref[i, pad + seq:2 * pad + seq, :] = jnp.zeros((pad, LANES), F32)
        xp_ref[i, pad:pad + seq, :] = src[...].astype(F32)

    hc = c // 2
    row_h = lax.broadcasted_iota(jnp.int32, (hc, hc), 0)
    col_h = lax.broadcasted_iota(jnp.int32, (hc, hc), 1)
    for bit in range(GDN_LEVELS):
        lvl_ref[bit] = (((row_h ^ col_h) >> bit) == 1).astype(BF16)
    lvl_ref[GDN_LEVELS] = (row_h == col_h).astype(BF16)
    tri_ref[0] = jnp.where(row_h > col_h, 0.0, NEG)
    tri_ref[1] = jnp.where(row_h < col_h, 0.0, NEG)
    tri_ref[2] = (row_h == col_h).astype(F32)

    lane_c = lax.broadcasted_iota(jnp.int32, (8, c), 1)
    sub_c = lax.broadcasted_iota(jnp.int32, (8, c), 0)

    def prefix(x):
        s = 1
        while s < c:
            x = x + jnp.where(lane_c >= s, pltpu.roll(x, s, 1), 0.0)
            s *= 2
        return x

    def suffix(x):
        s = 1
        while s < c:
            x = x + jnp.where(lane_c < c - s, pltpu.roll(x, c - s, 1), 0.0)
            s *= 2
        return x

    def conv(i, cw_ref, t0):
        half = (GDN_CONV - 1) // 2
        acc = None
        for j in range(GDN_CONV):
            tap = xp_ref[i, pl.ds(t0 + (pad - half + j), c), :] * cw_ref[j:j + 1, :]
            acc = tap if acc is None else acc + tap
        return _silu(acc)

    def l2n(x):
        return x * lax.rsqrt(_lane_sum(x * x) + EPS)

    half = lambda i: slice(i * hc, (i + 1) * hc)


    def prologue(ns, chains):
        st = []
        for n in ns:
            t0 = pl.multiple_of(n * c, c)
            st.append(dict(n=n, t0=t0, rows=pl.ds(t0, c)))
        for name, idx, cw_ref in (("q", 0, cwq_ref), ("k", 1, cwk_ref), ("v", 2, cwv_ref)):
            for s in st:
                s[name] = conv(idx, cw_ref, s["t0"])
            yield
        for s in st:
            s["q"], s["k"] = l2n(s["q"]), l2n(s["k"])
        yield
        for s in st:
            s["qs"] = s["q"] * scale
            s["kb"] = s["k"].astype(BF16)
            s["qk"] = _dot_nt(s["qs"].astype(BF16), s["kb"])
            bar = bar_ref[:, s["rows"]]
            g_r = (-LOG2E * jnp.exp(prr_ref[:, 0:1])) * _softplus(bar + prr_ref[:, 1:2])
            s["pre"], s["suf"] = prefix(g_r), suffix(g_r)
            s["tot"] = jnp.sum(g_r, axis=1, keepdims=True)
            packed = jnp.where(sub_c < 2, jax.nn.sigmoid(bar), jnp.where(sub_c == 2, s["pre"], s["suf"]))
            s["cols"] = jnp.concatenate([packed, jnp.zeros((LANES - 8, c), F32)], axis=0).T
            o_ref[s["rows"], :] = jnp.zeros((c, LANES), F32)
        yield
        for d in range(2):
            for s in st:
                s["kkb", d] = _dot_nt((s["k"] * s["cols"][:, d:d + 1]).astype(BF16), s["kb"])
            yield
        for d in range(2):
            first, second = (0, 1) if d == 0 else (1, 0)
            for s in st:
                gr = s["pre"][2:3, :] if d == 0 else s["suf"][3:4, :]
                gc = s["cols"][:, 2 + d:3 + d]
                blocks, attn = {}, {}
                for r, t in ((first, first), (second, second), (second, first)):
                    e = gc[half(r)] - gr[:, half(t)]
                    if r == t:
                        e = e + tri_ref[d]
                    dec = jnp.exp2(e)
                    blocks[r, t] = (s["kkb", d][half(r), half(t)] * dec).astype(BF16)
                    if r == t:
                        dec = dec + tri_ref[2]
                    attn[r, t] = (s["qk"][half(r), half(t)] * dec).astype(BF16)
                attn[first, second] = jnp.zeros((hc, hc), BF16)
                a_full = jnp.concatenate([jnp.concatenate([attn[r, 0], attn[r, 1]], axis=1) for r in range(2)], axis=0)
                diag = [blocks[0, 0], blocks[1, 1]]
                chains.append(dict(n=s["n"], rows=s["rows"], d=d, k=s["k"], v=s["v"],
                                   qs=s["qs"], beta=s["cols"][:, d:d + 1], gc=gc, tot=s["tot"][2 + d:3 + d, :],
                                   m=diag, off=blocks[second, first], a=a_full,
                                   p=[lvl_ref[GDN_LEVELS] - blk * lvl_ref[0] for blk in diag]))
            yield

    def levels(chains):
        for bit in range(1, GDN_LEVELS):
            lvl = lvl_ref[bit]
            for ch in chains:
                ch["x"] = [_dot(p, m * lvl).astype(BF16) for p, m in zip(ch["p"], ch["m"])]
            yield
            for ch in chains:
                ch["p"] = [p - _dot(x, p).astype(BF16) for p, x in zip(ch["p"], ch["x"])]
            yield

    def epilogue(chains):
        order = lambda ch: (0, 1) if ch["d"] == 0 else (1, 0)
        for ch in chains:
            ch["eg"] = jnp.exp2(ch["gc"])
            ch["rhs"] = jnp.concatenate([ch["v"] * ch["beta"], ch["k"] * (ch["beta"] * ch["eg"])], axis=1)
            ch["x1"] = _dot(ch["p"][order(ch)[0]], ch["rhs"][half(order(ch)[0])].astype(BF16))
        yield
        for ch in chains:
            ch["cross"] = _dot(ch["off"], ch["x1"].astype(BF16))
        yield
        for ch in chains:
            second = order(ch)[1]
            ch["x2"] = _dot(ch["p"][second], (ch["rhs"][half(second)] - ch["cross"]).astype(BF16))
        yield
        for ch in chains:
            xs = (ch["x1"], ch["x2"]) if ch["d"] == 0 else (ch["x2"], ch["x1"])
            ch["xb"] = jnp.concatenate(xs, axis=0).astype(BF16)
            kd = ch["k"] * jnp.exp2(ch["tot"] - ch["gc"])
            ch["kx"] = _dot(kd.T.astype(BF16), ch["xb"])
            ch["ax"] = _dot(ch["a"], ch["xb"])
        yield
        for ch in chains:
            d, rows = ch["d"], ch["rows"]
            blk = pl.ds(pl.multiple_of(ch["n"] * GDN_DK, GDN_DK), GDN_DK)
            b_ref[d, blk, :] = ch["kx"][:, :GDN_DV]
            kw_ref[d, blk, :] = ch["kx"][:, GDN_DV:].astype(BF16)
            o_ref[rows, :] = o_ref[rows, :] + ch["ax"][:, :GDN_DV]
            qd_ref[d, rows, :] = (ch["qs"] * ch["eg"] - ch["ax"][:, GDN_DV:]).astype(BF16)
            egl_ref[d, pl.ds(pl.multiple_of(ch["n"] * 8, 8), 8), :] = jnp.broadcast_to(jnp.exp2(ch["tot"]), (8, LANES))
        yield

    def interleave(*gens):
        gens = list(gens)
        while gens:
            for g in list(gens):
                if next(g, StopIteration) is StopIteration:
                    gens.remove(g)

    ngroups = nc // GDN_PREP_CHUNKS
    groups = [[g * GDN_PREP_CHUNKS + i for i in range(GDN_PREP_CHUNKS)] for g in range(ngroups)]
    chains = [[] for _ in range(ngroups)]
    interleave(prologue(groups[0], chains[0]))
    for g in range(ngroups):
        gens = [levels(chains[g])]
        if g + 1 < ngroups:
            gens.append(prologue(groups[g + 1], chains[g + 1]))
        if g >= 1:
            gens.append(epilogue(chains[g - 1]))
        interleave(*gens)
    interleave(epilogue(chains[ngroups - 1]))

    def scan(n, carry):
        states = list(carry)
        for d in range(2):
            idx = n if d == 0 else nc - 1 - n
            t0 = pl.multiple_of(idx * c, c)
            rows = pl.ds(t0, c)
            blk = pl.ds(pl.multiple_of(idx * GDN_DK, GDN_DK), GDN_DK)
            s = states[d]
            sb = s.astype(BF16)
            o_ref[rows, :] = o_ref[rows, :] + _dot(qd_ref[d, rows, :], sb)
            egl = egl_ref[d, pl.ds(pl.multiple_of(idx * 8, 8), 8), :][0:1, :]
            states[d] = s * egl - _dot(kw_ref[d, blk, :], sb) + b_ref[d, blk, :]
        return tuple(states)

    zero = jnp.zeros((GDN_DK, GDN_DV), F32)
    lax.fori_loop(0, nc, scan, (zero, zero))

    def fin(n, carry):
        rows = pl.ds(pl.multiple_of(n * c, c), c)
        y = _rms(o_ref[rows, :], gain_ref[...]) * _silu(gate_ref[rows, :])
        y_ref[rows, :] = y.astype(y_ref.dtype)
        return carry

    lax.fori_loop(0, nc, fin, 0)


def _gdn(h, bar, conv_w, prr, gain, bsz, seq):
    nh = GDN_HEADS
    blk = lambda off: pl.BlockSpec((seq, LANES), lambda b, hh, off=off: (b, off // LANES + hh))
    cw = lambda off: pl.BlockSpec((GDN_CONV, LANES), lambda b, hh, off=off: (0, off // LANES + hh))
    c = GDN_CHUNK
    assert seq % (GDN_PREP_CHUNKS * c) == 0
    return pl.pallas_call(
        _gdn_kernel,
        grid=(bsz, nh),
        in_specs=[
            blk(EV_GQ), blk(EV_GK), blk(EV_GV), blk(EV_GG),
            pl.BlockSpec((None, None, 8, seq), lambda b, hh: (b, hh, 0, 0)),
            cw(0), cw(GDN_HEADS * GDN_DK), cw(2 * GDN_HEADS * GDN_DK),
            pl.BlockSpec((None, 8, 2), lambda b, hh: (hh, 0, 0)),
            pl.BlockSpec((1, LANES), lambda b, hh: (0, 0)),
        ],
        out_specs=pl.BlockSpec((seq, LANES), lambda b, hh: (b, hh)),
        out_shape=jax.ShapeDtypeStruct((bsz * seq, nh * GDN_DV), BF16),
        scratch_shapes=[
            pltpu.VMEM((3, seq + 16, LANES), F32),
            pltpu.VMEM((2, seq, LANES), BF16),
            pltpu.VMEM((2, (seq // c) * GDN_DK, GDN_DV), BF16),
            pltpu.VMEM((2, (seq // c) * GDN_DK, GDN_DV), F32),
            pltpu.VMEM((2, (seq // c) * 8, LANES), F32),
            pltpu.VMEM((seq, LANES), F32),
            pltpu.VMEM((GDN_LEVELS + 1, c // 2, c // 2), BF16),
            pltpu.VMEM((3, c // 2, c // 2), F32),
        ],
        compiler_params=_cparams(("parallel", "parallel")),
        name="gdn",
    )(h, h, h, h, bar, conv_w, conv_w, conv_w, prr, gain.reshape(1, LANES))


def _flash_scratch(nmaps, tq, tk, dv):
    per_map = [pltpu.VMEM((tq, tk), F32), pltpu.VMEM((tq, tk), F32), pltpu.VMEM((tq, tk), BF16),
               pltpu.VMEM((tq, LANES), F32), pltpu.VMEM((tq, LANES), F32), pltpu.VMEM((tq, dv), F32)]
    return per_map * nmaps


def _flash_core(qs, k_ref, v_ref, nsteps, step_of, loop, scratch):
    nmaps = len(qs)
    maps = [scratch[6 * i:6 * i + 6] for i in range(nmaps)]
    tq, tk = maps[0][0].shape
    lo, hi = loop
    assert tk % LANES == 0 and tq % ATT_RB == 0 and 0 < lo <= hi < nsteps and (hi - lo) % 2 == 0
    nlb = tk // LANES

    def chunk(ref, t):
        j = step_of(t)[0]
        start = j * tk if isinstance(j, int) else pl.multiple_of(j * tk, tk)
        return ref[pl.ds(start, tk), :]

    def qk(i, t, slot):
        maps[i][slot][...] = _dot_nt(qs[i], chunk(k_ref, t))

    maps[nmaps - 1][5][...] = jnp.zeros(maps[nmaps - 1][5].shape, F32)

    def pv(i, t):
        acc = maps[i][5]
        out = _dot(maps[i][2][...], chunk(v_ref, t))
        acc[...] = out if i < nmaps - 1 and isinstance(t, int) and t == 0 else acc[...] + out

    def softmax(i, t, slot, first):
        s_ref, p_ref, m_ref, l_ref, acc_ref = maps[i][slot], maps[i][2], maps[i][3], maps[i][4], maps[i][5]
        _, tile, const = step_of(t)
        for rb in range(tq // ATT_RB):
            r = slice(rb * ATT_RB, (rb + 1) * ATT_RB)
            s = s_ref[r, :]
            if tile is not None:
                s = s + tile[r, :]
            blocks = [s[:, b * LANES:(b + 1) * LANES] for b in range(nlb)]
            mx = functools.reduce(jnp.maximum, blocks)
            mx = jnp.broadcast_to(jnp.max(mx, axis=-1, keepdims=True), (ATT_RB, LANES))
            if const is not None:
                mx = mx + const
            m_new = mx if first else jnp.maximum(m_ref[r, :], mx)
            shift = m_new if const is None else m_new - const
            ps = [jnp.exp2(b - shift) for b in blocks]
            row_sum = jnp.broadcast_to(jnp.sum(functools.reduce(jnp.add, ps), axis=-1, keepdims=True), m_new.shape)
            if first:
                l_ref[r, :] = row_sum
            else:
                alpha = jnp.exp2(m_ref[r, :] - m_new)
                l_ref[r, :] = alpha * l_ref[r, :] + row_sum
                acc_ref[r, :] = acc_ref[r, :] * alpha
            m_ref[r, :] = m_new
            p_ref[r, :] = jnp.concatenate(ps, axis=1).astype(BF16)

    def stage(t, slot, first=False, last=False):
        for i in range(nmaps):
            if i > 0:
                pv(i - 1, t)
            elif not first:
                pv(nmaps - 1, t - 1)
            if not last:
                qk(i, t + 1, 1 - slot)
            softmax(i, t, slot, first)

    def pair(tt, carry):
        for u in range(2):
            stage(lo + 2 * tt + u, (lo + u) % 2)
        return carry

    for i in range(nmaps):
        qk(i, 0, 0)
    for t in range(lo):
        stage(t, t % 2, first=t == 0)
    lax.fori_loop(0, (hi - lo) // 2, pair, 0)
    for t in range(hi, nsteps):
        stage(t, t % 2, last=t == nsteps - 1)
    pv(nmaps - 1, nsteps - 1)
    return [(mp[5][...], mp[4][...]) for mp in maps]


def _diff_prep_kernel(q_ref, k_ref, v_ref, qg_ref, kg_ref, qo_ref, ko_ref, vo_ref):
    r = lax.broadcasted_iota(jnp.int32, (LANES, LANES), 0) < DIFF_DQK
    cc = lax.broadcasted_iota(jnp.int32, (LANES, LANES), 1) < DIFF_DQK
    half_mean = jnp.where(r == cc, 1.0 / DIFF_DQK, 0.0).astype(BF16)

    def halfnorm(x, gain):
        x = x.astype(F32)
        return x * lax.rsqrt(_dot((x * x).astype(BF16), half_mean) + EPS) * gain

    for hh in range(DIFF_HEADS):
        sl = slice(hh * LANES, (hh + 1) * LANES)
        qo_ref[:, sl] = (halfnorm(q_ref[:, sl], qg_ref[...]) * (DIFF_DQK ** -0.5 * LOG2E)).astype(BF16)
        ko_ref[:, sl] = halfnorm(k_ref[:, sl], kg_ref[...]).astype(BF16)
    vo_ref[...] = v_ref[...].astype(BF16)


def _diff_prep(h, qg, kg, tm):
    m = h.shape[0]
    w = DIFF_HEADS * LANES
    spec = lambda off: pl.BlockSpec((tm, w), lambda i, off=off: (i, off // w))
    vec = pl.BlockSpec((1, LANES), lambda i: (0, 0))
    out = jax.ShapeDtypeStruct((m, w), BF16)
    return pl.pallas_call(
        _diff_prep_kernel,
        grid=(m // tm,),
        in_specs=[spec(EV_DQ), spec(EV_DK), spec(EV_DV), vec, vec],
        out_specs=[pl.BlockSpec((tm, w), lambda i: (i, 0))] * 3,
        out_shape=[out, out, out],
        compiler_params=_cparams(("parallel",)),
        name="diff_prep",
    )(h, h, h, jnp.tile(qg, 2).reshape(1, LANES), jnp.tile(kg, 2).reshape(1, LANES))


def _toeplitz(vec, tq, tk):
    w = vec.shape[-1]
    full = pltpu.roll(jnp.broadcast_to(vec, (tq, w)), w - (tq - 1), 1, stride=1, stride_axis=0)
    return full[:, :tk]


def _diff_flash_kernel(q_ref, k_ref, v_ref, gate_ref, bvec_ref, lam_ref, sub_ref, y_ref, bias_ref, *scratch,
                       lambda_init):
    tq, tk = bias_ref.shape[-2:]

    @pl.when(pl.program_id(2) == 0)
    def _():
        for d in range(bias_ref.shape[0]):
            bias_ref[d] = _toeplitz(bvec_ref[d:d + 1, :], tq, tk)

    lane = lax.broadcasted_iota(jnp.int32, (1, LANES), 1)
    nk = k_ref.shape[0] // tk
    near = 3
    nfar = nk - near
    c_before, c_after = bvec_ref[0:1, 0:1], bvec_ref[4:5, 0:1]
    lam = lam_ref[...]
    lam_full = (jnp.exp(jnp.sum(lam[0:1] * lam[1:2], axis=-1, keepdims=True))
                - jnp.exp(jnp.sum(lam[2:3] * lam[3:4], axis=-1, keepdims=True)) + lambda_init)

    def query_block(blk, carry):
        qi = pl.program_id(2) * ATT_QBLOCKS + blk
        rows = pl.ds(pl.multiple_of(blk * tq, tq), tq)
        q = q_ref[rows, :]
        zero = jnp.zeros_like(q)
        q0 = jnp.where(lane < DIFF_DQK, q, zero)
        q1 = jnp.where(lane < DIFF_DQK, zero, q)

        n0 = jnp.clip(qi - 1, 0, nk - near)

        def step_of(t):
            if isinstance(t, int) and t >= nfar:
                j = n0 + (t - nfar)
                return j, bias_ref.at[jnp.clip(j - qi, -2, 2) + 2], None
            j = jnp.where(t < n0, t, t + near)
            j = jnp.where(t < nfar, j, n0 + (t - nfar))
            return j, None, jnp.where(j < qi, c_before, c_after)

        loop = (1, nfar - (nfar - 1) % 2)
        (a0, l0), (a1, l1) = _flash_core([q0, q1], k_ref, v_ref, nk, step_of, loop, scratch)
        o = a0 / l0 - lam_full * (a1 / l1)
        o = _rms(o, sub_ref[...]) * (1.0 - lambda_init)
        y_ref[rows, :] = (o * _silu(gate_ref[rows, :])).astype(y_ref.dtype)
        return carry

    lax.fori_loop(0, ATT_QBLOCKS, query_block, 0)


def _diff_flash(dq, dk, dv, h, bvec, lam, subln, lambda_init, bsz, seq):
    tq, tk = ATT_TQ, ATT_TK
    rows = tq * ATT_QBLOCKS
    nq = seq // rows
    nh = DIFF_HEADS
    return pl.pallas_call(
        functools.partial(_diff_flash_kernel, lambda_init=lambda_init),
        grid=(bsz, nh, nq),
        in_specs=[
            pl.BlockSpec((rows, LANES), lambda b, hh, i: (b * nq + i, hh)),
            pl.BlockSpec((seq, LANES), lambda b, hh, i: (b, hh)),
            pl.BlockSpec((seq, LANES), lambda b, hh, i: (b, hh)),
            pl.BlockSpec((rows, LANES), lambda b, hh, i: (b * nq + i, EV_DG // LANES + hh)),
            pl.BlockSpec((None,) + bvec.shape[1:], lambda b, hh, i: (hh, 0, 0)),
            pl.BlockSpec((4, DIFF_DQK), lambda b, hh, i: (0, 0)),
            pl.BlockSpec((1, LANES), lambda b, hh, i: (0, 0)),
        ],
        out_specs=pl.BlockSpec((rows, LANES), lambda b, hh, i: (b * nq + i, hh)),
        out_shape=jax.ShapeDtypeStruct((bsz * seq, nh * DIFF_DV), BF16),
        scratch_shapes=[pltpu.VMEM((5, tq, tk), F32)] + _flash_scratch(2, tq, tk, DIFF_DV),
        compiler_params=_cparams(("parallel", "parallel", "arbitrary")),
        name="diff_flash",
    )(dq, dk, dv, h, bvec, lam, subln.reshape(1, LANES))


def _mem_attn_kernel(q_ref, gate_ref, mk_ref, mv_ref, qg_ref, kg_ref, y_ref):
    q = (_rms(q_ref[...], qg_ref[...]) * (MEM_DH ** -0.5)).astype(BF16)
    mk = _rms(mk_ref[...], kg_ref[...]).astype(BF16)
    s = _dot_nt(q, mk)
    p = jnp.exp(s - jnp.max(s, axis=-1, keepdims=True))
    o = _dot(p.astype(BF16), mv_ref[...].astype(BF16)) / jnp.sum(p, axis=-1, keepdims=True)
    y_ref[...] = (o * _silu(gate_ref[...])).astype(y_ref.dtype)


def _mem_attn(h, memkv, qg, kg, q_off, g_off, bsz, seq, tq):
    nq = seq // tq
    nh = MEM_HEADS
    vec = pl.BlockSpec((1, LANES), lambda b, hh, i: (0, 0))
    return pl.pallas_call(
        _mem_attn_kernel,
        grid=(bsz, nh, nq),
        in_specs=[
            pl.BlockSpec((tq, LANES), lambda b, hh, i: (b * nq + i, q_off // LANES + hh)),
            pl.BlockSpec((tq, LANES), lambda b, hh, i: (b * nq + i, g_off // LANES + hh)),
            pl.BlockSpec((MEM_LEN, LANES), lambda b, hh, i: (b, hh)),
            pl.BlockSpec((MEM_LEN, LANES), lambda b, hh, i: (b, nh + hh)),
            vec, vec,
        ],
        out_specs=pl.BlockSpec((tq, LANES), lambda b, hh, i: (b * nq + i, hh)),
        out_shape=jax.ShapeDtypeStruct((bsz * seq, nh * MEM_DH), BF16),
        compiler_params=_cparams(("parallel", "parallel", "parallel")),
        name="mem_attn",
    )(h, h, memkv, memkv, qg.reshape(1, LANES), kg.reshape(1, LANES))


def _swa_kernel(sink_ref, q_ref, k_ref, v_ref, gate_ref, bias_ref, qg_ref, kg_ref, y_ref, tile_ref):
    tq = q_ref.shape[0]
    seq = k_ref.shape[0]
    grp = SWA_HEADS // SWA_KV_HEADS
    kvh = pl.program_id(1)
    qi = pl.program_id(2)
    nq = pl.num_programs(2)

    @pl.when(jnp.logical_or(qi <= 1, qi == nq - 1))
    def _():
        for g in range(grp):
            tile_ref[g] = _toeplitz(bias_ref[g:g + 1, :], tq, SWA_TW)

    ws = pl.multiple_of(jnp.clip(qi * tq - WINDOW, 0, seq - SWA_TW), WINDOW)
    kw = _rms(k_ref[pl.ds(ws, SWA_TW), :], kg_ref[...]).astype(BF16)
    vw = v_ref[pl.ds(ws, SWA_TW), :].astype(BF16)
    heads = [slice(g * SWA_DH, (g + 1) * SWA_DH) for g in range(grp)]
    qs = [(_rms(q_ref[:, sl], qg_ref[...]) * (SWA_DH ** -0.5)).astype(BF16) for sl in heads]
    ss = [_dot_nt(q, kw) + tile_ref[g] for g, q in enumerate(qs)]
    ps, dens = [], []
    for g, s in enumerate(ss):
        sink = sink_ref[kvh * grp + g]
        mx = jnp.maximum(jnp.max(s, axis=-1, keepdims=True), sink)
        p = jnp.exp(s - mx)
        dens.append(jnp.sum(p, axis=-1, keepdims=True) + jnp.exp(sink - mx))
        ps.append(p.astype(BF16))
    for sl, p, den in zip(heads, ps, dens):
        y_ref[:, sl] = (_dot(p, vw) / den * _silu(gate_ref[:, sl])).astype(y_ref.dtype)


def _swa(h, bias, sink, qg, kg, bsz, seq):
    tq = SWA_TQ
    nq = seq // tq
    grp = SWA_HEADS // SWA_KV_HEADS
    gw = grp * SWA_DH
    assert seq >= SWA_TW and nq >= 2
    vec = pl.BlockSpec((1, LANES), lambda b, kv, i: (0, 0))

    def bias_idx(b, kv, i):
        return (jnp.where(i == 0, 0, jnp.where(i == nq - 1, 2, 1)), kv, 0, 0)

    return pl.pallas_call(
        _swa_kernel,
        grid=(bsz, SWA_KV_HEADS, nq),
        in_specs=[
            pl.BlockSpec(memory_space=pltpu.SMEM),
            pl.BlockSpec((tq, gw), lambda b, kv, i: (b * nq + i, OD_SQ // gw + kv)),
            pl.BlockSpec((seq, LANES), lambda b, kv, i: (b, OD_SK // LANES + kv)),
            pl.BlockSpec((seq, LANES), lambda b, kv, i: (b, OD_SV // LANES + kv)),
            pl.BlockSpec((tq, gw), lambda b, kv, i: (b * nq + i, OD_SG // gw + kv)),
            pl.BlockSpec((None, None, grp, bias.shape[-1]), bias_idx),
            vec, vec,
        ],
        out_specs=pl.BlockSpec((tq, gw), lambda b, kv, i: (b * nq + i, kv)),
        out_shape=jax.ShapeDtypeStruct((bsz * seq, SWA_HEADS * SWA_DH), BF16),
        scratch_shapes=[pltpu.VMEM((grp, tq, SWA_TW), F32)],
        compiler_params=_cparams(("parallel", "parallel", "arbitrary")),
        name="swa",
    )(sink, h, h, h, h, bias, qg.reshape(1, LANES), kg.reshape(1, LANES))


def _mla_prep_kernel(q_ref, ckv_ref, kr_ref, cos_ref, sin_ref, kvg_ref, wup_ref,
                     qgn_ref, qgr_ref, kgn_ref, kgr_ref, qo_ref, ko_ref, vo_ref):
    dqk = MLA_NOPE + MLA_ROPE
    lane = lax.broadcasted_iota(jnp.int32, (1, LANES), 1)
    lo = lane < MLA_ROPE
    first = lane < MLA_ROPE // 2
    cos = cos_ref[...]
    sin = sin_ref[...]

    def rope(t):
        rot = jnp.where(first, pltpu.roll(t, LANES - MLA_ROPE // 2, 1), pltpu.roll(t, MLA_ROPE // 2, 1))
        return t * cos + rot * sin

    ckv = _rms(ckv_ref[...], kvg_ref[...]).astype(BF16)
    kv = _dot(ckv, wup_ref[...])
    kr = kr_ref[...]
    kr2 = kr * kr
    c = dqk ** -0.5 * LOG2E
    hw = MLA_NOPE + MLA_DV
    for hh in range(MLA_HEADS):
        kn = kv[:, hh * hw:hh * hw + MLA_NOPE]
        inv = lax.rsqrt(_lane_sum(kn * kn + kr2, 1.0 / dqk) + EPS)
        ko_ref[:, hh * 2 * LANES:hh * 2 * LANES + LANES] = (kn * inv * kgn_ref[...]).astype(BF16)
        ko_ref[:, hh * 2 * LANES + LANES:(hh + 1) * 2 * LANES] = rope(kr * inv * kgr_ref[...]).astype(BF16)
        vo_ref[:, hh * LANES:(hh + 1) * LANES] = kv[:, hh * hw + MLA_NOPE:(hh + 1) * hw].astype(BF16)

        qn = q_ref[:, hh * LANES:(hh + 1) * LANES].astype(F32)
        pair = q_ref[:, OD_MLQ_ROPE + (hh // 2) * LANES:OD_MLQ_ROPE + (hh // 2 + 1) * LANES].astype(F32)
        if hh % 2 == 1:
            pair = pltpu.roll(pair, MLA_ROPE, 1)
        qr = jnp.where(lo, pair, 0.0)
        inv = lax.rsqrt(_lane_sum(qn * qn + qr * qr, 1.0 / dqk) + EPS)
        qo_ref[:, hh * 2 * LANES:hh * 2 * LANES + LANES] = (qn * inv * qgn_ref[...] * c).astype(BF16)
        qo_ref[:, hh * 2 * LANES + LANES:(hh + 1) * 2 * LANES] = (rope(qr * inv * qgr_ref[...]) * c).astype(BF16)


def _mla_prep(h, side, cos, sin, kv_gain, w_up, qg, kg, tm):
    m = h.shape[0]
    nh = MLA_HEADS
    qw = nh * (MLA_NOPE + MLA_ROPE)
    pad = lambda g: jnp.pad(g[MLA_NOPE:], (0, LANES - MLA_ROPE)).reshape(1, LANES)
    vec = pl.BlockSpec((1, LANES), lambda i: (0, 0))
    row = lambda i: (i, 0)
    return pl.pallas_call(
        _mla_prep_kernel,
        grid=(m // tm,),
        in_specs=[
            pl.BlockSpec((tm, qw), row),
            pl.BlockSpec((tm, MLA_KV_RANK), lambda i: (i, OD_CKV // MLA_KV_RANK)),
            pl.BlockSpec((tm, LANES), row),
            pl.BlockSpec((tm, LANES), row),
            pl.BlockSpec((tm, LANES), row),
            pl.BlockSpec((1, MLA_KV_RANK), lambda i: (0, 0)),
            pl.BlockSpec(w_up.shape, lambda i: (0, 0)),
            vec, vec, vec, vec,
        ],
        out_specs=[
            pl.BlockSpec((tm, nh * 2 * LANES), row),
            pl.BlockSpec((tm, nh * 2 * LANES), row),
            pl.BlockSpec((tm, nh * MLA_DV), row),
        ],
        out_shape=[
            jax.ShapeDtypeStruct((m, nh * 2 * LANES), BF16),
            jax.ShapeDtypeStruct((m, nh * 2 * LANES), BF16),
            jax.ShapeDtypeStruct((m, nh * MLA_DV), BF16),
        ],
        compiler_params=_cparams(("parallel",)),
        name="mla_prep",
    )(h, h, side, cos, sin, kv_gain.reshape(1, MLA_KV_RANK), w_up.astype(BF16),
      qg[:MLA_NOPE].reshape(1, LANES), pad(qg), kg[:MLA_NOPE].reshape(1, LANES), pad(kg))


def _mla_flash_kernel(q_ref, k_ref, v_ref, gate_ref, y_ref, *scratch):
    nk = k_ref.shape[0] // ATT_TK
    tq = ATT_TQ

    def query_block(blk, carry):
        rows = pl.ds(pl.multiple_of(blk * tq, tq), tq)
        (acc, l), = _flash_core([q_ref[rows, :]], k_ref, v_ref, nk, lambda t: (t, None, None),
                                (1, nk - 1 - nk % 2), scratch)
        y_ref[rows, :] = (acc / l * _silu(gate_ref[rows, :])).astype(y_ref.dtype)
        return carry

    lax.fori_loop(0, ATT_QBLOCKS, query_block, 0)


def _mla_flash(qm, km, vm, h, bsz, seq):
    tq = ATT_TQ
    rows = tq * ATT_QBLOCKS
    nq = seq // rows
    nh = MLA_HEADS
    return pl.pallas_call(
        _mla_flash_kernel,
        grid=(bsz, nh, nq),
        in_specs=[
            pl.BlockSpec((rows, 2 * LANES), lambda b, hh, i: (b * nq + i, hh)),
            pl.BlockSpec((seq, 2 * LANES), lambda b, hh, i: (b, hh)),
            pl.BlockSpec((seq, LANES), lambda b, hh, i: (b, hh)),
            pl.BlockSpec((rows, LANES), lambda b, hh, i: (b * nq + i, OD_MLG // LANES + hh)),
        ],
        out_specs=pl.BlockSpec((rows, LANES), lambda b, hh, i: (b * nq + i, hh)),
        out_shape=jax.ShapeDtypeStruct((bsz * seq, nh * MLA_DV), BF16),
        scratch_shapes=_flash_scratch(1, tq, ATT_TK, MLA_DV),
        compiler_params=_cparams(("parallel", "parallel", "arbitrary")),
        name="mla_flash",
    )(qm, km, vm, h)


def _t5_bucket(rel):
    half = REL_BUCKETS // 2
    max_exact = half // 2
    ret = (rel > 0).astype(jnp.int32) * half
    n = jnp.abs(rel)
    nf = jnp.maximum(n, 1).astype(F32)
    large = max_exact + (jnp.log(nf / max_exact) / math.log(REL_MAX_DIST / max_exact)
                         * (half - max_exact)).astype(jnp.int32)
    large = jnp.minimum(large, half - 1)
    return ret + jnp.where(n < max_exact, n, large)


def _round_up(n, m):
    return (n + m - 1) // m * m


def _diff_bias_vecs(rel_table, tq, tk):
    assert tq == tk and tk + 1 >= REL_MAX_DIST
    m = jnp.arange(_round_up(tq + tk - 1, LANES))
    rows = [rel_table[_t5_bucket(m - (tq - 1) + d * tk)] for d in (-2, -1, 0, 1, 2)]
    return jnp.pad(jnp.stack(rows, axis=0).transpose(2, 0, 1) * LOG2E, ((0, 0), (0, 3), (0, 0))).astype(F32)


def _swa_bias_vecs(rel_table):
    m = jnp.arange(_round_up(SWA_TQ + SWA_TW - 1, LANES))
    rows = []
    for shift in (0, WINDOW, SWA_TW - SWA_TQ):
        rel = m - (SWA_TQ - 1) - shift
        rows.append(jnp.where((jnp.abs(rel) <= WINDOW)[None], rel_table[_t5_bucket(rel)].T, NEG))
    return jnp.stack(rows, axis=0).reshape(3, SWA_KV_HEADS, SWA_HEADS // SWA_KV_HEADS, -1).astype(F32)


def _od_main_weight(wt):
    mlq_lo = 2 * SWA_HEADS * SWA_DH + 2 * SWA_KV_HEADS * SWA_DH
    mlq_hi = mlq_lo + MLA_HEADS * (MLA_NOPE + MLA_ROPE)
    mlq = wt[mlq_lo:mlq_hi].reshape(MLA_HEADS, MLA_NOPE + MLA_ROPE, -1)
    nope = mlq[:, :MLA_NOPE].reshape(MLA_HEADS * MLA_NOPE, -1)
    rope = mlq[:, MLA_NOPE:].reshape(MLA_HEADS * MLA_ROPE, -1)
    return jnp.concatenate([nope, rope, wt[:mlq_lo], wt[mlq_hi:OD_KR_LO], wt[OD_KR_HI:]], axis=0).astype(BF16)


def _rope_tables(positions):
    half = MLA_ROPE // 2
    inv_freq = ROPE_BASE ** (-jnp.arange(half, dtype=F32) / half)
    ang = positions.astype(F32).reshape(-1, 1) * inv_freq
    cos, sin = jnp.cos(ang), jnp.sin(ang)
    z = jnp.zeros((ang.shape[0], LANES - MLA_ROPE), F32)
    return jnp.concatenate([cos, cos, z], axis=1), jnp.concatenate([-sin, sin, z], axis=1)


def _side_weight(wt, lo, hi):
    return jnp.pad(wt[lo:hi], ((0, LANES - (hi - lo)), (0, 0))).astype(BF16)


def _mem_kv(mem2, gain, w_kv):
    zero_side = jnp.zeros((LANES, D_MODEL), BF16)
    memkv, _ = _norm_proj(mem2, gain, jnp.swapaxes(w_kv, 0, 1).astype(BF16), zero_side, tm=mem2.shape[0] // 2,
                          tn=512, out_dtype=F32)
    return memkv


def _even_layer(x2, mem2, rel_bias, norm_g, w_in, conv_w, a_log, dt_bias, gdn_gain, dq_gain, dk_gain, lam, subln,
                mem_norm, mem_w_kv, mem_qn, mem_kn, w_out, lambda_init, bsz, seq):
    wt = jnp.swapaxes(w_in, 0, 1)
    w_main = jnp.concatenate([wt[:EV_SIDE_LO], wt[EV_SIDE_HI:]], axis=0).astype(BF16)
    h, side = _norm_proj(x2, norm_g, w_main, _side_weight(wt, EV_SIDE_LO, EV_SIDE_HI), tm=1024, tn=EV_MAIN // 6,
                         out_dtype=BF16)

    nh = GDN_HEADS
    bar = side[:, :4 * nh].reshape(bsz, seq, 4, nh).transpose(0, 3, 2, 1)
    bar = jnp.pad(bar, ((0, 0), (0, 0), (0, 4), (0, 0)))
    prr = jnp.pad(jnp.stack([a_log.T, dt_bias.T], axis=-1), ((0, 0), (2, 4), (0, 0)))
    ya = _gdn(h, bar, conv_w, prr, gdn_gain, bsz, seq)

    dq, dk, dv = _diff_prep(h, dq_gain, dk_gain, tm=1024)
    bias = _diff_bias_vecs(rel_bias, ATT_TQ, ATT_TK)
    yb = _diff_flash(dq, dk, dv, h, bias, lam, subln, lambda_init, bsz, seq)

    memkv = _mem_kv(mem2, mem_norm, mem_w_kv)
    ym = _mem_attn(h, memkv, mem_qn, mem_kn, EV_MQ, EV_MG, bsz, seq, tq=seq)
    return _out_proj(x2, ya, yb, ym, w_out, tm=512)


def _odd_layer(x2, mem2, positions, rel_bias, norm_g, w_in, swa_qn, swa_kn, sink, kv_norm, w_kv_up, mla_qn, mla_kn,
               mem_norm, mem_w_kv, mem_qn, mem_kn, w_out, bsz, seq):
    wt = jnp.swapaxes(w_in, 0, 1)
    h, side = _norm_proj(x2, norm_g, _od_main_weight(wt), _side_weight(wt, OD_KR_LO, OD_KR_HI), tm=1024,
                         tn=OD_MAIN // 4, out_dtype=BF16)

    ya = _swa(h, _swa_bias_vecs(rel_bias), sink, swa_qn, swa_kn, bsz, seq)

    cos, sin = _rope_tables(positions)
    qm, km, vm = _mla_prep(h, side, cos, sin, kv_norm, w_kv_up, mla_qn, mla_kn, tm=512)
    yb = _mla_flash(qm, km, vm, h, bsz, seq)

    memkv = _mem_kv(mem2, mem_norm, mem_w_kv)
    ym = _mem_attn(h, memkv, mem_qn, mem_kn, OD_MQ, OD_MG, bsz, seq, tq=seq)
    return _out_proj(x2, ya, yb, ym, w_out, tm=512)


def kernel(x, mem, positions, rel_bias, ev_norm, ev_w_in, ev_conv, ev_a_log, ev_dt_bias, ev_gdn_norm, ev_diff_qnorm, ev_diff_knorm, ev_diff_lambda, ev_diff_subln, ev_mem_norm, ev_mem_w_kv, ev_mem_qnorm, ev_mem_knorm, ev_w_out, od_norm, od_w_in, od_swa_qnorm, od_swa_knorm, od_swa_sink, od_mla_kv_norm, od_mla_w_kv_up, od_mla_qnorm, od_mla_knorm, od_mem_norm, od_mem_w_kv, od_mem_qnorm, od_mem_knorm, od_w_out):
    bsz, seq, d = x.shape
    depth = ev_norm.shape[0] + od_norm.shape[0]
    x2 = x.reshape(bsz * seq, d)
    mem2 = mem.reshape(bsz * mem.shape[1], d)
    for layer in range(depth):
        i = layer // 2
        if layer % 2 == 0:
            lambda_init = 0.8 - 0.6 * math.exp(-0.3 * layer)
            x2 = _even_layer(x2, mem2, rel_bias, ev_norm[i], ev_w_in[i], ev_conv[i], ev_a_log[i], ev_dt_bias[i],
                             ev_gdn_norm[i], ev_diff_qnorm[i], ev_diff_knorm[i], ev_diff_lambda[i], ev_diff_subln[i],
                             ev_mem_norm[i], ev_mem_w_kv[i], ev_mem_qnorm[i], ev_mem_knorm[i], ev_w_out[i],
                             lambda_init, bsz, seq)
        else:
            x2 = _odd_layer(x2, mem2, positions, rel_bias, od_norm[i], od_w_in[i], od_swa_qnorm[i], od_swa_knorm[i],
                            od_swa_sink[i], od_mla_kv_norm[i], od_mla_w_kv_up[i], od_mla_qnorm[i], od_mla_knorm[i],
                            od_mem_norm[i], od_mem_w_kv[i], od_mem_qnorm[i], od_mem_knorm[i], od_w_out[i], bsz, seq)
    return x2.reshape(bsz, seq, d)
```

```python
import functools
import math

import jax
import jax.numpy as jnp
from jax import lax
from jax.experimental import pallas as pl
from jax.experimental.pallas import tpu as pltpu

F32 = jnp.float32
BF16 = jnp.bfloat16
EPS = 1e-6
NEG = -1e30

V7X_VMEM_BYTES = 64 * 1024 * 1024
VMEM_LIMIT = V7X_VMEM_BYTES - 8 * 1024 * 1024
LANES = 128

D_MODEL = 2048
MEM_LEN = 256
GDN_HEADS, GDN_DK, GDN_DV, GDN_CONV = 8, 128, 128, 5
DIFF_HEADS, DIFF_DQK, DIFF_DV = 8, 64, 128
SWA_HEADS, SWA_KV_HEADS, SWA_DH, WINDOW = 8, 2, 128, 128
MLA_HEADS, MLA_NOPE, MLA_ROPE, MLA_DV, MLA_KV_RANK = 8, 128, 64, 128, 512
ROPE_BASE = 10000.0
MEM_HEADS, MEM_DH = 4, 128
REL_BUCKETS, REL_MAX_DIST = 32, 128

GDN_CHUNK = 256
GDN_LEVELS = (GDN_CHUNK // 2).bit_length() - 1
GDN_PREP_CHUNKS = 2
ATT_TQ = 512
ATT_TK = 512
ATT_RB = 32
LOG2E = math.log2(math.e)
SWA_TQ = 256
SWA_TW = SWA_TQ + 2 * WINDOW

EV_GQ, EV_GK, EV_GV, EV_GG = 0, 1024, 2048, 3072
EV_DQ, EV_DK, EV_DV, EV_DG = 4096, 5120, 6144, 7168
EV_MQ, EV_MG = 8192, 8704
EV_MAIN = 9216
EV_SIDE_LO, EV_SIDE_HI = 3072, 3104

OD_MLQ_NOPE, OD_MLQ_ROPE = 0, 1024
OD_SQ, OD_SK, OD_SV, OD_SG = 1536, 2560, 2816, 3072
OD_CKV, OD_MLG, OD_MQ, OD_MG = 4096, 4608, 5632, 6144
OD_MAIN = 6656
OD_KR_LO, OD_KR_HI = 4608, 4672


def _cparams(sem):
    return pltpu.CompilerParams(dimension_semantics=sem, vmem_limit_bytes=VMEM_LIMIT)


def _dot(a, b):
    return jnp.dot(a, b, preferred_element_type=F32)


def _dot_nt(a, b):
    return lax.dot_general(a, b, (((1,), (1,)), ((), ())), preferred_element_type=F32)


def _silu(x):
    x = x.astype(F32)
    return x * jax.nn.sigmoid(x)


def _softplus(x):
    return jnp.maximum(x, 0.0) + jnp.log(1.0 + jnp.exp(-jnp.abs(x)))


def _lane_sum(x, scale=1.0):
    return _dot(x.astype(BF16), jnp.ones((LANES, LANES), BF16)) * scale


def _rms(x, gain):
    x = x.astype(F32)
    if x.shape[-1] == LANES:
        ms = _lane_sum(x * x, 1.0 / LANES)
    else:
        ms = jnp.mean(x * x, axis=-1, keepdims=True)
    return x * lax.rsqrt(ms + EPS) * gain


def _norm_proj_kernel(x_ref, g_ref, w_ref, ws_ref, o_ref, os_ref, xn_ref):
    @pl.when(pl.program_id(1) == 0)
    def _():
        xn = _rms(x_ref[...], g_ref[...]).astype(BF16)
        xn_ref[...] = xn
        os_ref[...] = _dot_nt(xn, ws_ref[...])

    o_ref[...] = _dot_nt(xn_ref[...], w_ref[...]).astype(o_ref.dtype)


def _norm_proj(x, gain, w_main, w_side, tm, tn, out_dtype):
    m, k = x.shape
    n = w_main.shape[0]
    ns = w_side.shape[0]
    assert m % tm == 0 and n % tn == 0
    return pl.pallas_call(
        _norm_proj_kernel,
        grid=(m // tm, n // tn),
        in_specs=[
            pl.BlockSpec((tm, k), lambda i, j: (i, 0)),
            pl.BlockSpec((1, k), lambda i, j: (0, 0)),
            pl.BlockSpec((tn, k), lambda i, j: (j, 0)),
            pl.BlockSpec((ns, k), lambda i, j: (0, 0)),
        ],
        out_specs=[
            pl.BlockSpec((tm, tn), lambda i, j: (i, j)),
            pl.BlockSpec((tm, ns), lambda i, j: (i, 0)),
        ],
        out_shape=[jax.ShapeDtypeStruct((m, n), out_dtype), jax.ShapeDtypeStruct((m, ns), F32)],
        scratch_shapes=[pltpu.VMEM((tm, k), BF16)],
        compiler_params=_cparams(("parallel", "arbitrary")),
        name="norm_proj",
    )(x, gain.reshape(1, k), w_main, w_side)


def _out_proj_kernel(x_ref, ya_ref, yb_ref, ym_ref, wa_ref, wb_ref, wm_ref, o_ref):
    acc = _dot(ya_ref[...], wa_ref[...])
    acc = acc + _dot(yb_ref[...], wb_ref[...])
    acc = acc + _dot(ym_ref[...], wm_ref[...])
    o_ref[...] = x_ref[...] + acc


def _out_proj(x, ya, yb, ym, w_out, tm):
    m, d = x.shape
    na, nb, nm = ya.shape[1], yb.shape[1], ym.shape[1]
    wa = w_out[:na].astype(BF16)
    wb = w_out[na:na + nb].astype(BF16)
    wm = w_out[na + nb:].astype(BF16)
    row = lambda i: (i, 0)
    fixed = lambda i: (0, 0)
    return pl.pallas_call(
        _out_proj_kernel,
        grid=(m // tm,),
        in_specs=[
            pl.BlockSpec((tm, d), row), pl.BlockSpec((tm, na), row), pl.BlockSpec((tm, nb), row),
            pl.BlockSpec((tm, nm), row),
            pl.BlockSpec((na, d), fixed), pl.BlockSpec((nb, d), fixed), pl.BlockSpec((nm, d), fixed),
        ],
        out_specs=pl.BlockSpec((tm, d), row),
        out_shape=jax.ShapeDtypeStruct((m, d), F32),
        compiler_params=_cparams(("parallel",)),
        name="out_proj",
    )(x, ya, yb, ym, wa, wb, wm)


def _gdn_kernel(q_ref, k_ref, v_ref, gate_ref, bar_ref, cwq_ref, cwk_ref, cwv_ref, prr_ref, gain_ref, y_ref,
                xp_ref, qd_ref, kw_ref, b_ref, egl_ref, o_ref, lvl_ref, tri_ref):
    seq = q_ref.shape[0]
    c = GDN_CHUNK
    nc = seq // c
    pad = 8
    scale = GDN_DK ** -0.5

    for i, src in enumerate((q_ref, k_ref, v_ref)):
        xp_ref[i, 0:pad, :] = jnp.zeros((pad, LANES), F32)
        xp_ref[i, pad + seq:2 * pad + seq, :] = jnp.zeros((pad, LANES), F32)
        xp_ref[i, pad:pad + seq, :] = src[...].astype(F32)

    hc = c // 2
    row_h = lax.broadcasted_iota(jnp.int32, (hc, hc), 0)
    col_h = lax.broadcasted_iota(jnp.int32, (hc, hc), 1)
    for bit in range(GDN_LEVELS):
        lvl_ref[bit] = (((row_h ^ col_h) >> bit) == 1).astype(BF16)
    lvl_ref[GDN_LEVELS] = (row_h == col_h).astype(BF16)
    tri_ref[0] = jnp.where(row_h > col_h, 0.0, NEG)
    tri_ref[1] = jnp.where(row_h < col_h, 0.0, NEG)
    tri_ref[2] = (row_h == col_h).astype(F32)

    lane_c = lax.broadcasted_iota(jnp.int32, (8, c), 1)
    sub_c = lax.broadcasted_iota(jnp.int32, (8, c), 0)

    def prefix(x):
        s = 1
        while s < c:
            x = x + jnp.where(lane_c >= s, pltpu.roll(x, s, 1), 0.0)
            s *= 2
        return x

    def suffix(x):
        s = 1
        while s < c:
            x = x + jnp.where(lane_c < c - s, pltpu.roll(x, c - s, 1), 0.0)
            s *= 2
        return x

    def conv(i, cw_ref, t0):
        half = (GDN_CONV - 1) // 2
        acc = None
        for j in range(GDN_CONV):
            tap = xp_ref[i, pl.ds(t0 + (pad - half + j), c), :] * cw_ref[j:j + 1, :]
            acc = tap if acc is None else acc + tap
        return _silu(acc)

    def l2n(x):
        return x * lax.rsqrt(_lane_sum(x * x) + EPS)

    half = lambda i: slice(i * hc, (i + 1) * hc)


    def prologue(ns, chains):
        st = []
        for n in ns:
            t0 = pl.multiple_of(n * c, c)
            st.append(dict(n=n, t0=t0, rows=pl.ds(t0, c)))
        for name, idx, cw_ref in (("q", 0, cwq_ref), ("k", 1, cwk_ref), ("v", 2, cwv_ref)):
            for s in st:
                s[name] = conv(idx, cw_ref, s["t0"])
            yield
        for s in st:
            s["q"], s["k"] = l2n(s["q"]), l2n(s["k"])
        yield
        for s in st:
            s["qs"] = s["q"] * scale
            s["kb"] = s["k"].astype(BF16)
            s["qk"] = _dot_nt(s["qs"].astype(BF16), s["kb"])
            bar = bar_ref[:, s["rows"]]
            g_r = (-LOG2E * jnp.exp(prr_ref[:, 0:1])) * _softplus(bar + prr_ref[:, 1:2])
            s["pre"], s["suf"] = prefix(g_r), suffix(g_r)
            s["tot"] = jnp.sum(g_r, axis=1, keepdims=True)
            packed = jnp.where(sub_c < 2, jax.nn.sigmoid(bar), jnp.where(sub_c == 2, s["pre"], s["suf"]))
            s["cols"] = jnp.concatenate([packed, jnp.zeros((LANES - 8, c), F32)], axis=0).T
            o_ref[s["rows"], :] = jnp.zeros((c, LANES), F32)
        yield
        for d in range(2):
            for s in st:
                s["kkb", d] = _dot_nt((s["k"] * s["cols"][:, d:d + 1]).astype(BF16), s["kb"])
            yield
        for d in range(2):
            first, second = (0, 1) if d == 0 else (1, 0)
            for s in st:
                gr = s["pre"][2:3, :] if d == 0 else s["suf"][3:4, :]
                gc = s["cols"][:, 2 + d:3 + d]
                blocks, attn = {}, {}
                for r, t in ((first, first), (second, second), (second, first)):
                    e = gc[half(r)] - gr[:, half(t)]
                    if r == t:
                        e = e + tri_ref[d]
                    dec = jnp.exp2(e)
                    blocks[r, t] = (s["kkb", d][half(r), half(t)] * dec).astype(BF16)
                    if r == t:
                        dec = dec + tri_ref[2]
                    attn[r, t] = (s["qk"][half(r), half(t)] * dec).astype(BF16)
                attn[first, second] = jnp.zeros((hc, hc), BF16)
                a_full = jnp.concatenate([jnp.concatenate([attn[r, 0], attn[r, 1]], axis=1) for r in range(2)], axis=0)
                diag = [blocks[0, 0], blocks[1, 1]]
                chains.append(dict(n=s["n"], rows=s["rows"], d=d, k=s["k"], v=s["v"],
                                   qs=s["qs"], beta=s["cols"][:, d:d + 1], gc=gc, tot=s["tot"][2 + d:3 + d, :],
                                   m=diag, off=blocks[second, first], a=a_full,
                                   p=[lvl_ref[GDN_LEVELS] - blk * lvl_ref[0] for blk in diag]))
            yield

    def levels(chains):
        for bit in range(1, GDN_LEVELS):
            lvl = lvl_ref[bit]
            for ch in chains:
                ch["x"] = [_dot(p, m * lvl).astype(BF16) for p, m in zip(ch["p"], ch["m"])]
            yield
            for ch in chains:
                ch["p"] = [p - _dot(x, p).astype(BF16) for p, x in zip(ch["p"], ch["x"])]
            yield

    def epilogue(chains):
        order = lambda ch: (0, 1) if ch["d"] == 0 else (1, 0)
        for ch in chains:
            ch["eg"] = jnp.exp2(ch["gc"])
            ch["rhs"] = jnp.concatenate([ch["v"] * ch["beta"], ch["k"] * (ch["beta"] * ch["eg"])], axis=1)
            ch["x1"] = _dot(ch["p"][order(ch)[0]], ch["rhs"][half(order(ch)[0])].astype(BF16))
        yield
        for ch in chains:
            ch["cross"] = _dot(ch["off"], ch["x1"].astype(BF16))
        yield
        for ch in chains:
            second = order(ch)[1]
            ch["x2"] = _dot(ch["p"][second], (ch["rhs"][half(second)] - ch["cross"]).astype(BF16))
        yield
        for ch in chains:
            xs = (ch["x1"], ch["x2"]) if ch["d"] == 0 else (ch["x2"], ch["x1"])
            ch["xb"] = jnp.concatenate(xs, axis=0).astype(BF16)
            kd = ch["k"] * jnp.exp2(ch["tot"] - ch["gc"])
            ch["kx"] = _dot(kd.T.astype(BF16), ch["xb"])
            ch["ax"] = _dot(ch["a"], ch["xb"])
        yield
        for ch in chains:
            d, rows = ch["d"], ch["rows"]
            blk = pl.ds(pl.multiple_of(ch["n"] * GDN_DK, GDN_DK), GDN_DK)
            b_ref[d, blk, :] = ch["kx"][:, :GDN_DV]
            kw_ref[d, blk, :] = ch["kx"][:, GDN_DV:].astype(BF16)
            o_ref[rows, :] = o_ref[rows, :] + ch["ax"][:, :GDN_DV]
            qd_ref[d, rows, :] = (ch["qs"] * ch["eg"] - ch["ax"][:, GDN_DV:]).astype(BF16)
            egl_ref[d, pl.ds(pl.multiple_of(ch["n"] * 8, 8), 8), :] = jnp.broadcast_to(jnp.exp2(ch["tot"]), (8, LANES))
        yield

    def interleave(*gens):
        gens = list(gens)
        while gens:
            for g in list(gens):
                if next(g, StopIteration) is StopIteration:
                    gens.remove(g)

    ngroups = nc // GDN_PREP_CHUNKS
    groups = [[g * GDN_PREP_CHUNKS + i for i in range(GDN_PREP_CHUNKS)] for g in range(ngroups)]
    chains = [[] for _ in range(ngroups)]
    interleave(prologue(groups[0], chains[0]))
    for g in range(ngroups):
        gens = [levels(chains[g])]
        if g + 1 < ngroups:
            gens.append(prologue(groups[g + 1], chains[g + 1]))
        if g >= 1:
            gens.append(epilogue(chains[g - 1]))
        interleave(*gens)
    interleave(epilogue(chains[ngroups - 1]))

    def scan(n, carry):
        states = list(carry)
        for d in range(2):
            idx = n if d == 0 else nc - 1 - n
            t0 = pl.multiple_of(idx * c, c)
            rows = pl.ds(t0, c)
            blk = pl.ds(pl.multiple_of(idx * GDN_DK, GDN_DK), GDN_DK)
            s = states[d]
            sb = s.astype(BF16)
            o_ref[rows, :] = o_ref[rows, :] + _dot(qd_ref[d, rows, :], sb)
            egl = egl_ref[d, pl.ds(pl.multiple_of(idx * 8, 8), 8), :][0:1, :]
            states[d] = s * egl - _dot(kw_ref[d, blk, :], sb) + b_ref[d, blk, :]
        return tuple(states)

    zero = jnp.zeros((GDN_DK, GDN_DV), F32)
    lax.fori_loop(0, nc, scan, (zero, zero))

    def fin(n, carry):
        rows = pl.ds(pl.multiple_of(n * c, c), c)
        y = _rms(o_ref[rows, :], gain_ref[...]) * _silu(gate_ref[rows, :])
        y_ref[rows, :] = y.astype(y_ref.dtype)
        return carry

    lax.fori_loop(0, nc, fin, 0)


def _gdn(h, bar, conv_w, prr, gain, bsz, seq):
    nh = GDN_HEADS
    blk = lambda off: pl.BlockSpec((seq, LANES), lambda b, hh, off=off: (b, off // LANES + hh))
    cw = lambda off: pl.BlockSpec((GDN_CONV, LANES), lambda b, hh, off=off: (0, off // LANES + hh))
    c = GDN_CHUNK
    assert seq % (GDN_PREP_CHUNKS * c) == 0
    return pl.pallas_call(
        _gdn_kernel,
        grid=(bsz, nh),
        in_specs=[
            blk(EV_GQ), blk(EV_GK), blk(EV_GV), blk(EV_GG),
            pl.BlockSpec((None, None, 8, seq), lambda b, hh: (b, hh, 0, 0)),
            cw(0), cw(GDN_HEADS * GDN_DK), cw(2 * GDN_HEADS * GDN_DK),
            pl.BlockSpec((None, 8, 2), lambda b, hh: (hh, 0, 0)),
            pl.BlockSpec((1, LANES), lambda b, hh: (0, 0)),
        ],
        out_specs=pl.BlockSpec((seq, LANES), lambda b, hh: (b, hh)),
        out_shape=jax.ShapeDtypeStruct((bsz * seq, nh * GDN_DV), BF16),
        scratch_shapes=[
            pltpu.VMEM((3, seq + 16, LANES), F32),
            pltpu.VMEM((2, seq, LANES), BF16),
            pltpu.VMEM((2, (seq // c) * GDN_DK, GDN_DV), BF16),
            pltpu.VMEM((2, (seq // c) * GDN_DK, GDN_DV), F32),
            pltpu.VMEM((2, (seq // c) * 8, LANES), F32),
            pltpu.VMEM((seq, LANES), F32),
            pltpu.VMEM((GDN_LEVELS + 1, c // 2, c // 2), BF16),
            pltpu.VMEM((3, c // 2, c // 2), F32),
        ],
        compiler_params=_cparams(("parallel", "parallel")),
        name="gdn",
    )(h, h, h, h, bar, conv_w, conv_w, conv_w, prr, gain.reshape(1, LANES))


def _flash_scratch(nmaps, tq, tk, dv):
    per_map = [pltpu.VMEM((tq, tk), F32), pltpu.VMEM((tq, tk), F32), pltpu.VMEM((tq, tk), BF16),
               pltpu.VMEM((tq, LANES), F32), pltpu.VMEM((tq, LANES), F32), pltpu.VMEM((tq, dv), F32)]
    return per_map * nmaps


def _flash_core(qs, k_ref, v_ref, nsteps, step_of, loop, scratch):
    nmaps = len(qs)
    maps = [scratch[6 * i:6 * i + 6] for i in range(nmaps)]
    tq, tk = maps[0][0].shape
    lo, hi = loop
    assert tk % LANES == 0 and tq % ATT_RB == 0 and 0 < lo <= hi < nsteps and (hi - lo) % 2 == 0
    nlb = tk // LANES

    def chunk(ref, t):
        j = step_of(t)[0]
        start = j * tk if isinstance(j, int) else pl.multiple_of(j * tk, tk)
        return ref[pl.ds(start, tk), :]

    def qk(i, t, slot):
        maps[i][slot][...] = _dot_nt(qs[i], chunk(k_ref, t))

    maps[nmaps - 1][5][...] = jnp.zeros(maps[nmaps - 1][5].shape, F32)

    def pv(i, t):
        acc = maps[i][5]
        out = _dot(maps[i][2][...], chunk(v_ref, t))
        acc[...] = out if i < nmaps - 1 and isinstance(t, int) and t == 0 else acc[...] + out

    def softmax(i, t, slot, first):
        s_ref, p_ref, m_ref, l_ref, acc_ref = maps[i][slot], maps[i][2], maps[i][3], maps[i][4], maps[i][5]
        _, tile, const = step_of(t)
        for rb in range(tq // ATT_RB):
            r = slice(rb * ATT_RB, (rb + 1) * ATT_RB)
            s = s_ref[r, :]
            if tile is not None:
                s = s + tile[r, :]
            blocks = [s[:, b * LANES:(b + 1) * LANES] for b in range(nlb)]
            mx = functools.reduce(jnp.maximum, blocks)
            mx = jnp.broadcast_to(jnp.max(mx, axis=-1, keepdims=True), (ATT_RB, LANES))
            if const is not None:
                mx = mx + const
            m_new = mx if first else jnp.maximum(m_ref[r, :], mx)
            shift = m_new if const is None else m_new - const
            ps = [jnp.exp2(b - shift) for b in blocks]
            row_sum = jnp.broadcast_to(jnp.sum(functools.reduce(jnp.add, ps), axis=-1, keepdims=True), m_new.shape)
            if first:
                l_ref[r, :] = row_sum
            else:
                alpha = jnp.exp2(m_ref[r, :] - m_new)
                l_ref[r, :] = alpha * l_ref[r, :] + row_sum
                acc_ref[r, :] = acc_ref[r, :] * alpha
            m_ref[r, :] = m_new
            p_ref[r, :] = jnp.concatenate(ps, axis=1).astype(BF16)

    def stage(t, slot, first=False, last=False):
        for i in range(nmaps):
            if i > 0:
                pv(i - 1, t)
            elif not first:
                pv(nmaps - 1, t - 1)
            if not last:
                qk(i, t + 1, 1 - slot)
            softmax(i, t, slot, first)

    def pair(tt, carry):
        for u in range(2):
            stage(lo + 2 * tt + u, (lo + u) % 2)
        return carry

    for i in range(nmaps):
        qk(i, 0, 0)
    for t in range(lo):
        stage(t, t % 2, first=t == 0)
    lax.fori_loop(0, (hi - lo) // 2, pair, 0)
    for t in range(hi, nsteps):
        stage(t, t % 2, last=t == nsteps - 1)
    pv(nmaps - 1, nsteps - 1)
    return [(mp[5][...], mp[4][...]) for mp in maps]


def _diff_prep_kernel(q_ref, k_ref, v_ref, qg_ref, kg_ref, qo_ref, ko_ref, vo_ref):
    r = lax.broadcasted_iota(jnp.int32, (LANES, LANES), 0) < DIFF_DQK
    cc = lax.broadcasted_iota(jnp.int32, (LANES, LANES), 1) < DIFF_DQK
    half_mean = jnp.where(r == cc, 1.0 / DIFF_DQK, 0.0).astype(BF16)

    def halfnorm(x, gain):
        x = x.astype(F32)
        return x * lax.rsqrt(_dot((x * x).astype(BF16), half_mean) + EPS) * gain

    for hh in range(DIFF_HEADS):
        sl = slice(hh * LANES, (hh + 1) * LANES)
        qo_ref[:, sl] = (halfnorm(q_ref[:, sl], qg_ref[...]) * (DIFF_DQK ** -0.5 * LOG2E)).astype(BF16)
        ko_ref[:, sl] = halfnorm(k_ref[:, sl], kg_ref[...]).astype(BF16)
    vo_ref[...] = v_ref[...].astype(BF16)


def _diff_prep(h, qg, kg, tm):
    m = h.shape[0]
    w = DIFF_HEADS * LANES
    spec = lambda off: pl.BlockSpec((tm, w), lambda i, off=off: (i, off // w))
    vec = pl.BlockSpec((1, LANES), lambda i: (0, 0))
    out = jax.ShapeDtypeStruct((m, w), BF16)
    return pl.pallas_call(
        _diff_prep_kernel,
        grid=(m // tm,),
        in_specs=[spec(EV_DQ), spec(EV_DK), spec(EV_DV), vec, vec],
        out_specs=[pl.BlockSpec((tm, w), lambda i: (i, 0))] * 3,
        out_shape=[out, out, out],
        compiler_params=_cparams(("parallel",)),
        name="diff_prep",
    )(h, h, h, jnp.tile(qg, 2).reshape(1, LANES), jnp.tile(kg, 2).reshape(1, LANES))


def _toeplitz(vec, tq, tk):
    w = vec.shape[-1]
    full = pltpu.roll(jnp.broadcast_to(vec, (tq, w)), w - (tq - 1), 1, stride=1, stride_axis=0)
    return full[:, :tk]


def _diff_flash_kernel(q_ref, k_ref, v_ref, gate_ref, bvec_ref, lam_ref, sub_ref, y_ref, bias_ref, *scratch,
                       lambda_init):
    tq = q_ref.shape[0]
    tk = bias_ref.shape[-1]
    qi = pl.program_id(2)

    @pl.when(qi == 0)
    def _():
        for d in range(bias_ref.shape[0]):
            bias_ref[d] = _toeplitz(bvec_ref[d:d + 1, :], tq, tk)

    q = q_ref[...]
    lane = lax.broadcasted_iota(jnp.int32, (1, LANES), 1)
    zero = jnp.zeros_like(q)
    q0 = jnp.where(lane < DIFF_DQK, q, zero)
    q1 = jnp.where(lane < DIFF_DQK, zero, q)

    nk = k_ref.shape[0] // tk
    near = 3
    n0 = jnp.clip(qi - 1, 0, nk - near)
    c_before, c_after = bvec_ref[0:1, 0:1], bvec_ref[4:5, 0:1]
    nfar = nk - near

    def step_of(t):
        if isinstance(t, int) and t >= nfar:
            j = n0 + (t - nfar)
            return j, bias_ref.at[jnp.clip(j - qi, -2, 2) + 2], None
        j = jnp.where(t < n0, t, t + near)
        j = jnp.where(t < nfar, j, n0 + (t - nfar))
        return j, None, jnp.where(j < qi, c_before, c_after)

    loop = (1, nfar - (nfar - 1) % 2)
    (a0, l0), (a1, l1) = _flash_core([q0, q1], k_ref, v_ref, nk, step_of, loop, scratch)

    lam = lam_ref[...]
    lam_full = (jnp.exp(jnp.sum(lam[0:1] * lam[1:2], axis=-1, keepdims=True))
                - jnp.exp(jnp.sum(lam[2:3] * lam[3:4], axis=-1, keepdims=True)) + lambda_init)
    o = a0 / l0 - lam_full * (a1 / l1)
    o = _rms(o, sub_ref[...]) * (1.0 - lambda_init)
    y_ref[...] = (o * _silu(gate_ref[...])).astype(y_ref.dtype)


def _diff_flash(dq, dk, dv, h, bvec, lam, subln, lambda_init, bsz, seq):
    tq, tk = ATT_TQ, ATT_TK
    nq = seq // tq
    nh = DIFF_HEADS
    return pl.pallas_call(
        functools.partial(_diff_flash_kernel, lambda_init=lambda_init),
        grid=(bsz, nh, nq),
        in_specs=[
            pl.BlockSpec((tq, LANES), lambda b, hh, i: (b * nq + i, hh)),
            pl.BlockSpec((seq, LANES), lambda b, hh, i: (b, hh)),
            pl.BlockSpec((seq, LANES), lambda b, hh, i: (b, hh)),
            pl.BlockSpec((tq, LANES), lambda b, hh, i: (b * nq + i, EV_DG // LANES + hh)),
            pl.BlockSpec((None,) + bvec.shape[1:], lambda b, hh, i: (hh, 0, 0)),
            pl.BlockSpec((4, DIFF_DQK), lambda b, hh, i: (0, 0)),
            pl.BlockSpec((1, LANES), lambda b, hh, i: (0, 0)),
        ],
        out_specs=pl.BlockSpec((tq, LANES), lambda b, hh, i: (b * nq + i, hh)),
        out_shape=jax.ShapeDtypeStruct((bsz * seq, nh * DIFF_DV), BF16),
        scratch_shapes=[pltpu.VMEM((5, tq, tk), F32)] + _flash_scratch(2, tq, tk, DIFF_DV),
        compiler_params=_cparams(("parallel", "parallel", "arbitrary")),
        name="diff_flash",
    )(dq, dk, dv, h, bvec, lam, subln.reshape(1, LANES))


def _mem_attn_kernel(q_ref, gate_ref, mk_ref, mv_ref, qg_ref, kg_ref, y_ref):
    q = (_rms(q_ref[...], qg_ref[...]) * (MEM_DH ** -0.5)).astype(BF16)
    mk = _rms(mk_ref[...], kg_ref[...]).astype(BF16)
    s = _dot_nt(q, mk)
    p = jnp.exp(s - jnp.max(s, axis=-1, keepdims=True))
    o = _dot(p.astype(BF16), mv_ref[...].astype(BF16)) / jnp.sum(p, axis=-1, keepdims=True)
    y_ref[...] = (o * _silu(gate_ref[...])).astype(y_ref.dtype)


def _mem_attn(h, memkv, qg, kg, q_off, g_off, bsz, seq, tq):
    nq = seq // tq
    nh = MEM_HEADS
    vec = pl.BlockSpec((1, LANES), lambda b, hh, i: (0, 0))
    return pl.pallas_call(
        _mem_attn_kernel,
        grid=(bsz, nh, nq),
        in_specs=[
            pl.BlockSpec((tq, LANES), lambda b, hh, i: (b * nq + i, q_off // LANES + hh)),
            pl.BlockSpec((tq, LANES), lambda b, hh, i: (b * nq + i, g_off // LANES + hh)),
            pl.BlockSpec((MEM_LEN, LANES), lambda b, hh, i: (b, hh)),
            pl.BlockSpec((MEM_LEN, LANES), lambda b, hh, i: (b, nh + hh)),
            vec, vec,
        ],
        out_specs=pl.BlockSpec((tq, LANES), lambda b, hh, i: (b * nq + i, hh)),
        out_shape=jax.ShapeDtypeStruct((bsz * seq, nh * MEM_DH), BF16),
        compiler_params=_cparams(("parallel", "parallel", "parallel")),
        name="mem_attn",
    )(h, h, memkv, memkv, qg.reshape(1, LANES), kg.reshape(1, LANES))


def _swa_kernel(sink_ref, q0_ref, q1_ref, k_ref, v_ref, g0_ref, g1_ref, bias_ref, qg_ref, kg_ref, y_ref, tile_ref):
    tq = q0_ref.shape[0]
    seq = k_ref.shape[0]
    grp = SWA_HEADS // SWA_KV_HEADS
    qi = pl.program_id(1)
    nq = pl.num_programs(1)

    @pl.when(jnp.logical_or(qi <= 1, qi == nq - 1))
    def _():
        for kvh in range(SWA_KV_HEADS):
            for g in range(grp):
                tile_ref[kvh * grp + g] = _toeplitz(bias_ref[kvh, g:g + 1, :], tq, SWA_TW)

    ws = pl.multiple_of(jnp.clip(qi * tq - WINDOW, 0, seq - SWA_TW), WINDOW)
    heads = [slice(g * SWA_DH, (g + 1) * SWA_DH) for g in range(grp)]
    for kvh, (q_ref, gate_ref) in enumerate(((q0_ref, g0_ref), (q1_ref, g1_ref))):
        kv_cols = slice(kvh * SWA_DH, (kvh + 1) * SWA_DH)
        kw = _rms(k_ref[pl.ds(ws, SWA_TW), kv_cols], kg_ref[...]).astype(BF16)
        vw = v_ref[pl.ds(ws, SWA_TW), kv_cols].astype(BF16)
        qs = [(_rms(q_ref[:, sl], qg_ref[...]) * (SWA_DH ** -0.5)).astype(BF16) for sl in heads]
        ss = [_dot_nt(q, kw) + tile_ref[kvh * grp + g] for g, q in enumerate(qs)]
        ps, dens = [], []
        for g, s in enumerate(ss):
            sink = sink_ref[kvh * grp + g]
            mx = jnp.maximum(jnp.max(s, axis=-1, keepdims=True), sink)
            p = jnp.exp(s - mx)
            dens.append(jnp.sum(p, axis=-1, keepdims=True) + jnp.exp(sink - mx))
            ps.append(p.astype(BF16))
        for g, (sl, p, den) in enumerate(zip(heads, ps, dens)):
            out = slice((kvh * grp + g) * SWA_DH, (kvh * grp + g + 1) * SWA_DH)
            y_ref[:, out] = (_dot(p, vw) / den * _silu(gate_ref[:, sl])).astype(y_ref.dtype)


def _swa(h, bias, sink, qg, kg, bsz, seq):
    tq = SWA_TQ
    nq = seq // tq
    grp = SWA_HEADS // SWA_KV_HEADS
    gw = grp * SWA_DH
    kvw = SWA_KV_HEADS * SWA_DH
    assert seq >= SWA_TW and nq >= 2 and SWA_KV_HEADS == 2
    vec = pl.BlockSpec((1, LANES), lambda b, i: (0, 0))
    cols = lambda off, kv: pl.BlockSpec((tq, gw), lambda b, i, off=off, kv=kv: (b * nq + i, off // gw + kv))

    def bias_idx(b, i):
        return (jnp.where(i == 0, 0, jnp.where(i == nq - 1, 2, 1)), 0, 0, 0)

    return pl.pallas_call(
        _swa_kernel,
        grid=(bsz, nq),
        in_specs=[
            pl.BlockSpec(memory_space=pltpu.SMEM),
            cols(OD_SQ, 0), cols(OD_SQ, 1),
            pl.BlockSpec((seq, kvw), lambda b, i: (b, OD_SK // kvw)),
            pl.BlockSpec((seq, kvw), lambda b, i: (b, OD_SV // kvw)),
            cols(OD_SG, 0), cols(OD_SG, 1),
            pl.BlockSpec((None, SWA_KV_HEADS, grp, bias.shape[-1]), bias_idx),
            vec, vec,
        ],
        out_specs=pl.BlockSpec((tq, SWA_HEADS * SWA_DH), lambda b, i: (b * nq + i, 0)),
        out_shape=jax.ShapeDtypeStruct((bsz * seq, SWA_HEADS * SWA_DH), BF16),
        scratch_shapes=[pltpu.VMEM((SWA_HEADS, tq, SWA_TW), F32)],
        compiler_params=_cparams(("parallel", "arbitrary")),
        name="swa",
    )(sink, h, h, h, h, h, h, bias, qg.reshape(1, LANES), kg.reshape(1, LANES))


def _mla_prep_kernel(q_ref, ckv_ref, kr_ref, cos_ref, sin_ref, kvg_ref, wup_ref,
                     qgn_ref, qgr_ref, kgn_ref, kgr_ref, qo_ref, ko_ref, vo_ref):
    dqk = MLA_NOPE + MLA_ROPE
    lane = lax.broadcasted_iota(jnp.int32, (1, LANES), 1)
    lo = lane < MLA_ROPE
    first = lane < MLA_ROPE // 2
    cos = cos_ref[...]
    sin = sin_ref[...]

    def rope(t):
        rot = jnp.where(first, pltpu.roll(t, LANES - MLA_ROPE // 2, 1), pltpu.roll(t, MLA_ROPE // 2, 1))
        return t * cos + rot * sin

    ckv = _rms(ckv_ref[...], kvg_ref[...]).astype(BF16)
    kv = _dot(ckv, wup_ref[...])
    kr = kr_ref[...]
    kr2 = kr * kr
    c = dqk ** -0.5 * LOG2E
    hw = MLA_NOPE + MLA_DV
    for hh in range(MLA_HEADS):
        kn = kv[:, hh * hw:hh * hw + MLA_NOPE]
        inv = lax.rsqrt(_lane_sum(kn * kn + kr2, 1.0 / dqk) + EPS)
        ko_ref[:, hh * 2 * LANES:hh * 2 * LANES + LANES] = (kn * inv * kgn_ref[...]).astype(BF16)
        ko_ref[:, hh * 2 * LANES + LANES:(hh + 1) * 2 * LANES] = rope(kr * inv * kgr_ref[...]).astype(BF16)
        vo_ref[:, hh * LANES:(hh + 1) * LANES] = kv[:, hh * hw + MLA_NOPE:(hh + 1) * hw].astype(BF16)

        qn = q_ref[:, hh * LANES:(hh + 1) * LANES].astype(F32)
        pair = q_ref[:, OD_MLQ_ROPE + (hh // 2) * LANES:OD_MLQ_ROPE + (hh // 2 + 1) * LANES].astype(F32)
        if hh % 2 == 1:
            pair = pltpu.roll(pair, MLA_ROPE, 1)
        qr = jnp.where(lo, pair, 0.0)
        inv = lax.rsqrt(_lane_sum(qn * qn + qr * qr, 1.0 / dqk) + EPS)
        qo_ref[:, hh * 2 * LANES:hh * 2 * LANES + LANES] = (qn * inv * qgn_ref[...] * c).astype(BF16)
        qo_ref[:, hh * 2 * LANES + LANES:(hh + 1) * 2 * LANES] = (rope(qr * inv * qgr_ref[...]) * c).astype(BF16)


def _mla_prep(h, side, cos, sin, kv_gain, w_up, qg, kg, tm):
    m = h.shape[0]
    nh = MLA_HEADS
    qw = nh * (MLA_NOPE + MLA_ROPE)
    pad = lambda g: jnp.pad(g[MLA_NOPE:], (0, LANES - MLA_ROPE)).reshape(1, LANES)
    vec = pl.BlockSpec((1, LANES), lambda i: (0, 0))
    row = lambda i: (i, 0)
    return pl.pallas_call(
        _mla_prep_kernel,
        grid=(m // tm,),
        in_specs=[
            pl.BlockSpec((tm, qw), row),
            pl.BlockSpec((tm, MLA_KV_RANK), lambda i: (i, OD_CKV // MLA_KV_RANK)),
            pl.BlockSpec((tm, LANES), row),
            pl.BlockSpec((tm, LANES), row),
            pl.BlockSpec((tm, LANES), row),
            pl.BlockSpec((1, MLA_KV_RANK), lambda i: (0, 0)),
            pl.BlockSpec(w_up.shape, lambda i: (0, 0)),
            vec, vec, vec, vec,
        ],
        out_specs=[
            pl.BlockSpec((tm, nh * 2 * LANES), row),
            pl.BlockSpec((tm, nh * 2 * LANES), row),
            pl.BlockSpec((tm, nh * MLA_DV), row),
        ],
        out_shape=[
            jax.ShapeDtypeStruct((m, nh * 2 * LANES), BF16),
            jax.ShapeDtypeStruct((m, nh * 2 * LANES), BF16),
            jax.ShapeDtypeStruct((m, nh * MLA_DV), BF16),
        ],
        compiler_params=_cparams(("parallel",)),
        name="mla_prep",
    )(h, h, side, cos, sin, kv_gain.reshape(1, MLA_KV_RANK), w_up.astype(BF16),
      qg[:MLA_NOPE].reshape(1, LANES), pad(qg), kg[:MLA_NOPE].reshape(1, LANES), pad(kg))


def _mla_flash_kernel(q_ref, k_ref, v_ref, gate_ref, y_ref, *scratch):
    nk = k_ref.shape[0] // ATT_TK
    (acc, l), = _flash_core([q_ref[...]], k_ref, v_ref, nk, lambda t: (t, None, None), (1, nk - 1 - nk % 2), scratch)
    y_ref[...] = (acc / l * _silu(gate_ref[...])).astype(y_ref.dtype)


def _mla_flash(qm, km, vm, h, bsz, seq):
    tq = ATT_TQ
    nq = seq // tq
    nh = MLA_HEADS
    return pl.pallas_call(
        _mla_flash_kernel,
        grid=(bsz, nh, nq),
        in_specs=[
            pl.BlockSpec((tq, 2 * LANES), lambda b, hh, i: (b * nq + i, hh)),
            pl.BlockSpec((seq, 2 * LANES), lambda b, hh, i: (b, hh)),
            pl.BlockSpec((seq, LANES), lambda b, hh, i: (b, hh)),
            pl.BlockSpec((tq, LANES), lambda b, hh, i: (b * nq + i, OD_MLG // LANES + hh)),
        ],
        out_specs=pl.BlockSpec((tq, LANES), lambda b, hh, i: (b * nq + i, hh)),
        out_shape=jax.ShapeDtypeStruct((bsz * seq, nh * MLA_DV), BF16),
        scratch_shapes=_flash_scratch(1, tq, ATT_TK, MLA_DV),
        compiler_params=_cparams(("parallel", "parallel", "arbitrary")),
        name="mla_flash",
    )(qm, km, vm, h)


def _t5_bucket(rel):
    half = REL_BUCKETS // 2
    max_exact = half // 2
    ret = (rel > 0).astype(jnp.int32) * half
    n = jnp.abs(rel)
    nf = jnp.maximum(n, 1).astype(F32)
    large = max_exact + (jnp.log(nf / max_exact) / math.log(REL_MAX_DIST / max_exact)
                         * (half - max_exact)).astype(jnp.int32)
    large = jnp.minimum(large, half - 1)
    return ret + jnp.where(n < max_exact, n, large)


def _round_up(n, m):
    return (n + m - 1) // m * m


def _diff_bias_vecs(rel_table, tq, tk):
    assert tq == tk and tk + 1 >= REL_MAX_DIST
    m = jnp.arange(_round_up(tq + tk - 1, LANES))
    rows = [rel_table[_t5_bucket(m - (tq - 1) + d * tk)] for d in (-2, -1, 0, 1, 2)]
    return jnp.pad(jnp.stack(rows, axis=0).transpose(2, 0, 1) * LOG2E, ((0, 0), (0, 3), (0, 0))).astype(F32)


def _swa_bias_vecs(rel_table):
    m = jnp.arange(_round_up(SWA_TQ + SWA_TW - 1, LANES))
    rows = []
    for shift in (0, WINDOW, SWA_TW - SWA_TQ):
        rel = m - (SWA_TQ - 1) - shift
        rows.append(jnp.where((jnp.abs(rel) <= WINDOW)[None], rel_table[_t5_bucket(rel)].T, NEG))
    return jnp.stack(rows, axis=0).reshape(3, SWA_KV_HEADS, SWA_HEADS // SWA_KV_HEADS, -1).astype(F32)


def _od_main_weight(wt):
    mlq_lo = 2 * SWA_HEADS * SWA_DH + 2 * SWA_KV_HEADS * SWA_DH
    mlq_hi = mlq_lo + MLA_HEADS * (MLA_NOPE + MLA_ROPE)
    mlq = wt[mlq_lo:mlq_hi].reshape(MLA_HEADS, MLA_NOPE + MLA_ROPE, -1)
    nope = mlq[:, :MLA_NOPE].reshape(MLA_HEADS * MLA_NOPE, -1)
    rope = mlq[:, MLA_NOPE:].reshape(MLA_HEADS * MLA_ROPE, -1)
    return jnp.concatenate([nope, rope, wt[:mlq_lo], wt[mlq_hi:OD_KR_LO], wt[OD_KR_HI:]], axis=0).astype(BF16)


def _rope_tables(positions):
    half = MLA_ROPE // 2
    inv_freq = ROPE_BASE ** (-jnp.arange(half, dtype=F32) / half)
    ang = positions.astype(F32).reshape(-1, 1) * inv_freq
    cos, sin = jnp.cos(ang), jnp.sin(ang)
    z = jnp.zeros((ang.shape[0], LANES - MLA_ROPE), F32)
    return jnp.concatenate([cos, cos, z], axis=1), jnp.concatenate([-sin, sin, z], axis=1)


def _side_weight(wt, lo, hi):
    return jnp.pad(wt[lo:hi], ((0, LANES - (hi - lo)), (0, 0))).astype(BF16)


def _mem_kv(mem2, gain, w_kv):
    zero_side = jnp.zeros((LANES, D_MODEL), BF16)
    memkv, _ = _norm_proj(mem2, gain, jnp.swapaxes(w_kv, 0, 1).astype(BF16), zero_side, tm=mem2.shape[0] // 2,
                          tn=512, out_dtype=F32)
    return memkv


def _even_layer(x2, mem2, rel_bias, norm_g, w_in, conv_w, a_log, dt_bias, gdn_gain, dq_gain, dk_gain, lam, subln,
                mem_norm, mem_w_kv, mem_qn, mem_kn, w_out, lambda_init, bsz, seq):
    wt = jnp.swapaxes(w_in, 0, 1)
    w_main = jnp.concatenate([wt[:EV_SIDE_LO], wt[EV_SIDE_HI:]], axis=0).astype(BF16)
    h, side = _norm_proj(x2, norm_g, w_main, _side_weight(wt, EV_SIDE_LO, EV_SIDE_HI), tm=1024, tn=EV_MAIN // 6,
                         out_dtype=BF16)

    nh = GDN_HEADS
    bar = side[:, :4 * nh].reshape(bsz, seq, 4, nh).transpose(0, 3, 2, 1)
    bar = jnp.pad(bar, ((0, 0), (0, 0), (0, 4), (0, 0)))
    prr = jnp.pad(jnp.stack([a_log.T, dt_bias.T], axis=-1), ((0, 0), (2, 4), (0, 0)))
    ya = _gdn(h, bar, conv_w, prr, gdn_gain, bsz, seq)

    dq, dk, dv = _diff_prep(h, dq_gain, dk_gain, tm=1024)
    bias = _diff_bias_vecs(rel_bias, ATT_TQ, ATT_TK)
    yb = _diff_flash(dq, dk, dv, h, bias, lam, subln, lambda_init, bsz, seq)

    memkv = _mem_kv(mem2, mem_norm, mem_w_kv)
    ym = _mem_attn(h, memkv, mem_qn, mem_kn, EV_MQ, EV_MG, bsz, seq, tq=seq)
    return _out_proj(x2, ya, yb, ym, w_out, tm=512)


def _odd_layer(x2, mem2, positions, rel_bias, norm_g, w_in, swa_qn, swa_kn, sink, kv_norm, w_kv_up, mla_qn, mla_kn,
               mem_norm, mem_w_kv, mem_qn, mem_kn, w_out, bsz, seq):
    wt = jnp.swapaxes(w_in, 0, 1)
    h, side = _norm_proj(x2, norm_g, _od_main_weight(wt), _side_weight(wt, OD_KR_LO, OD_KR_HI), tm=1024,
                         tn=OD_MAIN // 4, out_dtype=BF16)

    ya = _swa(h, _swa_bias_vecs(rel_bias), sink, swa_qn, swa_kn, bsz, seq)

    cos, sin = _rope_tables(positions)
    qm, km, vm = _mla_prep(h, side, cos, sin, kv_norm, w_kv_up, mla_qn, mla_kn, tm=512)
    yb = _mla_flash(qm, km, vm, h, bsz, seq)

    memkv = _mem_kv(mem2, mem_norm, mem_w_kv)
    ym = _mem_attn(h, memkv, mem_qn, mem_kn, OD_MQ, OD_MG, bsz, seq, tq=seq)
    return _out_proj(x2, ya, yb, ym, w_out, tm=512)


def kernel(x, mem, positions, rel_bias, ev_norm, ev_w_in, ev_conv, ev_a_log, ev_dt_bias, ev_gdn_norm, ev_diff_qnorm, ev_diff_knorm, ev_diff_lambda, ev_diff_subln, ev_mem_norm, ev_mem_w_kv, ev_mem_qnorm, ev_mem_knorm, ev_w_out, od_norm, od_w_in, od_swa_qnorm, od_swa_knorm, od_swa_sink, od_mla_kv_norm, od_mla_w_kv_up, od_mla_qnorm, od_mla_knorm, od_mem_norm, od_mem_w_kv, od_mem_qnorm, od_mem_knorm, od_w_out):
    bsz, seq, d = x.shape
    depth = ev_norm.shape[0] + od_norm.shape[0]
    x2 = x.reshape(bsz * seq, d)
    mem2 = mem.reshape(bsz * mem.shape[1], d)
    for layer in range(depth):
        i = layer // 2
        if layer % 2 == 0:
            lambda_init = 0.8 - 0.6 * math.exp(-0.3 * layer)
            x2 = _even_layer(x2, mem2, rel_bias, ev_norm[i], ev_w_in[i], ev_conv[i], ev_a_log[i], ev_dt_bias[i],
                             ev_gdn_norm[i], ev_diff_qnorm[i], ev_diff_knorm[i], ev_diff_lambda[i], ev_diff_subln[i],
                             ev_mem_norm[i], ev_mem_w_kv[i], ev_mem_qnorm[i], ev_mem_knorm[i], ev_w_out[i],
                             lambda_init, bsz, seq)
        else:
            x2 = _odd_layer(x2, mem2, positions, rel_bias, od_norm[i], od_w_in[i], od_swa_qnorm[i], od_swa_knorm[i],
                            od_swa_sink[i], od_mla_kv_norm[i], od_mla_w_kv_up[i], od_mla_qnorm[i], od_mla_knorm[i],
                            od_mem_norm[i], od_mem_w_kv[i], od_mem_qnorm[i], od_mem_knorm[i], od_w_out[i], bsz, seq)
    return x2.reshape(bsz, seq, d)
```
